```python
import jax, jax.numpy as jnp
from jax import lax
import numpy as np

D_MODEL = 4096
BATCH = 8
SEQ = 4096
DEPTH = 1

HEAD_DIM = 64
MIX_WIDTH = D_MODEL
RWKV_WIDTH = 3 * D_MODEL // 8
FOX_WIDTH = 3 * D_MODEL // 8
MEM_WIDTH = D_MODEL // 4
RWKV_HEADS = RWKV_WIDTH // HEAD_DIM
FOX_HEADS = FOX_WIDTH // HEAD_DIM
MEM_HEADS = 4
MEM_HEAD_DIM = MEM_WIDTH // MEM_HEADS
N_MEM = 256
DECAY_LORA = 128
ICLR_LORA = 128
Q_BLOCK = 128
RMS_EPS = 1e-6
GN_EPS = 64e-5

RWKV_SHIFT_SPLITS = [RWKV_WIDTH, RWKV_WIDTH, RWKV_WIDTH, DECAY_LORA, ICLR_LORA]
RWKV_SHIFT_WIDTH = 3 * RWKV_WIDTH + DECAY_LORA + ICLR_LORA
REST_SPLITS = [RWKV_WIDTH, FOX_WIDTH, FOX_WIDTH, FOX_WIDTH, FOX_HEADS, FOX_WIDTH, MEM_WIDTH, MEM_WIDTH]
REST_WIDTH = RWKV_WIDTH + 4 * FOX_WIDTH + FOX_HEADS + 2 * MEM_WIDTH
IN_WIDTH = RWKV_SHIFT_WIDTH + REST_WIDTH

kernel_name = "hymba_rwkv7_fox_memxattn_layer"


def _offsets(sizes):
    return [int(s) for s in np.cumsum(sizes)[:-1]]


def rms_norm(x, g):
    xf = x.astype(jnp.float32)
    y = xf * lax.rsqrt(jnp.mean(xf * xf, axis=-1, keepdims=True) + RMS_EPS)
    return (y * g.astype(jnp.float32)).astype(x.dtype)


def token_shift(p, mu):
    prev = jnp.pad(p, ((0, 0), (1, 0), (0, 0)))[:, :-1]
    return p + (prev - p) * mu


def rwkv7_branch(r, k, v, wl, al, w0, w_decay_up, a0, w_iclr_up, k_k, k_a, r_k, ln_x_w, ln_x_b):
    B, T, _ = r.shape
    H, N = RWKV_HEADS, HEAD_DIM
    f32 = jnp.float32
    w_pre = -jax.nn.softplus(-(w0 + jnp.tanh(wl) @ w_decay_up).astype(f32)) - 0.5
    decay = jnp.exp(-jnp.exp(w_pre))
    alpha = jax.nn.sigmoid((a0 + al @ w_iclr_up).astype(f32))
    kk = (k * k_k).astype(f32).reshape(B, T, H, N)
    kk = kk * lax.rsqrt(jnp.maximum(jnp.sum(kk * kk, axis=-1, keepdims=True), 1e-24))
    k_mod = k.astype(f32) * (1.0 + (alpha - 1.0) * k_a)

    def heads(z):
        return z.astype(f32).reshape(B, T, H, N)

    r_h, k_h, v_h, w_h, al_h = heads(r), heads(k_mod), heads(v), heads(decay), heads(alpha)
    a_h = -kk
    b_h = kk * al_h

    def step(S, inp):
        r_t, w_t, k_t, v_t, a_t, b_t = inp
        Sa = jnp.einsum('bhij,bhj->bhi', S, a_t)
        S = S * w_t[:, :, None, :] + Sa[..., None] * b_t[:, :, None, :] + v_t[..., None] * k_t[:, :, None, :]
        y_t = jnp.einsum('bhij,bhj->bhi', S, r_t)
        return S, y_t

    tm = lambda z: jnp.moveaxis(z, 1, 0)
    S0 = jnp.zeros((B, H, N, N), f32)
    _, y = lax.scan(step, S0, (tm(r_h), tm(w_h), tm(k_h), tm(v_h), tm(a_h), tm(b_h)))
    y = jnp.moveaxis(y, 0, 1)
    mean = jnp.mean(y, axis=-1, keepdims=True)
    var = jnp.mean(jnp.square(y - mean), axis=-1, keepdims=True)
    y = (y - mean) * lax.rsqrt(var + GN_EPS) * ln_x_w.astype(f32).reshape(H, N) + ln_x_b.astype(f32).reshape(H, N)
    bonus = jnp.sum(r_h * k_h * r_k.astype(f32), axis=-1, keepdims=True) * v_h
    return (y + bonus).reshape(B, T, H * N).astype(r.dtype)


def fox_attention(q, k, v, log_f):
    B, H, T, D = q.shape
    n_blk = T // Q_BLOCK
    cum = jnp.cumsum(log_f.astype(jnp.float32), axis=-1)
    kpos = jnp.arange(T)
    scale = D ** -0.5

    def one_block(i):
        start = i * Q_BLOCK
        qb = lax.dynamic_slice_in_dim(q, start, Q_BLOCK, axis=2)
        cb = lax.dynamic_slice_in_dim(cum, start, Q_BLOCK, axis=2)
        s = jnp.einsum('bhqd,bhkd->bhqk', qb, k).astype(jnp.float32) * scale + cb[..., :, None] - cum[..., None, :]
        qpos = start + jnp.arange(Q_BLOCK)
        s = jnp.where(kpos[None, :] <= qpos[:, None], s, -jnp.inf)
        p = jax.nn.softmax(s, axis=-1)
        return jnp.einsum('bhqk,bhkd->bhqd', p.astype(v.dtype), v)

    out = lax.map(one_block, jnp.arange(n_blk))
    return jnp.moveaxis(out, 0, 2).reshape(B, H, T, D)


def memory_cross_attention(q, mem, g_mem, w_mem_kv):
    B, T, _ = q.shape
    M = mem.shape[1]
    mkv = rms_norm(mem, g_mem) @ w_mem_kv
    mk, mv = jnp.split(mkv, 2, axis=-1)
    qh = q.reshape(B, T, MEM_HEADS, MEM_HEAD_DIM)
    mk = mk.reshape(B, M, MEM_HEADS, MEM_HEAD_DIM)
    mv = mv.reshape(B, M, MEM_HEADS, MEM_HEAD_DIM)
    s = jnp.einsum('bthd,bmhd->bhtm', qh, mk).astype(jnp.float32) * (MEM_HEAD_DIM ** -0.5)
    p = jax.nn.softmax(s, axis=-1)
    o = jnp.einsum('bhtm,bmhd->bthd', p.astype(mv.dtype), mv)
    return o.reshape(B, T, MEM_WIDTH)


def hybrid_layer(x, mem, g_pre, w_in, mu_rwkv, w0, w_decay_up, a0, w_iclr_up, k_k, k_a, r_k,
                 ln_x_w, ln_x_b, b_f, g_mem, w_mem_kv, w_out, g_post):
    B, T, _ = x.shape
    h = rms_norm(x, g_pre)
    p = h @ w_in
    p_shift = token_shift(p[..., :RWKV_SHIFT_WIDTH], mu_rwkv)
    r, k, v, wl, al = jnp.split(p_shift, _offsets(RWKV_SHIFT_SPLITS), axis=-1)
    g_rwkv, fq, fk, fv, f_logit, g_fox, mq, g_mq = jnp.split(p[..., RWKV_SHIFT_WIDTH:], _offsets(REST_SPLITS), axis=-1)

    y_rwkv = rwkv7_branch(r, k, v, wl, al, w0, w_decay_up, a0, w_iclr_up, k_k, k_a, r_k, ln_x_w, ln_x_b)

    to_heads = lambda z: z.reshape(B, T, FOX_HEADS, HEAD_DIM).transpose(0, 2, 1, 3)
    log_f = jax.nn.log_sigmoid((f_logit + b_f).astype(jnp.float32)).transpose(0, 2, 1)
    y_fox = fox_attention(to_heads(fq), to_heads(fk), to_heads(fv), log_f)
    y_fox = y_fox.transpose(0, 2, 1, 3).reshape(B, T, FOX_WIDTH)

    y_mem = memory_cross_attention(mq, mem, g_mem, w_mem_kv)

    y = jnp.concatenate([y_rwkv * jax.nn.silu(g_rwkv),
                         y_fox * jax.nn.silu(g_fox),
                         y_mem * jax.nn.silu(g_mq)], axis=-1)
    y = y @ w_out
    return x + rms_norm(y, g_post)


def _fwd_setup_inputs(seed: int = 0) -> dict:
    key = jax.random.key(seed)
    ks = jax.random.split(key, 20)
    f32 = jnp.float32
    nrm = lambda k, shape, s: jax.random.normal(k, shape, f32) * s
    n = jnp.arange(RWKV_WIDTH, dtype=f32) / (RWKV_WIDTH - 1)
    return {
        "x": nrm(ks[0], (BATCH, SEQ, D_MODEL), 1.0),
        "mem": nrm(ks[1], (BATCH, N_MEM, D_MODEL), 1.0),
        "g_pre": 1.0 + nrm(ks[2], (DEPTH, D_MODEL), 0.02),
        "w_in": nrm(ks[3], (DEPTH, D_MODEL, IN_WIDTH), D_MODEL ** -0.5),
        "mu_rwkv": jax.random.uniform(ks[4], (DEPTH, RWKV_SHIFT_WIDTH), f32),
        "w0": (-5.5 + 5.0 * n ** 0.85)[None, :] + nrm(ks[5], (DEPTH, RWKV_WIDTH), 0.1),
        "w_decay_up": nrm(ks[6], (DEPTH, DECAY_LORA, RWKV_WIDTH), DECAY_LORA ** -0.5),
        "a0": nrm(ks[7], (DEPTH, RWKV_WIDTH), 0.1),
        "w_iclr_up": nrm(ks[8], (DEPTH, ICLR_LORA, RWKV_WIDTH), ICLR_LORA ** -0.5),
        "k_k": 0.85 + nrm(ks[9], (DEPTH, RWKV_WIDTH), 0.02),
        "k_a": 1.0 + nrm(ks[10], (DEPTH, RWKV_WIDTH), 0.02),
        "r_k": -0.04 + nrm(ks[11], (DEPTH, RWKV_HEADS, HEAD_DIM), 0.02),
        "ln_x_w": 1.0 + nrm(ks[12], (DEPTH, RWKV_WIDTH), 0.02),
        "ln_x_b": nrm(ks[13], (DEPTH, RWKV_WIDTH), 0.02),
        "b_f": 2.0 + nrm(ks[14], (DEPTH, FOX_HEADS), 0.5),
        "g_mem": 1.0 + nrm(ks[15], (DEPTH, D_MODEL), 0.02),
        "w_mem_kv": nrm(ks[16], (DEPTH, D_MODEL, 2 * MEM_WIDTH), D_MODEL ** -0.5),
        "w_out": nrm(ks[17], (DEPTH, MIX_WIDTH, D_MODEL), MIX_WIDTH ** -0.5),
        "g_post": 1.0 + nrm(ks[18], (DEPTH, D_MODEL), 0.02),
    }


def _fwd_reference(x, mem, g_pre, w_in, mu_rwkv, w0, w_decay_up, a0, w_iclr_up, k_k, k_a, r_k,
              ln_x_w, ln_x_b, b_f, g_mem, w_mem_kv, w_out, g_post):
    for l in range(DEPTH):
        x = hybrid_layer(x, mem, g_pre[l], w_in[l], mu_rwkv[l], w0[l], w_decay_up[l], a0[l],
                         w_iclr_up[l], k_k[l], k_a[l], r_k[l], ln_x_w[l], ln_x_b[l], b_f[l],
                         g_mem[l], w_mem_kv[l], w_out[l], g_post[l])
    return x


import jax as _jax
import jax.numpy as _jnp

TWIN_FORMAT = 'train_step'
FWD_PARAMS = ['x', 'mem', 'g_pre', 'w_in', 'mu_rwkv', 'w0', 'w_decay_up', 'a0', 'w_iclr_up', 'k_k', 'k_a', 'r_k', 'ln_x_w', 'ln_x_b', 'b_f', 'g_mem', 'w_mem_kv', 'w_out', 'g_post']
TWIN_WEIGHTS = ['g_pre', 'w_in', 'mu_rwkv', 'w0', 'w_decay_up', 'a0', 'w_iclr_up', 'k_k', 'k_a', 'r_k', 'ln_x_w', 'ln_x_b', 'b_f', 'g_mem', 'w_mem_kv', 'w_out', 'g_post']
TWIN_DIFF_INPUT = 'x'
TWIN_INPUTS = ['x', 'mem', 'g_pre', 'w_in', 'mu_rwkv', 'w0', 'w_decay_up', 'a0', 'w_iclr_up', 'k_k', 'k_a', 'r_k', 'ln_x_w', 'ln_x_b', 'b_f', 'g_mem', 'w_mem_kv', 'w_out', 'g_post', 'loss_target', 'm_g_pre', 'm_w_in', 'm_mu_rwkv', 'm_w0', 'm_w_decay_up', 'm_a0', 'm_w_iclr_up', 'm_k_k', 'm_k_a', 'm_r_k', 'm_ln_x_w', 'm_ln_x_b', 'm_b_f', 'm_g_mem', 'm_w_mem_kv', 'm_w_out', 'm_g_post', 'v_g_pre', 'v_w_in', 'v_mu_rwkv', 'v_w0', 'v_w_decay_up', 'v_a0', 'v_w_iclr_up', 'v_k_k', 'v_k_a', 'v_r_k', 'v_ln_x_w', 'v_ln_x_b', 'v_b_f', 'v_g_mem', 'v_w_mem_kv', 'v_w_out', 'v_g_post']
TWIN_OUTPUTS = ['loss', 'grad_x', 'grad_g_pre', 'grad_w_in', 'grad_mu_rwkv', 'grad_w0', 'grad_w_decay_up', 'grad_a0', 'grad_w_iclr_up', 'grad_k_k', 'grad_k_a', 'grad_r_k', 'grad_ln_x_w', 'grad_ln_x_b', 'grad_b_f', 'grad_g_mem', 'grad_w_mem_kv', 'grad_w_out', 'grad_g_post', 'delta_g_pre', 'delta_w_in', 'delta_mu_rwkv', 'delta_w0', 'delta_w_decay_up', 'delta_a0', 'delta_w_iclr_up', 'delta_k_k', 'delta_k_a', 'delta_r_k', 'delta_ln_x_w', 'delta_ln_x_b', 'delta_b_f', 'delta_g_mem', 'delta_w_mem_kv', 'delta_w_out', 'delta_g_post', 'new_m_g_pre', 'new_m_w_in', 'new_m_mu_rwkv', 'new_m_w0', 'new_m_w_decay_up', 'new_m_a0', 'new_m_w_iclr_up', 'new_m_k_k', 'new_m_k_a', 'new_m_r_k', 'new_m_ln_x_w', 'new_m_ln_x_b', 'new_m_b_f', 'new_m_g_mem', 'new_m_w_mem_kv', 'new_m_w_out', 'new_m_g_post', 'new_v_g_pre', 'new_v_w_in', 'new_v_mu_rwkv', 'new_v_w0', 'new_v_w_decay_up', 'new_v_a0', 'new_v_w_iclr_up', 'new_v_k_k', 'new_v_k_a', 'new_v_r_k', 'new_v_ln_x_w', 'new_v_ln_x_b', 'new_v_b_f', 'new_v_g_mem', 'new_v_w_mem_kv', 'new_v_w_out', 'new_v_g_post']
TWIN_LEAF_KINDS = {'loss': 'loss', 'grad_x': 'grad_x', 'grad_g_pre': 'grad_w', 'grad_w_in': 'grad_w', 'grad_mu_rwkv': 'grad_w', 'grad_w0': 'grad_w', 'grad_w_decay_up': 'grad_w', 'grad_a0': 'grad_w', 'grad_w_iclr_up': 'grad_w', 'grad_k_k': 'grad_w', 'grad_k_a': 'grad_w', 'grad_r_k': 'grad_w', 'grad_ln_x_w': 'grad_w', 'grad_ln_x_b': 'grad_w', 'grad_b_f': 'grad_w', 'grad_g_mem': 'grad_w', 'grad_w_mem_kv': 'grad_w', 'grad_w_out': 'grad_w', 'grad_g_post': 'grad_w', 'delta_g_pre': 'delta_w', 'delta_w_in': 'delta_w', 'delta_mu_rwkv': 'delta_w', 'delta_w0': 'delta_w', 'delta_w_decay_up': 'delta_w', 'delta_a0': 'delta_w', 'delta_w_iclr_up': 'delta_w', 'delta_k_k': 'delta_w', 'delta_k_a': 'delta_w', 'delta_r_k': 'delta_w', 'delta_ln_x_w': 'delta_w', 'delta_ln_x_b': 'delta_w', 'delta_b_f': 'delta_w', 'delta_g_mem': 'delta_w', 'delta_w_mem_kv': 'delta_w', 'delta_w_out': 'delta_w', 'delta_g_post': 'delta_w', 'new_m_g_pre': 'new_m', 'new_m_w_in': 'new_m', 'new_m_mu_rwkv': 'new_m', 'new_m_w0': 'new_m', 'new_m_w_decay_up': 'new_m', 'new_m_a0': 'new_m', 'new_m_w_iclr_up': 'new_m', 'new_m_k_k': 'new_m', 'new_m_k_a': 'new_m', 'new_m_r_k': 'new_m', 'new_m_ln_x_w': 'new_m', 'new_m_ln_x_b': 'new_m', 'new_m_b_f': 'new_m', 'new_m_g_mem': 'new_m', 'new_m_w_mem_kv': 'new_m', 'new_m_w_out': 'new_m', 'new_m_g_post': 'new_m', 'new_v_g_pre': 'new_v', 'new_v_w_in': 'new_v', 'new_v_mu_rwkv': 'new_v', 'new_v_w0': 'new_v', 'new_v_w_decay_up': 'new_v', 'new_v_a0': 'new_v', 'new_v_w_iclr_up': 'new_v', 'new_v_k_k': 'new_v', 'new_v_k_a': 'new_v', 'new_v_r_k': 'new_v', 'new_v_ln_x_w': 'new_v', 'new_v_ln_x_b': 'new_v', 'new_v_b_f': 'new_v', 'new_v_g_mem': 'new_v', 'new_v_w_mem_kv': 'new_v', 'new_v_w_out': 'new_v', 'new_v_g_post': 'new_v'}


def _forward(args):
    return _fwd_reference(*[args[k] for k in FWD_PARAMS])


def _output_shape():
    out = _jax.eval_shape(lambda: _forward(_fwd_setup_inputs(0)))
    return out.shape, out.dtype

N_MICROBATCH = 1
ADAM_LR = 0.001
ADAM_B1 = 0.9
ADAM_B2 = 0.999
ADAM_EPS = 1e-08
ADAM_WD = 0.01
ADAM_STEP = 10
PER_EXAMPLE_BATCH_AXIS = {'x': 0, 'mem': 0, 'loss_target': 0}
SHARED_INPUTS = []
_WEIGHT_DTYPES = {'g_pre': _jnp.float32, 'w_in': _jnp.float32, 'mu_rwkv': _jnp.float32, 'w0': _jnp.float32, 'w_decay_up': _jnp.float32, 'a0': _jnp.float32, 'w_iclr_up': _jnp.float32, 'k_k': _jnp.float32, 'k_a': _jnp.float32, 'r_k': _jnp.float32, 'ln_x_w': _jnp.float32, 'ln_x_b': _jnp.float32, 'b_f': _jnp.float32, 'g_mem': _jnp.float32, 'w_mem_kv': _jnp.float32, 'w_out': _jnp.float32, 'g_post': _jnp.float32}
MOMENT_SCALE = {'g_pre': 1.335957e-01, 'w_in': 6.988859e-02, 'mu_rwkv': 1.634136e-01, 'w0': 5.313498e-02, 'w_decay_up': 9.292234e-03, 'a0': 3.892179e-02, 'w_iclr_up': 3.461855e-02, 'k_k': 7.039460e-02, 'k_a': 1.027758e-01, 'r_k': 2.320820e-01, 'ln_x_w': 1.054325e-01, 'ln_x_b': 1.296429e-01, 'b_f': 1.569828e-01, 'g_mem': 7.411215e-03, 'w_mem_kv': 1.026368e-02, 'w_out': 6.287930e-02, 'g_post': 8.001974e+00}


def _to_microbatches(a, axis):
    t = _jnp.moveaxis(a, axis, 0)
    t = t.reshape((N_MICROBATCH, t.shape[0] // N_MICROBATCH) + t.shape[1:])
    return _jnp.moveaxis(t, 1, axis + 1)


def setup_inputs(seed: int = 0) -> dict:
    inp = _fwd_setup_inputs(seed)
    key = _jax.random.fold_in(_jax.random.key(seed), 7919)
    shape, _ = _output_shape()
    out = dict(inp)
    out["loss_target"] = _jax.random.normal(_jax.random.fold_in(key, 0), shape, _jnp.float32)
    for i, name in enumerate(TWIN_WEIGHTS):
        w = inp[name].astype(_jnp.float32)
        if MOMENT_SCALE is None:
            s = _jnp.sqrt(_jnp.mean(_jnp.square(w)) + 1e-30)
        else:
            s = MOMENT_SCALE[name]
        km, kv = _jax.random.split(_jax.random.fold_in(key, i + 1))
        out[name] = w
        out["m_" + name] = s * _jax.random.normal(km, w.shape, _jnp.float32)
        out["v_" + name] = (s * s) * _jax.random.uniform(kv, w.shape, _jnp.float32, 0.5, 1.5)
    if N_MICROBATCH > 1:
        for name, axis in PER_EXAMPLE_BATCH_AXIS.items():
            out[name] = _to_microbatches(out[name], axis)
    return {'x': out['x'], 'mem': out['mem'], 'g_pre': out['g_pre'], 'w_in': out['w_in'], 'mu_rwkv': out['mu_rwkv'], 'w0': out['w0'], 'w_decay_up': out['w_decay_up'], 'a0': out['a0'], 'w_iclr_up': out['w_iclr_up'], 'k_k': out['k_k'], 'k_a': out['k_a'], 'r_k': out['r_k'], 'ln_x_w': out['ln_x_w'], 'ln_x_b': out['ln_x_b'], 'b_f': out['b_f'], 'g_mem': out['g_mem'], 'w_mem_kv': out['w_mem_kv'], 'w_out': out['w_out'], 'g_post': out['g_post'], 'loss_target': out['loss_target'], 'm_g_pre': out['m_g_pre'], 'm_w_in': out['m_w_in'], 'm_mu_rwkv': out['m_mu_rwkv'], 'm_w0': out['m_w0'], 'm_w_decay_up': out['m_w_decay_up'], 'm_a0': out['m_a0'], 'm_w_iclr_up': out['m_w_iclr_up'], 'm_k_k': out['m_k_k'], 'm_k_a': out['m_k_a'], 'm_r_k': out['m_r_k'], 'm_ln_x_w': out['m_ln_x_w'], 'm_ln_x_b': out['m_ln_x_b'], 'm_b_f': out['m_b_f'], 'm_g_mem': out['m_g_mem'], 'm_w_mem_kv': out['m_w_mem_kv'], 'm_w_out': out['m_w_out'], 'm_g_post': out['m_g_post'], 'v_g_pre': out['v_g_pre'], 'v_w_in': out['v_w_in'], 'v_mu_rwkv': out['v_mu_rwkv'], 'v_w0': out['v_w0'], 'v_w_decay_up': out['v_w_decay_up'], 'v_a0': out['v_a0'], 'v_w_iclr_up': out['v_w_iclr_up'], 'v_k_k': out['v_k_k'], 'v_k_a': out['v_k_a'], 'v_r_k': out['v_r_k'], 'v_ln_x_w': out['v_ln_x_w'], 'v_ln_x_b': out['v_ln_x_b'], 'v_b_f': out['v_b_f'], 'v_g_mem': out['v_g_mem'], 'v_w_mem_kv': out['v_w_mem_kv'], 'v_w_out': out['v_w_out'], 'v_g_post': out['v_g_post']}


def _loss(weights, diff, rest, loss_target):
    with _jax.named_scope("forward"):
        args = {**rest, TWIN_DIFF_INPUT: diff, **{k: w.astype(_WEIGHT_DTYPES[k]) for k, w in weights.items()}}
        y = _forward(args)
    with _jax.named_scope("loss_head"):
        err = _jnp.square(y.astype(_jnp.float32) - loss_target)
        return 0.5 * _jnp.sum(_jnp.mean(err, axis=-1)) if err.ndim else 0.5 * err


def _adamw(w, g, m, v):
    m = ADAM_B1 * m + (1.0 - ADAM_B1) * g
    v = ADAM_B2 * v + (1.0 - ADAM_B2) * _jnp.square(g)
    m_hat = m / (1.0 - ADAM_B1 ** ADAM_STEP)
    v_hat = v / (1.0 - ADAM_B2 ** ADAM_STEP)
    delta = -ADAM_LR * (m_hat / (_jnp.sqrt(v_hat) + ADAM_EPS) + ADAM_WD * w)
    return delta, m, v


def reference(x, mem, g_pre, w_in, mu_rwkv, w0, w_decay_up, a0, w_iclr_up, k_k, k_a, r_k, ln_x_w, ln_x_b, b_f, g_mem, w_mem_kv, w_out, g_post, loss_target, m_g_pre, m_w_in, m_mu_rwkv, m_w0, m_w_decay_up, m_a0, m_w_iclr_up, m_k_k, m_k_a, m_r_k, m_ln_x_w, m_ln_x_b, m_b_f, m_g_mem, m_w_mem_kv, m_w_out, m_g_post, v_g_pre, v_w_in, v_mu_rwkv, v_w0, v_w_decay_up, v_a0, v_w_iclr_up, v_k_k, v_k_a, v_r_k, v_ln_x_w, v_ln_x_b, v_b_f, v_g_mem, v_w_mem_kv, v_w_out, v_g_post):
    given = dict(x=x, mem=mem, g_pre=g_pre, w_in=w_in, mu_rwkv=mu_rwkv, w0=w0, w_decay_up=w_decay_up, a0=a0, w_iclr_up=w_iclr_up, k_k=k_k, k_a=k_a, r_k=r_k, ln_x_w=ln_x_w, ln_x_b=ln_x_b, b_f=b_f, g_mem=g_mem, w_mem_kv=w_mem_kv, w_out=w_out, g_post=g_post, loss_target=loss_target, m_g_pre=m_g_pre, m_w_in=m_w_in, m_mu_rwkv=m_mu_rwkv, m_w0=m_w0, m_w_decay_up=m_w_decay_up, m_a0=m_a0, m_w_iclr_up=m_w_iclr_up, m_k_k=m_k_k, m_k_a=m_k_a, m_r_k=m_r_k, m_ln_x_w=m_ln_x_w, m_ln_x_b=m_ln_x_b, m_b_f=m_b_f, m_g_mem=m_g_mem, m_w_mem_kv=m_w_mem_kv, m_w_out=m_w_out, m_g_post=m_g_post, v_g_pre=v_g_pre, v_w_in=v_w_in, v_mu_rwkv=v_mu_rwkv, v_w0=v_w0, v_w_decay_up=v_w_decay_up, v_a0=v_a0, v_w_iclr_up=v_w_iclr_up, v_k_k=v_k_k, v_k_a=v_k_a, v_r_k=v_r_k, v_ln_x_w=v_ln_x_w, v_ln_x_b=v_ln_x_b, v_b_f=v_b_f, v_g_mem=v_g_mem, v_w_mem_kv=v_w_mem_kv, v_w_out=v_w_out, v_g_post=v_g_post)
    weights = {n: given[n] for n in TWIN_WEIGHTS}
    shared = {n: given[n] for n in SHARED_INPUTS}
    per_example = {n: given[n] for n in ['x', 'mem']}
    grad_fn = _jax.value_and_grad(_loss, argnums=(0, 1))

    def one_microbatch(ex, loss_target):
        ex = dict(ex)
        diff = ex.pop(TWIN_DIFF_INPUT)
        return grad_fn(weights, diff, {**shared, **ex}, loss_target)

    if N_MICROBATCH == 1:
        loss, (grad_w, grad_x) = one_microbatch(per_example, given["loss_target"])
    else:
        def body(carry, xs):
            loss_sum, grad_sum = carry
            l_k, (gw_k, gx_k) = one_microbatch(xs[0], xs[1])
            with _jax.named_scope("update"):
                return (loss_sum + l_k, _jax.tree.map(_jnp.add, grad_sum, gw_k)), gx_k

        init = (_jnp.zeros((), _jnp.float32), _jax.tree.map(_jnp.zeros_like, weights))
        (loss, grad_w), grad_x = _jax.lax.scan(body, init, (per_example, given["loss_target"]))
    with _jax.named_scope("update"):
        delta_w, new_m, new_v = {}, {}, {}
        for n in TWIN_WEIGHTS:
            delta_w[n], new_m[n], new_v[n] = _adamw(weights[n], grad_w[n], given["m_" + n], given["v_" + n])
    return (loss, grad_x, *[grad_w[n] for n in TWIN_WEIGHTS], *[delta_w[n] for n in TWIN_WEIGHTS],
            *[new_m[n] for n in TWIN_WEIGHTS], *[new_v[n] for n in TWIN_WEIGHTS])
```

```python
import functools

import jax
import jax.numpy as jnp
from jax import lax
from jax.experimental import pallas as pl
from jax.experimental.pallas import tpu as pltpu

F32, BF16 = jnp.float32, jnp.bfloat16
HI = lax.Precision.HIGHEST
NDEV = 8
AXES = ("x", "y", "c")
HEAD = 64
CHUNK = 64
MEM_HEADS = 4
LANE = 128
RMS_EPS = 1e-6
GN_EPS = 64e-5
NEG = -1e30
ADAM_LR, ADAM_B1, ADAM_B2, ADAM_EPS, ADAM_WD, ADAM_STEP = 0.001, 0.9, 0.999, 1e-08, 0.01, 10
VMEM_LIMIT = 56 * 1024 * 1024


def _pcall(body, **kw):
    return pl.pallas_call(body, **kw)


def _cp(sem=None, vmem=VMEM_LIMIT):
    return pltpu.CompilerParams(dimension_semantics=sem, vmem_limit_bytes=vmem)


def _tile(n, pref):
    for t in (pref, 1024, 512, 256, 128, 64, 32, 16, 8):
        if t <= pref and n % t == 0:
            return t
    return n


def _dg(a, b, ta, tb, mode):
    ca = 0 if ta else 1
    cb = 1 if tb else 0
    if mode == "bf":
        a, b, prec = a.astype(BF16), b.astype(BF16), None
    else:
        prec = HI
    return lax.dot_general(a, b, (((ca,), (cb,)), ((), ())), preferred_element_type=F32, precision=prec)


@functools.partial(jax.custom_vjp, nondiff_argnums=(2, 3, 4))
def _mm(a, b, ta, tb, mode):
    return _dg(a, b, ta, tb, mode)


def _mm_fwd(a, b, ta, tb, mode):
    return _dg(a, b, ta, tb, mode), (a, b)


def _mm_bwd(ta, tb, mode, res, g):
    a, b = res
    da = _dg(g, b, False, not tb, mode) if not ta else _dg(b, g, tb, True, mode)
    db = _dg(a, g, not ta, False, mode) if not tb else _dg(g, a, True, ta, mode)
    return da, db


_mm.defvjp(_mm_fwd, _mm_bwd)


def _sigmoid(z):
    return 1.0 / (1.0 + jnp.exp(-z))


def _softplus(z):
    return jnp.maximum(z, 0.0) + jnp.log(1.0 + jnp.exp(-jnp.abs(z)))


def _silu(z):
    return z * _sigmoid(z)


def _rms(x, g):
    return x * lax.rsqrt(jnp.mean(x * x, axis=-1, keepdims=True) + RMS_EPS) * g


def _exchange(x, name, gather):
    blk = x.shape if gather else x.shape[1:]

    def body(x_ref, o_ref, send_sems, recv_sems, local_sem):
        ix, iy, ic = lax.axis_index("x"), lax.axis_index("y"), lax.axis_index("c")
        me = 4 * ix + 2 * iy + ic

        def src(dest):
            return x_ref if gather else x_ref.at[dest]

        mine = pltpu.make_async_copy(src(me), o_ref.at[me], local_sem)
        mine.start()
        copies = []
        for k in range(1, NDEV):
            px = 1 - ix if (k >> 2) & 1 else ix
            py = 1 - iy if (k >> 1) & 1 else iy
            pc = 1 - ic if k & 1 else ic
            peer = 4 * px + 2 * py + pc
            cp = pltpu.make_async_remote_copy(
                src_ref=src(peer), dst_ref=o_ref.at[me], send_sem=send_sems.at[k - 1], recv_sem=recv_sems.at[k - 1],
                device_id=(px, py, pc), device_id_type=pl.DeviceIdType.MESH)
            cp.start()
            copies.append((cp, peer))
        for k, (cp, peer) in enumerate(copies):
            cp.wait_send()
            pltpu.make_async_remote_copy(
                src_ref=src(peer), dst_ref=o_ref.at[peer], send_sem=send_sems.at[k], recv_sem=recv_sems.at[k],
                device_id=(ix, iy, ic), device_id_type=pl.DeviceIdType.MESH).wait_recv()
        mine.wait()

    return _pcall(
        body, name=name,
        out_shape=jax.ShapeDtypeStruct((NDEV,) + tuple(blk), x.dtype),
        in_specs=[pl.BlockSpec(memory_space=pltpu.HBM)],
        out_specs=pl.BlockSpec(memory_space=pltpu.HBM),
        scratch_shapes=[pltpu.SemaphoreType.DMA((NDEV - 1,)), pltpu.SemaphoreType.DMA((NDEV - 1,)),
                        pltpu.SemaphoreType.DMA(())],
    )(x)


def _matmul(a, b, ta, tb, name, out_dtype=F32, tm=512, tn=1024, tk=512):
    M, K = (a.shape[1], a.shape[0]) if ta else a.shape
    N = b.shape[0] if tb else b.shape[1]
    assert (b.shape[1] if tb else b.shape[0]) == K
    tm, tn, tk = _tile(M, tm), _tile(N, tn), _tile(K, tk)
    nk = K // tk

    def body(a_ref, b_ref, o_ref, acc):
        kk = pl.program_id(2)

        @pl.when(kk == 0)
        def _():
            acc[...] = jnp.zeros_like(acc)

        acc[...] += _dg(a_ref[...], b_ref[...], ta, tb, "bf")

        @pl.when(kk == nk - 1)
        def _():
            o_ref[...] = acc[...].astype(o_ref.dtype)

    a_spec = pl.BlockSpec((tk, tm), lambda i, j, k: (k, i)) if ta else pl.BlockSpec((tm, tk), lambda i, j, k: (i, k))
    b_spec = pl.BlockSpec((tn, tk), lambda i, j, k: (j, k)) if tb else pl.BlockSpec((tk, tn), lambda i, j, k: (k, j))
    return _pcall(
        body, name=name, grid=(M // tm, N // tn, nk),
        in_specs=[a_spec, b_spec], out_specs=pl.BlockSpec((tm, tn), lambda i, j, k: (i, j)),
        out_shape=jax.ShapeDtypeStruct((M, N), out_dtype),
        scratch_shapes=[pltpu.VMEM((tm, tn), F32)],
        compiler_params=_cp(("parallel", "parallel", "arbitrary")),
    )(a, b)


def _row_spec(tr, width, col_block=0):
    return pl.BlockSpec((tr, width), lambda i: (i, col_block))


def _full_spec(shape):
    nd = len(shape)
    return pl.BlockSpec(tuple(shape), lambda i: (0,) * nd)


def _rmsnorm_fwd(x, g, name, tr=256):
    R, D = x.shape
    tr = _tile(R, tr)

    def body(x_ref, g_ref, o_ref):
        o_ref[...] = _rms(x_ref[...], g_ref[...]).astype(BF16)

    return _pcall(body, name=name, grid=(R // tr,),
                  in_specs=[_row_spec(tr, D), _full_spec((1, D))], out_specs=_row_spec(tr, D),
                  out_shape=jax.ShapeDtypeStruct((R, D), BF16), compiler_params=_cp(("arbitrary",)))(x, g)


def _rmsnorm_bwd(x, g, dy, extra, name, tr=128):
    R, D = x.shape
    tr = _tile(R, tr)
    has_extra = extra is not None

    def body(*refs):
        if has_extra:
            x_ref, g_ref, dy_ref, e_ref, dx_ref, dg_ref = refs
        else:
            x_ref, g_ref, dy_ref, dx_ref, dg_ref = refs
        _, vjp = jax.vjp(_rms, x_ref[...], g_ref[...])
        dx, dg = vjp(dy_ref[...])
        dx_ref[...] = dx + e_ref[...] if has_extra else dx

        @pl.when(pl.program_id(0) == 0)
        def _():
            dg_ref[...] = jnp.zeros_like(dg_ref)

        dg_ref[...] += dg

    ins = [x, g, dy] + ([extra] if has_extra else [])
    specs = [_row_spec(tr, D), _full_spec((1, D)), _row_spec(tr, D)] + ([_row_spec(tr, D)] if has_extra else [])
    return _pcall(body, name=name, grid=(R // tr,), in_specs=specs,
                  out_specs=[_row_spec(tr, D), _full_spec((1, D))],
                  out_shape=[jax.ShapeDtypeStruct((R, D), F32), jax.ShapeDtypeStruct((1, D), F32)],
                  compiler_params=_cp(("arbitrary",)))(*ins)


def _head_indicator(C, hp):
    e = (jnp.arange(C)[:, None] // HEAD == jnp.arange(hp)[None, :]).astype(F32)
    return e, e.T


def _prep_fn(C, L, ps, prev, mu, w0, a0, k_k, k_a, wdu, wiu, E, ET):
    sh = ps + (prev - ps) * mu
    r, k, v = sh[:, :C], sh[:, C:2 * C], sh[:, 2 * C:3 * C]
    wl, al = sh[:, 3 * C:3 * C + L], sh[:, 3 * C + L:3 * C + 2 * L]
    wd = w0 + _mm(jnp.tanh(wl), wdu, False, False, "bf")
    w_pre = -_softplus(-wd) - 0.5
    lw = -jnp.exp(w_pre)
    alpha = _sigmoid(a0 + _mm(al, wiu, False, False, "bf"))
    kk = k * k_k
    ss = _mm(kk * kk, E, False, False, "hi")
    kk = kk * _mm(lax.rsqrt(jnp.maximum(ss, 1e-24)), ET, False, False, "hi")
    k_mod = k * (1.0 + (alpha - 1.0) * k_a)
    return r, lw, k_mod, v, -kk, kk * alpha


def _prep_fwd(p, prev, params, C, L, tr=128):
    T = p.shape[0]
    SH = 3 * C + 2 * L
    tr = _tile(T, tr)
    hp = params[-1].shape[0]

    def body(ps_ref, prev_ref, mu, w0, a0, kk_, ka_, wdu, wiu, E, ET, *outs):
        vals = _prep_fn(C, L, ps_ref[...], prev_ref[...], mu[...], w0[...], a0[...], kk_[...], ka_[...],
                        wdu[...], wiu[...], E[...], ET[...])
        for o, v in zip(outs, vals):
            o[...] = v

    pspecs = [_full_spec(a.shape) for a in params]
    return _pcall(body, name="rwkv_prep_fwd", grid=(T // tr,),
                  in_specs=[_row_spec(tr, SH), _row_spec(tr, SH)] + pspecs,
                  out_specs=[_row_spec(tr, C)] * 6,
                  out_shape=[jax.ShapeDtypeStruct((T, C), F32)] * 6,
                  compiler_params=_cp(("arbitrary",)))(p, prev, *params)


def _prep_bwd(p, prev, params, cts, C, L, tr=64):
    T = p.shape[0]
    SH = 3 * C + 2 * L
    tr = _tile(T, tr)
    nparam = 7

    def body(ps_ref, prev_ref, mu, w0, a0, kk_, ka_, wdu, wiu, E, ET, c0, c1, c2, c3, c4, c5, e0, e2, e3,
             dps_ref, dprev_ref, *dpar):
        f = functools.partial(_prep_fn, C, L)
        fe = lambda ps, prev, *par: f(ps, prev, *par, E[...], ET[...])
        _, vjp = jax.vjp(fe, ps_ref[...], prev_ref[...], mu[...], w0[...], a0[...], kk_[...], ka_[...], wdu[...], wiu[...])
        grads = vjp((c0[...] + e0[...], c1[...], c2[...] + e2[...], c3[...] + e3[...], c4[...], c5[...]))
        dps_ref[...] = grads[0]
        dprev_ref[...] = grads[1]

        @pl.when(pl.program_id(0) == 0)
        def _():
            for d in dpar:
                d[...] = jnp.zeros_like(d)

        for d, gval in zip(dpar, grads[2:]):
            d[...] += gval

    pspecs = [_full_spec(a.shape) for a in params]
    par_shapes = [a.shape for a in params[:nparam]]
    return _pcall(body, name="rwkv_prep_bwd", grid=(T // tr,),
                  in_specs=[_row_spec(tr, SH), _row_spec(tr, SH)] + pspecs + [_row_spec(tr, C)] * 9,
                  out_specs=[_row_spec(tr, SH), _row_spec(tr, SH)] + [_full_spec(s) for s in par_shapes],
                  out_shape=[jax.ShapeDtypeStruct((T, SH), F32)] * 2 + [jax.ShapeDtypeStruct(s, F32) for s in par_shapes],
                  compiler_params=_cp(("arbitrary",)))(p, prev, *params, *cts)


def _shift_combine(d_direct, d_prev_up, tr=256):
    T, W = d_direct.shape
    tr = _tile(T, tr)

    def body(a_ref, b_ref, o_ref):
        o_ref[...] = (a_ref[...] + b_ref[...]).astype(BF16)

    return _pcall(body, name="shift_combine", grid=(T // tr,),
                  in_specs=[_row_spec(tr, W)] * 2, out_specs=_row_spec(tr, W),
                  out_shape=jax.ShapeDtypeStruct((T, W), BF16), compiler_params=_cp(("arbitrary",)))(d_direct, d_prev_up)


def _mix_fn(y, r, kmod, v, g_rwkv, yfox, g_fox, ymem, g_mq, lnw, lnb, rk, E, ET):
    inv = 1.0 / HEAD
    mean = _mm(y, E, False, False, "hi") * inv
    yc = y - _mm(mean, ET, False, False, "hi")
    var = _mm(yc * yc, E, False, False, "hi") * inv
    yn = yc * _mm(lax.rsqrt(var + GN_EPS), ET, False, False, "hi") * lnw + lnb
    bonus = _mm(_mm(r * kmod * rk, E, False, False, "hi"), ET, False, False, "hi") * v
    o1 = (yn + bonus) * _silu(g_rwkv)
    return jnp.concatenate([o1, yfox * _silu(g_fox), ymem * _silu(g_mq)], axis=1)


def _mix_specs(tr, C, MW):
    return [_row_spec(tr, C)] * 7 + [_row_spec(tr, MW)] * 2


def _mix_fwd(acts, params, C, MW, tr=128):
    T = acts[0].shape[0]
    D = 2 * C + MW
    tr = _tile(T, tr)

    def body(y_, r_, k_, v_, g1, yf, g2, ym, g3, lnw, lnb, rk, E, ET, o_ref):
        o_ref[...] = _mix_fn(y_[...], r_[...], k_[...], v_[...], g1[...], yf[...], g2[...], ym[...], g3[...],
                             lnw[...], lnb[...], rk[...], E[...], ET[...]).astype(BF16)

    return _pcall(body, name="mix_fwd", grid=(T // tr,),
                  in_specs=_mix_specs(tr, C, MW) + [_full_spec(a.shape) for a in params],
                  out_specs=_row_spec(tr, D), out_shape=jax.ShapeDtypeStruct((T, D), BF16),
                  compiler_params=_cp(("arbitrary",)))(*acts, *params)


def _mix_bwd(acts, params, dycat, C, MW, tr=64):
    T = acts[0].shape[0]
    D = 2 * C + MW
    tr = _tile(T, tr)

    def body(y_, r_, k_, v_, g1, yf, g2, ym, g3, lnw, lnb, rk, E, ET, dy_ref, *outs):
        fe = lambda *a: _mix_fn(*a, E[...], ET[...])
        _, vjp = jax.vjp(fe, y_[...], r_[...], k_[...], v_[...], g1[...], yf[...], g2[...], ym[...], g3[...],
                         lnw[...], lnb[...], rk[...])
        grads = vjp(dy_ref[...])
        for o, gval in zip(outs[:9], grads[:9]):
            o[...] = gval

        @pl.when(pl.program_id(0) == 0)
        def _():
            for o in outs[9:]:
                o[...] = jnp.zeros_like(o)

        for o, gval in zip(outs[9:], grads[9:]):
            o[...] += gval

    widths = [C, C, C, C, C, C, C, MW, MW]
    return _pcall(body, name="mix_bwd", grid=(T // tr,),
                  in_specs=_mix_specs(tr, C, MW) + [_full_spec(a.shape) for a in params] + [_row_spec(tr, D)],
                  out_specs=[_row_spec(tr, w) for w in widths] + [_full_spec((1, C))] * 3,
                  out_shape=[jax.ShapeDtypeStruct((T, w), F32) for w in widths] + [jax.ShapeDtypeStruct((1, C), F32)] * 3,
                  compiler_params=_cp(("arbitrary",)))(*acts, *params, dycat)


def _post(yo, x, tgt, g_post, tr=128):
    T, D = x.shape
    tr = _tile(T, tr)

    def body(yo_ref, x_ref, t_ref, g_ref, dyo_ref, dout_ref, loss_ref, dg_ref):
        n, vjp = jax.vjp(_rms, yo_ref[...], g_ref[...])
        diff = (x_ref[...] + n) - t_ref[...]
        part = 0.5 * jnp.sum(jnp.mean(diff * diff, axis=-1, keepdims=True), axis=0, keepdims=True)
        d_out = diff * (1.0 / D)
        dyo, dg = vjp(d_out)
        dyo_ref[...] = dyo.astype(BF16)
        dout_ref[...] = d_out

        @pl.when(pl.program_id(0) == 0)
        def _():
            loss_ref[...] = jnp.zeros_like(loss_ref)
            dg_ref[...] = jnp.zeros_like(dg_ref)

        loss_ref[...] += jnp.broadcast_to(part, loss_ref.shape)
        dg_ref[...] += dg

    return _pcall(body, name="post_loss", grid=(T // tr,),
                  in_specs=[_row_spec(tr, D)] * 3 + [_full_spec((1, D))],
                  out_specs=[_row_spec(tr, D), _row_spec(tr, D), _full_spec((1, LANE)), _full_spec((1, D))],
                  out_shape=[jax.ShapeDtypeStruct((T, D), BF16), jax.ShapeDtypeStruct((T, D), F32),
                             jax.ShapeDtypeStruct((1, LANE), F32), jax.ShapeDtypeStruct((1, D), F32)],
                  compiler_params=_cp(("arbitrary",)))(yo, x, tgt, g_post)


def _memattn_fn(MW, q, mkv):
    hd = MW // MEM_HEADS
    scale = hd ** -0.5
    outs = []
    for h in range(MEM_HEADS):
        qh = q[:, h * hd:(h + 1) * hd]
        kh = mkv[:, h * hd:(h + 1) * hd]
        vh = mkv[:, MW + h * hd:MW + (h + 1) * hd]
        s = _mm(qh, kh, False, True, "bf") * scale
        e = jnp.exp(s - lax.stop_gradient(jnp.max(s, axis=-1, keepdims=True)))
        pr = e / jnp.sum(e, axis=-1, keepdims=True)
        outs.append(_mm(pr, vh, False, False, "bf"))
    return jnp.concatenate(outs, axis=1)


def _memattn_fwd(p, mkv, MW, tr=256):
    T = p.shape[0]
    tr = _tile(T, tr)

    def body(q_ref, kv_ref, o_ref):
        o_ref[...] = _memattn_fn(MW, q_ref[...], kv_ref[...])

    return _pcall(body, name="memattn_fwd", grid=(T // tr,),
                  in_specs=[_row_spec(tr, MW), _full_spec(mkv.shape)], out_specs=_row_spec(tr, MW),
                  out_shape=jax.ShapeDtypeStruct((T, MW), F32), compiler_params=_cp(("arbitrary",)))(p, mkv)


def _memattn_bwd(p, mkv, do, MW, tr=256):
    T = p.shape[0]
    tr = _tile(T, tr)

    def body(q_ref, kv_ref, do_ref, dq_ref, dkv_ref):
        _, vjp = jax.vjp(functools.partial(_memattn_fn, MW), q_ref[...], kv_ref[...])
        dq, dkv = vjp(do_ref[...])
        dq_ref[...] = dq

        @pl.when(pl.program_id(0) == 0)
        def _():
            dkv_ref[...] = jnp.zeros_like(dkv_ref)

        dkv_ref[...] += dkv

    return _pcall(body, name="memattn_bwd", grid=(T // tr,),
                  in_specs=[_row_spec(tr, MW), _full_spec(mkv.shape), _row_spec(tr, MW)],
                  out_specs=[_row_spec(tr, MW), _full_spec(mkv.shape)],
                  out_shape=[jax.ShapeDtypeStruct((T, MW), F32), jax.ShapeDtypeStruct(mkv.shape, F32)],
                  compiler_params=_cp(("arbitrary",)))(p, mkv, do)


def _fox_cum_fwd(p, b_f_pad, off, blk=512):
    T = p.shape[0]
    blk = _tile(T, blk)

    def body(f_ref, b_ref, cum_ref, cumt_ref, carry):
        @pl.when(pl.program_id(0) == 0)
        def _():
            carry[...] = jnp.zeros_like(carry)

        z = f_ref[...] + b_ref[...]
        logf = -_softplus(-z)
        row = lax.broadcasted_iota(jnp.int32, (blk, blk), 0)
        col = lax.broadcasted_iota(jnp.int32, (blk, blk), 1)
        tri = (col <= row).astype(F32)
        c = _dg(tri, logf, False, False, "hi") + carry[...]
        cum_ref[...] = c
        cumt_ref[...] = c.T
        carry[...] += jnp.sum(logf, axis=0, keepdims=True)

    return _pcall(body, name="fox_cum_fwd", grid=(T // blk,),
                  in_specs=[_row_spec(blk, LANE, off // LANE), _full_spec((1, LANE))],
                  out_specs=[_row_spec(blk, LANE), pl.BlockSpec((LANE, blk), lambda i: (0, i))],
                  out_shape=[jax.ShapeDtypeStruct((T, LANE), F32), jax.ShapeDtypeStruct((LANE, T), F32)],
                  scratch_shapes=[pltpu.VMEM((1, LANE), F32)], compiler_params=_cp(("arbitrary",)))(p, b_f_pad)


def _fox_cum_bwd(p, b_f_pad, dcum, off, blk=512):
    T = p.shape[0]
    blk = _tile(T, blk)
    nb = T // blk

    def body(f_ref, b_ref, dc_ref, df_ref, db_ref, carry):
        @pl.when(pl.program_id(0) == 0)
        def _():
            carry[...] = jnp.zeros_like(carry)
            db_ref[...] = jnp.zeros_like(db_ref)

        row = lax.broadcasted_iota(jnp.int32, (blk, blk), 0)
        col = lax.broadcasted_iota(jnp.int32, (blk, blk), 1)
        tri = (col >= row).astype(F32)
        dlogf = _dg(tri, dc_ref[...], False, False, "hi") + carry[...]
        carry[...] += jnp.sum(dc_ref[...], axis=0, keepdims=True)
        z = f_ref[...] + b_ref[...]
        dz = dlogf * (1.0 - _sigmoid(z))
        df_ref[...] = dz
        db_ref[...] += jnp.sum(dz, axis=0, keepdims=True)

    rev = lambda i: (nb - 1 - i, 0)
    return _pcall(body, name="fox_cum_bwd", grid=(nb,),
                  in_specs=[pl.BlockSpec((blk, LANE), lambda i: (nb - 1 - i, off // LANE)), _full_spec((1, LANE)),
                            pl.BlockSpec((blk, LANE), rev)],
                  out_specs=[pl.BlockSpec((blk, LANE), rev), _full_spec((1, LANE))],
                  out_shape=[jax.ShapeDtypeStruct((T, LANE), F32), jax.ShapeDtypeStruct((1, LANE), F32)],
                  scratch_shapes=[pltpu.VMEM((1, LANE), F32)], compiler_params=_cp(("arbitrary",)))(p, b_f_pad, dcum)


def _fox_masked_scores(q, k, cq, ck, q0, k0, tq, tk):
    s = _dg(q, k, False, True, "bf") * (HEAD ** -0.5) + cq - ck
    qpos = q0 + lax.broadcasted_iota(jnp.int32, (tq, tk), 0)
    kpos = k0 + lax.broadcasted_iota(jnp.int32, (tq, tk), 1)
    return jnp.where(kpos <= qpos, s, NEG)


def _fox_fwd(p, cq, ck, C, offs, tq=256):
    T = p.shape[0]
    tq = _tile(T, tq)
    npair = C // LANE
    cb = lambda name: offs[name] // LANE

    def body(q_ref, k_ref, v_ref, cq_ref, ck_ref, o_ref, lse_ref):
        i = pl.program_id(1)
        lse_ref[...] = jnp.zeros_like(lse_ref)
        for hh in range(2):
            sl = slice(hh * HEAD, (hh + 1) * HEAD)
            q = q_ref[:, sl].astype(BF16)
            cqv = cq_ref[0, :, hh:hh + 1]

            def step(j, carry):
                m, l, acc = carry
                off = pl.multiple_of(j * tq, tq)
                k = k_ref[pl.ds(off, tq), sl].astype(BF16)
                v = v_ref[pl.ds(off, tq), sl].astype(BF16)
                ckv = ck_ref[0, hh:hh + 1, pl.ds(off, tq)]
                s = _fox_masked_scores(q, k, cqv, ckv, i * tq, off, tq, tq)
                m_new = jnp.maximum(m, jnp.max(s, axis=-1, keepdims=True))
                pr = jnp.exp(s - m_new)
                al = jnp.exp(m - m_new)
                l = al * l + jnp.sum(pr, axis=-1, keepdims=True)
                acc = al * acc + _dg(pr, v, False, False, "bf")
                return m_new, l, acc

            m0 = jnp.full((tq, 1), NEG, F32)
            m, l, acc = lax.fori_loop(0, i + 1, step, (m0, jnp.zeros((tq, 1), F32), jnp.zeros((tq, HEAD), F32)))
            o_ref[:, sl] = acc / l
            lse_ref[0, :, hh:hh + 1] = m + jnp.log(l)

    return _pcall(body, name="fox_fwd", grid=(npair, T // tq),
                  in_specs=[pl.BlockSpec((tq, LANE), lambda h, i: (i, cb("fq") + h)),
                            pl.BlockSpec((T, LANE), lambda h, i: (0, cb("fk") + h)),
                            pl.BlockSpec((T, LANE), lambda h, i: (0, cb("fv") + h)),
                            pl.BlockSpec((1, tq, 8), lambda h, i: (h, i, 0)),
                            pl.BlockSpec((1, 8, T), lambda h, i: (h, 0, 0))],
                  out_specs=[pl.BlockSpec((tq, LANE), lambda h, i: (i, h)),
                             pl.BlockSpec((1, tq, 8), lambda h, i: (h, i, 0))],
                  out_shape=[jax.ShapeDtypeStruct((T, C), F32), jax.ShapeDtypeStruct((npair, T, 8), F32)],
                  compiler_params=_cp(("arbitrary", "arbitrary")))(p, p, p, cq, ck)


def _fox_delta(p, cq, ck, do, lse, C, offs, tq=256):
    T = p.shape[0]
    tq = _tile(T, tq)
    npair = C // LANE
    cb = lambda name: offs[name] // LANE

    def body(q_ref, k_ref, v_ref, cq_ref, ck_ref, do_ref, lse_ref, d_ref):
        i = pl.program_id(1)
        d_ref[...] = jnp.zeros_like(d_ref)
        for hh in range(2):
            sl = slice(hh * HEAD, (hh + 1) * HEAD)
            q = q_ref[:, sl].astype(BF16)
            dov = do_ref[:, sl].astype(BF16)
            cqv = cq_ref[0, :, hh:hh + 1]
            lsev = lse_ref[0, :, hh:hh + 1]

            def step(j, acc):
                off = pl.multiple_of(j * tq, tq)
                k = k_ref[pl.ds(off, tq), sl].astype(BF16)
                v = v_ref[pl.ds(off, tq), sl].astype(BF16)
                ckv = ck_ref[0, hh:hh + 1, pl.ds(off, tq)]
                s = _fox_masked_scores(q, k, cqv, ckv, i * tq, off, tq, tq)
                pr = jnp.exp(s - lsev)
                dp = _dg(dov, v, False, True, "bf")
                return acc + jnp.sum(pr * dp, axis=-1, keepdims=True)

            d_ref[0, :, hh:hh + 1] = lax.fori_loop(0, i + 1, step, jnp.zeros((tq, 1), F32))

    return _pcall(body, name="fox_delta", grid=(npair, T // tq),
                  in_specs=[pl.BlockSpec((tq, LANE), lambda h, i: (i, cb("fq") + h)),
                            pl.BlockSpec((T, LANE), lambda h, i: (0, cb("fk") + h)),
                            pl.BlockSpec((T, LANE), lambda h, i: (0, cb("fv") + h)),
                            pl.BlockSpec((1, tq, 8), lambda h, i: (h, i, 0)),
                            pl.BlockSpec((1, 8, T), lambda h, i: (h, 0, 0)),
                            pl.BlockSpec((tq, LANE), lambda h, i: (i, h)),
                            pl.BlockSpec((1, tq, 8), lambda h, i: (h, i, 0))],
                  out_specs=pl.BlockSpec((1, tq, 8), lambda h, i: (h, i, 0)),
                  out_shape=jax.ShapeDtypeStruct((npair, T, 8), F32),
                  compiler_params=_cp(("arbitrary", "arbitrary")))(p, p, p, cq, ck, do, lse)


def _fox_bwd(p, cq, ck, delta, do, lse, C, offs, tk=256):
    T = p.shape[0]
    tk = _tile(T, tk)
    nblk = T // tk
    npair = C // LANE
    cb = lambda name: offs[name] // LANE
    scale = HEAD ** -0.5

    def body(q_ref, k_ref, v_ref, cq_ref, ck_ref, dl_ref, do_ref, lse_ref, dq_ref, dk_ref, dv_ref, dck_ref):
        j = pl.program_id(1)

        @pl.when(j == 0)
        def _():
            dq_ref[...] = jnp.zeros_like(dq_ref)

        dck_ref[...] = jnp.zeros_like(dck_ref)
        for hh in range(2):
            sl = slice(hh * HEAD, (hh + 1) * HEAD)
            k = k_ref[:, sl].astype(BF16)
            v = v_ref[:, sl].astype(BF16)
            ckv = ck_ref[0, hh:hh + 1, :]

            def step(i, carry):
                dk, dv, dck = carry
                off = pl.multiple_of(i * tk, tk)
                q = q_ref[pl.ds(off, tk), sl].astype(BF16)
                dov = do_ref[pl.ds(off, tk), sl]
                cqv = cq_ref[0, pl.ds(off, tk), hh:hh + 1]
                lsev = lse_ref[0, pl.ds(off, tk), hh:hh + 1]
                s = _fox_masked_scores(q, k, cqv, ckv, off, j * tk, tk, tk)
                pr = jnp.exp(s - lsev)
                dv = dv + _dg(pr, dov, True, False, "bf")
                dp = _dg(dov, v, False, True, "bf")
                ds = pr * (dp - dl_ref[0, pl.ds(off, tk), hh:hh + 1])
                dk = dk + _dg(ds, q, True, False, "bf") * scale
                dq_ref[pl.ds(off, tk), sl] += _dg(ds, k, False, False, "bf") * scale
                dck = dck - jnp.sum(ds, axis=0, keepdims=True)
                return dk, dv, dck

            z = jnp.zeros((tk, HEAD), F32)
            dk, dv, dck = lax.fori_loop(j, nblk, step, (z, z, jnp.zeros((1, tk), F32)))
            dk_ref[:, sl] = dk
            dv_ref[:, sl] = dv
            dck_ref[0, hh:hh + 1, :] = dck

    full = lambda h, j: (0, h)
    return _pcall(body, name="fox_bwd", grid=(npair, nblk),
                  in_specs=[pl.BlockSpec((T, LANE), lambda h, j: (0, cb("fq") + h)),
                            pl.BlockSpec((tk, LANE), lambda h, j: (j, cb("fk") + h)),
                            pl.BlockSpec((tk, LANE), lambda h, j: (j, cb("fv") + h)),
                            pl.BlockSpec((1, T, 8), lambda h, j: (h, 0, 0)),
                            pl.BlockSpec((1, 8, tk), lambda h, j: (h, 0, j)),
                            pl.BlockSpec((1, T, 8), lambda h, j: (h, 0, 0)), pl.BlockSpec((T, LANE), full),
                            pl.BlockSpec((1, T, 8), lambda h, j: (h, 0, 0))],
                  out_specs=[pl.BlockSpec((T, LANE), full),
                             pl.BlockSpec((tk, LANE), lambda h, j: (j, h)),
                             pl.BlockSpec((tk, LANE), lambda h, j: (j, h)),
                             pl.BlockSpec((1, 8, tk), lambda h, j: (h, 0, j))],
                  out_shape=[jax.ShapeDtypeStruct((T, C), F32)] * 3 + [jax.ShapeDtypeStruct((npair, 8, T), F32)],
                  compiler_params=_cp(("arbitrary", "arbitrary")))(p, p, p, cq, ck, delta, do, lse)


def _chunk_fn(S0, r, lw, k, v, a, b):
    n = r.shape[0]
    row = lax.broadcasted_iota(jnp.int32, (n, n), 0)
    col = lax.broadcasted_iota(jnp.int32, (n, n), 1)
    incl, strict = col <= row, col < row
    mm = lambda x, y, ta=False, tb=False: _mm(x, y, ta, tb, "hi")
    g = mm(incl.astype(F32), lw)
    einv = jnp.exp(-g)
    rt, at, bt, kt = r * jnp.exp(g), a * jnp.exp(g - lw), b * einv, k * einv
    a_ab = jnp.where(strict, mm(at, bt, tb=True), 0.0)
    a_ak = jnp.where(strict, mm(at, kt, tb=True), 0.0)
    r_b = jnp.where(incl, mm(rt, bt, tb=True), 0.0)
    r_k = jnp.where(incl, mm(rt, kt, tb=True), 0.0)
    u = mm(at, S0, tb=True) + mm(a_ak, v)
    pw = a_ab
    for it in range(6):
        u = u + mm(pw, u)
        if it < 5:
            pw = mm(pw, pw)
    y = mm(rt, S0, tb=True) + mm(r_b, u) + mm(r_k, v)
    g_end = jnp.sum(lw, axis=0, keepdims=True)
    s_end = (S0 + mm(u, bt, ta=True) + mm(v, kt, ta=True)) * jnp.exp(g_end)
    return y, s_end


def _scan_fwd(r, lw, k, v, a, b, tc=256):
    T, C = r.shape
    tc = _tile(T, tc)
    ncs = tc // CHUNK
    npair = C // LANE

    def body(r_ref, lw_ref, k_ref, v_ref, a_ref, b_ref, y_ref, ck_ref, state):
        @pl.when(pl.program_id(1) == 0)
        def _():
            state[...] = jnp.zeros_like(state)

        for c in range(ncs):
            rows = slice(c * CHUNK, (c + 1) * CHUNK)
            ck_ref[0, c] = state[...]
            for hh in range(2):
                sl = slice(hh * HEAD, (hh + 1) * HEAD)
                y, s_end = _chunk_fn(state[:, sl], r_ref[rows, sl], lw_ref[rows, sl], k_ref[rows, sl],
                                     v_ref[rows, sl], a_ref[rows, sl], b_ref[rows, sl])
                y_ref[rows, sl] = y
                state[:, sl] = s_end

    spec = pl.BlockSpec((tc, LANE), lambda h, t: (t, h))
    return _pcall(body, name="rwkv_scan_fwd", grid=(npair, T // tc),
                  in_specs=[spec] * 6,
                  out_specs=[spec, pl.BlockSpec((1, ncs, HEAD, LANE), lambda h, t: (h, t, 0, 0))],
                  out_shape=[jax.ShapeDtypeStruct((T, C), F32),
                             jax.ShapeDtypeStruct((npair, T // CHUNK, HEAD, LANE), F32)],
                  scratch_shapes=[pltpu.VMEM((HEAD, LANE), F32)],
                  compiler_params=_cp(("arbitrary", "arbitrary")))(r, lw, k, v, a, b)


def _scan_bwd(r, lw, k, v, a, b, ckpt, dy, tc=256):
    T, C = r.shape
    tc = _tile(T, tc)
    ncs = tc // CHUNK
    npair = C // LANE
    nt = T // tc

    def body(r_ref, lw_ref, k_ref, v_ref, a_ref, b_ref, ck_ref, dy_ref, dr, dlw, dk, dv, da, db, dstate):
        @pl.when(pl.program_id(1) == 0)
        def _():
            dstate[...] = jnp.zeros_like(dstate)

        outs = (dr, dlw, dk, dv, da, db)
        for c in reversed(range(ncs)):
            rows = slice(c * CHUNK, (c + 1) * CHUNK)
            for hh in range(2):
                sl = slice(hh * HEAD, (hh + 1) * HEAD)
                _, vjp = jax.vjp(_chunk_fn, ck_ref[0, c, :, sl], r_ref[rows, sl], lw_ref[rows, sl], k_ref[rows, sl],
                                 v_ref[rows, sl], a_ref[rows, sl], b_ref[rows, sl])
                grads = vjp((dy_ref[rows, sl], dstate[:, sl]))
                dstate[:, sl] = grads[0]
                for o, gval in zip(outs, grads[1:]):
                    o[rows, sl] = gval

    spec = pl.BlockSpec((tc, LANE), lambda h, t: (nt - 1 - t, h))
    return _pcall(body, name="rwkv_scan_bwd", grid=(npair, nt),
                  in_specs=[spec] * 6 + [pl.BlockSpec((1, ncs, HEAD, LANE), lambda h, t: (h, nt - 1 - t, 0, 0)), spec],
                  out_specs=[spec] * 6, out_shape=[jax.ShapeDtypeStruct((T, C), F32)] * 6,
                  scratch_shapes=[pltpu.VMEM((HEAD, LANE), F32)],
                  compiler_params=_cp(("arbitrary", "arbitrary")))(r, lw, k, v, a, b, ckpt, dy)


def _adam(w, g, m, v):
    m = ADAM_B1 * m + (1.0 - ADAM_B1) * g
    v = ADAM_B2 * v + (1.0 - ADAM_B2) * (g * g)
    m_hat = m / (1.0 - ADAM_B1 ** ADAM_STEP)
    v_hat = v / (1.0 - ADAM_B2 ** ADAM_STEP)
    return -ADAM_LR * (m_hat / (jnp.sqrt(v_hat) + ADAM_EPS) + ADAM_WD * w), m, v


def _sum_adam(parts, w, m, v, name):
    _, R, W = parts.shape
    tr = _tile(R, max(8, min(256, (1 << 17) // W)))

    def body(p_ref, w_ref, m_ref, v_ref, g_ref, d_ref, nm_ref, nv_ref):
        g = p_ref[0]
        for s in range(1, NDEV):
            g = g + p_ref[s]
        d, nm, nv = _adam(w_ref[...], g, m_ref[...], v_ref[...])
        g_ref[...] = g
        d_ref[...] = d
        nm_ref[...] = nm
        nv_ref[...] = nv

    return _pcall(body, name=name, grid=(R // tr,),
                  in_specs=[pl.BlockSpec((NDEV, tr, W), lambda i: (0, i, 0))] + [_row_spec(tr, W)] * 3,
                  out_specs=[_row_spec(tr, W)] * 4, out_shape=[jax.ShapeDtypeStruct((R, W), F32)] * 4,
                  compiler_params=_cp(("arbitrary",)))(parts, w, m, v)


def _pad_lanes(vec, width):
    return jnp.pad(vec, ((0, 0), (0, width - vec.shape[1])))


def kernel(x, mem, g_pre, w_in, mu_rwkv, w0, w_decay_up, a0, w_iclr_up, k_k, k_a, r_k, ln_x_w, ln_x_b, b_f, g_mem, w_mem_kv, w_out, g_post, loss_target, m_g_pre, m_w_in, m_mu_rwkv, m_w0, m_w_decay_up, m_a0, m_w_iclr_up, m_k_k, m_k_a, m_r_k, m_ln_x_w, m_ln_x_b, m_b_f, m_g_mem, m_w_mem_kv, m_w_out, m_g_post, v_g_pre, v_w_in, v_mu_rwkv, v_w0, v_w_decay_up, v_a0, v_w_iclr_up, v_k_k, v_k_a, v_r_k, v_ln_x_w, v_ln_x_b, v_b_f, v_g_mem, v_w_mem_kv, v_w_out, v_g_post):
    T, D = x.shape[1], x.shape[2]
    C = w0.shape[1]
    L = w_decay_up.shape[1]
    H = C // HEAD
    MW = w_mem_kv.shape[2] // 2
    SH = 3 * C + 2 * L
    IN = NDEV * w_in.shape[2]
    assert IN == SH + 5 * C + H + 2 * MW and D == 2 * C + MW and H % 2 == 0 and H <= LANE
    assert C % LANE == 0 and L % LANE == 0 and MW % (MEM_HEADS * HEAD) == 0 and T % CHUNK == 0
    offs = dict(grw=SH, fq=SH + C, fk=SH + 2 * C, fv=SH + 3 * C, gfx=SH + 4 * C, mq=SH + 5 * C, gmq=SH + 5 * C + MW,
                fl=SH + 5 * C + 2 * MW)
    NI = -(-(offs["fl"] + LANE) // 512) * 512
    l_fl = SH + 4 * C

    x2, mem2, tgt2 = x[0], mem[0], loss_target[0]

    wg = _exchange(w_in[0].astype(BF16), "gather_w_in", True)
    w_full = jnp.transpose(wg, (1, 0, 2)).reshape(D, IN)
    w_perm = jnp.concatenate([w_full[:, :l_fl], w_full[:, l_fl + H:], w_full[:, l_fl:l_fl + H],
                              jnp.zeros((D, NI - IN), BF16)], axis=1)
    w_out_f = _exchange(w_out[0].astype(BF16), "gather_w_out", True).reshape(D, D)
    w_kv_f = _exchange(w_mem_kv[0].astype(BF16), "gather_w_mem_kv", True).reshape(D, 2 * MW)
    lora = _exchange(jnp.concatenate([w_decay_up[0], w_iclr_up[0]], axis=0), "gather_lora", True)
    lora = jnp.transpose(lora, (1, 0, 2)).reshape(2 * L, C)
    wdu_f, wiu_f = lora[:L], lora[L:]

    E, ET = _head_indicator(C, LANE)
    prep_params = [mu_rwkv, w0, a0, k_k, k_a, wdu_f, wiu_f, E, ET]
    mix_params = [ln_x_w, ln_x_b, r_k.reshape(1, C), E, ET]
    b_f_pad = _pad_lanes(b_f, LANE)

    h = _rmsnorm_fwd(x2, g_pre, "rmsnorm_pre")
    p = _matmul(h, w_perm, False, False, "in_proj")
    ps = p[:, :SH]
    col = lambda name, w: p[:, offs[name]:offs[name] + w]
    g_rwkv, g_fox, mq, g_mq = col("grw", C), col("gfx", C), col("mq", MW), col("gmq", MW)
    prev = jnp.concatenate([jnp.zeros((1, SH), F32), ps[:-1]], axis=0)
    r, lw, kmod, v, a, b = _prep_fwd(p, prev, prep_params, C, L)
    y_scan, ckpt = _scan_fwd(r, lw, kmod, v, a, b)

    cum, cum_t = _fox_cum_fwd(p, b_f_pad, offs["fl"])
    ck = jnp.pad(cum_t[:H].reshape(H // 2, 2, T), ((0, 0), (0, 6), (0, 0)))
    cq = jnp.pad(cum[:, :H].reshape(T, H // 2, 2).transpose(1, 0, 2), ((0, 0), (0, 0), (0, 6)))
    y_fox, lse = _fox_fwd(p, cq, ck, C, offs)

    memn = _rmsnorm_fwd(mem2, g_mem, "rmsnorm_mem")
    mkv = _matmul(memn, w_kv_f, False, False, "mem_kv_proj")
    y_mem = _memattn_fwd(mq, mkv, MW)

    acts = [y_scan, r, kmod, v, g_rwkv, y_fox, g_fox, y_mem, g_mq]
    ycat = _mix_fwd(acts, mix_params, C, MW)
    yo = _matmul(ycat, w_out_f, False, False, "out_proj")
    d_yo, d_out, loss_part, dg_post = _post(yo, x2, tgt2, g_post)
    loss = lax.psum(loss_part[0, 0], AXES)

    g_w_out = _matmul(ycat, d_yo, True, False, "grad_w_out")
    d_ycat = _matmul(d_yo, w_out_f, False, True, "d_ycat")
    (d_y, d_r1, d_k1, d_v1, d_grw, d_yfox, d_gfx, d_ymem, d_gmq, dg_lnw, dg_lnb, dg_rk) = _mix_bwd(
        acts, mix_params, d_ycat, C, MW)

    d_mq, d_mkv = _memattn_bwd(mq, mkv, d_ymem, MW)
    g_w_kv = _matmul(memn, d_mkv, True, False, "grad_w_mem_kv")
    d_memn = _matmul(d_mkv, w_kv_f, False, True, "d_memn")
    _, dg_mem = _rmsnorm_bwd(mem2, g_mem, d_memn, None, "rmsnorm_mem_bwd")

    delta = _fox_delta(p, cq, ck, d_yfox, lse, C, offs)
    d_fq, d_fk, d_fv, d_ck = _fox_bwd(p, cq, ck, delta, d_yfox, lse, C, offs)
    d_cum = _pad_lanes(d_ck[:, :2, :].reshape(H, T).T, LANE)
    d_fl, dg_bf = _fox_cum_bwd(p, b_f_pad, d_cum, offs["fl"])

    d_r, d_lw, d_k, d_v, d_a, d_b = _scan_bwd(r, lw, kmod, v, a, b, ckpt, d_y)
    cts = [d_r, d_lw, d_k, d_v, d_a, d_b, d_r1, d_k1, d_v1]
    (d_ps, d_prev, dg_mu, dg_w0, dg_a0, dg_kk, dg_ka, dg_wdu, dg_wiu) = _prep_bwd(p, prev, prep_params, cts, C, L)
    d_prev_up = jnp.concatenate([d_prev[1:], jnp.zeros((1, SH), F32)], axis=0)
    d_sh = _shift_combine(d_ps, d_prev_up)

    tobf = lambda z: z.astype(BF16)
    dp = jnp.concatenate([d_sh, tobf(d_grw), tobf(d_fq), tobf(d_fk), tobf(d_fv), tobf(d_gfx), tobf(d_mq), tobf(d_gmq),
                          tobf(d_fl), jnp.zeros((T, NI - offs["fl"] - LANE), BF16)], axis=1)
    g_w_perm = _matmul(h, dp, True, False, "grad_w_in")
    d_h = _matmul(dp, w_perm, False, True, "d_h")
    grad_x, dg_pre = _rmsnorm_bwd(x2, g_pre, d_h, d_out, "rmsnorm_pre_bwd")

    g_w_log = jnp.concatenate([g_w_perm[:, :l_fl], g_w_perm[:, offs["fl"]:offs["fl"] + H],
                               g_w_perm[:, l_fl:offs["fl"]]], axis=1)
    parts_in = _exchange(jnp.transpose(g_w_log.reshape(D, NDEV, IN // NDEV), (1, 0, 2)), "scatter_grad_w_in", False)
    parts_out = _exchange(g_w_out.reshape(NDEV, D // NDEV, D), "scatter_grad_w_out", False)
    parts_kv = _exchange(g_w_kv.reshape(NDEV, D // NDEV, 2 * MW), "scatter_grad_w_mem_kv", False)
    g_lora = jnp.concatenate([dg_wdu, dg_wiu], axis=0)
    parts_lora = _exchange(jnp.transpose(g_lora.reshape(2 * L, NDEV, C // NDEV), (1, 0, 2)), "scatter_grad_lora", False)

    gw_in, dw_in, nm_w_in, nv_w_in = _sum_adam(parts_in, w_in[0], m_w_in[0], v_w_in[0], "adam_w_in")
    gw_out, dw_out, nm_w_out, nv_w_out = _sum_adam(parts_out, w_out[0], m_w_out[0], v_w_out[0], "adam_w_out")
    gw_kv, dw_kv, nm_w_kv, nv_w_kv = _sum_adam(parts_kv, w_mem_kv[0], m_w_mem_kv[0], v_w_mem_kv[0], "adam_w_mem_kv")
    cat2 = lambda u, w_: jnp.concatenate([u[0], w_[0]], axis=0)
    lora_res = _sum_adam(parts_lora, cat2(w_decay_up, w_iclr_up), cat2(m_w_decay_up, m_w_iclr_up),
                         cat2(v_w_decay_up, v_w_iclr_up), "adam_lora")

    small = [("g_pre", g_pre, m_g_pre, v_g_pre, dg_pre), ("mu_rwkv", mu_rwkv, m_mu_rwkv, v_mu_rwkv, dg_mu),
             ("w0", w0, m_w0, v_w0, dg_w0), ("a0", a0, m_a0, v_a0, dg_a0), ("k_k", k_k, m_k_k, v_k_k, dg_kk),
             ("k_a", k_a, m_k_a, v_k_a, dg_ka), ("r_k", r_k.reshape(1, C), m_r_k.reshape(1, C), v_r_k.reshape(1, C), dg_rk),
             ("ln_x_w", ln_x_w, m_ln_x_w, v_ln_x_w, dg_lnw), ("ln_x_b", ln_x_b, m_ln_x_b, v_ln_x_b, dg_lnb),
             ("b_f", _pad_lanes(b_f, LANE), _pad_lanes(m_b_f, LANE), _pad_lanes(v_b_f, LANE), dg_bf),
             ("g_mem", g_mem, m_g_mem, v_g_mem, dg_mem), ("g_post", g_post, m_g_post, v_g_post, dg_post)]
    widths = [s[1].shape[1] for s in small]
    pack = lambda idx: jnp.concatenate([s[idx] for s in small], axis=1).reshape(-1, LANE)
    parts_small = _exchange(pack(4), "gather_small_grads", True)
    res_small = _sum_adam(parts_small, pack(1), pack(2), pack(3), "adam_small")

    def unpack(flat):
        flat = flat.reshape(1, -1)
        out, o = {}, 0
        for (name, *_), wd in zip(small, widths):
            out[name] = flat[:, o:o + wd]
            o += wd
        out["b_f"] = out["b_f"][:, :H]
        out["r_k"] = out["r_k"].reshape(1, H, HEAD)
        return out

    sg, sd, sm, sv = [unpack(z) for z in res_small]
    big = {"w_in": (gw_in, dw_in, nm_w_in, nv_w_in), "w_out": (gw_out, dw_out, nm_w_out, nv_w_out),
           "w_mem_kv": (gw_kv, dw_kv, nm_w_kv, nv_w_kv),
           "w_decay_up": tuple(z[:L] for z in lora_res), "w_iclr_up": tuple(z[L:] for z in lora_res)}
    order = ["g_pre", "w_in", "mu_rwkv", "w0", "w_decay_up", "a0", "w_iclr_up", "k_k", "k_a", "r_k", "ln_x_w", "ln_x_b",
             "b_f", "g_mem", "w_mem_kv", "w_out", "g_post"]

    def pick(name, idx):
        if name in big:
            return big[name][idx][None]
        return (sg, sd, sm, sv)[idx][name]

    outs = [loss, grad_x[None]]
    for idx in range(4):
        outs += [pick(n, idx) for n in order]
    return tuple(outs)
```

```python
import functools

import jax
import jax.numpy as jnp
from jax import lax
from jax.experimental import pallas as pl
from jax.experimental.pallas import tpu as pltpu

F32, BF16 = jnp.float32, jnp.bfloat16
HI = lax.Precision.HIGHEST
NDEV = 8
AXES = ("x", "y", "c")
HEAD = 64
CHUNK = 64
MEM_HEADS = 4
LANE = 128
RMS_EPS = 1e-6
GN_EPS = 64e-5
NEG = -1e30
ADAM_LR, ADAM_B1, ADAM_B2, ADAM_EPS, ADAM_WD, ADAM_STEP = 0.001, 0.9, 0.999, 1e-08, 0.01, 10
VMEM_LIMIT = 56 * 1024 * 1024


def _pcall(body, **kw):
    return pl.pallas_call(body, **kw)


def _cp(sem=None, vmem=VMEM_LIMIT):
    return pltpu.CompilerParams(dimension_semantics=sem, vmem_limit_bytes=vmem)


def _tile(n, pref):
    for t in (pref, 1024, 512, 256, 128, 64, 32, 16, 8):
        if t <= pref and n % t == 0:
            return t
    return n


def _dg(a, b, ta, tb, mode):
    ca = 0 if ta else 1
    cb = 1 if tb else 0
    if mode == "bf":
        a, b, prec = a.astype(BF16), b.astype(BF16), None
    else:
        prec = HI
    return lax.dot_general(a, b, (((ca,), (cb,)), ((), ())), preferred_element_type=F32, precision=prec)


@functools.partial(jax.custom_vjp, nondiff_argnums=(2, 3, 4))
def _mm(a, b, ta, tb, mode):
    return _dg(a, b, ta, tb, mode)


def _mm_fwd(a, b, ta, tb, mode):
    return _dg(a, b, ta, tb, mode), (a, b)


def _mm_bwd(ta, tb, mode, res, g):
    a, b = res
    da = _dg(g, b, False, not tb, mode) if not ta else _dg(b, g, tb, True, mode)
    db = _dg(a, g, not ta, False, mode) if not tb else _dg(g, a, True, ta, mode)
    return da, db


_mm.defvjp(_mm_fwd, _mm_bwd)


def _sigmoid(z):
    return 1.0 / (1.0 + jnp.exp(-z))


def _softplus(z):
    return jnp.maximum(z, 0.0) + jnp.log(1.0 + jnp.exp(-jnp.abs(z)))


def _silu(z):
    return z * _sigmoid(z)


def _rms(x, g):
    return x * lax.rsqrt(jnp.mean(x * x, axis=-1, keepdims=True) + RMS_EPS) * g


GROUPS = {"all": (1, 1, 1), "chips": (1, 1, 0), "cores": (0, 0, 1)}


def _exchange(x, name, gather, group="all"):
    flags = GROUPS[group]
    n = 2 ** sum(flags)
    blk = x.shape if gather else x.shape[1:]
    assert gather or x.shape[0] == n
    flips = [k for k in range(1, NDEV) if all(f or not (k >> s) & 1 for f, s in zip(flags, (2, 1, 0)))]

    def rank(coords):
        r = 0
        for f, cc in zip(flags, coords):
            if f:
                r = 2 * r + cc
        return r

    def body(x_ref, o_ref, send_sems, recv_sems, local_sem):
        ix, iy, ic = lax.axis_index("x"), lax.axis_index("y"), lax.axis_index("c")
        me = rank((ix, iy, ic))

        def src(dest):
            return x_ref if gather else x_ref.at[dest]

        mine = pltpu.make_async_copy(src(me), o_ref.at[me], local_sem)
        mine.start()
        copies = []
        for s, k in enumerate(flips):
            px = 1 - ix if (k >> 2) & 1 else ix
            py = 1 - iy if (k >> 1) & 1 else iy
            pc = 1 - ic if k & 1 else ic
            peer = rank((px, py, pc))
            cp = pltpu.make_async_remote_copy(
                src_ref=src(peer), dst_ref=o_ref.at[me], send_sem=send_sems.at[s], recv_sem=recv_sems.at[s],
                device_id=(px, py, pc), device_id_type=pl.DeviceIdType.MESH)
            cp.start()
            copies.append((cp, peer))
        for s, (cp, peer) in enumerate(copies):
            cp.wait_send()
            pltpu.make_async_remote_copy(
                src_ref=src(peer), dst_ref=o_ref.at[peer], send_sem=send_sems.at[s], recv_sem=recv_sems.at[s],
                device_id=(ix, iy, ic), device_id_type=pl.DeviceIdType.MESH).wait_recv()
        mine.wait()

    return _pcall(
        body, name=name,
        out_shape=jax.ShapeDtypeStruct((n,) + tuple(blk), x.dtype),
        in_specs=[pl.BlockSpec(memory_space=pltpu.HBM)],
        out_specs=pl.BlockSpec(memory_space=pltpu.HBM),
        scratch_shapes=[pltpu.SemaphoreType.DMA((n - 1,)), pltpu.SemaphoreType.DMA((n - 1,)),
                        pltpu.SemaphoreType.DMA(())],
    )(x)


def _gather_two_level(x, name):
    chips = _exchange(x, name + "_chips", True, "chips")
    both = _exchange(chips, name + "_cores", True, "cores")
    return jnp.swapaxes(both, 0, 1).reshape((NDEV,) + x.shape)


def _pair_add(parts, name):
    _, n, R, W = parts.shape
    tr = _tile(R, max(8, min(256, (1 << 18) // W)))

    def body(p_ref, o_ref):
        o_ref[...] = (p_ref[0] + p_ref[1]).astype(BF16)

    return _pcall(body, name=name, grid=(n, R // tr),
                  in_specs=[pl.BlockSpec((2, 1, tr, W), lambda c, i: (0, c, i, 0))],
                  out_specs=pl.BlockSpec((1, tr, W), lambda c, i: (c, i, 0)),
                  out_shape=jax.ShapeDtypeStruct((n, R, W), BF16),
                  compiler_params=_cp(("arbitrary", "arbitrary")))(parts)


def _reduce_scatter_two_level(g, name):
    _, R, W = g.shape
    by_core = jnp.swapaxes(g.reshape(4, 2, R, W), 0, 1)
    swapped = _exchange(by_core, name + "_cores", False, "cores")
    return _exchange(_pair_add(swapped, name + "_add"), name + "_chips", False, "chips")


MAX_FULL_K = 4096


def _matmul(a, b, ta, tb, name, out_dtype=F32, tm=1024, tn=1024, tk=3072):
    M, K = (a.shape[1], a.shape[0]) if ta else a.shape
    N = b.shape[0] if tb else b.shape[1]
    assert (b.shape[1] if tb else b.shape[0]) == K
    if K <= MAX_FULL_K:
        tk = K
    else:
        tm, tk = min(tm, 512), _tile(K, tk)
    tm, tn = _tile(M, tm), _tile(N, tn)
    nk = K // tk

    def body_full(a_ref, b_ref, o_ref):
        o_ref[...] = _dg(a_ref[...], b_ref[...], ta, tb, "bf").astype(o_ref.dtype)

    def body_acc(a_ref, b_ref, o_ref, acc):
        kk = pl.program_id(2)

        @pl.when(kk == 0)
        def _():
            acc[...] = jnp.zeros_like(acc)

        acc[...] += _dg(a_ref[...], b_ref[...], ta, tb, "bf")

        @pl.when(kk == nk - 1)
        def _():
            o_ref[...] = acc[...].astype(o_ref.dtype)

    a_spec = pl.BlockSpec((tk, tm), lambda i, j, k: (k, i)) if ta else pl.BlockSpec((tm, tk), lambda i, j, k: (i, k))
    b_spec = pl.BlockSpec((tn, tk), lambda i, j, k: (j, k)) if tb else pl.BlockSpec((tk, tn), lambda i, j, k: (k, j))
    return _pcall(
        body_full if nk == 1 else body_acc, name=name, grid=(M // tm, N // tn, nk),
        in_specs=[a_spec, b_spec], out_specs=pl.BlockSpec((tm, tn), lambda i, j, k: (i, j)),
        out_shape=jax.ShapeDtypeStruct((M, N), out_dtype),
        scratch_shapes=[] if nk == 1 else [pltpu.VMEM((tm, tn), F32)],
        compiler_params=_cp(("parallel", "parallel", "arbitrary")),
    )(a, b)


def _row_spec(tr, width, col_block=0):
    return pl.BlockSpec((tr, width), lambda i: (i, col_block))


def _full_spec(shape):
    nd = len(shape)
    return pl.BlockSpec(tuple(shape), lambda i: (0,) * nd)


def _rmsnorm_fwd(x, g, name, tr=256):
    R, D = x.shape
    tr = _tile(R, tr)

    def body(x_ref, g_ref, o_ref):
        o_ref[...] = _rms(x_ref[...], g_ref[...]).astype(BF16)

    return _pcall(body, name=name, grid=(R // tr,),
                  in_specs=[_row_spec(tr, D), _full_spec((1, D))], out_specs=_row_spec(tr, D),
                  out_shape=jax.ShapeDtypeStruct((R, D), BF16), compiler_params=_cp(("arbitrary",)))(x, g)


def _rmsnorm_bwd(x, g, dy, extra, name, tr=128):
    R, D = x.shape
    tr = _tile(R, tr)
    has_extra = extra is not None

    def body(*refs):
        if has_extra:
            x_ref, g_ref, dy_ref, e_ref, dx_ref, dg_ref = refs
        else:
            x_ref, g_ref, dy_ref, dx_ref, dg_ref = refs
        _, vjp = jax.vjp(_rms, x_ref[...], g_ref[...])
        dx, dg = vjp(dy_ref[...])
        dx_ref[...] = dx + e_ref[...] if has_extra else dx

        @pl.when(pl.program_id(0) == 0)
        def _():
            dg_ref[...] = jnp.zeros_like(dg_ref)

        dg_ref[...] += dg

    ins = [x, g, dy] + ([extra] if has_extra else [])
    specs = [_row_spec(tr, D), _full_spec((1, D)), _row_spec(tr, D)] + ([_row_spec(tr, D)] if has_extra else [])
    return _pcall(body, name=name, grid=(R // tr,), in_specs=specs,
                  out_specs=[_row_spec(tr, D), _full_spec((1, D))],
                  out_shape=[jax.ShapeDtypeStruct((R, D), F32), jax.ShapeDtypeStruct((1, D), F32)],
                  compiler_params=_cp(("arbitrary",)))(*ins)


def _head_indicator(C, hp):
    e = (jnp.arange(C)[:, None] // HEAD == jnp.arange(hp)[None, :]).astype(F32)
    return e, e.T


def _prep_fn(C, L, ps, prev, mu, w0, a0, k_k, k_a, wdu, wiu, E, ET):
    sh = ps + (prev - ps) * mu
    r, k, v = sh[:, :C], sh[:, C:2 * C], sh[:, 2 * C:3 * C]
    wl, al = sh[:, 3 * C:3 * C + L], sh[:, 3 * C + L:3 * C + 2 * L]
    wd = w0 + _mm(jnp.tanh(wl), wdu, False, False, "bf")
    w_pre = -_softplus(-wd) - 0.5
    lw = -jnp.exp(w_pre)
    alpha = _sigmoid(a0 + _mm(al, wiu, False, False, "bf"))
    kk = k * k_k
    ss = _mm(kk * kk, E, False, False, "hi")
    kk = kk * _mm(lax.rsqrt(jnp.maximum(ss, 1e-24)), ET, False, False, "hi")
    k_mod = k * (1.0 + (alpha - 1.0) * k_a)
    return r, lw, k_mod, v, -kk, kk * alpha


def _prep_fwd(p, prev, params, C, L, tr=128):
    T = p.shape[0]
    SH = 3 * C + 2 * L
    tr = _tile(T, tr)
    hp = params[-1].shape[0]

    def body(ps_ref, prev_ref, mu, w0, a0, kk_, ka_, wdu, wiu, E, ET, *outs):
        vals = _prep_fn(C, L, ps_ref[...], prev_ref[...], mu[...], w0[...], a0[...], kk_[...], ka_[...],
                        wdu[...], wiu[...], E[...], ET[...])
        for o, v in zip(outs, vals):
            o[...] = v

    pspecs = [_full_spec(a.shape) for a in params]
    return _pcall(body, name="rwkv_prep_fwd", grid=(T // tr,),
                  in_specs=[_row_spec(tr, SH), _row_spec(tr, SH)] + pspecs,
                  out_specs=[_row_spec(tr, C)] * 6,
                  out_shape=[jax.ShapeDtypeStruct((T, C), F32)] * 6,
                  compiler_params=_cp(("arbitrary",)))(p, prev, *params)


def _prep_bwd(p, prev, params, cts, C, L, tr=64):
    T = p.shape[0]
    SH = 3 * C + 2 * L
    tr = _tile(T, tr)
    nparam = 7

    def body(ps_ref, prev_ref, mu, w0, a0, kk_, ka_, wdu, wiu, E, ET, c0, c1, c2, c3, c4, c5, e0, e2, e3,
             dps_ref, dprev_ref, *dpar):
        f = functools.partial(_prep_fn, C, L)
        fe = lambda ps, prev, *par: f(ps, prev, *par, E[...], ET[...])
        _, vjp = jax.vjp(fe, ps_ref[...], prev_ref[...], mu[...], w0[...], a0[...], kk_[...], ka_[...], wdu[...], wiu[...])
        grads = vjp((c0[...] + e0[...], c1[...], c2[...] + e2[...], c3[...] + e3[...], c4[...], c5[...]))
        dps_ref[...] = grads[0]
        dprev_ref[...] = grads[1]

        @pl.when(pl.program_id(0) == 0)
        def _():
            for d in dpar:
                d[...] = jnp.zeros_like(d)

        for d, gval in zip(dpar, grads[2:]):
            d[...] += gval

    pspecs = [_full_spec(a.shape) for a in params]
    par_shapes = [a.shape for a in params[:nparam]]
    return _pcall(body, name="rwkv_prep_bwd", grid=(T // tr,),
                  in_specs=[_row_spec(tr, SH), _row_spec(tr, SH)] + pspecs + [_row_spec(tr, C)] * 9,
                  out_specs=[_row_spec(tr, SH), _row_spec(tr, SH)] + [_full_spec(s) for s in par_shapes],
                  out_shape=[jax.ShapeDtypeStruct((T, SH), F32)] * 2 + [jax.ShapeDtypeStruct(s, F32) for s in par_shapes],
                  compiler_params=_cp(("arbitrary",)))(p, prev, *params, *cts)


def _shift_combine(d_direct, d_prev_up, tr=256):
    T, W = d_direct.shape
    tr = _tile(T, tr)

    def body(a_ref, b_ref, o_ref):
        o_ref[...] = (a_ref[...] + b_ref[...]).astype(BF16)

    return _pcall(body, name="shift_combine", grid=(T // tr,),
                  in_specs=[_row_spec(tr, W)] * 2, out_specs=_row_spec(tr, W),
                  out_shape=jax.ShapeDtypeStruct((T, W), BF16), compiler_params=_cp(("arbitrary",)))(d_direct, d_prev_up)


def _mix_fn(y, r, kmod, v, g_rwkv, yfox, g_fox, ymem, g_mq, lnw, lnb, rk, E, ET):
    inv = 1.0 / HEAD
    mean = _mm(y, E, False, False, "hi") * inv
    yc = y - _mm(mean, ET, False, False, "hi")
    var = _mm(yc * yc, E, False, False, "hi") * inv
    yn = yc * _mm(lax.rsqrt(var + GN_EPS), ET, False, False, "hi") * lnw + lnb
    bonus = _mm(_mm(r * kmod * rk, E, False, False, "hi"), ET, False, False, "hi") * v
    o1 = (yn + bonus) * _silu(g_rwkv)
    return jnp.concatenate([o1, yfox * _silu(g_fox), ymem * _silu(g_mq)], axis=1)


def _mix_specs(tr, C, MW):
    return [_row_spec(tr, C)] * 7 + [_row_spec(tr, MW)] * 2


def _mix_fwd(acts, params, C, MW, tr=128):
    T = acts[0].shape[0]
    D = 2 * C + MW
    tr = _tile(T, tr)

    def body(y_, r_, k_, v_, g1, yf, g2, ym, g3, lnw, lnb, rk, E, ET, o_ref):
        o_ref[...] = _mix_fn(y_[...], r_[...], k_[...], v_[...], g1[...], yf[...], g2[...], ym[...], g3[...],
                             lnw[...], lnb[...], rk[...], E[...], ET[...]).astype(BF16)

    return _pcall(body, name="mix_fwd", grid=(T // tr,),
                  in_specs=_mix_specs(tr, C, MW) + [_full_spec(a.shape) for a in params],
                  out_specs=_row_spec(tr, D), out_shape=jax.ShapeDtypeStruct((T, D), BF16),
                  compiler_params=_cp(("arbitrary",)))(*acts, *params)


def _mix_bwd(acts, params, dycat, C, MW, tr=64):
    T = acts[0].shape[0]
    D = 2 * C + MW
    tr = _tile(T, tr)

    def body(y_, r_, k_, v_, g1, yf, g2, ym, g3, lnw, lnb, rk, E, ET, dy_ref, *outs):
        fe = lambda *a: _mix_fn(*a, E[...], ET[...])
        _, vjp = jax.vjp(fe, y_[...], r_[...], k_[...], v_[...], g1[...], yf[...], g2[...], ym[...], g3[...],
                         lnw[...], lnb[...], rk[...])
        grads = vjp(dy_ref[...])
        for o, gval in zip(outs[:9], grads[:9]):
            o[...] = gval

        @pl.when(pl.program_id(0) == 0)
        def _():
            for o in outs[9:]:
                o[...] = jnp.zeros_like(o)

        for o, gval in zip(outs[9:], grads[9:]):
            o[...] += gval

    widths = [C, C, C, C, C, C, C, MW, MW]
    return _pcall(body, name="mix_bwd", grid=(T // tr,),
                  in_specs=_mix_specs(tr, C, MW) + [_full_spec(a.shape) for a in params] + [_row_spec(tr, D)],
                  out_specs=[_row_spec(tr, w) for w in widths] + [_full_spec((1, C))] * 3,
                  out_shape=[jax.ShapeDtypeStruct((T, w), F32) for w in widths] + [jax.ShapeDtypeStruct((1, C), F32)] * 3,
                  compiler_params=_cp(("arbitrary",)))(*acts, *params, dycat)


def _post(yo, x, tgt, g_post, tr=128):
    T, D = x.shape
    tr = _tile(T, tr)

    def body(yo_ref, x_ref, t_ref, g_ref, dyo_ref, dout_ref, loss_ref, dg_ref):
        n, vjp = jax.vjp(_rms, yo_ref[...], g_ref[...])
        diff = (x_ref[...] + n) - t_ref[...]
        part = 0.5 * jnp.sum(jnp.mean(diff * diff, axis=-1, keepdims=True), axis=0, keepdims=True)
        d_out = diff * (1.0 / D)
        dyo, dg = vjp(d_out)
        dyo_ref[...] = dyo.astype(BF16)
        dout_ref[...] = d_out

        @pl.when(pl.program_id(0) == 0)
        def _():
            loss_ref[...] = jnp.zeros_like(loss_ref)
            dg_ref[...] = jnp.zeros_like(dg_ref)

        loss_ref[...] += jnp.broadcast_to(part, loss_ref.shape)
        dg_ref[...] += dg

    return _pcall(body, name="post_loss", grid=(T // tr,),
                  in_specs=[_row_spec(tr, D)] * 3 + [_full_spec((1, D))],
                  out_specs=[_row_spec(tr, D), _row_spec(tr, D), _full_spec((1, LANE)), _full_spec((1, D))],
                  out_shape=[jax.ShapeDtypeStruct((T, D), BF16), jax.ShapeDtypeStruct((T, D), F32),
                             jax.ShapeDtypeStruct((1, LANE), F32), jax.ShapeDtypeStruct((1, D), F32)],
                  compiler_params=_cp(("arbitrary",)))(yo, x, tgt, g_post)


def _memattn_fn(MW, q, mkv):
    hd = MW // MEM_HEADS
    scale = hd ** -0.5
    outs = []
    for h in range(MEM_HEADS):
        qh = q[:, h * hd:(h + 1) * hd]
        kh = mkv[:, h * hd:(h + 1) * hd]
        vh = mkv[:, MW + h * hd:MW + (h + 1) * hd]
        s = _mm(qh, kh, False, True, "bf") * scale
        e = jnp.exp(s - lax.stop_gradient(jnp.max(s, axis=-1, keepdims=True)))
        pr = e / jnp.sum(e, axis=-1, keepdims=True)
        outs.append(_mm(pr, vh, False, False, "bf"))
    return jnp.concatenate(outs, axis=1)


def _memattn_fwd(p, mkv, MW, tr=256):
    T = p.shape[0]
    tr = _tile(T, tr)

    def body(q_ref, kv_ref, o_ref):
        o_ref[...] = _memattn_fn(MW, q_ref[...], kv_ref[...])

    return _pcall(body, name="memattn_fwd", grid=(T // tr,),
                  in_specs=[_row_spec(tr, MW), _full_spec(mkv.shape)], out_specs=_row_spec(tr, MW),
                  out_shape=jax.ShapeDtypeStruct((T, MW), F32), compiler_params=_cp(("arbitrary",)))(p, mkv)


def _memattn_bwd(p, mkv, do, MW, tr=256):
    T = p.shape[0]
    tr = _tile(T, tr)

    def body(q_ref, kv_ref, do_ref, dq_ref, dkv_ref):
        _, vjp = jax.vjp(functools.partial(_memattn_fn, MW), q_ref[...], kv_ref[...])
        dq, dkv = vjp(do_ref[...])
        dq_ref[...] = dq

        @pl.when(pl.program_id(0) == 0)
        def _():
            dkv_ref[...] = jnp.zeros_like(dkv_ref)

        dkv_ref[...] += dkv

    return _pcall(body, name="memattn_bwd", grid=(T // tr,),
                  in_specs=[_row_spec(tr, MW), _full_spec(mkv.shape), _row_spec(tr, MW)],
                  out_specs=[_row_spec(tr, MW), _full_spec(mkv.shape)],
                  out_shape=[jax.ShapeDtypeStruct((T, MW), F32), jax.ShapeDtypeStruct(mkv.shape, F32)],
                  compiler_params=_cp(("arbitrary",)))(p, mkv, do)


def _fox_cum_fwd(p, b_f_pad, off, blk=512):
    T = p.shape[0]
    blk = _tile(T, blk)

    def body(f_ref, b_ref, cum_ref, cumt_ref, carry):
        @pl.when(pl.program_id(0) == 0)
        def _():
            carry[...] = jnp.zeros_like(carry)

        z = f_ref[...] + b_ref[...]
        logf = -_softplus(-z)
        row = lax.broadcasted_iota(jnp.int32, (blk, blk), 0)
        col = lax.broadcasted_iota(jnp.int32, (blk, blk), 1)
        tri = (col <= row).astype(F32)
        c = _dg(tri, logf, False, False, "hi") + carry[...]
        cum_ref[...] = c
        cumt_ref[...] = c.T
        carry[...] += jnp.sum(logf, axis=0, keepdims=True)

    return _pcall(body, name="fox_cum_fwd", grid=(T // blk,),
                  in_specs=[_row_spec(blk, LANE, off // LANE), _full_spec((1, LANE))],
                  out_specs=[_row_spec(blk, LANE), pl.BlockSpec((LANE, blk), lambda i: (0, i))],
                  out_shape=[jax.ShapeDtypeStruct((T, LANE), F32), jax.ShapeDtypeStruct((LANE, T), F32)],
                  scratch_shapes=[pltpu.VMEM((1, LANE), F32)], compiler_params=_cp(("arbitrary",)))(p, b_f_pad)


def _fox_cum_bwd(p, b_f_pad, dcum, off, blk=512):
    T = p.shape[0]
    blk = _tile(T, blk)
    nb = T // blk

    def body(f_ref, b_ref, dc_ref, df_ref, db_ref, carry):
        @pl.when(pl.program_id(0) == 0)
        def _():
            carry[...] = jnp.zeros_like(carry)
            db_ref[...] = jnp.zeros_like(db_ref)

        row = lax.broadcasted_iota(jnp.int32, (blk, blk), 0)
        col = lax.broadcasted_iota(jnp.int32, (blk, blk), 1)
        tri = (col >= row).astype(F32)
        dlogf = _dg(tri, dc_ref[...], False, False, "hi") + carry[...]
        carry[...] += jnp.sum(dc_ref[...], axis=0, keepdims=True)
        z = f_ref[...] + b_ref[...]
        dz = dlogf * (1.0 - _sigmoid(z))
        df_ref[...] = dz
        db_ref[...] += jnp.sum(dz, axis=0, keepdims=True)

    rev = lambda i: (nb - 1 - i, 0)
    return _pcall(body, name="fox_cum_bwd", grid=(nb,),
                  in_specs=[pl.BlockSpec((blk, LANE), lambda i: (nb - 1 - i, off // LANE)), _full_spec((1, LANE)),
                            pl.BlockSpec((blk, LANE), rev)],
                  out_specs=[pl.BlockSpec((blk, LANE), rev), _full_spec((1, LANE))],
                  out_shape=[jax.ShapeDtypeStruct((T, LANE), F32), jax.ShapeDtypeStruct((1, LANE), F32)],
                  scratch_shapes=[pltpu.VMEM((1, LANE), F32)], compiler_params=_cp(("arbitrary",)))(p, b_f_pad, dcum)


def _fox_masked_scores(q, k, cq, ck, q0, k0, tq, tk):
    s = _dg(q, k, False, True, "bf") * (HEAD ** -0.5) + cq - ck
    qpos = q0 + lax.broadcasted_iota(jnp.int32, (tq, tk), 0)
    kpos = k0 + lax.broadcasted_iota(jnp.int32, (tq, tk), 1)
    return jnp.where(kpos <= qpos, s, NEG)


def _fox_fwd(p, cq, ck, C, offs, tq=256):
    T = p.shape[0]
    tq = _tile(T, tq)
    npair = C // LANE
    cb = lambda name: offs[name] // LANE

    def body(q_ref, k_ref, v_ref, cq_ref, ck_ref, o_ref, lse_ref):
        i = pl.program_id(1)
        lse_ref[...] = jnp.zeros_like(lse_ref)
        for hh in range(2):
            sl = slice(hh * HEAD, (hh + 1) * HEAD)
            q = q_ref[:, sl].astype(BF16)
            cqv = cq_ref[0, :, hh:hh + 1]

            def step(j, carry):
                m, l, acc = carry
                off = pl.multiple_of(j * tq, tq)
                k = k_ref[pl.ds(off, tq), sl].astype(BF16)
                v = v_ref[pl.ds(off, tq), sl].astype(BF16)
                ckv = ck_ref[0, hh:hh + 1, pl.ds(off, tq)]
                s = _fox_masked_scores(q, k, cqv, ckv, i * tq, off, tq, tq)
                m_new = jnp.maximum(m, jnp.max(s, axis=-1, keepdims=True))
                pr = jnp.exp(s - m_new)
                al = jnp.exp(m - m_new)
                l = al * l + jnp.sum(pr, axis=-1, keepdims=True)
                acc = al * acc + _dg(pr, v, False, False, "bf")
                return m_new, l, acc

            m0 = jnp.full((tq, 1), NEG, F32)
            m, l, acc = lax.fori_loop(0, i + 1, step, (m0, jnp.zeros((tq, 1), F32), jnp.zeros((tq, HEAD), F32)))
            o_ref[:, sl] = acc / l
            lse_ref[0, :, hh:hh + 1] = m + jnp.log(l)

    return _pcall(body, name="fox_fwd", grid=(npair, T // tq),
                  in_specs=[pl.BlockSpec((tq, LANE), lambda h, i: (i, cb("fq") + h)),
                            pl.BlockSpec((T, LANE), lambda h, i: (0, cb("fk") + h)),
                            pl.BlockSpec((T, LANE), lambda h, i: (0, cb("fv") + h)),
                            pl.BlockSpec((1, tq, 8), lambda h, i: (h, i, 0)),
                            pl.BlockSpec((1, 8, T), lambda h, i: (h, 0, 0))],
                  out_specs=[pl.BlockSpec((tq, LANE), lambda h, i: (i, h)),
                             pl.BlockSpec((1, tq, 8), lambda h, i: (h, i, 0))],
                  out_shape=[jax.ShapeDtypeStruct((T, C), F32), jax.ShapeDtypeStruct((npair, T, 8), F32)],
                  compiler_params=_cp(("arbitrary", "arbitrary")))(p, p, p, cq, ck)


def _fox_delta(p, cq, ck, do, lse, C, offs, tq=256):
    T = p.shape[0]
    tq = _tile(T, tq)
    npair = C // LANE
    cb = lambda name: offs[name] // LANE

    def body(q_ref, k_ref, v_ref, cq_ref, ck_ref, do_ref, lse_ref, d_ref):
        i = pl.program_id(1)
        d_ref[...] = jnp.zeros_like(d_ref)
        for hh in range(2):
            sl = slice(hh * HEAD, (hh + 1) * HEAD)
            q = q_ref[:, sl].astype(BF16)
            dov = do_ref[:, sl].astype(BF16)
            cqv = cq_ref[0, :, hh:hh + 1]
            lsev = lse_ref[0, :, hh:hh + 1]

            def step(j, acc):
                off = pl.multiple_of(j * tq, tq)
                k = k_ref[pl.ds(off, tq), sl].astype(BF16)
                v = v_ref[pl.ds(off, tq), sl].astype(BF16)
                ckv = ck_ref[0, hh:hh + 1, pl.ds(off, tq)]
                s = _fox_masked_scores(q, k, cqv, ckv, i * tq, off, tq, tq)
                pr = jnp.exp(s - lsev)
                dp = _dg(dov, v, False, True, "bf")
                return acc + jnp.sum(pr * dp, axis=-1, keepdims=True)

            d_ref[0, :, hh:hh + 1] = lax.fori_loop(0, i + 1, step, jnp.zeros((tq, 1), F32))

    return _pcall(body, name="fox_delta", grid=(npair, T // tq),
                  in_specs=[pl.BlockSpec((tq, LANE), lambda h, i: (i, cb("fq") + h)),
                            pl.BlockSpec((T, LANE), lambda h, i: (0, cb("fk") + h)),
                            pl.BlockSpec((T, LANE), lambda h, i: (0, cb("fv") + h)),
                            pl.BlockSpec((1, tq, 8), lambda h, i: (h, i, 0)),
                            pl.BlockSpec((1, 8, T), lambda h, i: (h, 0, 0)),
                            pl.BlockSpec((tq, LANE), lambda h, i: (i, h)),
                            pl.BlockSpec((1, tq, 8), lambda h, i: (h, i, 0))],
                  out_specs=pl.BlockSpec((1, tq, 8), lambda h, i: (h, i, 0)),
                  out_shape=jax.ShapeDtypeStruct((npair, T, 8), F32),
                  compiler_params=_cp(("arbitrary", "arbitrary")))(p, p, p, cq, ck, do, lse)


def _fox_bwd(p, cq, ck, delta, do, lse, C, offs, tk=256):
    T = p.shape[0]
    tk = _tile(T, tk)
    nblk = T // tk
    npair = C // LANE
    cb = lambda name: offs[name] // LANE
    scale = HEAD ** -0.5

    def body(q_ref, k_ref, v_ref, cq_ref, ck_ref, dl_ref, do_ref, lse_ref, dq_ref, dk_ref, dv_ref, dck_ref):
        j = pl.program_id(1)

        @pl.when(j == 0)
        def _():
            dq_ref[...] = jnp.zeros_like(dq_ref)

        dck_ref[...] = jnp.zeros_like(dck_ref)
        for hh in range(2):
            sl = slice(hh * HEAD, (hh + 1) * HEAD)
            k = k_ref[:, sl].astype(BF16)
            v = v_ref[:, sl].astype(BF16)
            ckv = ck_ref[0, hh:hh + 1, :]

            def step(i, carry):
                dk, dv, dck = carry
                off = pl.multiple_of(i * tk, tk)
                q = q_ref[pl.ds(off, tk), sl].astype(BF16)
                dov = do_ref[pl.ds(off, tk), sl]
                cqv = cq_ref[0, pl.ds(off, tk), hh:hh + 1]
                lsev = lse_ref[0, pl.ds(off, tk), hh:hh + 1]
                s = _fox_masked_scores(q, k, cqv, ckv, off, j * tk, tk, tk)
                pr = jnp.exp(s - lsev)
                dv = dv + _dg(pr, dov, True, False, "bf")
                dp = _dg(dov, v, False, True, "bf")
                ds = pr * (dp - dl_ref[0, pl.ds(off, tk), hh:hh + 1])
                dk = dk + _dg(ds, q, True, False, "bf") * scale
                dq_ref[pl.ds(off, tk), sl] += _dg(ds, k, False, False, "bf") * scale
                dck = dck - jnp.sum(ds, axis=0, keepdims=True)
                return dk, dv, dck

            z = jnp.zeros((tk, HEAD), F32)
            dk, dv, dck = lax.fori_loop(j, nblk, step, (z, z, jnp.zeros((1, tk), F32)))
            dk_ref[:, sl] = dk
            dv_ref[:, sl] = dv
            dck_ref[0, hh:hh + 1, :] = dck

    full = lambda h, j: (0, h)
    return _pcall(body, name="fox_bwd", grid=(npair, nblk),
                  in_specs=[pl.BlockSpec((T, LANE), lambda h, j: (0, cb("fq") + h)),
                            pl.BlockSpec((tk, LANE), lambda h, j: (j, cb("fk") + h)),
                            pl.BlockSpec((tk, LANE), lambda h, j: (j, cb("fv") + h)),
                            pl.BlockSpec((1, T, 8), lambda h, j: (h, 0, 0)),
                            pl.BlockSpec((1, 8, tk), lambda h, j: (h, 0, j)),
                            pl.BlockSpec((1, T, 8), lambda h, j: (h, 0, 0)), pl.BlockSpec((T, LANE), full),
                            pl.BlockSpec((1, T, 8), lambda h, j: (h, 0, 0))],
                  out_specs=[pl.BlockSpec((T, LANE), full),
                             pl.BlockSpec((tk, LANE), lambda h, j: (j, h)),
                             pl.BlockSpec((tk, LANE), lambda h, j: (j, h)),
                             pl.BlockSpec((1, 8, tk), lambda h, j: (h, 0, j))],
                  out_shape=[jax.ShapeDtypeStruct((T, C), F32)] * 3 + [jax.ShapeDtypeStruct((npair, 8, T), F32)],
                  compiler_params=_cp(("arbitrary", "arbitrary")))(p, p, p, cq, ck, delta, do, lse)


def _chunk_fn(S0, r, lw, k, v, a, b):
    n = r.shape[0]
    row = lax.broadcasted_iota(jnp.int32, (n, n), 0)
    col = lax.broadcasted_iota(jnp.int32, (n, n), 1)
    incl, strict = col <= row, col < row
    mm = lambda x, y, ta=False, tb=False: _mm(x, y, ta, tb, "hi")
    g = mm(incl.astype(F32), lw)
    einv = jnp.exp(-g)
    rt, at, bt, kt = r * jnp.exp(g), a * jnp.exp(g - lw), b * einv, k * einv
    a_ab = jnp.where(strict, mm(at, bt, tb=True), 0.0)
    a_ak = jnp.where(strict, mm(at, kt, tb=True), 0.0)
    r_b = jnp.where(incl, mm(rt, bt, tb=True), 0.0)
    r_k = jnp.where(incl, mm(rt, kt, tb=True), 0.0)
    u = mm(at, S0, tb=True) + mm(a_ak, v)
    pw = a_ab
    for it in range(6):
        u = u + mm(pw, u)
        if it < 5:
            pw = mm(pw, pw)
    y = mm(rt, S0, tb=True) + mm(r_b, u) + mm(r_k, v)
    g_end = jnp.sum(lw, axis=0, keepdims=True)
    s_end = (S0 + mm(u, bt, ta=True) + mm(v, kt, ta=True)) * jnp.exp(g_end)
    return y, s_end


def _scan_fwd(r, lw, k, v, a, b, tc=256):
    T, C = r.shape
    tc = _tile(T, tc)
    ncs = tc // CHUNK
    npair = C // LANE

    def body(r_ref, lw_ref, k_ref, v_ref, a_ref, b_ref, y_ref, ck_ref, state):
        @pl.when(pl.program_id(1) == 0)
        def _():
            state[...] = jnp.zeros_like(state)

        for c in range(ncs):
            rows = slice(c * CHUNK, (c + 1) * CHUNK)
            ck_ref[0, c] = state[...]
            for hh in range(2):
                sl = slice(hh * HEAD, (hh + 1) * HEAD)
                y, s_end = _chunk_fn(state[:, sl], r_ref[rows, sl], lw_ref[rows, sl], k_ref[rows, sl],
                                     v_ref[rows, sl], a_ref[rows, sl], b_ref[rows, sl])
                y_ref[rows, sl] = y
                state[:, sl] = s_end

    spec = pl.BlockSpec((tc, LANE), lambda h, t: (t, h))
    return _pcall(body, name="rwkv_scan_fwd", grid=(npair, T // tc),
                  in_specs=[spec] * 6,
                  out_specs=[spec, pl.BlockSpec((1, ncs, HEAD, LANE), lambda h, t: (h, t, 0, 0))],
                  out_shape=[jax.ShapeDtypeStruct((T, C), F32),
                             jax.ShapeDtypeStruct((npair, T // CHUNK, HEAD, LANE), F32)],
                  scratch_shapes=[pltpu.VMEM((HEAD, LANE), F32)],
                  compiler_params=_cp(("arbitrary", "arbitrary")))(r, lw, k, v, a, b)


def _scan_bwd(r, lw, k, v, a, b, ckpt, dy, tc=256):
    T, C = r.shape
    tc = _tile(T, tc)
    ncs = tc // CHUNK
    npair = C // LANE
    nt = T // tc

    def body(r_ref, lw_ref, k_ref, v_ref, a_ref, b_ref, ck_ref, dy_ref, dr, dlw, dk, dv, da, db, dstate):
        @pl.when(pl.program_id(1) == 0)
        def _():
            dstate[...] = jnp.zeros_like(dstate)

        outs = (dr, dlw, dk, dv, da, db)
        for c in reversed(range(ncs)):
            rows = slice(c * CHUNK, (c + 1) * CHUNK)
            for hh in range(2):
                sl = slice(hh * HEAD, (hh + 1) * HEAD)
                _, vjp = jax.vjp(_chunk_fn, ck_ref[0, c, :, sl], r_ref[rows, sl], lw_ref[rows, sl], k_ref[rows, sl],
                                 v_ref[rows, sl], a_ref[rows, sl], b_ref[rows, sl])
                grads = vjp((dy_ref[rows, sl], dstate[:, sl]))
                dstate[:, sl] = grads[0]
                for o, gval in zip(outs, grads[1:]):
                    o[rows, sl] = gval

    spec = pl.BlockSpec((tc, LANE), lambda h, t: (nt - 1 - t, h))
    return _pcall(body, name="rwkv_scan_bwd", grid=(npair, nt),
                  in_specs=[spec] * 6 + [pl.BlockSpec((1, ncs, HEAD, LANE), lambda h, t: (h, nt - 1 - t, 0, 0)), spec],
                  out_specs=[spec] * 6, out_shape=[jax.ShapeDtypeStruct((T, C), F32)] * 6,
                  scratch_shapes=[pltpu.VMEM((HEAD, LANE), F32)],
                  compiler_params=_cp(("arbitrary", "arbitrary")))(r, lw, k, v, a, b, ckpt, dy)


def _adam(w, g, m, v):
    m = ADAM_B1 * m + (1.0 - ADAM_B1) * g
    v = ADAM_B2 * v + (1.0 - ADAM_B2) * (g * g)
    m_hat = m / (1.0 - ADAM_B1 ** ADAM_STEP)
    v_hat = v / (1.0 - ADAM_B2 ** ADAM_STEP)
    return -ADAM_LR * (m_hat / (jnp.sqrt(v_hat) + ADAM_EPS) + ADAM_WD * w), m, v


def _sum_adam(parts, w, m, v, name):
    n, R, W = parts.shape
    tr = _tile(R, max(8, min(256, (1 << 20) // (n * W))))

    def body(p_ref, w_ref, m_ref, v_ref, g_ref, d_ref, nm_ref, nv_ref):
        g = p_ref[0].astype(F32)
        for s in range(1, n):
            g = g + p_ref[s].astype(F32)
        d, nm, nv = _adam(w_ref[...], g, m_ref[...], v_ref[...])
        g_ref[...] = g
        d_ref[...] = d
        nm_ref[...] = nm
        nv_ref[...] = nv

    return _pcall(body, name=name, grid=(R // tr,),
                  in_specs=[pl.BlockSpec((n, tr, W), lambda i: (0, i, 0))] + [_row_spec(tr, W)] * 3,
                  out_specs=[_row_spec(tr, W)] * 4, out_shape=[jax.ShapeDtypeStruct((R, W), F32)] * 4,
                  compiler_params=_cp(("arbitrary",)))(parts, w, m, v)


def _pad_lanes(vec, width):
    return jnp.pad(vec, ((0, 0), (0, width - vec.shape[1])))


def kernel(x, mem, g_pre, w_in, mu_rwkv, w0, w_decay_up, a0, w_iclr_up, k_k, k_a, r_k, ln_x_w, ln_x_b, b_f, g_mem, w_mem_kv, w_out, g_post, loss_target, m_g_pre, m_w_in, m_mu_rwkv, m_w0, m_w_decay_up, m_a0, m_w_iclr_up, m_k_k, m_k_a, m_r_k, m_ln_x_w, m_ln_x_b, m_b_f, m_g_mem, m_w_mem_kv, m_w_out, m_g_post, v_g_pre, v_w_in, v_mu_rwkv, v_w0, v_w_decay_up, v_a0, v_w_iclr_up, v_k_k, v_k_a, v_r_k, v_ln_x_w, v_ln_x_b, v_b_f, v_g_mem, v_w_mem_kv, v_w_out, v_g_post):
    T, D = x.shape[1], x.shape[2]
    C = w0.shape[1]
    L = w_decay_up.shape[1]
    H = C // HEAD
    MW = w_mem_kv.shape[2] // 2
    SH = 3 * C + 2 * L
    IN = NDEV * w_in.shape[2]
    assert IN == SH + 5 * C + H + 2 * MW and D == 2 * C + MW and H % 2 == 0 and H <= LANE
    assert C % LANE == 0 and L % LANE == 0 and MW % (MEM_HEADS * HEAD) == 0 and T % CHUNK == 0
    offs = dict(grw=SH, fq=SH + C, fk=SH + 2 * C, fv=SH + 3 * C, gfx=SH + 4 * C, mq=SH + 5 * C, gmq=SH + 5 * C + MW,
                fl=SH + 5 * C + 2 * MW)
    NI = -(-(offs["fl"] + LANE) // 512) * 512
    l_fl = SH + 4 * C

    x2, mem2, tgt2 = x[0], mem[0], loss_target[0]

    wg = _gather_two_level(w_in[0].astype(BF16), "gather_w_in")
    w_full = jnp.transpose(wg, (1, 0, 2)).reshape(D, IN)
    w_perm = jnp.concatenate([w_full[:, :l_fl], w_full[:, l_fl + H:], w_full[:, l_fl:l_fl + H],
                              jnp.zeros((D, NI - IN), BF16)], axis=1)
    w_out_f = _gather_two_level(w_out[0].astype(BF16), "gather_w_out").reshape(D, D)
    w_kv_f = _gather_two_level(w_mem_kv[0].astype(BF16), "gather_w_mem_kv").reshape(D, 2 * MW)
    lora = _exchange(jnp.concatenate([w_decay_up[0], w_iclr_up[0]], axis=0), "gather_lora", True)
    lora = jnp.transpose(lora, (1, 0, 2)).reshape(2 * L, C)
    wdu_f, wiu_f = lora[:L], lora[L:]

    E, ET = _head_indicator(C, LANE)
    prep_params = [mu_rwkv, w0, a0, k_k, k_a, wdu_f, wiu_f, E, ET]
    mix_params = [ln_x_w, ln_x_b, r_k.reshape(1, C), E, ET]
    b_f_pad = _pad_lanes(b_f, LANE)

    h = _rmsnorm_fwd(x2, g_pre, "rmsnorm_pre")
    p = _matmul(h, w_perm, False, False, "in_proj")
    ps = p[:, :SH]
    col = lambda name, w: p[:, offs[name]:offs[name] + w]
    g_rwkv, g_fox, mq, g_mq = col("grw", C), col("gfx", C), col("mq", MW), col("gmq", MW)
    prev = jnp.concatenate([jnp.zeros((1, SH), F32), ps[:-1]], axis=0)
    r, lw, kmod, v, a, b = _prep_fwd(p, prev, prep_params, C, L)
    y_scan, ckpt = _scan_fwd(r, lw, kmod, v, a, b)

    cum, cum_t = _fox_cum_fwd(p, b_f_pad, offs["fl"])
    ck = jnp.pad(cum_t[:H].reshape(H // 2, 2, T), ((0, 0), (0, 6), (0, 0)))
    cq = jnp.pad(cum[:, :H].reshape(T, H // 2, 2).transpose(1, 0, 2), ((0, 0), (0, 0), (0, 6)))
    y_fox, lse = _fox_fwd(p, cq, ck, C, offs)

    memn = _rmsnorm_fwd(mem2, g_mem, "rmsnorm_mem")
    mkv = _matmul(memn, w_kv_f, False, False, "mem_kv_proj")
    y_mem = _memattn_fwd(mq, mkv, MW)

    acts = [y_scan, r, kmod, v, g_rwkv, y_fox, g_fox, y_mem, g_mq]
    ycat = _mix_fwd(acts, mix_params, C, MW)
    yo = _matmul(ycat, w_out_f, False, False, "out_proj")
    d_yo, d_out, loss_part, dg_post = _post(yo, x2, tgt2, g_post)
    loss = lax.psum(loss_part[0, 0], AXES)

    g_w_out = _matmul(ycat, d_yo, True, False, "grad_w_out")
    d_ycat = _matmul(d_yo, w_out_f, False, True, "d_ycat")
    (d_y, d_r1, d_k1, d_v1, d_grw, d_yfox, d_gfx, d_ymem, d_gmq, dg_lnw, dg_lnb, dg_rk) = _mix_bwd(
        acts, mix_params, d_ycat, C, MW)

    d_mq, d_mkv = _memattn_bwd(mq, mkv, d_ymem, MW)
    g_w_kv = _matmul(memn, d_mkv, True, False, "grad_w_mem_kv")
    d_memn = _matmul(d_mkv, w_kv_f, False, True, "d_memn")
    _, dg_mem = _rmsnorm_bwd(mem2, g_mem, d_memn, None, "rmsnorm_mem_bwd")

    delta = _fox_delta(p, cq, ck, d_yfox, lse, C, offs)
    d_fq, d_fk, d_fv, d_ck = _fox_bwd(p, cq, ck, delta, d_yfox, lse, C, offs)
    d_cum = _pad_lanes(d_ck[:, :2, :].reshape(H, T).T, LANE)
    d_fl, dg_bf = _fox_cum_bwd(p, b_f_pad, d_cum, offs["fl"])

    d_r, d_lw, d_k, d_v, d_a, d_b = _scan_bwd(r, lw, kmod, v, a, b, ckpt, d_y)
    cts = [d_r, d_lw, d_k, d_v, d_a, d_b, d_r1, d_k1, d_v1]
    (d_ps, d_prev, dg_mu, dg_w0, dg_a0, dg_kk, dg_ka, dg_wdu, dg_wiu) = _prep_bwd(p, prev, prep_params, cts, C, L)
    d_prev_up = jnp.concatenate([d_prev[1:], jnp.zeros((1, SH), F32)], axis=0)
    d_sh = _shift_combine(d_ps, d_prev_up)

    tobf = lambda z: z.astype(BF16)
    dp = jnp.concatenate([d_sh, tobf(d_grw), tobf(d_fq), tobf(d_fk), tobf(d_fv), tobf(d_gfx), tobf(d_mq), tobf(d_gmq),
                          tobf(d_fl), jnp.zeros((T, NI - offs["fl"] - LANE), BF16)], axis=1)
    g_w_perm = _matmul(h, dp, True, False, "grad_w_in")
    d_h = _matmul(dp, w_perm, False, True, "d_h")
    grad_x, dg_pre = _rmsnorm_bwd(x2, g_pre, d_h, d_out, "rmsnorm_pre_bwd")

    g_w_log = jnp.concatenate([g_w_perm[:, :l_fl], g_w_perm[:, offs["fl"]:offs["fl"] + H],
                               g_w_perm[:, l_fl:offs["fl"]]], axis=1)
    parts_in = _reduce_scatter_two_level(jnp.transpose(g_w_log.reshape(D, NDEV, IN // NDEV), (1, 0, 2)), "scatter_grad_w_in")
    parts_out = _reduce_scatter_two_level(g_w_out.reshape(NDEV, D // NDEV, D), "scatter_grad_w_out")
    parts_kv = _reduce_scatter_two_level(g_w_kv.reshape(NDEV, D // NDEV, 2 * MW), "scatter_grad_w_mem_kv")
    g_lora = jnp.concatenate([dg_wdu, dg_wiu], axis=0)
    parts_lora = _exchange(jnp.transpose(g_lora.reshape(2 * L, NDEV, C // NDEV), (1, 0, 2)), "scatter_grad_lora", False)

    gw_in, dw_in, nm_w_in, nv_w_in = _sum_adam(parts_in, w_in[0], m_w_in[0], v_w_in[0], "adam_w_in")
    gw_out, dw_out, nm_w_out, nv_w_out = _sum_adam(parts_out, w_out[0], m_w_out[0], v_w_out[0], "adam_w_out")
    gw_kv, dw_kv, nm_w_kv, nv_w_kv = _sum_adam(parts_kv, w_mem_kv[0], m_w_mem_kv[0], v_w_mem_kv[0], "adam_w_mem_kv")
    cat2 = lambda u, w_: jnp.concatenate([u[0], w_[0]], axis=0)
    lora_res = _sum_adam(parts_lora, cat2(w_decay_up, w_iclr_up), cat2(m_w_decay_up, m_w_iclr_up),
                         cat2(v_w_decay_up, v_w_iclr_up), "adam_lora")

    small = [("g_pre", g_pre, m_g_pre, v_g_pre, dg_pre), ("mu_rwkv", mu_rwkv, m_mu_rwkv, v_mu_rwkv, dg_mu),
             ("w0", w0, m_w0, v_w0, dg_w0), ("a0", a0, m_a0, v_a0, dg_a0), ("k_k", k_k, m_k_k, v_k_k, dg_kk),
             ("k_a", k_a, m_k_a, v_k_a, dg_ka), ("r_k", r_k.reshape(1, C), m_r_k.reshape(1, C), v_r_k.reshape(1, C), dg_rk),
             ("ln_x_w", ln_x_w, m_ln_x_w, v_ln_x_w, dg_lnw), ("ln_x_b", ln_x_b, m_ln_x_b, v_ln_x_b, dg_lnb),
             ("b_f", _pad_lanes(b_f, LANE), _pad_lanes(m_b_f, LANE), _pad_lanes(v_b_f, LANE), dg_bf),
             ("g_mem", g_mem, m_g_mem, v_g_mem, dg_mem), ("g_post", g_post, m_g_post, v_g_post, dg_post)]
    widths = [s[1].shape[1] for s in small]
    pack = lambda idx: jnp.concatenate([s[idx] for s in small], axis=1).reshape(-1, LANE)
    parts_small = _exchange(pack(4), "gather_small_grads", True)
    res_small = _sum_adam(parts_small, pack(1), pack(2), pack(3), "adam_small")

    def unpack(flat):
        flat = flat.reshape(1, -1)
        out, o = {}, 0
        for (name, *_), wd in zip(small, widths):
            out[name] = flat[:, o:o + wd]
            o += wd
        out["b_f"] = out["b_f"][:, :H]
        out["r_k"] = out["r_k"].reshape(1, H, HEAD)
        return out

    sg, sd, sm, sv = [unpack(z) for z in res_small]
    big = {"w_in": (gw_in, dw_in, nm_w_in, nv_w_in), "w_out": (gw_out, dw_out, nm_w_out, nv_w_out),
           "w_mem_kv": (gw_kv, dw_kv, nm_w_kv, nv_w_kv),
           "w_decay_up": tuple(z[:L] for z in lora_res), "w_iclr_up": tuple(z[L:] for z in lora_res)}
    order = ["g_pre", "w_in", "mu_rwkv", "w0", "w_decay_up", "a0", "w_iclr_up", "k_k", "k_a", "r_k", "ln_x_w", "ln_x_b",
             "b_f", "g_mem", "w_mem_kv", "w_out", "g_post"]

    def pick(name, idx):
        if name in big:
            return big[name][idx][None]
        return (sg, sd, sm, sv)[idx][name]

    outs = [loss, grad_x[None]]
    for idx in range(4):
        outs += [pick(n, idx) for n in order]
    return tuple(outs)
```

```python
import functools

import jax
import jax.numpy as jnp
from jax import lax
from jax.experimental import pallas as pl
from jax.experimental.pallas import tpu as pltpu

F32, BF16 = jnp.float32, jnp.bfloat16
HI = lax.Precision.HIGHEST
NDEV = 8
AXES = ("x", "y", "c")
HEAD = 64
CHUNK = 64
MEM_HEADS = 4
LANE = 128
RMS_EPS = 1e-6
GN_EPS = 64e-5
NEG = -1e30
ADAM_LR, ADAM_B1, ADAM_B2, ADAM_EPS, ADAM_WD, ADAM_STEP = 0.001, 0.9, 0.999, 1e-08, 0.01, 10
VMEM_LIMIT = 56 * 1024 * 1024


def _pcall(body, **kw):
    return pl.pallas_call(body, **kw)


def _cp(sem=None, vmem=VMEM_LIMIT):
    return pltpu.CompilerParams(dimension_semantics=sem, vmem_limit_bytes=vmem)


def _tile(n, pref):
    for t in (pref, 1024, 512, 256, 128, 64, 32, 16, 8):
        if t <= pref and n % t == 0:
            return t
    return n


def _dg(a, b, ta, tb, mode):
    ca = 0 if ta else 1
    cb = 1 if tb else 0
    if mode == "bf":
        a, b, prec = a.astype(BF16), b.astype(BF16), None
    else:
        prec = HI
    return lax.dot_general(a, b, (((ca,), (cb,)), ((), ())), preferred_element_type=F32, precision=prec)


@functools.partial(jax.custom_vjp, nondiff_argnums=(2, 3, 4))
def _mm(a, b, ta, tb, mode):
    return _dg(a, b, ta, tb, mode)


def _mm_fwd(a, b, ta, tb, mode):
    return _dg(a, b, ta, tb, mode), (a, b)


def _mm_bwd(ta, tb, mode, res, g):
    a, b = res
    da = _dg(g, b, False, not tb, mode) if not ta else _dg(b, g, tb, True, mode)
    db = _dg(a, g, not ta, False, mode) if not tb else _dg(g, a, True, ta, mode)
    return da, db


_mm.defvjp(_mm_fwd, _mm_bwd)


def _sigmoid(z):
    return 1.0 / (1.0 + jnp.exp(-z))


def _softplus(z):
    return jnp.maximum(z, 0.0) + jnp.log(1.0 + jnp.exp(-jnp.abs(z)))


def _silu(z):
    return z * _sigmoid(z)


def _rms(x, g):
    return x * lax.rsqrt(jnp.mean(x * x, axis=-1, keepdims=True) + RMS_EPS) * g


GROUPS = {"all": (1, 1, 1), "chips": (1, 1, 0), "cores": (0, 0, 1)}
D2D_SPLIT = 16
ICI_SPLIT = 4


def _exchange(x, name, gather, group="all", nsplit=1):
    flags = GROUPS[group]
    n = 2 ** sum(flags)
    blk = x.shape if gather else x.shape[1:]
    assert gather or x.shape[0] == n
    rows, width = 1, blk[-1]
    for d in blk[:-1]:
        rows *= d
    while nsplit > 1 and (rows % nsplit or (rows // nsplit) % 16):
        nsplit //= 2
    cs = rows // nsplit
    flips = [k for k in range(1, NDEV) if all(f or not (k >> s) & 1 for f, s in zip(flags, (2, 1, 0)))]

    def rank(coords):
        r = 0
        for f, cc in zip(flags, coords):
            if f:
                r = 2 * r + cc
        return r

    def body(x_ref, o_ref, send_sems, recv_sems, local_sems):
        ix, iy, ic = lax.axis_index("x"), lax.axis_index("y"), lax.axis_index("c")
        me = rank((ix, iy, ic))

        def src(dest, q):
            ref = x_ref if gather else x_ref.at[dest]
            return ref.at[pl.ds(q * cs, cs)]

        def dst(slot, q):
            return o_ref.at[slot, pl.ds(q * cs, cs)]

        mine = [pltpu.make_async_copy(src(me, q), dst(me, q), local_sems.at[q]) for q in range(nsplit)]
        for cp in mine:
            cp.start()
        copies = []
        for s, k in enumerate(flips):
            px = 1 - ix if (k >> 2) & 1 else ix
            py = 1 - iy if (k >> 1) & 1 else iy
            pc = 1 - ic if k & 1 else ic
            peer = rank((px, py, pc))
            for q in range(nsplit):
                sem = s * nsplit + q
                cp = pltpu.make_async_remote_copy(
                    src_ref=src(peer, q), dst_ref=dst(me, q), send_sem=send_sems.at[sem], recv_sem=recv_sems.at[sem],
                    device_id=(px, py, pc), device_id_type=pl.DeviceIdType.MESH)
                cp.start()
                copies.append((cp, peer, q, sem))
        for cp, peer, q, sem in copies:
            cp.wait_send()
            pltpu.make_async_remote_copy(
                src_ref=src(peer, q), dst_ref=dst(peer, q), send_sem=send_sems.at[sem], recv_sem=recv_sems.at[sem],
                device_id=(ix, iy, ic), device_id_type=pl.DeviceIdType.MESH).wait_recv()
        for cp in mine:
            cp.wait()

    xin = x.reshape((rows, width) if gather else (n, rows, width))
    out = _pcall(
        body, name=name,
        out_shape=jax.ShapeDtypeStruct((n, rows, width), x.dtype),
        in_specs=[pl.BlockSpec(memory_space=pltpu.HBM)],
        out_specs=pl.BlockSpec(memory_space=pltpu.HBM),
        scratch_shapes=[pltpu.SemaphoreType.DMA(((n - 1) * nsplit,)), pltpu.SemaphoreType.DMA(((n - 1) * nsplit,)),
                        pltpu.SemaphoreType.DMA((nsplit,))],
    )(xin)
    return out.reshape((n,) + tuple(blk))


def _gather_two_level(x, name):
    chips = _exchange(x, name + "_chips", True, "chips", ICI_SPLIT)
    both = _exchange(chips, name + "_cores", True, "cores", D2D_SPLIT)
    return jnp.swapaxes(both, 0, 1).reshape((NDEV,) + x.shape)


def _pair_add(parts, name):
    _, n, R, W = parts.shape
    tr = _tile(R, max(8, min(256, (1 << 18) // W)))

    def body(p_ref, o_ref):
        o_ref[...] = (p_ref[0] + p_ref[1]).astype(BF16)

    return _pcall(body, name=name, grid=(n, R // tr),
                  in_specs=[pl.BlockSpec((2, 1, tr, W), lambda c, i: (0, c, i, 0))],
                  out_specs=pl.BlockSpec((1, tr, W), lambda c, i: (c, i, 0)),
                  out_shape=jax.ShapeDtypeStruct((n, R, W), BF16),
                  compiler_params=_cp(("arbitrary", "arbitrary")))(parts)


def _reduce_scatter_two_level(g, name):
    _, R, W = g.shape
    by_core = jnp.swapaxes(g.reshape(4, 2, R, W), 0, 1)
    swapped = _exchange(by_core, name + "_cores", False, "cores", D2D_SPLIT)
    return _exchange(_pair_add(swapped, name + "_add"), name + "_chips", False, "chips", ICI_SPLIT)


MAX_FULL_K = 4096


def _matmul(a, b, ta, tb, name, out_dtype=F32, tm=1024, tn=1024, tk=3072):
    M, K = (a.shape[1], a.shape[0]) if ta else a.shape
    N = b.shape[0] if tb else b.shape[1]
    assert (b.shape[1] if tb else b.shape[0]) == K
    if K <= MAX_FULL_K:
        tk = K
    else:
        tm, tk = min(tm, 512), _tile(K, tk)
    tm, tn = _tile(M, tm), _tile(N, tn)
    nk = K // tk

    def body_full(a_ref, b_ref, o_ref):
        o_ref[...] = _dg(a_ref[...], b_ref[...], ta, tb, "bf").astype(o_ref.dtype)

    def body_acc(a_ref, b_ref, o_ref, acc):
        kk = pl.program_id(2)

        @pl.when(kk == 0)
        def _():
            acc[...] = jnp.zeros_like(acc)

        acc[...] += _dg(a_ref[...], b_ref[...], ta, tb, "bf")

        @pl.when(kk == nk - 1)
        def _():
            o_ref[...] = acc[...].astype(o_ref.dtype)

    a_spec = pl.BlockSpec((tk, tm), lambda i, j, k: (k, i)) if ta else pl.BlockSpec((tm, tk), lambda i, j, k: (i, k))
    b_spec = pl.BlockSpec((tn, tk), lambda i, j, k: (j, k)) if tb else pl.BlockSpec((tk, tn), lambda i, j, k: (k, j))
    return _pcall(
        body_full if nk == 1 else body_acc, name=name, grid=(M // tm, N // tn, nk),
        in_specs=[a_spec, b_spec], out_specs=pl.BlockSpec((tm, tn), lambda i, j, k: (i, j)),
        out_shape=jax.ShapeDtypeStruct((M, N), out_dtype),
        scratch_shapes=[] if nk == 1 else [pltpu.VMEM((tm, tn), F32)],
        compiler_params=_cp(("parallel", "parallel", "arbitrary")),
    )(a, b)


def _row_spec(tr, width, col_block=0):
    return pl.BlockSpec((tr, width), lambda i: (i, col_block))


def _full_spec(shape):
    nd = len(shape)
    return pl.BlockSpec(tuple(shape), lambda i: (0,) * nd)


def _rmsnorm_fwd(x, g, name, tr=256):
    R, D = x.shape
    tr = _tile(R, tr)

    def body(x_ref, g_ref, o_ref):
        o_ref[...] = _rms(x_ref[...], g_ref[...]).astype(BF16)

    return _pcall(body, name=name, grid=(R // tr,),
                  in_specs=[_row_spec(tr, D), _full_spec((1, D))], out_specs=_row_spec(tr, D),
                  out_shape=jax.ShapeDtypeStruct((R, D), BF16), compiler_params=_cp(("arbitrary",)))(x, g)


def _rmsnorm_bwd(x, g, dy, extra, name, tr=128):
    R, D = x.shape
    tr = _tile(R, tr)
    has_extra = extra is not None

    def body(*refs):
        if has_extra:
            x_ref, g_ref, dy_ref, e_ref, dx_ref, dg_ref = refs
        else:
            x_ref, g_ref, dy_ref, dx_ref, dg_ref = refs
        _, vjp = jax.vjp(_rms, x_ref[...], g_ref[...])
        dx, dg = vjp(dy_ref[...])
        dx_ref[...] = dx + e_ref[...] if has_extra else dx

        @pl.when(pl.program_id(0) == 0)
        def _():
            dg_ref[...] = jnp.zeros_like(dg_ref)

        dg_ref[...] += dg

    ins = [x, g, dy] + ([extra] if has_extra else [])
    specs = [_row_spec(tr, D), _full_spec((1, D)), _row_spec(tr, D)] + ([_row_spec(tr, D)] if has_extra else [])
    return _pcall(body, name=name, grid=(R // tr,), in_specs=specs,
                  out_specs=[_row_spec(tr, D), _full_spec((1, D))],
                  out_shape=[jax.ShapeDtypeStruct((R, D), F32), jax.ShapeDtypeStruct((1, D), F32)],
                  compiler_params=_cp(("arbitrary",)))(*ins)


def _head_indicator(C, hp):
    e = (jnp.arange(C)[:, None] // HEAD == jnp.arange(hp)[None, :]).astype(F32)
    return e, e.T


def _prep_fn(C, L, ps, prev, mu, w0, a0, k_k, k_a, wdu, wiu, E, ET):
    sh = ps + (prev - ps) * mu
    r, k, v = sh[:, :C], sh[:, C:2 * C], sh[:, 2 * C:3 * C]
    wl, al = sh[:, 3 * C:3 * C + L], sh[:, 3 * C + L:3 * C + 2 * L]
    wd = w0 + _mm(jnp.tanh(wl), wdu, False, False, "bf")
    w_pre = -_softplus(-wd) - 0.5
    lw = -jnp.exp(w_pre)
    alpha = _sigmoid(a0 + _mm(al, wiu, False, False, "bf"))
    kk = k * k_k
    ss = _mm(kk * kk, E, False, False, "hi")
    kk = kk * _mm(lax.rsqrt(jnp.maximum(ss, 1e-24)), ET, False, False, "hi")
    k_mod = k * (1.0 + (alpha - 1.0) * k_a)
    return r, lw, k_mod, v, -kk, kk * alpha


def _prep_fwd(p, prev, params, C, L, tr=128):
    T = p.shape[0]
    SH = 3 * C + 2 * L
    tr = _tile(T, tr)
    hp = params[-1].shape[0]

    def body(ps_ref, prev_ref, mu, w0, a0, kk_, ka_, wdu, wiu, E, ET, *outs):
        vals = _prep_fn(C, L, ps_ref[...], prev_ref[...], mu[...], w0[...], a0[...], kk_[...], ka_[...],
                        wdu[...], wiu[...], E[...], ET[...])
        for o, v in zip(outs, vals):
            o[...] = v

    pspecs = [_full_spec(a.shape) for a in params]
    return _pcall(body, name="rwkv_prep_fwd", grid=(T // tr,),
                  in_specs=[_row_spec(tr, SH), _row_spec(tr, SH)] + pspecs,
                  out_specs=[_row_spec(tr, C)] * 6,
                  out_shape=[jax.ShapeDtypeStruct((T, C), F32)] * 6,
                  compiler_params=_cp(("arbitrary",)))(p, prev, *params)


def _prep_bwd(p, prev, params, cts, C, L, tr=64):
    T = p.shape[0]
    SH = 3 * C + 2 * L
    tr = _tile(T, tr)
    nparam = 7

    def body(ps_ref, prev_ref, mu, w0, a0, kk_, ka_, wdu, wiu, E, ET, c0, c1, c2, c3, c4, c5, e0, e2, e3,
             dps_ref, dprev_ref, *dpar):
        f = functools.partial(_prep_fn, C, L)
        fe = lambda ps, prev, *par: f(ps, prev, *par, E[...], ET[...])
        _, vjp = jax.vjp(fe, ps_ref[...], prev_ref[...], mu[...], w0[...], a0[...], kk_[...], ka_[...], wdu[...], wiu[...])
        grads = vjp((c0[...] + e0[...], c1[...], c2[...] + e2[...], c3[...] + e3[...], c4[...], c5[...]))
        dps_ref[...] = grads[0]
        dprev_ref[...] = grads[1]

        @pl.when(pl.program_id(0) == 0)
        def _():
            for d in dpar:
                d[...] = jnp.zeros_like(d)

        for d, gval in zip(dpar, grads[2:]):
            d[...] += gval

    pspecs = [_full_spec(a.shape) for a in params]
    par_shapes = [a.shape for a in params[:nparam]]
    return _pcall(body, name="rwkv_prep_bwd", grid=(T // tr,),
                  in_specs=[_row_spec(tr, SH), _row_spec(tr, SH)] + pspecs + [_row_spec(tr, C)] * 9,
                  out_specs=[_row_spec(tr, SH), _row_spec(tr, SH)] + [_full_spec(s) for s in par_shapes],
                  out_shape=[jax.ShapeDtypeStruct((T, SH), F32)] * 2 + [jax.ShapeDtypeStruct(s, F32) for s in par_shapes],
                  compiler_params=_cp(("arbitrary",)))(p, prev, *params, *cts)


def _shift_combine(d_direct, d_prev_up, tr=256):
    T, W = d_direct.shape
    tr = _tile(T, tr)

    def body(a_ref, b_ref, o_ref):
        o_ref[...] = (a_ref[...] + b_ref[...]).astype(BF16)

    return _pcall(body, name="shift_combine", grid=(T // tr,),
                  in_specs=[_row_spec(tr, W)] * 2, out_specs=_row_spec(tr, W),
                  out_shape=jax.ShapeDtypeStruct((T, W), BF16), compiler_params=_cp(("arbitrary",)))(d_direct, d_prev_up)


def _mix_fn(y, r, kmod, v, g_rwkv, yfox, g_fox, ymem, g_mq, lnw, lnb, rk, E, ET):
    inv = 1.0 / HEAD
    mean = _mm(y, E, False, False, "hi") * inv
    yc = y - _mm(mean, ET, False, False, "hi")
    var = _mm(yc * yc, E, False, False, "hi") * inv
    yn = yc * _mm(lax.rsqrt(var + GN_EPS), ET, False, False, "hi") * lnw + lnb
    bonus = _mm(_mm(r * kmod * rk, E, False, False, "hi"), ET, False, False, "hi") * v
    o1 = (yn + bonus) * _silu(g_rwkv)
    return jnp.concatenate([o1, yfox * _silu(g_fox), ymem * _silu(g_mq)], axis=1)


def _mix_specs(tr, C, MW):
    return [_row_spec(tr, C)] * 7 + [_row_spec(tr, MW)] * 2


def _mix_fwd(acts, params, C, MW, tr=128):
    T = acts[0].shape[0]
    D = 2 * C + MW
    tr = _tile(T, tr)

    def body(y_, r_, k_, v_, g1, yf, g2, ym, g3, lnw, lnb, rk, E, ET, o_ref):
        o_ref[...] = _mix_fn(y_[...], r_[...], k_[...], v_[...], g1[...], yf[...], g2[...], ym[...], g3[...],
                             lnw[...], lnb[...], rk[...], E[...], ET[...]).astype(BF16)

    return _pcall(body, name="mix_fwd", grid=(T // tr,),
                  in_specs=_mix_specs(tr, C, MW) + [_full_spec(a.shape) for a in params],
                  out_specs=_row_spec(tr, D), out_shape=jax.ShapeDtypeStruct((T, D), BF16),
                  compiler_params=_cp(("arbitrary",)))(*acts, *params)


def _mix_bwd(acts, params, dycat, C, MW, tr=64):
    T = acts[0].shape[0]
    D = 2 * C + MW
    tr = _tile(T, tr)

    def body(y_, r_, k_, v_, g1, yf, g2, ym, g3, lnw, lnb, rk, E, ET, dy_ref, *outs):
        fe = lambda *a: _mix_fn(*a, E[...], ET[...])
        _, vjp = jax.vjp(fe, y_[...], r_[...], k_[...], v_[...], g1[...], yf[...], g2[...], ym[...], g3[...],
                         lnw[...], lnb[...], rk[...])
        grads = vjp(dy_ref[...])
        for o, gval in zip(outs[:9], grads[:9]):
            o[...] = gval

        @pl.when(pl.program_id(0) == 0)
        def _():
            for o in outs[9:]:
                o[...] = jnp.zeros_like(o)

        for o, gval in zip(outs[9:], grads[9:]):
            o[...] += gval

    widths = [C, C, C, C, C, C, C, MW, MW]
    return _pcall(body, name="mix_bwd", grid=(T // tr,),
                  in_specs=_mix_specs(tr, C, MW) + [_full_spec(a.shape) for a in params] + [_row_spec(tr, D)],
                  out_specs=[_row_spec(tr, w) for w in widths] + [_full_spec((1, C))] * 3,
                  out_shape=[jax.ShapeDtypeStruct((T, w), F32) for w in widths] + [jax.ShapeDtypeStruct((1, C), F32)] * 3,
                  compiler_params=_cp(("arbitrary",)))(*acts, *params, dycat)


def _post(yo, x, tgt, g_post, tr=128):
    T, D = x.shape
    tr = _tile(T, tr)

    def body(yo_ref, x_ref, t_ref, g_ref, dyo_ref, dout_ref, loss_ref, dg_ref):
        n, vjp = jax.vjp(_rms, yo_ref[...], g_ref[...])
        diff = (x_ref[...] + n) - t_ref[...]
        part = 0.5 * jnp.sum(jnp.mean(diff * diff, axis=-1, keepdims=True), axis=0, keepdims=True)
        d_out = diff * (1.0 / D)
        dyo, dg = vjp(d_out)
        dyo_ref[...] = dyo.astype(BF16)
        dout_ref[...] = d_out

        @pl.when(pl.program_id(0) == 0)
        def _():
            loss_ref[...] = jnp.zeros_like(loss_ref)
            dg_ref[...] = jnp.zeros_like(dg_ref)

        loss_ref[...] += jnp.broadcast_to(part, loss_ref.shape)
        dg_ref[...] += dg

    return _pcall(body, name="post_loss", grid=(T // tr,),
                  in_specs=[_row_spec(tr, D)] * 3 + [_full_spec((1, D))],
                  out_specs=[_row_spec(tr, D), _row_spec(tr, D), _full_spec((1, LANE)), _full_spec((1, D))],
                  out_shape=[jax.ShapeDtypeStruct((T, D), BF16), jax.ShapeDtypeStruct((T, D), F32),
                             jax.ShapeDtypeStruct((1, LANE), F32), jax.ShapeDtypeStruct((1, D), F32)],
                  compiler_params=_cp(("arbitrary",)))(yo, x, tgt, g_post)


def _memattn_fn(MW, q, mkv):
    hd = MW // MEM_HEADS
    scale = hd ** -0.5
    outs = []
    for h in range(MEM_HEADS):
        qh = q[:, h * hd:(h + 1) * hd]
        kh = mkv[:, h * hd:(h + 1) * hd]
        vh = mkv[:, MW + h * hd:MW + (h + 1) * hd]
        s = _mm(qh, kh, False, True, "bf") * scale
        e = jnp.exp(s - lax.stop_gradient(jnp.max(s, axis=-1, keepdims=True)))
        pr = e / jnp.sum(e, axis=-1, keepdims=True)
        outs.append(_mm(pr, vh, False, False, "bf"))
    return jnp.concatenate(outs, axis=1)


def _memattn_fwd(p, mkv, MW, tr=256):
    T = p.shape[0]
    tr = _tile(T, tr)

    def body(q_ref, kv_ref, o_ref):
        o_ref[...] = _memattn_fn(MW, q_ref[...], kv_ref[...])

    return _pcall(body, name="memattn_fwd", grid=(T // tr,),
                  in_specs=[_row_spec(tr, MW), _full_spec(mkv.shape)], out_specs=_row_spec(tr, MW),
                  out_shape=jax.ShapeDtypeStruct((T, MW), F32), compiler_params=_cp(("arbitrary",)))(p, mkv)


def _memattn_bwd(p, mkv, do, MW, tr=256):
    T = p.shape[0]
    tr = _tile(T, tr)

    def body(q_ref, kv_ref, do_ref, dq_ref, dkv_ref):
        _, vjp = jax.vjp(functools.partial(_memattn_fn, MW), q_ref[...], kv_ref[...])
        dq, dkv = vjp(do_ref[...])
        dq_ref[...] = dq

        @pl.when(pl.program_id(0) == 0)
        def _():
            dkv_ref[...] = jnp.zeros_like(dkv_ref)

        dkv_ref[...] += dkv

    return _pcall(body, name="memattn_bwd", grid=(T // tr,),
                  in_specs=[_row_spec(tr, MW), _full_spec(mkv.shape), _row_spec(tr, MW)],
                  out_specs=[_row_spec(tr, MW), _full_spec(mkv.shape)],
                  out_shape=[jax.ShapeDtypeStruct((T, MW), F32), jax.ShapeDtypeStruct(mkv.shape, F32)],
                  compiler_params=_cp(("arbitrary",)))(p, mkv, do)


def _fox_cum_fwd(p, b_f_pad, off, blk=512):
    T = p.shape[0]
    blk = _tile(T, blk)

    def body(f_ref, b_ref, cum_ref, cumt_ref, carry):
        @pl.when(pl.program_id(0) == 0)
        def _():
            carry[...] = jnp.zeros_like(carry)

        z = f_ref[...] + b_ref[...]
        logf = -_softplus(-z)
        row = lax.broadcasted_iota(jnp.int32, (blk, blk), 0)
        col = lax.broadcasted_iota(jnp.int32, (blk, blk), 1)
        tri = (col <= row).astype(F32)
        c = _dg(tri, logf, False, False, "hi") + carry[...]
        cum_ref[...] = c
        cumt_ref[...] = c.T
        carry[...] += jnp.sum(logf, axis=0, keepdims=True)

    return _pcall(body, name="fox_cum_fwd", grid=(T // blk,),
                  in_specs=[_row_spec(blk, LANE, off // LANE), _full_spec((1, LANE))],
                  out_specs=[_row_spec(blk, LANE), pl.BlockSpec((LANE, blk), lambda i: (0, i))],
                  out_shape=[jax.ShapeDtypeStruct((T, LANE), F32), jax.ShapeDtypeStruct((LANE, T), F32)],
                  scratch_shapes=[pltpu.VMEM((1, LANE), F32)], compiler_params=_cp(("arbitrary",)))(p, b_f_pad)


def _fox_cum_bwd(p, b_f_pad, dcum, off, blk=512):
    T = p.shape[0]
    blk = _tile(T, blk)
    nb = T // blk

    def body(f_ref, b_ref, dc_ref, df_ref, db_ref, carry):
        @pl.when(pl.program_id(0) == 0)
        def _():
            carry[...] = jnp.zeros_like(carry)
            db_ref[...] = jnp.zeros_like(db_ref)

        row = lax.broadcasted_iota(jnp.int32, (blk, blk), 0)
        col = lax.broadcasted_iota(jnp.int32, (blk, blk), 1)
        tri = (col >= row).astype(F32)
        dlogf = _dg(tri, dc_ref[...], False, False, "hi") + carry[...]
        carry[...] += jnp.sum(dc_ref[...], axis=0, keepdims=True)
        z = f_ref[...] + b_ref[...]
        dz = dlogf * (1.0 - _sigmoid(z))
        df_ref[...] = dz
        db_ref[...] += jnp.sum(dz, axis=0, keepdims=True)

    rev = lambda i: (nb - 1 - i, 0)
    return _pcall(body, name="fox_cum_bwd", grid=(nb,),
                  in_specs=[pl.BlockSpec((blk, LANE), lambda i: (nb - 1 - i, off // LANE)), _full_spec((1, LANE)),
                            pl.BlockSpec((blk, LANE), rev)],
                  out_specs=[pl.BlockSpec((blk, LANE), rev), _full_spec((1, LANE))],
                  out_shape=[jax.ShapeDtypeStruct((T, LANE), F32), jax.ShapeDtypeStruct((1, LANE), F32)],
                  scratch_shapes=[pltpu.VMEM((1, LANE), F32)], compiler_params=_cp(("arbitrary",)))(p, b_f_pad, dcum)


def _fox_masked_scores(q, k, cq, ck, q0, k0, tq, tk):
    s = _dg(q, k, False, True, "bf") * (HEAD ** -0.5) + cq - ck
    qpos = q0 + lax.broadcasted_iota(jnp.int32, (tq, tk), 0)
    kpos = k0 + lax.broadcasted_iota(jnp.int32, (tq, tk), 1)
    return jnp.where(kpos <= qpos, s, NEG)


def _fox_fwd(p, cq, ck, C, offs, tq=256):
    T = p.shape[0]
    tq = _tile(T, tq)
    npair = C // LANE
    cb = lambda name: offs[name] // LANE

    def body(q_ref, k_ref, v_ref, cq_ref, ck_ref, o_ref, lse_ref):
        i = pl.program_id(1)
        lse_ref[...] = jnp.zeros_like(lse_ref)
        for hh in range(2):
            sl = slice(hh * HEAD, (hh + 1) * HEAD)
            q = q_ref[:, sl].astype(BF16)
            cqv = cq_ref[0, :, hh:hh + 1]

            def step(j, carry):
                m, l, acc = carry
                off = pl.multiple_of(j * tq, tq)
                k = k_ref[pl.ds(off, tq), sl].astype(BF16)
                v = v_ref[pl.ds(off, tq), sl].astype(BF16)
                ckv = ck_ref[0, hh:hh + 1, pl.ds(off, tq)]
                s = _fox_masked_scores(q, k, cqv, ckv, i * tq, off, tq, tq)
                m_new = jnp.maximum(m, jnp.max(s, axis=-1, keepdims=True))
                pr = jnp.exp(s - m_new)
                al = jnp.exp(m - m_new)
                l = al * l + jnp.sum(pr, axis=-1, keepdims=True)
                acc = al * acc + _dg(pr, v, False, False, "bf")
                return m_new, l, acc

            m0 = jnp.full((tq, 1), NEG, F32)
            m, l, acc = lax.fori_loop(0, i + 1, step, (m0, jnp.zeros((tq, 1), F32), jnp.zeros((tq, HEAD), F32)))
            o_ref[:, sl] = acc / l
            lse_ref[0, :, hh:hh + 1] = m + jnp.log(l)

    return _pcall(body, name="fox_fwd", grid=(npair, T // tq),
                  in_specs=[pl.BlockSpec((tq, LANE), lambda h, i: (i, cb("fq") + h)),
                            pl.BlockSpec((T, LANE), lambda h, i: (0, cb("fk") + h)),
                            pl.BlockSpec((T, LANE), lambda h, i: (0, cb("fv") + h)),
                            pl.BlockSpec((1, tq, 8), lambda h, i: (h, i, 0)),
                            pl.BlockSpec((1, 8, T), lambda h, i: (h, 0, 0))],
                  out_specs=[pl.BlockSpec((tq, LANE), lambda h, i: (i, h)),
                             pl.BlockSpec((1, tq, 8), lambda h, i: (h, i, 0))],
                  out_shape=[jax.ShapeDtypeStruct((T, C), F32), jax.ShapeDtypeStruct((npair, T, 8), F32)],
                  compiler_params=_cp(("arbitrary", "arbitrary")))(p, p, p, cq, ck)


def _fox_delta(p, cq, ck, do, lse, C, offs, tq=256):
    T = p.shape[0]
    tq = _tile(T, tq)
    npair = C // LANE
    cb = lambda name: offs[name] // LANE

    def body(q_ref, k_ref, v_ref, cq_ref, ck_ref, do_ref, lse_ref, d_ref):
        i = pl.program_id(1)
        d_ref[...] = jnp.zeros_like(d_ref)
        for hh in range(2):
            sl = slice(hh * HEAD, (hh + 1) * HEAD)
            q = q_ref[:, sl].astype(BF16)
            dov = do_ref[:, sl].astype(BF16)
            cqv = cq_ref[0, :, hh:hh + 1]
            lsev = lse_ref[0, :, hh:hh + 1]

            def step(j, acc):
                off = pl.multiple_of(j * tq, tq)
                k = k_ref[pl.ds(off, tq), sl].astype(BF16)
                v = v_ref[pl.ds(off, tq), sl].astype(BF16)
                ckv = ck_ref[0, hh:hh + 1, pl.ds(off, tq)]
                s = _fox_masked_scores(q, k, cqv, ckv, i * tq, off, tq, tq)
                pr = jnp.exp(s - lsev)
                dp = _dg(dov, v, False, True, "bf")
                return acc + jnp.sum(pr * dp, axis=-1, keepdims=True)

            d_ref[0, :, hh:hh + 1] = lax.fori_loop(0, i + 1, step, jnp.zeros((tq, 1), F32))

    return _pcall(body, name="fox_delta", grid=(npair, T // tq),
                  in_specs=[pl.BlockSpec((tq, LANE), lambda h, i: (i, cb("fq") + h)),
                            pl.BlockSpec((T, LANE), lambda h, i: (0, cb("fk") + h)),
                            pl.BlockSpec((T, LANE), lambda h, i: (0, cb("fv") + h)),
                            pl.BlockSpec((1, tq, 8), lambda h, i: (h, i, 0)),
                            pl.BlockSpec((1, 8, T), lambda h, i: (h, 0, 0)),
                            pl.BlockSpec((tq, LANE), lambda h, i: (i, h)),
                            pl.BlockSpec((1, tq, 8), lambda h, i: (h, i, 0))],
                  out_specs=pl.BlockSpec((1, tq, 8), lambda h, i: (h, i, 0)),
                  out_shape=jax.ShapeDtypeStruct((npair, T, 8), F32),
                  compiler_params=_cp(("arbitrary", "arbitrary")))(p, p, p, cq, ck, do, lse)


def _fox_bwd(p, cq, ck, delta, do, lse, C, offs, tk=256):
    T = p.shape[0]
    tk = _tile(T, tk)
    nblk = T // tk
    npair = C // LANE
    cb = lambda name: offs[name] // LANE
    scale = HEAD ** -0.5

    def body(q_ref, k_ref, v_ref, cq_ref, ck_ref, dl_ref, do_ref, lse_ref, dq_ref, dk_ref, dv_ref, dck_ref):
        j = pl.program_id(1)

        @pl.when(j == 0)
        def _():
            dq_ref[...] = jnp.zeros_like(dq_ref)

        dck_ref[...] = jnp.zeros_like(dck_ref)
        for hh in range(2):
            sl = slice(hh * HEAD, (hh + 1) * HEAD)
            k = k_ref[:, sl].astype(BF16)
            v = v_ref[:, sl].astype(BF16)
            ckv = ck_ref[0, hh:hh + 1, :]

            def step(i, carry):
                dk, dv, dck = carry
                off = pl.multiple_of(i * tk, tk)
                q = q_ref[pl.ds(off, tk), sl].astype(BF16)
                dov = do_ref[pl.ds(off, tk), sl]
                cqv = cq_ref[0, pl.ds(off, tk), hh:hh + 1]
                lsev = lse_ref[0, pl.ds(off, tk), hh:hh + 1]
                s = _fox_masked_scores(q, k, cqv, ckv, off, j * tk, tk, tk)
                pr = jnp.exp(s - lsev)
                dv = dv + _dg(pr, dov, True, False, "bf")
                dp = _dg(dov, v, False, True, "bf")
                ds = pr * (dp - dl_ref[0, pl.ds(off, tk), hh:hh + 1])
                dk = dk + _dg(ds, q, True, False, "bf") * scale
                dq_ref[pl.ds(off, tk), sl] += _dg(ds, k, False, False, "bf") * scale
                dck = dck - jnp.sum(ds, axis=0, keepdims=True)
                return dk, dv, dck

            z = jnp.zeros((tk, HEAD), F32)
            dk, dv, dck = lax.fori_loop(j, nblk, step, (z, z, jnp.zeros((1, tk), F32)))
            dk_ref[:, sl] = dk
            dv_ref[:, sl] = dv
            dck_ref[0, hh:hh + 1, :] = dck

    full = lambda h, j: (0, h)
    return _pcall(body, name="fox_bwd", grid=(npair, nblk),
                  in_specs=[pl.BlockSpec((T, LANE), lambda h, j: (0, cb("fq") + h)),
                            pl.BlockSpec((tk, LANE), lambda h, j: (j, cb("fk") + h)),
                            pl.BlockSpec((tk, LANE), lambda h, j: (j, cb("fv") + h)),
                            pl.BlockSpec((1, T, 8), lambda h, j: (h, 0, 0)),
                            pl.BlockSpec((1, 8, tk), lambda h, j: (h, 0, j)),
                            pl.BlockSpec((1, T, 8), lambda h, j: (h, 0, 0)), pl.BlockSpec((T, LANE), full),
                            pl.BlockSpec((1, T, 8), lambda h, j: (h, 0, 0))],
                  out_specs=[pl.BlockSpec((T, LANE), full),
                             pl.BlockSpec((tk, LANE), lambda h, j: (j, h)),
                             pl.BlockSpec((tk, LANE), lambda h, j: (j, h)),
                             pl.BlockSpec((1, 8, tk), lambda h, j: (h, 0, j))],
                  out_shape=[jax.ShapeDtypeStruct((T, C), F32)] * 3 + [jax.ShapeDtypeStruct((npair, 8, T), F32)],
                  compiler_params=_cp(("arbitrary", "arbitrary")))(p, p, p, cq, ck, delta, do, lse)


def _chunk_fn(S0, r, lw, k, v, a, b):
    n = r.shape[0]
    row = lax.broadcasted_iota(jnp.int32, (n, n), 0)
    col = lax.broadcasted_iota(jnp.int32, (n, n), 1)
    incl, strict = col <= row, col < row
    mm = lambda x, y, ta=False, tb=False: _mm(x, y, ta, tb, "hi")
    g = mm(incl.astype(F32), lw)
    einv = jnp.exp(-g)
    rt, at, bt, kt = r * jnp.exp(g), a * jnp.exp(g - lw), b * einv, k * einv
    a_ab = jnp.where(strict, mm(at, bt, tb=True), 0.0)
    a_ak = jnp.where(strict, mm(at, kt, tb=True), 0.0)
    r_b = jnp.where(incl, mm(rt, bt, tb=True), 0.0)
    r_k = jnp.where(incl, mm(rt, kt, tb=True), 0.0)
    u = mm(at, S0, tb=True) + mm(a_ak, v)
    pw = a_ab
    for it in range(6):
        u = u + mm(pw, u)
        if it < 5:
            pw = mm(pw, pw)
    y = mm(rt, S0, tb=True) + mm(r_b, u) + mm(r_k, v)
    g_end = jnp.sum(lw, axis=0, keepdims=True)
    s_end = (S0 + mm(u, bt, ta=True) + mm(v, kt, ta=True)) * jnp.exp(g_end)
    return y, s_end


def _scan_fwd(r, lw, k, v, a, b, tc=256):
    T, C = r.shape
    tc = _tile(T, tc)
    ncs = tc // CHUNK
    npair = C // LANE

    def body(r_ref, lw_ref, k_ref, v_ref, a_ref, b_ref, y_ref, ck_ref, state):
        @pl.when(pl.program_id(1) == 0)
        def _():
            state[...] = jnp.zeros_like(state)

        for c in range(ncs):
            rows = slice(c * CHUNK, (c + 1) * CHUNK)
            ck_ref[0, c] = state[...]
            for hh in range(2):
                sl = slice(hh * HEAD, (hh + 1) * HEAD)
                y, s_end = _chunk_fn(state[:, sl], r_ref[rows, sl], lw_ref[rows, sl], k_ref[rows, sl],
                                     v_ref[rows, sl], a_ref[rows, sl], b_ref[rows, sl])
                y_ref[rows, sl] = y
                state[:, sl] = s_end

    spec = pl.BlockSpec((tc, LANE), lambda h, t: (t, h))
    return _pcall(body, name="rwkv_scan_fwd", grid=(npair, T // tc),
                  in_specs=[spec] * 6,
                  out_specs=[spec, pl.BlockSpec((1, ncs, HEAD, LANE), lambda h, t: (h, t, 0, 0))],
                  out_shape=[jax.ShapeDtypeStruct((T, C), F32),
                             jax.ShapeDtypeStruct((npair, T // CHUNK, HEAD, LANE), F32)],
                  scratch_shapes=[pltpu.VMEM((HEAD, LANE), F32)],
                  compiler_params=_cp(("arbitrary", "arbitrary")))(r, lw, k, v, a, b)


def _scan_bwd(r, lw, k, v, a, b, ckpt, dy, tc=256):
    T, C = r.shape
    tc = _tile(T, tc)
    ncs = tc // CHUNK
    npair = C // LANE
    nt = T // tc

    def body(r_ref, lw_ref, k_ref, v_ref, a_ref, b_ref, ck_ref, dy_ref, dr, dlw, dk, dv, da, db, dstate):
        @pl.when(pl.program_id(1) == 0)
        def _():
            dstate[...] = jnp.zeros_like(dstate)

        outs = (dr, dlw, dk, dv, da, db)
        for c in reversed(range(ncs)):
            rows = slice(c * CHUNK, (c + 1) * CHUNK)
            for hh in range(2):
                sl = slice(hh * HEAD, (hh + 1) * HEAD)
                _, vjp = jax.vjp(_chunk_fn, ck_ref[0, c, :, sl], r_ref[rows, sl], lw_ref[rows, sl], k_ref[rows, sl],
                                 v_ref[rows, sl], a_ref[rows, sl], b_ref[rows, sl])
                grads = vjp((dy_ref[rows, sl], dstate[:, sl]))
                dstate[:, sl] = grads[0]
                for o, gval in zip(outs, grads[1:]):
                    o[rows, sl] = gval

    spec = pl.BlockSpec((tc, LANE), lambda h, t: (nt - 1 - t, h))
    return _pcall(body, name="rwkv_scan_bwd", grid=(npair, nt),
                  in_specs=[spec] * 6 + [pl.BlockSpec((1, ncs, HEAD, LANE), lambda h, t: (h, nt - 1 - t, 0, 0)), spec],
                  out_specs=[spec] * 6, out_shape=[jax.ShapeDtypeStruct((T, C), F32)] * 6,
                  scratch_shapes=[pltpu.VMEM((HEAD, LANE), F32)],
                  compiler_params=_cp(("arbitrary", "arbitrary")))(r, lw, k, v, a, b, ckpt, dy)


def _adam(w, g, m, v):
    m = ADAM_B1 * m + (1.0 - ADAM_B1) * g
    v = ADAM_B2 * v + (1.0 - ADAM_B2) * (g * g)
    m_hat = m / (1.0 - ADAM_B1 ** ADAM_STEP)
    v_hat = v / (1.0 - ADAM_B2 ** ADAM_STEP)
    return -ADAM_LR * (m_hat / (jnp.sqrt(v_hat) + ADAM_EPS) + ADAM_WD * w), m, v


def _sum_adam(parts, w, m, v, name):
    n, R, W = parts.shape
    tr = _tile(R, max(8, min(256, (1 << 20) // (n * W))))

    def body(p_ref, w_ref, m_ref, v_ref, g_ref, d_ref, nm_ref, nv_ref):
        g = p_ref[0].astype(F32)
        for s in range(1, n):
            g = g + p_ref[s].astype(F32)
        d, nm, nv = _adam(w_ref[...], g, m_ref[...], v_ref[...])
        g_ref[...] = g
        d_ref[...] = d
        nm_ref[...] = nm
        nv_ref[...] = nv

    return _pcall(body, name=name, grid=(R // tr,),
                  in_specs=[pl.BlockSpec((n, tr, W), lambda i: (0, i, 0))] + [_row_spec(tr, W)] * 3,
                  out_specs=[_row_spec(tr, W)] * 4, out_shape=[jax.ShapeDtypeStruct((R, W), F32)] * 4,
                  compiler_params=_cp(("arbitrary",)))(parts, w, m, v)


def _pad_lanes(vec, width):
    return jnp.pad(vec, ((0, 0), (0, width - vec.shape[1])))


def kernel(x, mem, g_pre, w_in, mu_rwkv, w0, w_decay_up, a0, w_iclr_up, k_k, k_a, r_k, ln_x_w, ln_x_b, b_f, g_mem, w_mem_kv, w_out, g_post, loss_target, m_g_pre, m_w_in, m_mu_rwkv, m_w0, m_w_decay_up, m_a0, m_w_iclr_up, m_k_k, m_k_a, m_r_k, m_ln_x_w, m_ln_x_b, m_b_f, m_g_mem, m_w_mem_kv, m_w_out, m_g_post, v_g_pre, v_w_in, v_mu_rwkv, v_w0, v_w_decay_up, v_a0, v_w_iclr_up, v_k_k, v_k_a, v_r_k, v_ln_x_w, v_ln_x_b, v_b_f, v_g_mem, v_w_mem_kv, v_w_out, v_g_post):
    T, D = x.shape[1], x.shape[2]
    C = w0.shape[1]
    L = w_decay_up.shape[1]
    H = C // HEAD
    MW = w_mem_kv.shape[2] // 2
    SH = 3 * C + 2 * L
    IN = NDEV * w_in.shape[2]
    assert IN == SH + 5 * C + H + 2 * MW and D == 2 * C + MW and H % 2 == 0 and H <= LANE
    assert C % LANE == 0 and L % LANE == 0 and MW % (MEM_HEADS * HEAD) == 0 and T % CHUNK == 0
    offs = dict(grw=SH, fq=SH + C, fk=SH + 2 * C, fv=SH + 3 * C, gfx=SH + 4 * C, mq=SH + 5 * C, gmq=SH + 5 * C + MW,
                fl=SH + 5 * C + 2 * MW)
    NI = -(-(offs["fl"] + LANE) // 512) * 512
    l_fl = SH + 4 * C

    x2, mem2, tgt2 = x[0], mem[0], loss_target[0]

    wg = _gather_two_level(w_in[0].astype(BF16), "gather_w_in")
    w_full = jnp.transpose(wg, (1, 0, 2)).reshape(D, IN)
    w_perm = jnp.concatenate([w_full[:, :l_fl], w_full[:, l_fl + H:], w_full[:, l_fl:l_fl + H],
                              jnp.zeros((D, NI - IN), BF16)], axis=1)
    w_out_f = _gather_two_level(w_out[0].astype(BF16), "gather_w_out").reshape(D, D)
    w_kv_f = _gather_two_level(w_mem_kv[0].astype(BF16), "gather_w_mem_kv").reshape(D, 2 * MW)
    lora = _exchange(jnp.concatenate([w_decay_up[0], w_iclr_up[0]], axis=0), "gather_lora", True)
    lora = jnp.transpose(lora, (1, 0, 2)).reshape(2 * L, C)
    wdu_f, wiu_f = lora[:L], lora[L:]

    E, ET = _head_indicator(C, LANE)
    prep_params = [mu_rwkv, w0, a0, k_k, k_a, wdu_f, wiu_f, E, ET]
    mix_params = [ln_x_w, ln_x_b, r_k.reshape(1, C), E, ET]
    b_f_pad = _pad_lanes(b_f, LANE)

    h = _rmsnorm_fwd(x2, g_pre, "rmsnorm_pre")
    p = _matmul(h, w_perm, False, False, "in_proj")
    ps = p[:, :SH]
    col = lambda name, w: p[:, offs[name]:offs[name] + w]
    g_rwkv, g_fox, mq, g_mq = col("grw", C), col("gfx", C), col("mq", MW), col("gmq", MW)
    prev = jnp.concatenate([jnp.zeros((1, SH), F32), ps[:-1]], axis=0)
    r, lw, kmod, v, a, b = _prep_fwd(p, prev, prep_params, C, L)
    y_scan, ckpt = _scan_fwd(r, lw, kmod, v, a, b)

    cum, cum_t = _fox_cum_fwd(p, b_f_pad, offs["fl"])
    ck = jnp.pad(cum_t[:H].reshape(H // 2, 2, T), ((0, 0), (0, 6), (0, 0)))
    cq = jnp.pad(cum[:, :H].reshape(T, H // 2, 2).transpose(1, 0, 2), ((0, 0), (0, 0), (0, 6)))
    y_fox, lse = _fox_fwd(p, cq, ck, C, offs)

    memn = _rmsnorm_fwd(mem2, g_mem, "rmsnorm_mem")
    mkv = _matmul(memn, w_kv_f, False, False, "mem_kv_proj")
    y_mem = _memattn_fwd(mq, mkv, MW)

    acts = [y_scan, r, kmod, v, g_rwkv, y_fox, g_fox, y_mem, g_mq]
    ycat = _mix_fwd(acts, mix_params, C, MW)
    yo = _matmul(ycat, w_out_f, False, False, "out_proj")
    d_yo, d_out, loss_part, dg_post = _post(yo, x2, tgt2, g_post)
    loss = lax.psum(loss_part[0, 0], AXES)

    g_w_out = _matmul(ycat, d_yo, True, False, "grad_w_out")
    d_ycat = _matmul(d_yo, w_out_f, False, True, "d_ycat")
    (d_y, d_r1, d_k1, d_v1, d_grw, d_yfox, d_gfx, d_ymem, d_gmq, dg_lnw, dg_lnb, dg_rk) = _mix_bwd(
        acts, mix_params, d_ycat, C, MW)

    d_mq, d_mkv = _memattn_bwd(mq, mkv, d_ymem, MW)
    g_w_kv = _matmul(memn, d_mkv, True, False, "grad_w_mem_kv")
    d_memn = _matmul(d_mkv, w_kv_f, False, True, "d_memn")
    _, dg_mem = _rmsnorm_bwd(mem2, g_mem, d_memn, None, "rmsnorm_mem_bwd")

    delta = _fox_delta(p, cq, ck, d_yfox, lse, C, offs)
    d_fq, d_fk, d_fv, d_ck = _fox_bwd(p, cq, ck, delta, d_yfox, lse, C, offs)
    d_cum = _pad_lanes(d_ck[:, :2, :].reshape(H, T).T, LANE)
    d_fl, dg_bf = _fox_cum_bwd(p, b_f_pad, d_cum, offs["fl"])

    d_r, d_lw, d_k, d_v, d_a, d_b = _scan_bwd(r, lw, kmod, v, a, b, ckpt, d_y)
    cts = [d_r, d_lw, d_k, d_v, d_a, d_b, d_r1, d_k1, d_v1]
    (d_ps, d_prev, dg_mu, dg_w0, dg_a0, dg_kk, dg_ka, dg_wdu, dg_wiu) = _prep_bwd(p, prev, prep_params, cts, C, L)
    d_prev_up = jnp.concatenate([d_prev[1:], jnp.zeros((1, SH), F32)], axis=0)
    d_sh = _shift_combine(d_ps, d_prev_up)

    tobf = lambda z: z.astype(BF16)
    dp = jnp.concatenate([d_sh, tobf(d_grw), tobf(d_fq), tobf(d_fk), tobf(d_fv), tobf(d_gfx), tobf(d_mq), tobf(d_gmq),
                          tobf(d_fl), jnp.zeros((T, NI - offs["fl"] - LANE), BF16)], axis=1)
    g_w_perm = _matmul(h, dp, True, False, "grad_w_in")
    d_h = _matmul(dp, w_perm, False, True, "d_h")
    grad_x, dg_pre = _rmsnorm_bwd(x2, g_pre, d_h, d_out, "rmsnorm_pre_bwd")

    g_w_log = jnp.concatenate([g_w_perm[:, :l_fl], g_w_perm[:, offs["fl"]:offs["fl"] + H],
                               g_w_perm[:, l_fl:offs["fl"]]], axis=1)
    parts_in = _reduce_scatter_two_level(jnp.transpose(g_w_log.reshape(D, NDEV, IN // NDEV), (1, 0, 2)), "scatter_grad_w_in")
    parts_out = _reduce_scatter_two_level(g_w_out.reshape(NDEV, D // NDEV, D), "scatter_grad_w_out")
    parts_kv = _reduce_scatter_two_level(g_w_kv.reshape(NDEV, D // NDEV, 2 * MW), "scatter_grad_w_mem_kv")
    g_lora = jnp.concatenate([dg_wdu, dg_wiu], axis=0)
    parts_lora = _exchange(jnp.transpose(g_lora.reshape(2 * L, NDEV, C // NDEV), (1, 0, 2)), "scatter_grad_lora", False)

    gw_in, dw_in, nm_w_in, nv_w_in = _sum_adam(parts_in, w_in[0], m_w_in[0], v_w_in[0], "adam_w_in")
    gw_out, dw_out, nm_w_out, nv_w_out = _sum_adam(parts_out, w_out[0], m_w_out[0], v_w_out[0], "adam_w_out")
    gw_kv, dw_kv, nm_w_kv, nv_w_kv = _sum_adam(parts_kv, w_mem_kv[0], m_w_mem_kv[0], v_w_mem_kv[0], "adam_w_mem_kv")
    cat2 = lambda u, w_: jnp.concatenate([u[0], w_[0]], axis=0)
    lora_res = _sum_adam(parts_lora, cat2(w_decay_up, w_iclr_up), cat2(m_w_decay_up, m_w_iclr_up),
                         cat2(v_w_decay_up, v_w_iclr_up), "adam_lora")

    small = [("g_pre", g_pre, m_g_pre, v_g_pre, dg_pre), ("mu_rwkv", mu_rwkv, m_mu_rwkv, v_mu_rwkv, dg_mu),
             ("w0", w0, m_w0, v_w0, dg_w0), ("a0", a0, m_a0, v_a0, dg_a0), ("k_k", k_k, m_k_k, v_k_k, dg_kk),
             ("k_a", k_a, m_k_a, v_k_a, dg_ka), ("r_k", r_k.reshape(1, C), m_r_k.reshape(1, C), v_r_k.reshape(1, C), dg_rk),
             ("ln_x_w", ln_x_w, m_ln_x_w, v_ln_x_w, dg_lnw), ("ln_x_b", ln_x_b, m_ln_x_b, v_ln_x_b, dg_lnb),
             ("b_f", _pad_lanes(b_f, LANE), _pad_lanes(m_b_f, LANE), _pad_lanes(v_b_f, LANE), dg_bf),
             ("g_mem", g_mem, m_g_mem, v_g_mem, dg_mem), ("g_post", g_post, m_g_post, v_g_post, dg_post)]
    widths = [s[1].shape[1] for s in small]
    pack = lambda idx: jnp.concatenate([s[idx] for s in small], axis=1).reshape(-1, LANE)
    parts_small = _exchange(pack(4), "gather_small_grads", True)
    res_small = _sum_adam(parts_small, pack(1), pack(2), pack(3), "adam_small")

    def unpack(flat):
        flat = flat.reshape(1, -1)
        out, o = {}, 0
        for (name, *_), wd in zip(small, widths):
            out[name] = flat[:, o:o + wd]
            o += wd
        out["b_f"] = out["b_f"][:, :H]
        out["r_k"] = out["r_k"].reshape(1, H, HEAD)
        return out

    sg, sd, sm, sv = [unpack(z) for z in res_small]
    big = {"w_in": (gw_in, dw_in, nm_w_in, nv_w_in), "w_out": (gw_out, dw_out, nm_w_out, nv_w_out),
           "w_mem_kv": (gw_kv, dw_kv, nm_w_kv, nv_w_kv),
           "w_decay_up": tuple(z[:L] for z in lora_res), "w_iclr_up": tuple(z[L:] for z in lora_res)}
    order = ["g_pre", "w_in", "mu_rwkv", "w0", "w_decay_up", "a0", "w_iclr_up", "k_k", "k_a", "r_k", "ln_x_w", "ln_x_b",
             "b_f", "g_mem", "w_mem_kv", "w_out", "g_post"]

    def pick(name, idx):
        if name in big:
            return big[name][idx][None]
        return (sg, sd, sm, sv)[idx][name]

    outs = [loss, grad_x[None]]
    for idx in range(4):
        outs += [pick(n, idx) for n in order]
    return tuple(outs)
```

```python
import functools

import jax
import jax.numpy as jnp
from jax import lax
from jax.experimental import pallas as pl
from jax.experimental.pallas import tpu as pltpu

F32, BF16 = jnp.float32, jnp.bfloat16
HI = lax.Precision.HIGHEST
NDEV = 8
AXES = ("x", "y", "c")
HEAD = 64
CHUNK = 64
MEM_HEADS = 4
LANE = 128
RMS_EPS = 1e-6
GN_EPS = 64e-5
NEG = -1e30
ADAM_LR, ADAM_B1, ADAM_B2, ADAM_EPS, ADAM_WD, ADAM_STEP = 0.001, 0.9, 0.999, 1e-08, 0.01, 10
VMEM_LIMIT = 56 * 1024 * 1024


def _pcall(body, **kw):
    return pl.pallas_call(body, **kw)


def _cp(sem=None, vmem=VMEM_LIMIT):
    return pltpu.CompilerParams(dimension_semantics=sem, vmem_limit_bytes=vmem)


def _tile(n, pref):
    for t in (pref, 1024, 512, 256, 128, 64, 32, 16, 8):
        if t <= pref and n % t == 0:
            return t
    return n


def _dg(a, b, ta, tb, mode):
    nb = a.ndim - 2
    ca = nb + (0 if ta else 1)
    cb = nb + (1 if tb else 0)
    dims = (((ca,), (cb,)), (tuple(range(nb)), tuple(range(nb))))
    if mode == "x3":
        a_hi, b_hi = a.astype(BF16), b.astype(BF16)
        a_lo, b_lo = (a - a_hi.astype(F32)).astype(BF16), (b - b_hi.astype(F32)).astype(BF16)
        dot = lambda u, w: lax.dot_general(u, w, dims, preferred_element_type=F32)
        return dot(a_hi, b_hi) + (dot(a_hi, b_lo) + dot(a_lo, b_hi))
    if mode == "bf":
        a, b, prec = a.astype(BF16), b.astype(BF16), None
    else:
        prec = HI
    return lax.dot_general(a, b, dims, preferred_element_type=F32, precision=prec)


@functools.partial(jax.custom_vjp, nondiff_argnums=(2, 3, 4))
def _mm(a, b, ta, tb, mode):
    return _dg(a, b, ta, tb, mode)


def _mm_fwd(a, b, ta, tb, mode):
    return _dg(a, b, ta, tb, mode), (a, b)


def _mm_bwd(ta, tb, mode, res, g):
    a, b = res
    da = _dg(g, b, False, not tb, mode) if not ta else _dg(b, g, tb, True, mode)
    db = _dg(a, g, not ta, False, mode) if not tb else _dg(g, a, True, ta, mode)
    return da, db


_mm.defvjp(_mm_fwd, _mm_bwd)


def _sigmoid(z):
    return 1.0 / (1.0 + jnp.exp(-z))


def _softplus(z):
    return jnp.maximum(z, 0.0) + jnp.log(1.0 + jnp.exp(-jnp.abs(z)))


def _silu(z):
    return z * _sigmoid(z)


def _rms(x, g):
    return x * lax.rsqrt(jnp.mean(x * x, axis=-1, keepdims=True) + RMS_EPS) * g


def _exchange(x, name, gather):
    blk = x.shape if gather else x.shape[1:]

    def body(x_ref, o_ref, send_sems, recv_sems, local_sem):
        ix, iy, ic = lax.axis_index("x"), lax.axis_index("y"), lax.axis_index("c")
        me = 4 * ix + 2 * iy + ic

        def src(dest):
            return x_ref if gather else x_ref.at[dest]

        mine = pltpu.make_async_copy(src(me), o_ref.at[me], local_sem)
        mine.start()
        copies = []
        for k in range(1, NDEV):
            px = 1 - ix if (k >> 2) & 1 else ix
            py = 1 - iy if (k >> 1) & 1 else iy
            pc = 1 - ic if k & 1 else ic
            peer = 4 * px + 2 * py + pc
            cp = pltpu.make_async_remote_copy(
                src_ref=src(peer), dst_ref=o_ref.at[me], send_sem=send_sems.at[k - 1], recv_sem=recv_sems.at[k - 1],
                device_id=(px, py, pc), device_id_type=pl.DeviceIdType.MESH)
            cp.start()
            copies.append((cp, peer))
        for k, (cp, peer) in enumerate(copies):
            cp.wait_send()
            pltpu.make_async_remote_copy(
                src_ref=src(peer), dst_ref=o_ref.at[peer], send_sem=send_sems.at[k], recv_sem=recv_sems.at[k],
                device_id=(ix, iy, ic), device_id_type=pl.DeviceIdType.MESH).wait_recv()
        mine.wait()

    return _pcall(
        body, name=name,
        out_shape=jax.ShapeDtypeStruct((NDEV,) + tuple(blk), x.dtype),
        in_specs=[pl.BlockSpec(memory_space=pltpu.HBM)],
        out_specs=pl.BlockSpec(memory_space=pltpu.HBM),
        scratch_shapes=[pltpu.SemaphoreType.DMA((NDEV - 1,)), pltpu.SemaphoreType.DMA((NDEV - 1,)),
                        pltpu.SemaphoreType.DMA(())],
    )(x)


MAX_FULL_K = 4096


def _matmul(a, b, ta, tb, name, out_dtype=F32, tm=1024, tn=1024, tk=3072):
    M, K = (a.shape[1], a.shape[0]) if ta else a.shape
    N = b.shape[0] if tb else b.shape[1]
    assert (b.shape[1] if tb else b.shape[0]) == K
    if K <= MAX_FULL_K:
        tk = K
    else:
        tm, tk = min(tm, 512), _tile(K, tk)
    tm, tn = _tile(M, tm), _tile(N, tn)
    nk = K // tk

    def body_full(a_ref, b_ref, o_ref):
        o_ref[...] = _dg(a_ref[...], b_ref[...], ta, tb, "bf").astype(o_ref.dtype)

    def body_acc(a_ref, b_ref, o_ref, acc):
        kk = pl.program_id(2)

        @pl.when(kk == 0)
        def _():
            acc[...] = jnp.zeros_like(acc)

        acc[...] += _dg(a_ref[...], b_ref[...], ta, tb, "bf")

        @pl.when(kk == nk - 1)
        def _():
            o_ref[...] = acc[...].astype(o_ref.dtype)

    a_spec = pl.BlockSpec((tk, tm), lambda i, j, k: (k, i)) if ta else pl.BlockSpec((tm, tk), lambda i, j, k: (i, k))
    b_spec = pl.BlockSpec((tn, tk), lambda i, j, k: (j, k)) if tb else pl.BlockSpec((tk, tn), lambda i, j, k: (k, j))
    return _pcall(
        body_full if nk == 1 else body_acc, name=name, grid=(M // tm, N // tn, nk),
        in_specs=[a_spec, b_spec], out_specs=pl.BlockSpec((tm, tn), lambda i, j, k: (i, j)),
        out_shape=jax.ShapeDtypeStruct((M, N), out_dtype),
        scratch_shapes=[] if nk == 1 else [pltpu.VMEM((tm, tn), F32)],
        compiler_params=_cp(("parallel", "parallel", "arbitrary")),
    )(a, b)


def _row_spec(tr, width, col_block=0):
    return pl.BlockSpec((tr, width), lambda i: (i, col_block))


def _full_spec(shape):
    nd = len(shape)
    return pl.BlockSpec(tuple(shape), lambda i: (0,) * nd)


def _rmsnorm_fwd(x, g, name, tr=256):
    R, D = x.shape
    tr = _tile(R, tr)

    def body(x_ref, g_ref, o_ref):
        o_ref[...] = _rms(x_ref[...], g_ref[...]).astype(BF16)

    return _pcall(body, name=name, grid=(R // tr,),
                  in_specs=[_row_spec(tr, D), _full_spec((1, D))], out_specs=_row_spec(tr, D),
                  out_shape=jax.ShapeDtypeStruct((R, D), BF16), compiler_params=_cp(("arbitrary",)))(x, g)


def _rmsnorm_bwd(x, g, dy, extra, name, tr=128):
    R, D = x.shape
    tr = _tile(R, tr)
    has_extra = extra is not None

    def body(*refs):
        if has_extra:
            x_ref, g_ref, dy_ref, e_ref, dx_ref, dg_ref = refs
        else:
            x_ref, g_ref, dy_ref, dx_ref, dg_ref = refs
        _, vjp = jax.vjp(_rms, x_ref[...], g_ref[...])
        dx, dg = vjp(dy_ref[...])
        dx_ref[...] = dx + e_ref[...] if has_extra else dx

        @pl.when(pl.program_id(0) == 0)
        def _():
            dg_ref[...] = jnp.zeros_like(dg_ref)

        dg_ref[...] += dg

    ins = [x, g, dy] + ([extra] if has_extra else [])
    specs = [_row_spec(tr, D), _full_spec((1, D)), _row_spec(tr, D)] + ([_row_spec(tr, D)] if has_extra else [])
    return _pcall(body, name=name, grid=(R // tr,), in_specs=specs,
                  out_specs=[_row_spec(tr, D), _full_spec((1, D))],
                  out_shape=[jax.ShapeDtypeStruct((R, D), F32), jax.ShapeDtypeStruct((1, D), F32)],
                  compiler_params=_cp(("arbitrary",)))(*ins)


def _head_indicator(C, hp):
    e = (jnp.arange(C)[:, None] // HEAD == jnp.arange(hp)[None, :]).astype(F32)
    return e, e.T


def _prep_fn(C, L, ps, prev, mu, w0, a0, k_k, k_a, wdu, wiu, E, ET):
    sh = ps + (prev - ps) * mu
    r, k, v = sh[:, :C], sh[:, C:2 * C], sh[:, 2 * C:3 * C]
    wl, al = sh[:, 3 * C:3 * C + L], sh[:, 3 * C + L:3 * C + 2 * L]
    wd = w0 + _mm(jnp.tanh(wl), wdu, False, False, "bf")
    w_pre = -_softplus(-wd) - 0.5
    lw = -jnp.exp(w_pre)
    alpha = _sigmoid(a0 + _mm(al, wiu, False, False, "bf"))
    kk = k * k_k
    ss = _mm(kk * kk, E, False, False, "hi")
    kk = kk * _mm(lax.rsqrt(jnp.maximum(ss, 1e-24)), ET, False, False, "hi")
    k_mod = k * (1.0 + (alpha - 1.0) * k_a)
    return r, lw, k_mod, v, -kk, kk * alpha


def _prep_fwd(p, prev, params, C, L, tr=128):
    T = p.shape[0]
    SH = 3 * C + 2 * L
    tr = _tile(T, tr)

    def body(ps_ref, prev_ref, mu, w0, a0, kk_, ka_, wdu, wiu, E, ET, *outs):
        vals = _prep_fn(C, L, ps_ref[...], prev_ref[...], mu[...], w0[...], a0[...], kk_[...], ka_[...],
                        wdu[...], wiu[...], E[...], ET[...])
        for o, v in zip(outs, vals):
            o[...] = v

    pspecs = [_full_spec(a.shape) for a in params]
    return _pcall(body, name="rwkv_prep_fwd", grid=(T // tr,),
                  in_specs=[_row_spec(tr, SH), _row_spec(tr, SH)] + pspecs,
                  out_specs=[_row_spec(tr, C)] * 6,
                  out_shape=[jax.ShapeDtypeStruct((T, C), F32)] * 6,
                  compiler_params=_cp(("arbitrary",)))(p, prev, *params)


def _prep_bwd(p, prev, params, cts, C, L, tr=64):
    T = p.shape[0]
    SH = 3 * C + 2 * L
    tr = _tile(T, tr)
    nparam = 7

    def body(ps_ref, prev_ref, mu, w0, a0, kk_, ka_, wdu, wiu, E, ET, c0, c1, c2, c3, c4, c5, e0, e2, e3,
             dps_ref, dprev_ref, *dpar):
        f = functools.partial(_prep_fn, C, L)
        fe = lambda ps, prev, *par: f(ps, prev, *par, E[...], ET[...])
        _, vjp = jax.vjp(fe, ps_ref[...], prev_ref[...], mu[...], w0[...], a0[...], kk_[...], ka_[...], wdu[...], wiu[...])
        grads = vjp((c0[...] + e0[...], c1[...], c2[...] + e2[...], c3[...] + e3[...], c4[...], c5[...]))
        dps_ref[...] = grads[0]
        dprev_ref[...] = grads[1]

        @pl.when(pl.program_id(0) == 0)
        def _():
            for d in dpar:
                d[...] = jnp.zeros_like(d)

        for d, gval in zip(dpar, grads[2:]):
            d[...] += gval

    pspecs = [_full_spec(a.shape) for a in params]
    par_shapes = [a.shape for a in params[:nparam]]
    return _pcall(body, name="rwkv_prep_bwd", grid=(T // tr,),
                  in_specs=[_row_spec(tr, SH), _row_spec(tr, SH)] + pspecs + [_row_spec(tr, C)] * 9,
                  out_specs=[_row_spec(tr, SH), _row_spec(tr, SH)] + [_full_spec(s) for s in par_shapes],
                  out_shape=[jax.ShapeDtypeStruct((T, SH), F32)] * 2 + [jax.ShapeDtypeStruct(s, F32) for s in par_shapes],
                  compiler_params=_cp(("arbitrary",)))(p, prev, *params, *cts)


def _shift_combine(d_direct, d_prev_up, tr=256):
    T, W = d_direct.shape
    tr = _tile(T, tr)

    def body(a_ref, b_ref, o_ref):
        o_ref[...] = (a_ref[...] + b_ref[...]).astype(BF16)

    return _pcall(body, name="shift_combine", grid=(T // tr,),
                  in_specs=[_row_spec(tr, W)] * 2, out_specs=_row_spec(tr, W),
                  out_shape=jax.ShapeDtypeStruct((T, W), BF16), compiler_params=_cp(("arbitrary",)))(d_direct, d_prev_up)


def _mix_fn(y, r, kmod, v, g_rwkv, yfox, g_fox, ymem, g_mq, lnw, lnb, rk, E, ET):
    inv = 1.0 / HEAD
    mean = _mm(y, E, False, False, "hi") * inv
    yc = y - _mm(mean, ET, False, False, "hi")
    var = _mm(yc * yc, E, False, False, "hi") * inv
    yn = yc * _mm(lax.rsqrt(var + GN_EPS), ET, False, False, "hi") * lnw + lnb
    bonus = _mm(_mm(r * kmod * rk, E, False, False, "hi"), ET, False, False, "hi") * v
    o1 = (yn + bonus) * _silu(g_rwkv)
    return jnp.concatenate([o1, yfox * _silu(g_fox), ymem * _silu(g_mq)], axis=1)


def _mix_specs(tr, C, MW):
    return [_row_spec(tr, C)] * 7 + [_row_spec(tr, MW)] * 2


def _mix_fwd(acts, params, C, MW, tr=128):
    T = acts[0].shape[0]
    D = 2 * C + MW
    tr = _tile(T, tr)

    def body(y_, r_, k_, v_, g1, yf, g2, ym, g3, lnw, lnb, rk, E, ET, o_ref):
        o_ref[...] = _mix_fn(y_[...], r_[...], k_[...], v_[...], g1[...], yf[...], g2[...], ym[...], g3[...],
                             lnw[...], lnb[...], rk[...], E[...], ET[...]).astype(BF16)

    return _pcall(body, name="mix_fwd", grid=(T // tr,),
                  in_specs=_mix_specs(tr, C, MW) + [_full_spec(a.shape) for a in params],
                  out_specs=_row_spec(tr, D), out_shape=jax.ShapeDtypeStruct((T, D), BF16),
                  compiler_params=_cp(("arbitrary",)))(*acts, *params)


def _mix_bwd(acts, params, dycat, C, MW, tr=64):
    T = acts[0].shape[0]
    D = 2 * C + MW
    tr = _tile(T, tr)

    def body(y_, r_, k_, v_, g1, yf, g2, ym, g3, lnw, lnb, rk, E, ET, dy_ref, *outs):
        fe = lambda *a: _mix_fn(*a, E[...], ET[...])
        _, vjp = jax.vjp(fe, y_[...], r_[...], k_[...], v_[...], g1[...], yf[...], g2[...], ym[...], g3[...],
                         lnw[...], lnb[...], rk[...])
        grads = vjp(dy_ref[...])
        for o, gval in zip(outs[:9], grads[:9]):
            o[...] = gval

        @pl.when(pl.program_id(0) == 0)
        def _():
            for o in outs[9:]:
                o[...] = jnp.zeros_like(o)

        for o, gval in zip(outs[9:], grads[9:]):
            o[...] += gval

    widths = [C, C, C, C, C, C, C, MW, MW]
    return _pcall(body, name="mix_bwd", grid=(T // tr,),
                  in_specs=_mix_specs(tr, C, MW) + [_full_spec(a.shape) for a in params] + [_row_spec(tr, D)],
                  out_specs=[_row_spec(tr, w) for w in widths] + [_full_spec((1, C))] * 3,
                  out_shape=[jax.ShapeDtypeStruct((T, w), F32) for w in widths] + [jax.ShapeDtypeStruct((1, C), F32)] * 3,
                  compiler_params=_cp(("arbitrary",)))(*acts, *params, dycat)


def _post(yo, x, tgt, g_post, tr=128):
    T, D = x.shape
    tr = _tile(T, tr)

    def body(yo_ref, x_ref, t_ref, g_ref, dyo_ref, dout_ref, loss_ref, dg_ref):
        n, vjp = jax.vjp(_rms, yo_ref[...], g_ref[...])
        diff = (x_ref[...] + n) - t_ref[...]
        part = 0.5 * jnp.sum(jnp.mean(diff * diff, axis=-1, keepdims=True), axis=0, keepdims=True)
        d_out = diff * (1.0 / D)
        dyo, dg = vjp(d_out)
        dyo_ref[...] = dyo.astype(BF16)
        dout_ref[...] = d_out

        @pl.when(pl.program_id(0) == 0)
        def _():
            loss_ref[...] = jnp.zeros_like(loss_ref)
            dg_ref[...] = jnp.zeros_like(dg_ref)

        loss_ref[...] += jnp.broadcast_to(part, loss_ref.shape)
        dg_ref[...] += dg

    return _pcall(body, name="post_loss", grid=(T // tr,),
                  in_specs=[_row_spec(tr, D)] * 3 + [_full_spec((1, D))],
                  out_specs=[_row_spec(tr, D), _row_spec(tr, D), _full_spec((1, LANE)), _full_spec((1, D))],
                  out_shape=[jax.ShapeDtypeStruct((T, D), BF16), jax.ShapeDtypeStruct((T, D), F32),
                             jax.ShapeDtypeStruct((1, LANE), F32), jax.ShapeDtypeStruct((1, D), F32)],
                  compiler_params=_cp(("arbitrary",)))(yo, x, tgt, g_post)


def _memattn_fn(MW, q, mkv):
    hd = MW // MEM_HEADS
    scale = hd ** -0.5
    outs = []
    for h in range(MEM_HEADS):
        qh = q[:, h * hd:(h + 1) * hd]
        kh = mkv[:, h * hd:(h + 1) * hd]
        vh = mkv[:, MW + h * hd:MW + (h + 1) * hd]
        s = _mm(qh, kh, False, True, "bf") * scale
        e = jnp.exp(s - lax.stop_gradient(jnp.max(s, axis=-1, keepdims=True)))
        pr = e / jnp.sum(e, axis=-1, keepdims=True)
        outs.append(_mm(pr, vh, False, False, "bf"))
    return jnp.concatenate(outs, axis=1)


def _memattn_fwd(p, mkv, MW, tr=256):
    T = p.shape[0]
    tr = _tile(T, tr)

    def body(q_ref, kv_ref, o_ref):
        o_ref[...] = _memattn_fn(MW, q_ref[...], kv_ref[...])

    return _pcall(body, name="memattn_fwd", grid=(T // tr,),
                  in_specs=[_row_spec(tr, MW), _full_spec(mkv.shape)], out_specs=_row_spec(tr, MW),
                  out_shape=jax.ShapeDtypeStruct((T, MW), F32), compiler_params=_cp(("arbitrary",)))(p, mkv)


def _memattn_bwd(p, mkv, do, MW, tr=256):
    T = p.shape[0]
    tr = _tile(T, tr)

    def body(q_ref, kv_ref, do_ref, dq_ref, dkv_ref):
        _, vjp = jax.vjp(functools.partial(_memattn_fn, MW), q_ref[...], kv_ref[...])
        dq, dkv = vjp(do_ref[...])
        dq_ref[...] = dq

        @pl.when(pl.program_id(0) == 0)
        def _():
            dkv_ref[...] = jnp.zeros_like(dkv_ref)

        dkv_ref[...] += dkv

    return _pcall(body, name="memattn_bwd", grid=(T // tr,),
                  in_specs=[_row_spec(tr, MW), _full_spec(mkv.shape), _row_spec(tr, MW)],
                  out_specs=[_row_spec(tr, MW), _full_spec(mkv.shape)],
                  out_shape=[jax.ShapeDtypeStruct((T, MW), F32), jax.ShapeDtypeStruct(mkv.shape, F32)],
                  compiler_params=_cp(("arbitrary",)))(p, mkv, do)


def _fox_cum_fwd(p, b_f_pad, off, blk=512):
    T = p.shape[0]
    blk = _tile(T, blk)

    def body(f_ref, b_ref, cum_ref, cumt_ref, carry):
        @pl.when(pl.program_id(0) == 0)
        def _():
            carry[...] = jnp.zeros_like(carry)

        z = f_ref[...] + b_ref[...]
        logf = -_softplus(-z)
        row = lax.broadcasted_iota(jnp.int32, (blk, blk), 0)
        col = lax.broadcasted_iota(jnp.int32, (blk, blk), 1)
        tri = (col <= row).astype(F32)
        c = _dg(tri, logf, False, False, "hi") + carry[...]
        cum_ref[...] = c
        cumt_ref[...] = c.T
        carry[...] += jnp.sum(logf, axis=0, keepdims=True)

    return _pcall(body, name="fox_cum_fwd", grid=(T // blk,),
                  in_specs=[_row_spec(blk, LANE, off // LANE), _full_spec((1, LANE))],
                  out_specs=[_row_spec(blk, LANE), pl.BlockSpec((LANE, blk), lambda i: (0, i))],
                  out_shape=[jax.ShapeDtypeStruct((T, LANE), F32), jax.ShapeDtypeStruct((LANE, T), F32)],
                  scratch_shapes=[pltpu.VMEM((1, LANE), F32)], compiler_params=_cp(("arbitrary",)))(p, b_f_pad)


def _fox_cum_bwd(p, b_f_pad, dcum, off, blk=512):
    T = p.shape[0]
    blk = _tile(T, blk)
    nb = T // blk

    def body(f_ref, b_ref, dc_ref, df_ref, db_ref, carry):
        @pl.when(pl.program_id(0) == 0)
        def _():
            carry[...] = jnp.zeros_like(carry)
            db_ref[...] = jnp.zeros_like(db_ref)

        row = lax.broadcasted_iota(jnp.int32, (blk, blk), 0)
        col = lax.broadcasted_iota(jnp.int32, (blk, blk), 1)
        tri = (col >= row).astype(F32)
        dlogf = _dg(tri, dc_ref[...], False, False, "hi") + carry[...]
        carry[...] += jnp.sum(dc_ref[...], axis=0, keepdims=True)
        z = f_ref[...] + b_ref[...]
        dz = dlogf * (1.0 - _sigmoid(z))
        df_ref[...] = dz
        db_ref[...] += jnp.sum(dz, axis=0, keepdims=True)

    rev = lambda i: (nb - 1 - i, 0)
    return _pcall(body, name="fox_cum_bwd", grid=(nb,),
                  in_specs=[pl.BlockSpec((blk, LANE), lambda i: (nb - 1 - i, off // LANE)), _full_spec((1, LANE)),
                            pl.BlockSpec((blk, LANE), rev)],
                  out_specs=[pl.BlockSpec((blk, LANE), rev), _full_spec((1, LANE))],
                  out_shape=[jax.ShapeDtypeStruct((T, LANE), F32), jax.ShapeDtypeStruct((1, LANE), F32)],
                  scratch_shapes=[pltpu.VMEM((1, LANE), F32)], compiler_params=_cp(("arbitrary",)))(p, b_f_pad, dcum)


FOX_SCALE = HEAD ** -0.5
FOX_TQ, FOX_TK = 512, 512


def _fox_scores(q, k, ck, q0=None, k0=None):
    s = _dg(q, k, False, True, "bf") - ck
    if q0 is None:
        return s
    qpos = q0 + lax.broadcasted_iota(jnp.int32, s.shape, 0)
    kpos = k0 + lax.broadcasted_iota(jnp.int32, s.shape, 1)
    return jnp.where(kpos <= qpos, s, NEG)


def _fox_c0(ck_ref, hh, pos):
    return ck_ref[0, hh:hh + 1, pl.ds(pl.multiple_of(pos, LANE), LANE)][:, 0:1]


def _fox_tiles(T):
    assert T % LANE == 0
    tq, tk = _tile(T, FOX_TQ), _tile(T, FOX_TK)
    shift = (tk // tq).bit_length() - 1
    assert tk == tq << shift
    return tq, tk, shift


def _fox_fwd(p, ck, C, offs):
    T = p.shape[0]
    tq, tk, shift = _fox_tiles(T)
    npair = C // LANE
    cb = lambda name: offs[name] // LANE

    def body(q_ref, k_ref, v_ref, ck_ref, o_ref, lse_ref):
        i = pl.program_id(1)
        nfull = i >> shift
        lse_ref[...] = jnp.zeros_like(lse_ref)
        heads = [slice(hh * HEAD, (hh + 1) * HEAD) for hh in range(2)]
        qs = [(q_ref[:, sl] * FOX_SCALE).astype(BF16) for sl in heads]
        c0s = [_fox_c0(ck_ref, hh, i * tq) for hh in range(2)]

        def step(j, carry, masked):
            off = pl.multiple_of(j * tk, tk)
            out = []
            for hh, sl in enumerate(heads):
                m, l, acc = carry[hh]
                k = k_ref[pl.ds(off, tk), sl].astype(BF16)
                v = v_ref[pl.ds(off, tk), sl].astype(BF16)
                ckv = ck_ref[0, hh:hh + 1, pl.ds(off, tk)] - c0s[hh]
                s = _fox_scores(qs[hh], k, ckv, i * tq, off) if masked else _fox_scores(qs[hh], k, ckv)
                m_new = jnp.maximum(m, jnp.max(s, axis=-1, keepdims=True))
                pr = jnp.exp(s - m_new)
                al = jnp.exp(m - m_new)
                l = al * l + jnp.sum(pr, axis=-1, keepdims=True)
                acc = al * acc + _dg(pr, v, False, False, "bf")
                out.append((m_new, l, acc))
            return tuple(out)

        one = (jnp.full((tq, 1), NEG, F32), jnp.zeros((tq, 1), F32), jnp.zeros((tq, HEAD), F32))
        carry = lax.fori_loop(0, nfull, lambda j, c: step(j, c, False), (one, one))
        for hh, (m, l, acc) in enumerate(step(nfull, carry, True)):
            o_ref[:, heads[hh]] = acc / l
            lse_ref[0, :, hh:hh + 1] = m + jnp.log(l)

    return _pcall(body, name="fox_fwd", grid=(npair, T // tq),
                  in_specs=[pl.BlockSpec((tq, LANE), lambda h, i: (i, cb("fq") + h)),
                            pl.BlockSpec((T, LANE), lambda h, i: (0, cb("fk") + h)),
                            pl.BlockSpec((T, LANE), lambda h, i: (0, cb("fv") + h)),
                            pl.BlockSpec((1, 8, T), lambda h, i: (h, 0, 0))],
                  out_specs=[pl.BlockSpec((tq, LANE), lambda h, i: (i, h)),
                             pl.BlockSpec((1, tq, 8), lambda h, i: (h, i, 0))],
                  out_shape=[jax.ShapeDtypeStruct((T, C), F32), jax.ShapeDtypeStruct((npair, T, 8), F32)],
                  compiler_params=_cp(("arbitrary", "arbitrary")))(p, p, p, ck)


def _fox_delta(p, ck, do, lse, C, offs):
    T = p.shape[0]
    tq, tk, shift = _fox_tiles(T)
    npair = C // LANE
    cb = lambda name: offs[name] // LANE

    def body(q_ref, k_ref, v_ref, ck_ref, do_ref, lse_ref, d_ref):
        i = pl.program_id(1)
        nfull = i >> shift
        d_ref[...] = jnp.zeros_like(d_ref)
        heads = [slice(hh * HEAD, (hh + 1) * HEAD) for hh in range(2)]
        qs = [(q_ref[:, sl] * FOX_SCALE).astype(BF16) for sl in heads]
        dos = [do_ref[:, sl].astype(BF16) for sl in heads]
        lses = [lse_ref[0, :, hh:hh + 1] for hh in range(2)]
        c0s = [_fox_c0(ck_ref, hh, i * tq) for hh in range(2)]

        def step(j, accs, masked):
            off = pl.multiple_of(j * tk, tk)
            out = []
            for hh, sl in enumerate(heads):
                k = k_ref[pl.ds(off, tk), sl].astype(BF16)
                v = v_ref[pl.ds(off, tk), sl].astype(BF16)
                ckv = ck_ref[0, hh:hh + 1, pl.ds(off, tk)] - c0s[hh]
                s = _fox_scores(qs[hh], k, ckv, i * tq, off) if masked else _fox_scores(qs[hh], k, ckv)
                pr = jnp.exp(s - lses[hh])
                dp = _dg(dos[hh], v, False, True, "bf")
                out.append(accs[hh] + jnp.sum(pr * dp, axis=-1, keepdims=True))
            return tuple(out)

        z = jnp.zeros((tq, 1), F32)
        accs = lax.fori_loop(0, nfull, lambda j, c: step(j, c, False), (z, z))
        for hh, acc in enumerate(step(nfull, accs, True)):
            d_ref[0, :, hh:hh + 1] = acc

    return _pcall(body, name="fox_delta", grid=(npair, T // tq),
                  in_specs=[pl.BlockSpec((tq, LANE), lambda h, i: (i, cb("fq") + h)),
                            pl.BlockSpec((T, LANE), lambda h, i: (0, cb("fk") + h)),
                            pl.BlockSpec((T, LANE), lambda h, i: (0, cb("fv") + h)),
                            pl.BlockSpec((1, 8, T), lambda h, i: (h, 0, 0)),
                            pl.BlockSpec((tq, LANE), lambda h, i: (i, h)),
                            pl.BlockSpec((1, tq, 8), lambda h, i: (h, i, 0))],
                  out_specs=pl.BlockSpec((1, tq, 8), lambda h, i: (h, i, 0)),
                  out_shape=jax.ShapeDtypeStruct((npair, T, 8), F32),
                  compiler_params=_cp(("arbitrary", "arbitrary")))(p, p, p, ck, do, lse)


def _fox_bwd(p, ck, delta, do, lse, C, offs):
    T = p.shape[0]
    tq, tk, shift = _fox_tiles(T)
    ratio = tk // tq
    nq = T // tq
    npair = C // LANE
    cb = lambda name: offs[name] // LANE

    def body(q_ref, k_ref, v_ref, ck_ref, ckall_ref, dl_ref, do_ref, lse_ref, dq_ref, dk_ref, dv_ref, dck_ref):
        j = pl.program_id(1)

        @pl.when(j == 0)
        def _():
            dq_ref[...] = jnp.zeros_like(dq_ref)

        dck_ref[...] = jnp.zeros_like(dck_ref)
        heads = [slice(hh * HEAD, (hh + 1) * HEAD) for hh in range(2)]
        ks = [k_ref[:, sl].astype(BF16) for sl in heads]
        vs = [v_ref[:, sl].astype(BF16) for sl in heads]
        cks = [ck_ref[0, hh:hh + 1, :] for hh in range(2)]

        def step(i, carry, masked):
            off = pl.multiple_of(i * tq, tq)
            out = []
            for hh, sl in enumerate(heads):
                dk, dv, dck = carry[hh]
                q = (q_ref[pl.ds(off, tq), sl] * FOX_SCALE).astype(BF16)
                dov = do_ref[pl.ds(off, tq), sl]
                lsev = lse_ref[0, pl.ds(off, tq), hh:hh + 1]
                ckv = cks[hh] - _fox_c0(ckall_ref, hh, off)
                s = _fox_scores(q, ks[hh], ckv, off, j * tk) if masked else _fox_scores(q, ks[hh], ckv)
                pr = jnp.exp(s - lsev)
                dv = dv + _dg(pr, dov, True, False, "bf")
                dp = _dg(dov, vs[hh], False, True, "bf")
                ds = pr * (dp - dl_ref[0, pl.ds(off, tq), hh:hh + 1])
                dk = dk + _dg(ds, q, True, False, "bf")
                dq_ref[pl.ds(off, tq), sl] += _dg(ds, ks[hh], False, False, "bf") * FOX_SCALE
                out.append((dk, dv, dck - jnp.sum(ds, axis=0, keepdims=True)))
            return tuple(out)

        z = jnp.zeros((tk, HEAD), F32)
        carry = ((z, z, jnp.zeros((1, tk), F32)),) * 2
        for r in range(ratio):
            carry = step(j * ratio + r, carry, True)
        carry = lax.fori_loop((j + 1) * ratio, nq, lambda i, c: step(i, c, False), carry)
        for hh, (dk, dv, dck) in enumerate(carry):
            dk_ref[:, heads[hh]] = dk
            dv_ref[:, heads[hh]] = dv
            dck_ref[0, hh:hh + 1, :] = dck

    full = lambda h, j: (0, h)
    return _pcall(body, name="fox_bwd", grid=(npair, T // tk),
                  in_specs=[pl.BlockSpec((T, LANE), lambda h, j: (0, cb("fq") + h)),
                            pl.BlockSpec((tk, LANE), lambda h, j: (j, cb("fk") + h)),
                            pl.BlockSpec((tk, LANE), lambda h, j: (j, cb("fv") + h)),
                            pl.BlockSpec((1, 8, tk), lambda h, j: (h, 0, j)),
                            pl.BlockSpec((1, 8, T), lambda h, j: (h, 0, 0)),
                            pl.BlockSpec((1, T, 8), lambda h, j: (h, 0, 0)), pl.BlockSpec((T, LANE), full),
                            pl.BlockSpec((1, T, 8), lambda h, j: (h, 0, 0))],
                  out_specs=[pl.BlockSpec((T, LANE), full),
                             pl.BlockSpec((tk, LANE), lambda h, j: (j, h)),
                             pl.BlockSpec((tk, LANE), lambda h, j: (j, h)),
                             pl.BlockSpec((1, 8, tk), lambda h, j: (h, 0, j))],
                  out_shape=[jax.ShapeDtypeStruct((T, C), F32)] * 3 + [jax.ShapeDtypeStruct((npair, 8, T), F32)],
                  compiler_params=_cp(("arbitrary", "arbitrary")))(p, p, p, ck, ck, delta, do, lse)


def _chunk_fn(S0, r, lw, k, v, a, b):
    nh, n = r.shape[0], r.shape[1]
    row = lax.broadcasted_iota(jnp.int32, (nh, n, n), 1)
    col = lax.broadcasted_iota(jnp.int32, (nh, n, n), 2)
    incl, strict = col <= row, col < row
    mm = lambda x, y, ta=False, tb=False: _mm(x, y, ta, tb, "x3")
    g = _mm(incl.astype(F32), lw, False, False, "hi")
    einv = jnp.exp(-g)
    rt, at, bt, kt = r * jnp.exp(g), a * jnp.exp(g - lw), b * einv, k * einv
    a_ab = jnp.where(strict, mm(at, bt, tb=True), 0.0)
    a_ak = jnp.where(strict, mm(at, kt, tb=True), 0.0)
    r_b = jnp.where(incl, mm(rt, bt, tb=True), 0.0)
    r_k = jnp.where(incl, mm(rt, kt, tb=True), 0.0)
    u = mm(at, S0, tb=True) + mm(a_ak, v)
    pw = a_ab
    for it in range(6):
        u = u + mm(pw, u)
        if it < 5:
            pw = mm(pw, pw)
    y = mm(rt, S0, tb=True) + mm(r_b, u) + mm(r_k, v)
    g_end = jnp.sum(lw, axis=1, keepdims=True)
    s_end = (S0 + mm(u, bt, ta=True) + mm(v, kt, ta=True)) * jnp.exp(g_end)
    return y, s_end


SCAN_HEADS = 12
SCAN_ROWS = 64


def _scan_group(C):
    nh = SCAN_HEADS
    while C % (nh * HEAD):
        nh -= 2
    return nh, nh * HEAD


def _scan_fwd(r, lw, k, v, a, b):
    T, C = r.shape
    tc = _tile(T, SCAN_ROWS)
    ncs = tc // CHUNK
    nh, GW = _scan_group(C)
    ngroup = C // GW

    def body(r_ref, lw_ref, k_ref, v_ref, a_ref, b_ref, y_ref, ck_ref, state):
        @pl.when(pl.program_id(1) == 0)
        def _():
            state[...] = jnp.zeros_like(state)

        heads = [slice(hh * HEAD, (hh + 1) * HEAD) for hh in range(nh)]
        split = lambda ref, rows: jnp.stack([ref[rows, sl] for sl in heads])
        st = split(state, slice(None))
        for c in range(ncs):
            rows = slice(c * CHUNK, (c + 1) * CHUNK)
            for hh, sl in enumerate(heads):
                ck_ref[0, c, :, sl] = st[hh]
            y, st = _chunk_fn(st, *[split(ref, rows) for ref in (r_ref, lw_ref, k_ref, v_ref, a_ref, b_ref)])
            for hh, sl in enumerate(heads):
                y_ref[rows, sl] = y[hh]
        for hh, sl in enumerate(heads):
            state[:, sl] = st[hh]

    spec = pl.BlockSpec((tc, GW), lambda h, t: (t, h))
    return _pcall(body, name="rwkv_scan_fwd", grid=(ngroup, T // tc),
                  in_specs=[spec] * 6,
                  out_specs=[spec, pl.BlockSpec((1, ncs, HEAD, GW), lambda h, t: (h, t, 0, 0))],
                  out_shape=[jax.ShapeDtypeStruct((T, C), F32),
                             jax.ShapeDtypeStruct((ngroup, T // CHUNK, HEAD, GW), F32)],
                  scratch_shapes=[pltpu.VMEM((HEAD, GW), F32)],
                  compiler_params=_cp(("arbitrary", "arbitrary")))(r, lw, k, v, a, b)


def _scan_bwd(r, lw, k, v, a, b, ckpt, dy):
    T, C = r.shape
    tc = _tile(T, SCAN_ROWS)
    ncs = tc // CHUNK
    nh, GW = _scan_group(C)
    ngroup = C // GW
    nt = T // tc

    def body(r_ref, lw_ref, k_ref, v_ref, a_ref, b_ref, ck_ref, dy_ref, dr, dlw, dk, dv, da, db, dstate):
        @pl.when(pl.program_id(1) == 0)
        def _():
            dstate[...] = jnp.zeros_like(dstate)

        outs = (dr, dlw, dk, dv, da, db)
        heads = [slice(hh * HEAD, (hh + 1) * HEAD) for hh in range(nh)]
        split = lambda ref, rows: jnp.stack([ref[rows, sl] for sl in heads])
        dst = split(dstate, slice(None))
        for c in reversed(range(ncs)):
            rows = slice(c * CHUNK, (c + 1) * CHUNK)
            s0 = jnp.stack([ck_ref[0, c, :, sl] for sl in heads])
            _, vjp = jax.vjp(_chunk_fn, s0, *[split(ref, rows) for ref in (r_ref, lw_ref, k_ref, v_ref, a_ref, b_ref)])
            grads = vjp((split(dy_ref, rows), dst))
            dst = grads[0]
            for o, gval in zip(outs, grads[1:]):
                for hh, sl in enumerate(heads):
                    o[rows, sl] = gval[hh]
        for hh, sl in enumerate(heads):
            dstate[:, sl] = dst[hh]

    spec = pl.BlockSpec((tc, GW), lambda h, t: (nt - 1 - t, h))
    return _pcall(body, name="rwkv_scan_bwd", grid=(ngroup, nt),
                  in_specs=[spec] * 6 + [pl.BlockSpec((1, ncs, HEAD, GW), lambda h, t: (h, nt - 1 - t, 0, 0)), spec],
                  out_specs=[spec] * 6, out_shape=[jax.ShapeDtypeStruct((T, C), F32)] * 6,
                  scratch_shapes=[pltpu.VMEM((HEAD, GW), F32)],
                  compiler_params=_cp(("arbitrary", "arbitrary")))(r, lw, k, v, a, b, ckpt, dy)


def _adam(w, g, m, v):
    m = ADAM_B1 * m + (1.0 - ADAM_B1) * g
    v = ADAM_B2 * v + (1.0 - ADAM_B2) * (g * g)
    m_hat = m / (1.0 - ADAM_B1 ** ADAM_STEP)
    v_hat = v / (1.0 - ADAM_B2 ** ADAM_STEP)
    return -ADAM_LR * (m_hat / (jnp.sqrt(v_hat) + ADAM_EPS) + ADAM_WD * w), m, v


def _sum_adam(parts, w, m, v, name):
    n, R, W = parts.shape
    tr = _tile(R, max(8, min(256, (1 << 20) // (n * W))))

    def body(p_ref, w_ref, m_ref, v_ref, g_ref, d_ref, nm_ref, nv_ref):
        g = p_ref[0].astype(F32)
        for s in range(1, n):
            g = g + p_ref[s].astype(F32)
        d, nm, nv = _adam(w_ref[...], g, m_ref[...], v_ref[...])
        g_ref[...] = g
        d_ref[...] = d
        nm_ref[...] = nm
        nv_ref[...] = nv

    return _pcall(body, name=name, grid=(R // tr,),
                  in_specs=[pl.BlockSpec((n, tr, W), lambda i: (0, i, 0))] + [_row_spec(tr, W)] * 3,
                  out_specs=[_row_spec(tr, W)] * 4, out_shape=[jax.ShapeDtypeStruct((R, W), F32)] * 4,
                  compiler_params=_cp(("arbitrary",)))(parts, w, m, v)


def _pad_lanes(vec, width):
    return jnp.pad(vec, ((0, 0), (0, width - vec.shape[1])))


def kernel(x, mem, g_pre, w_in, mu_rwkv, w0, w_decay_up, a0, w_iclr_up, k_k, k_a, r_k, ln_x_w, ln_x_b, b_f, g_mem, w_mem_kv, w_out, g_post, loss_target, m_g_pre, m_w_in, m_mu_rwkv, m_w0, m_w_decay_up, m_a0, m_w_iclr_up, m_k_k, m_k_a, m_r_k, m_ln_x_w, m_ln_x_b, m_b_f, m_g_mem, m_w_mem_kv, m_w_out, m_g_post, v_g_pre, v_w_in, v_mu_rwkv, v_w0, v_w_decay_up, v_a0, v_w_iclr_up, v_k_k, v_k_a, v_r_k, v_ln_x_w, v_ln_x_b, v_b_f, v_g_mem, v_w_mem_kv, v_w_out, v_g_post):
    T, D = x.shape[1], x.shape[2]
    C = w0.shape[1]
    L = w_decay_up.shape[1]
    H = C // HEAD
    MW = w_mem_kv.shape[2] // 2
    SH = 3 * C + 2 * L
    IN = NDEV * w_in.shape[2]
    assert IN == SH + 5 * C + H + 2 * MW and D == 2 * C + MW and H % 2 == 0 and H <= LANE
    assert C % LANE == 0 and L % LANE == 0 and MW % (MEM_HEADS * HEAD) == 0 and T % CHUNK == 0
    offs = dict(grw=SH, fq=SH + C, fk=SH + 2 * C, fv=SH + 3 * C, gfx=SH + 4 * C, mq=SH + 5 * C, gmq=SH + 5 * C + MW,
                fl=SH + 5 * C + 2 * MW)
    NI = -(-(offs["fl"] + LANE) // 512) * 512
    l_fl = SH + 4 * C

    x2, mem2, tgt2 = x[0], mem[0], loss_target[0]

    wg = _exchange(w_in[0].astype(BF16), "gather_w_in", True)
    w_full = jnp.transpose(wg, (1, 0, 2)).reshape(D, IN)
    w_perm = jnp.concatenate([w_full[:, :l_fl], w_full[:, l_fl + H:], w_full[:, l_fl:l_fl + H],
                              jnp.zeros((D, NI - IN), BF16)], axis=1)
    w_out_f = _exchange(w_out[0].astype(BF16), "gather_w_out", True).reshape(D, D)
    w_kv_f = _exchange(w_mem_kv[0].astype(BF16), "gather_w_mem_kv", True).reshape(D, 2 * MW)
    lora = _exchange(jnp.concatenate([w_decay_up[0], w_iclr_up[0]], axis=0), "gather_lora", True)
    lora = jnp.transpose(lora, (1, 0, 2)).reshape(2 * L, C)
    wdu_f, wiu_f = lora[:L], lora[L:]

    E, ET = _head_indicator(C, LANE)
    prep_params = [mu_rwkv, w0, a0, k_k, k_a, wdu_f, wiu_f, E, ET]
    mix_params = [ln_x_w, ln_x_b, r_k.reshape(1, C), E, ET]
    b_f_pad = _pad_lanes(b_f, LANE)

    h = _rmsnorm_fwd(x2, g_pre, "rmsnorm_pre")
    p = _matmul(h, w_perm, False, False, "in_proj")
    ps = p[:, :SH]
    col = lambda name, w: p[:, offs[name]:offs[name] + w]
    g_rwkv, g_fox, mq, g_mq = col("grw", C), col("gfx", C), col("mq", MW), col("gmq", MW)
    prev = jnp.concatenate([jnp.zeros((1, SH), F32), ps[:-1]], axis=0)
    r, lw, kmod, v, a, b = _prep_fwd(p, prev, prep_params, C, L)
    y_scan, ckpt = _scan_fwd(r, lw, kmod, v, a, b)

    cum, cum_t = _fox_cum_fwd(p, b_f_pad, offs["fl"])
    ck = jnp.pad(cum_t[:H].reshape(H // 2, 2, T), ((0, 0), (0, 6), (0, 0)))
    y_fox, lse = _fox_fwd(p, ck, C, offs)

    memn = _rmsnorm_fwd(mem2, g_mem, "rmsnorm_mem")
    mkv = _matmul(memn, w_kv_f, False, False, "mem_kv_proj")
    y_mem = _memattn_fwd(mq, mkv, MW)

    acts = [y_scan, r, kmod, v, g_rwkv, y_fox, g_fox, y_mem, g_mq]
    ycat = _mix_fwd(acts, mix_params, C, MW)
    yo = _matmul(ycat, w_out_f, False, False, "out_proj")
    d_yo, d_out, loss_part, dg_post = _post(yo, x2, tgt2, g_post)
    loss = lax.psum(loss_part[0, 0], AXES)

    g_w_out = _matmul(ycat, d_yo, True, False, "grad_w_out", BF16)
    d_ycat = _matmul(d_yo, w_out_f, False, True, "d_ycat")
    (d_y, d_r1, d_k1, d_v1, d_grw, d_yfox, d_gfx, d_ymem, d_gmq, dg_lnw, dg_lnb, dg_rk) = _mix_bwd(
        acts, mix_params, d_ycat, C, MW)

    d_mq, d_mkv = _memattn_bwd(mq, mkv, d_ymem, MW)
    g_w_kv = _matmul(memn, d_mkv, True, False, "grad_w_mem_kv", BF16)
    d_memn = _matmul(d_mkv, w_kv_f, False, True, "d_memn")
    _, dg_mem = _rmsnorm_bwd(mem2, g_mem, d_memn, None, "rmsnorm_mem_bwd")

    delta = _fox_delta(p, ck, d_yfox, lse, C, offs)
    d_fq, d_fk, d_fv, d_ck = _fox_bwd(p, ck, delta, d_yfox, lse, C, offs)
    d_cum = _pad_lanes(d_ck[:, :2, :].reshape(H, T).T, LANE)
    d_fl, dg_bf = _fox_cum_bwd(p, b_f_pad, d_cum, offs["fl"])

    d_r, d_lw, d_k, d_v, d_a, d_b = _scan_bwd(r, lw, kmod, v, a, b, ckpt, d_y)
    cts = [d_r, d_lw, d_k, d_v, d_a, d_b, d_r1, d_k1, d_v1]
    (d_ps, d_prev, dg_mu, dg_w0, dg_a0, dg_kk, dg_ka, dg_wdu, dg_wiu) = _prep_bwd(p, prev, prep_params, cts, C, L)
    d_prev_up = jnp.concatenate([d_prev[1:], jnp.zeros((1, SH), F32)], axis=0)
    d_sh = _shift_combine(d_ps, d_prev_up)

    tobf = lambda z: z.astype(BF16)
    dp = jnp.concatenate([d_sh, tobf(d_grw), tobf(d_fq), tobf(d_fk), tobf(d_fv), tobf(d_gfx), tobf(d_mq), tobf(d_gmq),
                          tobf(d_fl), jnp.zeros((T, NI - offs["fl"] - LANE), BF16)], axis=1)
    g_w_perm = _matmul(h, dp, True, False, "grad_w_in", BF16)
    d_h = _matmul(dp, w_perm, False, True, "d_h")
    grad_x, dg_pre = _rmsnorm_bwd(x2, g_pre, d_h, d_out, "rmsnorm_pre_bwd")

    g_w_log = jnp.concatenate([g_w_perm[:, :l_fl], g_w_perm[:, offs["fl"]:offs["fl"] + H],
                               g_w_perm[:, l_fl:offs["fl"]]], axis=1)
    parts_in = _exchange(jnp.transpose(g_w_log.reshape(D, NDEV, IN // NDEV), (1, 0, 2)), "scatter_grad_w_in", False)
    parts_out = _exchange(g_w_out.reshape(NDEV, D // NDEV, D), "scatter_grad_w_out", False)
    parts_kv = _exchange(g_w_kv.reshape(NDEV, D // NDEV, 2 * MW), "scatter_grad_w_mem_kv", False)
    g_lora = jnp.concatenate([dg_wdu, dg_wiu], axis=0)
    parts_lora = _exchange(jnp.transpose(g_lora.reshape(2 * L, NDEV, C // NDEV), (1, 0, 2)), "scatter_grad_lora", False)

    gw_in, dw_in, nm_w_in, nv_w_in = _sum_adam(parts_in, w_in[0], m_w_in[0], v_w_in[0], "adam_w_in")
    gw_out, dw_out, nm_w_out, nv_w_out = _sum_adam(parts_out, w_out[0], m_w_out[0], v_w_out[0], "adam_w_out")
    gw_kv, dw_kv, nm_w_kv, nv_w_kv = _sum_adam(parts_kv, w_mem_kv[0], m_w_mem_kv[0], v_w_mem_kv[0], "adam_w_mem_kv")
    cat2 = lambda u, w_: jnp.concatenate([u[0], w_[0]], axis=0)
    lora_res = _sum_adam(parts_lora, cat2(w_decay_up, w_iclr_up), cat2(m_w_decay_up, m_w_iclr_up),
                         cat2(v_w_decay_up, v_w_iclr_up), "adam_lora")

    small = [("g_pre", g_pre, m_g_pre, v_g_pre, dg_pre), ("mu_rwkv", mu_rwkv, m_mu_rwkv, v_mu_rwkv, dg_mu),
             ("w0", w0, m_w0, v_w0, dg_w0), ("a0", a0, m_a0, v_a0, dg_a0), ("k_k", k_k, m_k_k, v_k_k, dg_kk),
             ("k_a", k_a, m_k_a, v_k_a, dg_ka), ("r_k", r_k.reshape(1, C), m_r_k.reshape(1, C), v_r_k.reshape(1, C), dg_rk),
             ("ln_x_w", ln_x_w, m_ln_x_w, v_ln_x_w, dg_lnw), ("ln_x_b", ln_x_b, m_ln_x_b, v_ln_x_b, dg_lnb),
             ("b_f", _pad_lanes(b_f, LANE), _pad_lanes(m_b_f, LANE), _pad_lanes(v_b_f, LANE), dg_bf),
             ("g_mem", g_mem, m_g_mem, v_g_mem, dg_mem), ("g_post", g_post, m_g_post, v_g_post, dg_post)]
    widths = [s[1].shape[1] for s in small]
    pack = lambda idx: jnp.concatenate([s[idx] for s in small], axis=1).reshape(-1, LANE)
    parts_small = _exchange(pack(4), "gather_small_grads", True)
    res_small = _sum_adam(parts_small, pack(1), pack(2), pack(3), "adam_small")

    def unpack(flat):
        flat = flat.reshape(1, -1)
        out, o = {}, 0
        for (name, *_), wd in zip(small, widths):
            out[name] = flat[:, o:o + wd]
            o += wd
        out["b_f"] = out["b_f"][:, :H]
        out["r_k"] = out["r_k"].reshape(1, H, HEAD)
        return out

    sg, sd, sm, sv = [unpack(z) for z in res_small]
    big = {"w_in": (gw_in, dw_in, nm_w_in, nv_w_in), "w_out": (gw_out, dw_out, nm_w_out, nv_w_out),
           "w_mem_kv": (gw_kv, dw_kv, nm_w_kv, nv_w_kv),
           "w_decay_up": tuple(z[:L] for z in lora_res), "w_iclr_up": tuple(z[L:] for z in lora_res)}
    order = ["g_pre", "w_in", "mu_rwkv", "w0", "w_decay_up", "a0", "w_iclr_up", "k_k", "k_a", "r_k", "ln_x_w", "ln_x_b",
             "b_f", "g_mem", "w_mem_kv", "w_out", "g_post"]

    def pick(name, idx):
        if name in big:
            return big[name][idx][None]
        return (sg, sd, sm, sv)[idx][name]

    outs = [loss, grad_x[None]]
    for idx in range(4):
        outs += [pick(n, idx) for n in order]
    return tuple(outs)
```

```python
import functools

import jax
import jax.numpy as jnp
from jax import lax
from jax.experimental import pallas as pl
from jax.experimental.pallas import tpu as pltpu

F32, BF16 = jnp.float32, jnp.bfloat16
HI = lax.Precision.HIGHEST
NDEV = 8
AXES = ("x", "y", "c")
HEAD = 64
CHUNK = 64
MEM_HEADS = 4
LANE = 128
RMS_EPS = 1e-6
GN_EPS = 64e-5
NEG = -1e30
ADAM_LR, ADAM_B1, ADAM_B2, ADAM_EPS, ADAM_WD, ADAM_STEP = 0.001, 0.9, 0.999, 1e-08, 0.01, 10
VMEM_LIMIT = 56 * 1024 * 1024


def _pcall(body, **kw):
    return pl.pallas_call(body, **kw)


def _cp(sem=None, vmem=VMEM_LIMIT):
    return pltpu.CompilerParams(dimension_semantics=sem, vmem_limit_bytes=vmem)


def _tile(n, pref):
    for t in (pref, 1024, 512, 256, 128, 64, 32, 16, 8):
        if t <= pref and n % t == 0:
            return t
    return n


def _dg(a, b, ta, tb, mode):
    nb = a.ndim - 2
    ca = nb + (0 if ta else 1)
    cb = nb + (1 if tb else 0)
    dims = (((ca,), (cb,)), (tuple(range(nb)), tuple(range(nb))))
    if mode == "x3":
        a_hi, b_hi = a.astype(BF16), b.astype(BF16)
        a_lo, b_lo = (a - a_hi.astype(F32)).astype(BF16), (b - b_hi.astype(F32)).astype(BF16)
        dot = lambda u, w: lax.dot_general(u, w, dims, preferred_element_type=F32)
        return dot(a_hi, b_hi) + (dot(a_hi, b_lo) + dot(a_lo, b_hi))
    if mode == "bf":
        a, b, prec = a.astype(BF16), b.astype(BF16), None
    else:
        prec = HI
    return lax.dot_general(a, b, dims, preferred_element_type=F32, precision=prec)


@functools.partial(jax.custom_vjp, nondiff_argnums=(2, 3, 4))
def _mm(a, b, ta, tb, mode):
    return _dg(a, b, ta, tb, mode)


def _mm_fwd(a, b, ta, tb, mode):
    return _dg(a, b, ta, tb, mode), (a, b)


def _mm_bwd(ta, tb, mode, res, g):
    a, b = res
    da = _dg(g, b, False, not tb, mode) if not ta else _dg(b, g, tb, True, mode)
    db = _dg(a, g, not ta, False, mode) if not tb else _dg(g, a, True, ta, mode)
    return da, db


_mm.defvjp(_mm_fwd, _mm_bwd)


def _sigmoid(z):
    return 1.0 / (1.0 + jnp.exp(-z))


def _softplus(z):
    return jnp.maximum(z, 0.0) + jnp.log(1.0 + jnp.exp(-jnp.abs(z)))


def _silu(z):
    return z * _sigmoid(z)


def _rms(x, g):
    return x * lax.rsqrt(jnp.mean(x * x, axis=-1, keepdims=True) + RMS_EPS) * g


HBM_SPEC = pl.BlockSpec(memory_space=pltpu.HBM)
EXCHANGE_SEMS = [pltpu.SemaphoreType.DMA((NDEV - 1,)), pltpu.SemaphoreType.DMA((NDEV - 1,)), pltpu.SemaphoreType.DMA(())]


def _exchange_copies(gather, x_ref, o_ref, send_sems, recv_sems, local_sem, arrivals):
    ix, iy, ic = lax.axis_index("x"), lax.axis_index("y"), lax.axis_index("c")
    me = 4 * ix + 2 * iy + ic

    def src(dest):
        return x_ref if gather else x_ref.at[dest]

    mine = pltpu.make_async_copy(src(me), o_ref.at[me], local_sem)
    pairs = []
    for k in range(1, NDEV):
        px = 1 - ix if (k >> 2) & 1 else ix
        py = 1 - iy if (k >> 1) & 1 else iy
        pc = 1 - ic if k & 1 else ic
        peer = 4 * px + 2 * py + pc
        send = pltpu.make_async_remote_copy(
            src_ref=src(peer), dst_ref=o_ref.at[me], send_sem=send_sems.at[k - 1], recv_sem=recv_sems.at[k - 1],
            device_id=(px, py, pc), device_id_type=pl.DeviceIdType.MESH)
        arrival = arrivals and pltpu.make_async_remote_copy(
            src_ref=src(peer), dst_ref=o_ref.at[peer], send_sem=send_sems.at[k - 1], recv_sem=recv_sems.at[k - 1],
            device_id=(ix, iy, ic), device_id_type=pl.DeviceIdType.MESH)
        pairs.append((send, arrival))
    return mine, pairs


def _exchange_start(*args):
    mine, pairs = _exchange_copies(*args, arrivals=False)
    mine.start()
    for send, _ in pairs:
        send.start()


def _exchange_wait(*args):
    mine, pairs = _exchange_copies(*args, arrivals=True)
    for send, arrival in pairs:
        send.wait_send()
        arrival.wait_recv()
    mine.wait()


def _exchange_shape(x, gather):
    return jax.ShapeDtypeStruct((NDEV,) + tuple(x.shape if gather else x.shape[1:]), x.dtype)


def _exchange(x, name, gather):
    def body(x_ref, o_ref, *sems):
        _exchange_start(gather, x_ref, o_ref, *sems)
        _exchange_wait(gather, x_ref, o_ref, *sems)

    return _pcall(body, name=name, out_shape=_exchange_shape(x, gather), in_specs=[HBM_SPEC], out_specs=HBM_SPEC,
                  scratch_shapes=list(EXCHANGE_SEMS))(x)


MAX_FULL_K = 4096


def _matmul(a, b, ta, tb, name, out_dtype=F32, tm=1024, tn=1024, tk=3072, riders=()):
    M, K = (a.shape[1], a.shape[0]) if ta else a.shape
    N = b.shape[0] if tb else b.shape[1]
    assert (b.shape[1] if tb else b.shape[0]) == K
    if K <= MAX_FULL_K:
        tk = K
    else:
        tm, tk = min(tm, 512), _tile(K, tk)
    tm, tn = _tile(M, tm), _tile(N, tn)
    grid = (M // tm, N // tn, K // tk)
    nk, nr = grid[2], len(riders)

    def body(*refs):
        a_ref, b_ref, x_refs = refs[0], refs[1], refs[2:2 + nr]
        o_ref, xo_refs, rest = refs[2 + nr], refs[3 + nr:3 + 2 * nr], refs[3 + 2 * nr:]
        sems = rest[1:] if nk > 1 else rest
        ids = [pl.program_id(d) for d in range(3)]
        jobs = [(riders[q][1], x_refs[q], xo_refs[q]) + tuple(sems[3 * q:3 * q + 3]) for q in range(nr)]

        if nr:
            @pl.when((ids[0] == 0) & (ids[1] == 0) & (ids[2] == 0))
            def _():
                for job in jobs:
                    _exchange_start(*job)

        if nk == 1:
            o_ref[...] = _dg(a_ref[...], b_ref[...], ta, tb, "bf").astype(o_ref.dtype)
        else:
            acc = rest[0]

            @pl.when(ids[2] == 0)
            def _():
                acc[...] = jnp.zeros_like(acc)

            acc[...] += _dg(a_ref[...], b_ref[...], ta, tb, "bf")

            @pl.when(ids[2] == nk - 1)
            def _():
                o_ref[...] = acc[...].astype(o_ref.dtype)

        if nr:
            @pl.when((ids[0] == grid[0] - 1) & (ids[1] == grid[1] - 1) & (ids[2] == nk - 1))
            def _():
                for job in jobs:
                    _exchange_wait(*job)

    a_spec = pl.BlockSpec((tk, tm), lambda i, j, k: (k, i)) if ta else pl.BlockSpec((tm, tk), lambda i, j, k: (i, k))
    b_spec = pl.BlockSpec((tn, tk), lambda i, j, k: (j, k)) if tb else pl.BlockSpec((tk, tn), lambda i, j, k: (k, j))
    out = _pcall(
        body, name=name, grid=grid,
        in_specs=[a_spec, b_spec] + [HBM_SPEC] * nr,
        out_specs=[pl.BlockSpec((tm, tn), lambda i, j, k: (i, j))] + [HBM_SPEC] * nr,
        out_shape=[jax.ShapeDtypeStruct((M, N), out_dtype)] + [_exchange_shape(x, g) for x, g in riders],
        scratch_shapes=([pltpu.VMEM((tm, tn), F32)] if nk > 1 else []) + list(EXCHANGE_SEMS) * nr,
        compiler_params=_cp(("arbitrary", "arbitrary", "arbitrary")),
    )(a, b, *[x for x, _ in riders])
    return tuple(out) if nr else out[0]


def _row_spec(tr, width, col_block=0):
    return pl.BlockSpec((tr, width), lambda i: (i, col_block))


def _full_spec(shape):
    nd = len(shape)
    return pl.BlockSpec(tuple(shape), lambda i: (0,) * nd)


def _rmsnorm_fwd(x, g, name, tr=256):
    R, D = x.shape
    tr = _tile(R, tr)

    def body(x_ref, g_ref, o_ref):
        o_ref[...] = _rms(x_ref[...], g_ref[...]).astype(BF16)

    return _pcall(body, name=name, grid=(R // tr,),
                  in_specs=[_row_spec(tr, D), _full_spec((1, D))], out_specs=_row_spec(tr, D),
                  out_shape=jax.ShapeDtypeStruct((R, D), BF16), compiler_params=_cp(("arbitrary",)))(x, g)


def _rmsnorm_bwd(x, g, dy, extra, name, tr=128):
    R, D = x.shape
    tr = _tile(R, tr)
    has_extra = extra is not None

    def body(*refs):
        if has_extra:
            x_ref, g_ref, dy_ref, e_ref, dx_ref, dg_ref = refs
        else:
            x_ref, g_ref, dy_ref, dx_ref, dg_ref = refs
        _, vjp = jax.vjp(_rms, x_ref[...], g_ref[...])
        dx, dg = vjp(dy_ref[...])
        dx_ref[...] = dx + e_ref[...] if has_extra else dx

        @pl.when(pl.program_id(0) == 0)
        def _():
            dg_ref[...] = jnp.zeros_like(dg_ref)

        dg_ref[...] += dg

    ins = [x, g, dy] + ([extra] if has_extra else [])
    specs = [_row_spec(tr, D), _full_spec((1, D)), _row_spec(tr, D)] + ([_row_spec(tr, D)] if has_extra else [])
    return _pcall(body, name=name, grid=(R // tr,), in_specs=specs,
                  out_specs=[_row_spec(tr, D), _full_spec((1, D))],
                  out_shape=[jax.ShapeDtypeStruct((R, D), F32), jax.ShapeDtypeStruct((1, D), F32)],
                  compiler_params=_cp(("arbitrary",)))(*ins)


def _head_indicator(C, hp):
    e = (jnp.arange(C)[:, None] // HEAD == jnp.arange(hp)[None, :]).astype(F32)
    return e, e.T


def _prep_fn(C, L, ps, prev, mu, w0, a0, k_k, k_a, wdu, wiu, E, ET):
    sh = ps + (prev - ps) * mu
    r, k, v = sh[:, :C], sh[:, C:2 * C], sh[:, 2 * C:3 * C]
    wl, al = sh[:, 3 * C:3 * C + L], sh[:, 3 * C + L:3 * C + 2 * L]
    wd = w0 + _mm(jnp.tanh(wl), wdu, False, False, "bf")
    w_pre = -_softplus(-wd) - 0.5
    lw = -jnp.exp(w_pre)
    alpha = _sigmoid(a0 + _mm(al, wiu, False, False, "bf"))
    kk = k * k_k
    ss = _mm(kk * kk, E, False, False, "hi")
    kk = kk * _mm(lax.rsqrt(jnp.maximum(ss, 1e-24)), ET, False, False, "hi")
    k_mod = k * (1.0 + (alpha - 1.0) * k_a)
    return r, lw, k_mod, v, -kk, kk * alpha


def _prep_fwd(p, prev, params, C, L, tr=128):
    T = p.shape[0]
    SH = 3 * C + 2 * L
    tr = _tile(T, tr)

    def body(ps_ref, prev_ref, mu, w0, a0, kk_, ka_, wdu, wiu, E, ET, *outs):
        vals = _prep_fn(C, L, ps_ref[...], prev_ref[...], mu[...], w0[...], a0[...], kk_[...], ka_[...],
                        wdu[...], wiu[...], E[...], ET[...])
        for o, v in zip(outs, vals):
            o[...] = v

    pspecs = [_full_spec(a.shape) for a in params]
    return _pcall(body, name="rwkv_prep_fwd", grid=(T // tr,),
                  in_specs=[_row_spec(tr, SH), _row_spec(tr, SH)] + pspecs,
                  out_specs=[_row_spec(tr, C)] * 6,
                  out_shape=[jax.ShapeDtypeStruct((T, C), F32)] * 6,
                  compiler_params=_cp(("arbitrary",)))(p, prev, *params)


def _prep_bwd(p, prev, params, cts, C, L, tr=64):
    T = p.shape[0]
    SH = 3 * C + 2 * L
    tr = _tile(T, tr)
    nparam = 7

    def body(ps_ref, prev_ref, mu, w0, a0, kk_, ka_, wdu, wiu, E, ET, c0, c1, c2, c3, c4, c5, e0, e2, e3,
             dps_ref, dprev_ref, *dpar):
        f = functools.partial(_prep_fn, C, L)
        fe = lambda ps, prev, *par: f(ps, prev, *par, E[...], ET[...])
        _, vjp = jax.vjp(fe, ps_ref[...], prev_ref[...], mu[...], w0[...], a0[...], kk_[...], ka_[...], wdu[...], wiu[...])
        grads = vjp((c0[...] + e0[...], c1[...], c2[...] + e2[...], c3[...] + e3[...], c4[...], c5[...]))
        dps_ref[...] = grads[0]
        dprev_ref[...] = grads[1]

        @pl.when(pl.program_id(0) == 0)
        def _():
            for d in dpar:
                d[...] = jnp.zeros_like(d)

        for d, gval in zip(dpar, grads[2:]):
            d[...] += gval

    pspecs = [_full_spec(a.shape) for a in params]
    par_shapes = [a.shape for a in params[:nparam]]
    return _pcall(body, name="rwkv_prep_bwd", grid=(T // tr,),
                  in_specs=[_row_spec(tr, SH), _row_spec(tr, SH)] + pspecs + [_row_spec(tr, C)] * 9,
                  out_specs=[_row_spec(tr, SH), _row_spec(tr, SH)] + [_full_spec(s) for s in par_shapes],
                  out_shape=[jax.ShapeDtypeStruct((T, SH), F32)] * 2 + [jax.ShapeDtypeStruct(s, F32) for s in par_shapes],
                  compiler_params=_cp(("arbitrary",)))(p, prev, *params, *cts)


def _shift_combine(d_direct, d_prev_up, tr=256):
    T, W = d_direct.shape
    tr = _tile(T, tr)

    def body(a_ref, b_ref, o_ref):
        o_ref[...] = (a_ref[...] + b_ref[...]).astype(BF16)

    return _pcall(body, name="shift_combine", grid=(T // tr,),
                  in_specs=[_row_spec(tr, W)] * 2, out_specs=_row_spec(tr, W),
                  out_shape=jax.ShapeDtypeStruct((T, W), BF16), compiler_params=_cp(("arbitrary",)))(d_direct, d_prev_up)


def _mix_fn(y, r, kmod, v, g_rwkv, yfox, g_fox, ymem, g_mq, lnw, lnb, rk, E, ET):
    inv = 1.0 / HEAD
    mean = _mm(y, E, False, False, "hi") * inv
    yc = y - _mm(mean, ET, False, False, "hi")
    var = _mm(yc * yc, E, False, False, "hi") * inv
    yn = yc * _mm(lax.rsqrt(var + GN_EPS), ET, False, False, "hi") * lnw + lnb
    bonus = _mm(_mm(r * kmod * rk, E, False, False, "hi"), ET, False, False, "hi") * v
    o1 = (yn + bonus) * _silu(g_rwkv)
    return jnp.concatenate([o1, yfox * _silu(g_fox), ymem * _silu(g_mq)], axis=1)


def _mix_specs(tr, C, MW):
    return [_row_spec(tr, C)] * 7 + [_row_spec(tr, MW)] * 2


def _mix_fwd(acts, params, C, MW, tr=128):
    T = acts[0].shape[0]
    D = 2 * C + MW
    tr = _tile(T, tr)

    def body(y_, r_, k_, v_, g1, yf, g2, ym, g3, lnw, lnb, rk, E, ET, o_ref):
        o_ref[...] = _mix_fn(y_[...], r_[...], k_[...], v_[...], g1[...], yf[...], g2[...], ym[...], g3[...],
                             lnw[...], lnb[...], rk[...], E[...], ET[...]).astype(BF16)

    return _pcall(body, name="mix_fwd", grid=(T // tr,),
                  in_specs=_mix_specs(tr, C, MW) + [_full_spec(a.shape) for a in params],
                  out_specs=_row_spec(tr, D), out_shape=jax.ShapeDtypeStruct((T, D), BF16),
                  compiler_params=_cp(("arbitrary",)))(*acts, *params)


def _mix_bwd(acts, params, dycat, C, MW, tr=64):
    T = acts[0].shape[0]
    D = 2 * C + MW
    tr = _tile(T, tr)

    def body(y_, r_, k_, v_, g1, yf, g2, ym, g3, lnw, lnb, rk, E, ET, dy_ref, *outs):
        fe = lambda *a: _mix_fn(*a, E[...], ET[...])
        _, vjp = jax.vjp(fe, y_[...], r_[...], k_[...], v_[...], g1[...], yf[...], g2[...], ym[...], g3[...],
                         lnw[...], lnb[...], rk[...])
        grads = vjp(dy_ref[...])
        for o, gval in zip(outs[:9], grads[:9]):
            o[...] = gval

        @pl.when(pl.program_id(0) == 0)
        def _():
            for o in outs[9:]:
                o[...] = jnp.zeros_like(o)

        for o, gval in zip(outs[9:], grads[9:]):
            o[...] += gval

    widths = [C, C, C, C, C, C, C, MW, MW]
    return _pcall(body, name="mix_bwd", grid=(T // tr,),
                  in_specs=_mix_specs(tr, C, MW) + [_full_spec(a.shape) for a in params] + [_row_spec(tr, D)],
                  out_specs=[_row_spec(tr, w) for w in widths] + [_full_spec((1, C))] * 3,
                  out_shape=[jax.ShapeDtypeStruct((T, w), F32) for w in widths] + [jax.ShapeDtypeStruct((1, C), F32)] * 3,
                  compiler_params=_cp(("arbitrary",)))(*acts, *params, dycat)


def _post(yo, x, tgt, g_post, tr=128):
    T, D = x.shape
    tr = _tile(T, tr)

    def body(yo_ref, x_ref, t_ref, g_ref, dyo_ref, dout_ref, loss_ref, dg_ref):
        n, vjp = jax.vjp(_rms, yo_ref[...], g_ref[...])
        diff = (x_ref[...] + n) - t_ref[...]
        part = 0.5 * jnp.sum(jnp.mean(diff * diff, axis=-1, keepdims=True), axis=0, keepdims=True)
        d_out = diff * (1.0 / D)
        dyo, dg = vjp(d_out)
        dyo_ref[...] = dyo.astype(BF16)
        dout_ref[...] = d_out

        @pl.when(pl.program_id(0) == 0)
        def _():
            loss_ref[...] = jnp.zeros_like(loss_ref)
            dg_ref[...] = jnp.zeros_like(dg_ref)

        loss_ref[...] += jnp.broadcast_to(part, loss_ref.shape)
        dg_ref[...] += dg

    return _pcall(body, name="post_loss", grid=(T // tr,),
                  in_specs=[_row_spec(tr, D)] * 3 + [_full_spec((1, D))],
                  out_specs=[_row_spec(tr, D), _row_spec(tr, D), _full_spec((1, LANE)), _full_spec((1, D))],
                  out_shape=[jax.ShapeDtypeStruct((T, D), BF16), jax.ShapeDtypeStruct((T, D), F32),
                             jax.ShapeDtypeStruct((1, LANE), F32), jax.ShapeDtypeStruct((1, D), F32)],
                  compiler_params=_cp(("arbitrary",)))(yo, x, tgt, g_post)


def _memattn_fn(MW, q, mkv):
    hd = MW // MEM_HEADS
    scale = hd ** -0.5
    outs = []
    for h in range(MEM_HEADS):
        qh = q[:, h * hd:(h + 1) * hd]
        kh = mkv[:, h * hd:(h + 1) * hd]
        vh = mkv[:, MW + h * hd:MW + (h + 1) * hd]
        s = _mm(qh, kh, False, True, "bf") * scale
        e = jnp.exp(s - lax.stop_gradient(jnp.max(s, axis=-1, keepdims=True)))
        pr = e / jnp.sum(e, axis=-1, keepdims=True)
        outs.append(_mm(pr, vh, False, False, "bf"))
    return jnp.concatenate(outs, axis=1)


def _memattn_fwd(p, mkv, MW, tr=256):
    T = p.shape[0]
    tr = _tile(T, tr)

    def body(q_ref, kv_ref, o_ref):
        o_ref[...] = _memattn_fn(MW, q_ref[...], kv_ref[...])

    return _pcall(body, name="memattn_fwd", grid=(T // tr,),
                  in_specs=[_row_spec(tr, MW), _full_spec(mkv.shape)], out_specs=_row_spec(tr, MW),
                  out_shape=jax.ShapeDtypeStruct((T, MW), F32), compiler_params=_cp(("arbitrary",)))(p, mkv)


def _memattn_bwd(p, mkv, do, MW, tr=256):
    T = p.shape[0]
    tr = _tile(T, tr)

    def body(q_ref, kv_ref, do_ref, dq_ref, dkv_ref):
        _, vjp = jax.vjp(functools.partial(_memattn_fn, MW), q_ref[...], kv_ref[...])
        dq, dkv = vjp(do_ref[...])
        dq_ref[...] = dq

        @pl.when(pl.program_id(0) == 0)
        def _():
            dkv_ref[...] = jnp.zeros_like(dkv_ref)

        dkv_ref[...] += dkv

    return _pcall(body, name="memattn_bwd", grid=(T // tr,),
                  in_specs=[_row_spec(tr, MW), _full_spec(mkv.shape), _row_spec(tr, MW)],
                  out_specs=[_row_spec(tr, MW), _full_spec(mkv.shape)],
                  out_shape=[jax.ShapeDtypeStruct((T, MW), F32), jax.ShapeDtypeStruct(mkv.shape, F32)],
                  compiler_params=_cp(("arbitrary",)))(p, mkv, do)


def _fox_cum_fwd(p, b_f_pad, off, blk=512):
    T = p.shape[0]
    blk = _tile(T, blk)

    def body(f_ref, b_ref, cum_ref, cumt_ref, carry):
        @pl.when(pl.program_id(0) == 0)
        def _():
            carry[...] = jnp.zeros_like(carry)

        z = f_ref[...] + b_ref[...]
        logf = -_softplus(-z)
        row = lax.broadcasted_iota(jnp.int32, (blk, blk), 0)
        col = lax.broadcasted_iota(jnp.int32, (blk, blk), 1)
        tri = (col <= row).astype(F32)
        c = _dg(tri, logf, False, False, "hi") + carry[...]
        cum_ref[...] = c
        cumt_ref[...] = c.T
        carry[...] += jnp.sum(logf, axis=0, keepdims=True)

    return _pcall(body, name="fox_cum_fwd", grid=(T // blk,),
                  in_specs=[_row_spec(blk, LANE, off // LANE), _full_spec((1, LANE))],
                  out_specs=[_row_spec(blk, LANE), pl.BlockSpec((LANE, blk), lambda i: (0, i))],
                  out_shape=[jax.ShapeDtypeStruct((T, LANE), F32), jax.ShapeDtypeStruct((LANE, T), F32)],
                  scratch_shapes=[pltpu.VMEM((1, LANE), F32)], compiler_params=_cp(("arbitrary",)))(p, b_f_pad)


def _fox_cum_bwd(p, b_f_pad, dcum, off, blk=512):
    T = p.shape[0]
    blk = _tile(T, blk)
    nb = T // blk

    def body(f_ref, b_ref, dc_ref, df_ref, db_ref, carry):
        @pl.when(pl.program_id(0) == 0)
        def _():
            carry[...] = jnp.zeros_like(carry)
            db_ref[...] = jnp.zeros_like(db_ref)

        row = lax.broadcasted_iota(jnp.int32, (blk, blk), 0)
        col = lax.broadcasted_iota(jnp.int32, (blk, blk), 1)
        tri = (col >= row).astype(F32)
        dlogf = _dg(tri, dc_ref[...], False, False, "hi") + carry[...]
        carry[...] += jnp.sum(dc_ref[...], axis=0, keepdims=True)
        z = f_ref[...] + b_ref[...]
        dz = dlogf * (1.0 - _sigmoid(z))
        df_ref[...] = dz
        db_ref[...] += jnp.sum(dz, axis=0, keepdims=True)

    rev = lambda i: (nb - 1 - i, 0)
    return _pcall(body, name="fox_cum_bwd", grid=(nb,),
                  in_specs=[pl.BlockSpec((blk, LANE), lambda i: (nb - 1 - i, off // LANE)), _full_spec((1, LANE)),
                            pl.BlockSpec((blk, LANE), rev)],
                  out_specs=[pl.BlockSpec((blk, LANE), rev), _full_spec((1, LANE))],
                  out_shape=[jax.ShapeDtypeStruct((T, LANE), F32), jax.ShapeDtypeStruct((1, LANE), F32)],
                  scratch_shapes=[pltpu.VMEM((1, LANE), F32)], compiler_params=_cp(("arbitrary",)))(p, b_f_pad, dcum)


FOX_SCALE = HEAD ** -0.5
FOX_TQ, FOX_TK = 512, 512


def _fox_scores(q, k, ck, q0=None, k0=None):
    s = _dg(q, k, False, True, "bf") - ck
    if q0 is None:
        return s
    qpos = q0 + lax.broadcasted_iota(jnp.int32, s.shape, 0)
    kpos = k0 + lax.broadcasted_iota(jnp.int32, s.shape, 1)
    return jnp.where(kpos <= qpos, s, NEG)


def _fox_c0(ck_ref, hh, pos):
    return ck_ref[0, hh:hh + 1, pl.ds(pl.multiple_of(pos, LANE), LANE)][:, 0:1]


def _fox_tiles(T):
    assert T % LANE == 0
    tq, tk = _tile(T, FOX_TQ), _tile(T, FOX_TK)
    shift = (tk // tq).bit_length() - 1
    assert tk == tq << shift
    return tq, tk, shift


def _fox_fwd(p, ck, C, offs):
    T = p.shape[0]
    tq, tk, shift = _fox_tiles(T)
    npair = C // LANE
    cb = lambda name: offs[name] // LANE

    def body(q_ref, k_ref, v_ref, ck_ref, o_ref, lse_ref):
        i = pl.program_id(1)
        nfull = i >> shift
        lse_ref[...] = jnp.zeros_like(lse_ref)
        heads = [slice(hh * HEAD, (hh + 1) * HEAD) for hh in range(2)]
        qs = [(q_ref[:, sl] * FOX_SCALE).astype(BF16) for sl in heads]
        c0s = [_fox_c0(ck_ref, hh, i * tq) for hh in range(2)]

        def step(j, carry, masked):
            off = pl.multiple_of(j * tk, tk)
            out = []
            for hh, sl in enumerate(heads):
                m, l, acc = carry[hh]
                k = k_ref[pl.ds(off, tk), sl].astype(BF16)
                v = v_ref[pl.ds(off, tk), sl].astype(BF16)
                ckv = ck_ref[0, hh:hh + 1, pl.ds(off, tk)] - c0s[hh]
                s = _fox_scores(qs[hh], k, ckv, i * tq, off) if masked else _fox_scores(qs[hh], k, ckv)
                m_new = jnp.maximum(m, jnp.max(s, axis=-1, keepdims=True))
                pr = jnp.exp(s - m_new)
                al = jnp.exp(m - m_new)
                l = al * l + jnp.sum(pr, axis=-1, keepdims=True)
                acc = al * acc + _dg(pr, v, False, False, "bf")
                out.append((m_new, l, acc))
            return tuple(out)

        one = (jnp.full((tq, 1), NEG, F32), jnp.zeros((tq, 1), F32), jnp.zeros((tq, HEAD), F32))
        carry = lax.fori_loop(0, nfull, lambda j, c: step(j, c, False), (one, one))
        for hh, (m, l, acc) in enumerate(step(nfull, carry, True)):
            o_ref[:, heads[hh]] = acc / l
            lse_ref[0, :, hh:hh + 1] = m + jnp.log(l)

    return _pcall(body, name="fox_fwd", grid=(npair, T // tq),
                  in_specs=[pl.BlockSpec((tq, LANE), lambda h, i: (i, cb("fq") + h)),
                            pl.BlockSpec((T, LANE), lambda h, i: (0, cb("fk") + h)),
                            pl.BlockSpec((T, LANE), lambda h, i: (0, cb("fv") + h)),
                            pl.BlockSpec((1, 8, T), lambda h, i: (h, 0, 0))],
                  out_specs=[pl.BlockSpec((tq, LANE), lambda h, i: (i, h)),
                             pl.BlockSpec((1, tq, 8), lambda h, i: (h, i, 0))],
                  out_shape=[jax.ShapeDtypeStruct((T, C), F32), jax.ShapeDtypeStruct((npair, T, 8), F32)],
                  compiler_params=_cp(("arbitrary", "arbitrary")))(p, p, p, ck)


def _fox_delta(p, ck, do, lse, C, offs):
    T = p.shape[0]
    tq, tk, shift = _fox_tiles(T)
    npair = C // LANE
    cb = lambda name: offs[name] // LANE

    def body(q_ref, k_ref, v_ref, ck_ref, do_ref, lse_ref, d_ref):
        i = pl.program_id(1)
        nfull = i >> shift
        d_ref[...] = jnp.zeros_like(d_ref)
        heads = [slice(hh * HEAD, (hh + 1) * HEAD) for hh in range(2)]
        qs = [(q_ref[:, sl] * FOX_SCALE).astype(BF16) for sl in heads]
        dos = [do_ref[:, sl].astype(BF16) for sl in heads]
        lses = [lse_ref[0, :, hh:hh + 1] for hh in range(2)]
        c0s = [_fox_c0(ck_ref, hh, i * tq) for hh in range(2)]

        def step(j, accs, masked):
            off = pl.multiple_of(j * tk, tk)
            out = []
            for hh, sl in enumerate(heads):
                k = k_ref[pl.ds(off, tk), sl].astype(BF16)
                v = v_ref[pl.ds(off, tk), sl].astype(BF16)
                ckv = ck_ref[0, hh:hh + 1, pl.ds(off, tk)] - c0s[hh]
                s = _fox_scores(qs[hh], k, ckv, i * tq, off) if masked else _fox_scores(qs[hh], k, ckv)
                pr = jnp.exp(s - lses[hh])
                dp = _dg(dos[hh], v, False, True, "bf")
                out.append(accs[hh] + jnp.sum(pr * dp, axis=-1, keepdims=True))
            return tuple(out)

        z = jnp.zeros((tq, 1), F32)
        accs = lax.fori_loop(0, nfull, lambda j, c: step(j, c, False), (z, z))
        for hh, acc in enumerate(step(nfull, accs, True)):
            d_ref[0, :, hh:hh + 1] = acc

    return _pcall(body, name="fox_delta", grid=(npair, T // tq),
                  in_specs=[pl.BlockSpec((tq, LANE), lambda h, i: (i, cb("fq") + h)),
                            pl.BlockSpec((T, LANE), lambda h, i: (0, cb("fk") + h)),
                            pl.BlockSpec((T, LANE), lambda h, i: (0, cb("fv") + h)),
                            pl.BlockSpec((1, 8, T), lambda h, i: (h, 0, 0)),
                            pl.BlockSpec((tq, LANE), lambda h, i: (i, h)),
                            pl.BlockSpec((1, tq, 8), lambda h, i: (h, i, 0))],
                  out_specs=pl.BlockSpec((1, tq, 8), lambda h, i: (h, i, 0)),
                  out_shape=jax.ShapeDtypeStruct((npair, T, 8), F32),
                  compiler_params=_cp(("arbitrary", "arbitrary")))(p, p, p, ck, do, lse)


def _fox_bwd(p, ck, delta, do, lse, C, offs):
    T = p.shape[0]
    tq, tk, shift = _fox_tiles(T)
    ratio = tk // tq
    nq = T // tq
    npair = C // LANE
    cb = lambda name: offs[name] // LANE

    def body(q_ref, k_ref, v_ref, ck_ref, ckall_ref, dl_ref, do_ref, lse_ref, dq_ref, dk_ref, dv_ref, dck_ref):
        j = pl.program_id(1)

        @pl.when(j == 0)
        def _():
            dq_ref[...] = jnp.zeros_like(dq_ref)

        dck_ref[...] = jnp.zeros_like(dck_ref)
        heads = [slice(hh * HEAD, (hh + 1) * HEAD) for hh in range(2)]
        ks = [k_ref[:, sl].astype(BF16) for sl in heads]
        vs = [v_ref[:, sl].astype(BF16) for sl in heads]
        cks = [ck_ref[0, hh:hh + 1, :] for hh in range(2)]

        def step(i, carry, masked):
            off = pl.multiple_of(i * tq, tq)
            out = []
            for hh, sl in enumerate(heads):
                dk, dv, dck = carry[hh]
                q = (q_ref[pl.ds(off, tq), sl] * FOX_SCALE).astype(BF16)
                dov = do_ref[pl.ds(off, tq), sl]
                lsev = lse_ref[0, pl.ds(off, tq), hh:hh + 1]
                ckv = cks[hh] - _fox_c0(ckall_ref, hh, off)
                s = _fox_scores(q, ks[hh], ckv, off, j * tk) if masked else _fox_scores(q, ks[hh], ckv)
                pr = jnp.exp(s - lsev)
                dv = dv + _dg(pr, dov, True, False, "bf")
                dp = _dg(dov, vs[hh], False, True, "bf")
                ds = pr * (dp - dl_ref[0, pl.ds(off, tq), hh:hh + 1])
                dk = dk + _dg(ds, q, True, False, "bf")
                dq_ref[pl.ds(off, tq), sl] += _dg(ds, ks[hh], False, False, "bf") * FOX_SCALE
                out.append((dk, dv, dck - jnp.sum(ds, axis=0, keepdims=True)))
            return tuple(out)

        z = jnp.zeros((tk, HEAD), F32)
        carry = ((z, z, jnp.zeros((1, tk), F32)),) * 2
        for r in range(ratio):
            carry = step(j * ratio + r, carry, True)
        carry = lax.fori_loop((j + 1) * ratio, nq, lambda i, c: step(i, c, False), carry)
        for hh, (dk, dv, dck) in enumerate(carry):
            dk_ref[:, heads[hh]] = dk
            dv_ref[:, heads[hh]] = dv
            dck_ref[0, hh:hh + 1, :] = dck

    full = lambda h, j: (0, h)
    return _pcall(body, name="fox_bwd", grid=(npair, T // tk),
                  in_specs=[pl.BlockSpec((T, LANE), lambda h, j: (0, cb("fq") + h)),
                            pl.BlockSpec((tk, LANE), lambda h, j: (j, cb("fk") + h)),
                            pl.BlockSpec((tk, LANE), lambda h, j: (j, cb("fv") + h)),
                            pl.BlockSpec((1, 8, tk), lambda h, j: (h, 0, j)),
                            pl.BlockSpec((1, 8, T), lambda h, j: (h, 0, 0)),
                            pl.BlockSpec((1, T, 8), lambda h, j: (h, 0, 0)), pl.BlockSpec((T, LANE), full),
                            pl.BlockSpec((1, T, 8), lambda h, j: (h, 0, 0))],
                  out_specs=[pl.BlockSpec((T, LANE), full),
                             pl.BlockSpec((tk, LANE), lambda h, j: (j, h)),
                             pl.BlockSpec((tk, LANE), lambda h, j: (j, h)),
                             pl.BlockSpec((1, 8, tk), lambda h, j: (h, 0, j))],
                  out_shape=[jax.ShapeDtypeStruct((T, C), F32)] * 3 + [jax.ShapeDtypeStruct((npair, 8, T), F32)],
                  compiler_params=_cp(("arbitrary", "arbitrary")))(p, p, p, ck, ck, delta, do, lse)


def _chunk_fn(S0, r, lw, k, v, a, b):
    nh, n = r.shape[0], r.shape[1]
    row = lax.broadcasted_iota(jnp.int32, (nh, n, n), 1)
    col = lax.broadcasted_iota(jnp.int32, (nh, n, n), 2)
    incl, strict = col <= row, col < row
    mm = lambda x, y, ta=False, tb=False: _mm(x, y, ta, tb, "x3")
    g = _mm(incl.astype(F32), lw, False, False, "hi")
    einv = jnp.exp(-g)
    rt, at, bt, kt = r * jnp.exp(g), a * jnp.exp(g - lw), b * einv, k * einv
    a_ab = jnp.where(strict, mm(at, bt, tb=True), 0.0)
    a_ak = jnp.where(strict, mm(at, kt, tb=True), 0.0)
    r_b = jnp.where(incl, mm(rt, bt, tb=True), 0.0)
    r_k = jnp.where(incl, mm(rt, kt, tb=True), 0.0)
    u = mm(at, S0, tb=True) + mm(a_ak, v)
    pw = a_ab
    for it in range(6):
        u = u + mm(pw, u)
        if it < 5:
            pw = mm(pw, pw)
    y = mm(rt, S0, tb=True) + mm(r_b, u) + mm(r_k, v)
    g_end = jnp.sum(lw, axis=1, keepdims=True)
    s_end = (S0 + mm(u, bt, ta=True) + mm(v, kt, ta=True)) * jnp.exp(g_end)
    return y, s_end


SCAN_HEADS = 12
SCAN_ROWS = 64


def _scan_group(C):
    nh = SCAN_HEADS
    while C % (nh * HEAD):
        nh -= 2
    return nh, nh * HEAD


def _scan_fwd(r, lw, k, v, a, b):
    T, C = r.shape
    tc = _tile(T, SCAN_ROWS)
    ncs = tc // CHUNK
    nh, GW = _scan_group(C)
    ngroup = C // GW

    def body(r_ref, lw_ref, k_ref, v_ref, a_ref, b_ref, y_ref, ck_ref, state):
        @pl.when(pl.program_id(1) == 0)
        def _():
            state[...] = jnp.zeros_like(state)

        heads = [slice(hh * HEAD, (hh + 1) * HEAD) for hh in range(nh)]
        split = lambda ref, rows: jnp.stack([ref[rows, sl] for sl in heads])
        st = split(state, slice(None))
        for c in range(ncs):
            rows = slice(c * CHUNK, (c + 1) * CHUNK)
            for hh, sl in enumerate(heads):
                ck_ref[0, c, :, sl] = st[hh]
            y, st = _chunk_fn(st, *[split(ref, rows) for ref in (r_ref, lw_ref, k_ref, v_ref, a_ref, b_ref)])
            for hh, sl in enumerate(heads):
                y_ref[rows, sl] = y[hh]
        for hh, sl in enumerate(heads):
            state[:, sl] = st[hh]

    spec = pl.BlockSpec((tc, GW), lambda h, t: (t, h))
    return _pcall(body, name="rwkv_scan_fwd", grid=(ngroup, T // tc),
                  in_specs=[spec] * 6,
                  out_specs=[spec, pl.BlockSpec((1, ncs, HEAD, GW), lambda h, t: (h, t, 0, 0))],
                  out_shape=[jax.ShapeDtypeStruct((T, C), F32),
                             jax.ShapeDtypeStruct((ngroup, T // CHUNK, HEAD, GW), F32)],
                  scratch_shapes=[pltpu.VMEM((HEAD, GW), F32)],
                  compiler_params=_cp(("arbitrary", "arbitrary")))(r, lw, k, v, a, b)


def _scan_bwd(r, lw, k, v, a, b, ckpt, dy):
    T, C = r.shape
    tc = _tile(T, SCAN_ROWS)
    ncs = tc // CHUNK
    nh, GW = _scan_group(C)
    ngroup = C // GW
    nt = T // tc

    def body(r_ref, lw_ref, k_ref, v_ref, a_ref, b_ref, ck_ref, dy_ref, dr, dlw, dk, dv, da, db, dstate):
        @pl.when(pl.program_id(1) == 0)
        def _():
            dstate[...] = jnp.zeros_like(dstate)

        outs = (dr, dlw, dk, dv, da, db)
        heads = [slice(hh * HEAD, (hh + 1) * HEAD) for hh in range(nh)]
        split = lambda ref, rows: jnp.stack([ref[rows, sl] for sl in heads])
        dst = split(dstate, slice(None))
        for c in reversed(range(ncs)):
            rows = slice(c * CHUNK, (c + 1) * CHUNK)
            s0 = jnp.stack([ck_ref[0, c, :, sl] for sl in heads])
            _, vjp = jax.vjp(_chunk_fn, s0, *[split(ref, rows) for ref in (r_ref, lw_ref, k_ref, v_ref, a_ref, b_ref)])
            grads = vjp((split(dy_ref, rows), dst))
            dst = grads[0]
            for o, gval in zip(outs, grads[1:]):
                for hh, sl in enumerate(heads):
                    o[rows, sl] = gval[hh]
        for hh, sl in enumerate(heads):
            dstate[:, sl] = dst[hh]

    spec = pl.BlockSpec((tc, GW), lambda h, t: (nt - 1 - t, h))
    return _pcall(body, name="rwkv_scan_bwd", grid=(ngroup, nt),
                  in_specs=[spec] * 6 + [pl.BlockSpec((1, ncs, HEAD, GW), lambda h, t: (h, nt - 1 - t, 0, 0)), spec],
                  out_specs=[spec] * 6, out_shape=[jax.ShapeDtypeStruct((T, C), F32)] * 6,
                  scratch_shapes=[pltpu.VMEM((HEAD, GW), F32)],
                  compiler_params=_cp(("arbitrary", "arbitrary")))(r, lw, k, v, a, b, ckpt, dy)


def _adam(w, g, m, v):
    m = ADAM_B1 * m + (1.0 - ADAM_B1) * g
    v = ADAM_B2 * v + (1.0 - ADAM_B2) * (g * g)
    m_hat = m / (1.0 - ADAM_B1 ** ADAM_STEP)
    v_hat = v / (1.0 - ADAM_B2 ** ADAM_STEP)
    return -ADAM_LR * (m_hat / (jnp.sqrt(v_hat) + ADAM_EPS) + ADAM_WD * w), m, v


def _sum_adam(parts, w, m, v, name):
    n, R, W = parts.shape
    tr = _tile(R, max(8, min(256, (1 << 20) // (n * W))))

    def body(p_ref, w_ref, m_ref, v_ref, g_ref, d_ref, nm_ref, nv_ref):
        g = p_ref[0].astype(F32)
        for s in range(1, n):
            g = g + p_ref[s].astype(F32)
        d, nm, nv = _adam(w_ref[...], g, m_ref[...], v_ref[...])
        g_ref[...] = g
        d_ref[...] = d
        nm_ref[...] = nm
        nv_ref[...] = nv

    return _pcall(body, name=name, grid=(R // tr,),
                  in_specs=[pl.BlockSpec((n, tr, W), lambda i: (0, i, 0))] + [_row_spec(tr, W)] * 3,
                  out_specs=[_row_spec(tr, W)] * 4, out_shape=[jax.ShapeDtypeStruct((R, W), F32)] * 4,
                  compiler_params=_cp(("arbitrary",)))(parts, w, m, v)


def _pad_lanes(vec, width):
    return jnp.pad(vec, ((0, 0), (0, width - vec.shape[1])))


def kernel(x, mem, g_pre, w_in, mu_rwkv, w0, w_decay_up, a0, w_iclr_up, k_k, k_a, r_k, ln_x_w, ln_x_b, b_f, g_mem, w_mem_kv, w_out, g_post, loss_target, m_g_pre, m_w_in, m_mu_rwkv, m_w0, m_w_decay_up, m_a0, m_w_iclr_up, m_k_k, m_k_a, m_r_k, m_ln_x_w, m_ln_x_b, m_b_f, m_g_mem, m_w_mem_kv, m_w_out, m_g_post, v_g_pre, v_w_in, v_mu_rwkv, v_w0, v_w_decay_up, v_a0, v_w_iclr_up, v_k_k, v_k_a, v_r_k, v_ln_x_w, v_ln_x_b, v_b_f, v_g_mem, v_w_mem_kv, v_w_out, v_g_post):
    T, D = x.shape[1], x.shape[2]
    C = w0.shape[1]
    L = w_decay_up.shape[1]
    H = C // HEAD
    MW = w_mem_kv.shape[2] // 2
    SH = 3 * C + 2 * L
    IN = NDEV * w_in.shape[2]
    assert IN == SH + 5 * C + H + 2 * MW and D == 2 * C + MW and H % 2 == 0 and H <= LANE
    assert C % LANE == 0 and L % LANE == 0 and MW % (MEM_HEADS * HEAD) == 0 and T % CHUNK == 0
    offs = dict(grw=SH, fq=SH + C, fk=SH + 2 * C, fv=SH + 3 * C, gfx=SH + 4 * C, mq=SH + 5 * C, gmq=SH + 5 * C + MW,
                fl=SH + 5 * C + 2 * MW)
    NI = -(-(offs["fl"] + LANE) // 1024) * 1024
    l_fl = SH + 4 * C

    x2, mem2, tgt2 = x[0], mem[0], loss_target[0]

    wg = _exchange(w_in[0].astype(BF16), "gather_w_in", True)
    w_full = jnp.transpose(wg, (1, 0, 2)).reshape(D, IN)
    w_perm = jnp.concatenate([w_full[:, :l_fl], w_full[:, l_fl + H:], w_full[:, l_fl:l_fl + H],
                              jnp.zeros((D, NI - IN), BF16)], axis=1)

    h = _rmsnorm_fwd(x2, g_pre, "rmsnorm_pre")
    p, w_out_f, w_kv_f, lora = _matmul(
        h, w_perm, False, False, "in_proj",
        riders=[(w_out[0].astype(BF16), True), (w_mem_kv[0].astype(BF16), True),
                (jnp.concatenate([w_decay_up[0], w_iclr_up[0]], axis=0), True)])
    w_out_f, w_kv_f = w_out_f.reshape(D, D), w_kv_f.reshape(D, 2 * MW)
    lora = jnp.transpose(lora, (1, 0, 2)).reshape(2 * L, C)
    wdu_f, wiu_f = lora[:L], lora[L:]

    E, ET = _head_indicator(C, LANE)
    prep_params = [mu_rwkv, w0, a0, k_k, k_a, wdu_f, wiu_f, E, ET]
    mix_params = [ln_x_w, ln_x_b, r_k.reshape(1, C), E, ET]
    b_f_pad = _pad_lanes(b_f, LANE)
    ps = p[:, :SH]
    col = lambda name, w: p[:, offs[name]:offs[name] + w]
    g_rwkv, g_fox, mq, g_mq = col("grw", C), col("gfx", C), col("mq", MW), col("gmq", MW)
    prev = jnp.concatenate([jnp.zeros((1, SH), F32), ps[:-1]], axis=0)
    r, lw, kmod, v, a, b = _prep_fwd(p, prev, prep_params, C, L)
    y_scan, ckpt = _scan_fwd(r, lw, kmod, v, a, b)

    cum, cum_t = _fox_cum_fwd(p, b_f_pad, offs["fl"])
    ck = jnp.pad(cum_t[:H].reshape(H // 2, 2, T), ((0, 0), (0, 6), (0, 0)))
    y_fox, lse = _fox_fwd(p, ck, C, offs)

    memn = _rmsnorm_fwd(mem2, g_mem, "rmsnorm_mem")
    mkv = _matmul(memn, w_kv_f, False, False, "mem_kv_proj")
    y_mem = _memattn_fwd(mq, mkv, MW)

    acts = [y_scan, r, kmod, v, g_rwkv, y_fox, g_fox, y_mem, g_mq]
    ycat = _mix_fwd(acts, mix_params, C, MW)
    yo = _matmul(ycat, w_out_f, False, False, "out_proj")
    d_yo, d_out, loss_part, dg_post = _post(yo, x2, tgt2, g_post)
    loss = lax.psum(loss_part[0, 0], AXES)

    g_w_out = _matmul(ycat, d_yo, True, False, "grad_w_out", BF16)
    d_ycat = _matmul(d_yo, w_out_f, False, True, "d_ycat")
    (d_y, d_r1, d_k1, d_v1, d_grw, d_yfox, d_gfx, d_ymem, d_gmq, dg_lnw, dg_lnb, dg_rk) = _mix_bwd(
        acts, mix_params, d_ycat, C, MW)

    d_mq, d_mkv = _memattn_bwd(mq, mkv, d_ymem, MW)
    g_w_kv = _matmul(memn, d_mkv, True, False, "grad_w_mem_kv", BF16)
    d_memn = _matmul(d_mkv, w_kv_f, False, True, "d_memn")
    _, dg_mem = _rmsnorm_bwd(mem2, g_mem, d_memn, None, "rmsnorm_mem_bwd")

    delta = _fox_delta(p, ck, d_yfox, lse, C, offs)
    d_fq, d_fk, d_fv, d_ck = _fox_bwd(p, ck, delta, d_yfox, lse, C, offs)
    d_cum = _pad_lanes(d_ck[:, :2, :].reshape(H, T).T, LANE)
    d_fl, dg_bf = _fox_cum_bwd(p, b_f_pad, d_cum, offs["fl"])

    d_r, d_lw, d_k, d_v, d_a, d_b = _scan_bwd(r, lw, kmod, v, a, b, ckpt, d_y)
    cts = [d_r, d_lw, d_k, d_v, d_a, d_b, d_r1, d_k1, d_v1]
    (d_ps, d_prev, dg_mu, dg_w0, dg_a0, dg_kk, dg_ka, dg_wdu, dg_wiu) = _prep_bwd(p, prev, prep_params, cts, C, L)
    d_prev_up = jnp.concatenate([d_prev[1:], jnp.zeros((1, SH), F32)], axis=0)
    d_sh = _shift_combine(d_ps, d_prev_up)

    tobf = lambda z: z.astype(BF16)
    dp = jnp.concatenate([d_sh, tobf(d_grw), tobf(d_fq), tobf(d_fk), tobf(d_fv), tobf(d_gfx), tobf(d_mq), tobf(d_gmq),
                          tobf(d_fl), jnp.zeros((T, NI - offs["fl"] - LANE), BF16)], axis=1)
    g_lora = jnp.concatenate([dg_wdu, dg_wiu], axis=0)
    g_w_perm, parts_out, parts_kv, parts_lora = _matmul(
        h, dp, True, False, "grad_w_in", BF16,
        riders=[(g_w_out.reshape(NDEV, D // NDEV, D), False), (g_w_kv.reshape(NDEV, D // NDEV, 2 * MW), False),
                (jnp.transpose(g_lora.reshape(2 * L, NDEV, C // NDEV), (1, 0, 2)), False)])
    g_w_log = jnp.concatenate([g_w_perm[:, :l_fl], g_w_perm[:, offs["fl"]:offs["fl"] + H],
                               g_w_perm[:, l_fl:offs["fl"]]], axis=1)
    d_h, parts_in = _matmul(dp, w_perm, False, True, "d_h", tk=5120,
                            riders=[(jnp.transpose(g_w_log.reshape(D, NDEV, IN // NDEV), (1, 0, 2)), False)])
    grad_x, dg_pre = _rmsnorm_bwd(x2, g_pre, d_h, d_out, "rmsnorm_pre_bwd")

    gw_in, dw_in, nm_w_in, nv_w_in = _sum_adam(parts_in, w_in[0], m_w_in[0], v_w_in[0], "adam_w_in")
    gw_out, dw_out, nm_w_out, nv_w_out = _sum_adam(parts_out, w_out[0], m_w_out[0], v_w_out[0], "adam_w_out")
    gw_kv, dw_kv, nm_w_kv, nv_w_kv = _sum_adam(parts_kv, w_mem_kv[0], m_w_mem_kv[0], v_w_mem_kv[0], "adam_w_mem_kv")
    cat2 = lambda u, w_: jnp.concatenate([u[0], w_[0]], axis=0)
    lora_res = _sum_adam(parts_lora, cat2(w_decay_up, w_iclr_up), cat2(m_w_decay_up, m_w_iclr_up),
                         cat2(v_w_decay_up, v_w_iclr_up), "adam_lora")

    small = [("g_pre", g_pre, m_g_pre, v_g_pre, dg_pre), ("mu_rwkv", mu_rwkv, m_mu_rwkv, v_mu_rwkv, dg_mu),
             ("w0", w0, m_w0, v_w0, dg_w0), ("a0", a0, m_a0, v_a0, dg_a0), ("k_k", k_k, m_k_k, v_k_k, dg_kk),
             ("k_a", k_a, m_k_a, v_k_a, dg_ka), ("r_k", r_k.reshape(1, C), m_r_k.reshape(1, C), v_r_k.reshape(1, C), dg_rk),
             ("ln_x_w", ln_x_w, m_ln_x_w, v_ln_x_w, dg_lnw), ("ln_x_b", ln_x_b, m_ln_x_b, v_ln_x_b, dg_lnb),
             ("b_f", _pad_lanes(b_f, LANE), _pad_lanes(m_b_f, LANE), _pad_lanes(v_b_f, LANE), dg_bf),
             ("g_mem", g_mem, m_g_mem, v_g_mem, dg_mem), ("g_post", g_post, m_g_post, v_g_post, dg_post)]
    widths = [s[1].shape[1] for s in small]
    pack = lambda idx: jnp.concatenate([s[idx] for s in small], axis=1).reshape(-1, LANE)
    parts_small = _exchange(pack(4), "gather_small_grads", True)
    res_small = _sum_adam(parts_small, pack(1), pack(2), pack(3), "adam_small")

    def unpack(flat):
        flat = flat.reshape(1, -1)
        out, o = {}, 0
        for (name, *_), wd in zip(small, widths):
            out[name] = flat[:, o:o + wd]
            o += wd
        out["b_f"] = out["b_f"][:, :H]
        out["r_k"] = out["r_k"].reshape(1, H, HEAD)
        return out

    sg, sd, sm, sv = [unpack(z) for z in res_small]
    big = {"w_in": (gw_in, dw_in, nm_w_in, nv_w_in), "w_out": (gw_out, dw_out, nm_w_out, nv_w_out),
           "w_mem_kv": (gw_kv, dw_kv, nm_w_kv, nv_w_kv),
           "w_decay_up": tuple(z[:L] for z in lora_res), "w_iclr_up": tuple(z[L:] for z in lora_res)}
    order = ["g_pre", "w_in", "mu_rwkv", "w0", "w_decay_up", "a0", "w_iclr_up", "k_k", "k_a", "r_k", "ln_x_w", "ln_x_b",
             "b_f", "g_mem", "w_mem_kv", "w_out", "g_post"]

    def pick(name, idx):
        if name in big:
            return big[name][idx][None]
        return (sg, sd, sm, sv)[idx][name]

    outs = [loss, grad_x[None]]
    for idx in range(4):
        outs += [pick(n, idx) for n in order]
    return tuple(outs)
```

```python
import functools

import jax
import jax.numpy as jnp
from jax import lax
from jax.experimental import pallas as pl
from jax.experimental.pallas import tpu as pltpu

F32, BF16 = jnp.float32, jnp.bfloat16
HI = lax.Precision.HIGHEST
NDEV = 8
AXES = ("x", "y", "c")
HEAD = 64
CHUNK = 64
MEM_HEADS = 4
LANE = 128
RMS_EPS = 1e-6
GN_EPS = 64e-5
NEG = -1e30
ADAM_LR, ADAM_B1, ADAM_B2, ADAM_EPS, ADAM_WD, ADAM_STEP = 0.001, 0.9, 0.999, 1e-08, 0.01, 10
VMEM_LIMIT = 56 * 1024 * 1024


def _pcall(body, **kw):
    return pl.pallas_call(body, **kw)


def _cp(sem=None, vmem=VMEM_LIMIT):
    return pltpu.CompilerParams(dimension_semantics=sem, vmem_limit_bytes=vmem)


def _tile(n, pref):
    for t in (pref, 1024, 512, 256, 128, 64, 32, 16, 8):
        if t <= pref and n % t == 0:
            return t
    return n


def _dg(a, b, ta, tb, mode):
    nb = a.ndim - 2
    ca = nb + (0 if ta else 1)
    cb = nb + (1 if tb else 0)
    dims = (((ca,), (cb,)), (tuple(range(nb)), tuple(range(nb))))
    if mode == "x3":
        a_hi, b_hi = a.astype(BF16), b.astype(BF16)
        a_lo, b_lo = (a - a_hi.astype(F32)).astype(BF16), (b - b_hi.astype(F32)).astype(BF16)
        dot = lambda u, w: lax.dot_general(u, w, dims, preferred_element_type=F32)
        return dot(a_hi, b_hi) + (dot(a_hi, b_lo) + dot(a_lo, b_hi))
    if mode == "x2":
        a_hi, b = a.astype(BF16), b.astype(BF16)
        a_lo = (a - a_hi.astype(F32)).astype(BF16)
        dot = lambda u: lax.dot_general(u, b, dims, preferred_element_type=F32)
        return dot(a_hi) + dot(a_lo)
    if mode == "bf":
        a, b, prec = a.astype(BF16), b.astype(BF16), None
    else:
        prec = HI
    return lax.dot_general(a, b, dims, preferred_element_type=F32, precision=prec)


@functools.partial(jax.custom_vjp, nondiff_argnums=(2, 3, 4))
def _mm(a, b, ta, tb, mode):
    return _dg(a, b, ta, tb, mode)


def _mm_fwd(a, b, ta, tb, mode):
    return _dg(a, b, ta, tb, mode), (a, b)


def _mm_bwd(ta, tb, mode, res, g):
    a, b = res
    da = _dg(g, b, False, not tb, mode) if not ta else _dg(b, g, tb, True, mode)
    db = _dg(a, g, not ta, False, mode) if not tb else _dg(g, a, True, ta, mode)
    return da, db


_mm.defvjp(_mm_fwd, _mm_bwd)


@jax.custom_vjp
def _seg(a, e):
    return _dg(a, e, False, False, "x2")


def _seg_fwd(a, e):
    return _dg(a, e, False, False, "x2"), e


def _seg_bwd(e, g):
    return _dg(g, e, False, True, "x2"), jnp.zeros_like(e)


_seg.defvjp(_seg_fwd, _seg_bwd)


def _sigmoid(z):
    return 1.0 / (1.0 + jnp.exp(-z))


def _softplus(z):
    return jnp.maximum(z, 0.0) + jnp.log(1.0 + jnp.exp(-jnp.abs(z)))


def _silu(z):
    return z * _sigmoid(z)


def _rms(x, g):
    return x * lax.rsqrt(jnp.mean(x * x, axis=-1, keepdims=True) + RMS_EPS) * g


HBM_SPEC = pl.BlockSpec(memory_space=pltpu.HBM)
EXCHANGE_SEMS = [pltpu.SemaphoreType.DMA((NDEV - 1,)), pltpu.SemaphoreType.DMA((NDEV - 1,)), pltpu.SemaphoreType.DMA(())]


def _exchange_copies(gather, x_ref, o_ref, send_sems, recv_sems, local_sem, arrivals):
    ix, iy, ic = lax.axis_index("x"), lax.axis_index("y"), lax.axis_index("c")
    me = 4 * ix + 2 * iy + ic

    def src(dest):
        return x_ref if gather else x_ref.at[dest]

    mine = pltpu.make_async_copy(src(me), o_ref.at[me], local_sem)
    pairs = []
    for k in range(1, NDEV):
        px = 1 - ix if (k >> 2) & 1 else ix
        py = 1 - iy if (k >> 1) & 1 else iy
        pc = 1 - ic if k & 1 else ic
        peer = 4 * px + 2 * py + pc
        send = pltpu.make_async_remote_copy(
            src_ref=src(peer), dst_ref=o_ref.at[me], send_sem=send_sems.at[k - 1], recv_sem=recv_sems.at[k - 1],
            device_id=(px, py, pc), device_id_type=pl.DeviceIdType.MESH)
        arrival = arrivals and pltpu.make_async_remote_copy(
            src_ref=src(peer), dst_ref=o_ref.at[peer], send_sem=send_sems.at[k - 1], recv_sem=recv_sems.at[k - 1],
            device_id=(ix, iy, ic), device_id_type=pl.DeviceIdType.MESH)
        pairs.append((send, arrival))
    return mine, pairs


def _exchange_start(*args):
    mine, pairs = _exchange_copies(*args, arrivals=False)
    mine.start()
    for send, _ in pairs:
        send.start()


def _exchange_wait(*args):
    mine, pairs = _exchange_copies(*args, arrivals=True)
    for send, arrival in pairs:
        send.wait_send()
        arrival.wait_recv()
    mine.wait()


def _exchange_shape(x, gather):
    return jax.ShapeDtypeStruct((NDEV,) + tuple(x.shape if gather else x.shape[1:]), x.dtype)


def _exchange(x, name, gather):
    def body(x_ref, o_ref, *sems):
        _exchange_start(gather, x_ref, o_ref, *sems)
        _exchange_wait(gather, x_ref, o_ref, *sems)

    return _pcall(body, name=name, out_shape=_exchange_shape(x, gather), in_specs=[HBM_SPEC], out_specs=HBM_SPEC,
                  scratch_shapes=list(EXCHANGE_SEMS))(x)


MAX_FULL_K = 4096


def _matmul(a, b, ta, tb, name, out_dtype=F32, tm=1024, tn=1024, tk=3072, riders=()):
    M, K = (a.shape[1], a.shape[0]) if ta else a.shape
    N = b.shape[0] if tb else b.shape[1]
    assert (b.shape[1] if tb else b.shape[0]) == K
    if K <= MAX_FULL_K:
        tk = K
    else:
        tm, tk = min(tm, 512), _tile(K, tk)
    tm, tn = _tile(M, tm), _tile(N, tn)
    grid = (M // tm, N // tn, K // tk)
    nk, nr = grid[2], len(riders)

    def body(*refs):
        a_ref, b_ref, x_refs = refs[0], refs[1], refs[2:2 + nr]
        o_ref, xo_refs, rest = refs[2 + nr], refs[3 + nr:3 + 2 * nr], refs[3 + 2 * nr:]
        sems = rest[1:] if nk > 1 else rest
        ids = [pl.program_id(d) for d in range(3)]
        jobs = [(riders[q][1], x_refs[q], xo_refs[q]) + tuple(sems[3 * q:3 * q + 3]) for q in range(nr)]

        if nr:
            @pl.when((ids[0] == 0) & (ids[1] == 0) & (ids[2] == 0))
            def _():
                for job in jobs:
                    _exchange_start(*job)

        if nk == 1:
            o_ref[...] = _dg(a_ref[...], b_ref[...], ta, tb, "bf").astype(o_ref.dtype)
        else:
            acc = rest[0]

            @pl.when(ids[2] == 0)
            def _():
                acc[...] = jnp.zeros_like(acc)

            acc[...] += _dg(a_ref[...], b_ref[...], ta, tb, "bf")

            @pl.when(ids[2] == nk - 1)
            def _():
                o_ref[...] = acc[...].astype(o_ref.dtype)

        if nr:
            @pl.when((ids[0] == grid[0] - 1) & (ids[1] == grid[1] - 1) & (ids[2] == nk - 1))
            def _():
                for job in jobs:
                    _exchange_wait(*job)

    a_spec = pl.BlockSpec((tk, tm), lambda i, j, k: (k, i)) if ta else pl.BlockSpec((tm, tk), lambda i, j, k: (i, k))
    b_spec = pl.BlockSpec((tn, tk), lambda i, j, k: (j, k)) if tb else pl.BlockSpec((tk, tn), lambda i, j, k: (k, j))
    out = _pcall(
        body, name=name, grid=grid,
        in_specs=[a_spec, b_spec] + [HBM_SPEC] * nr,
        out_specs=[pl.BlockSpec((tm, tn), lambda i, j, k: (i, j))] + [HBM_SPEC] * nr,
        out_shape=[jax.ShapeDtypeStruct((M, N), out_dtype)] + [_exchange_shape(x, g) for x, g in riders],
        scratch_shapes=([pltpu.VMEM((tm, tn), F32)] if nk > 1 else []) + list(EXCHANGE_SEMS) * nr,
        compiler_params=_cp(("arbitrary", "arbitrary", "arbitrary")),
    )(a, b, *[x for x, _ in riders])
    return tuple(out) if nr else out[0]


def _row_spec(tr, width, col_block=0):
    return pl.BlockSpec((tr, width), lambda i: (i, col_block))


def _full_spec(shape):
    nd = len(shape)
    return pl.BlockSpec(tuple(shape), lambda i: (0,) * nd)


def _rmsnorm_fwd(x, g, name, tr=256):
    R, D = x.shape
    tr = _tile(R, tr)

    def body(x_ref, g_ref, o_ref):
        o_ref[...] = _rms(x_ref[...], g_ref[...]).astype(BF16)

    return _pcall(body, name=name, grid=(R // tr,),
                  in_specs=[_row_spec(tr, D), _full_spec((1, D))], out_specs=_row_spec(tr, D),
                  out_shape=jax.ShapeDtypeStruct((R, D), BF16), compiler_params=_cp(("arbitrary",)))(x, g)


def _rmsnorm_bwd(x, g, dy, extra, name, tr=128):
    R, D = x.shape
    tr = _tile(R, tr)
    has_extra = extra is not None

    def body(*refs):
        if has_extra:
            x_ref, g_ref, dy_ref, e_ref, dx_ref, dg_ref = refs
        else:
            x_ref, g_ref, dy_ref, dx_ref, dg_ref = refs
        _, vjp = jax.vjp(_rms, x_ref[...], g_ref[...])
        dx, dg = vjp(dy_ref[...])
        dx_ref[...] = dx + e_ref[...] if has_extra else dx

        @pl.when(pl.program_id(0) == 0)
        def _():
            dg_ref[...] = jnp.zeros_like(dg_ref)

        dg_ref[...] += dg

    ins = [x, g, dy] + ([extra] if has_extra else [])
    specs = [_row_spec(tr, D), _full_spec((1, D)), _row_spec(tr, D)] + ([_row_spec(tr, D)] if has_extra else [])
    return _pcall(body, name=name, grid=(R // tr,), in_specs=specs,
                  out_specs=[_row_spec(tr, D), _full_spec((1, D))],
                  out_shape=[jax.ShapeDtypeStruct((R, D), F32), jax.ShapeDtypeStruct((1, D), F32)],
                  compiler_params=_cp(("arbitrary",)))(*ins)


def _head_indicator(C, hp):
    e = (jnp.arange(C)[:, None] // HEAD == jnp.arange(hp)[None, :]).astype(F32)
    return e, e.T


def _prep_fn(C, L, ps, prev, mu, w0, a0, k_k, k_a, wdu, wiu, E, ET):
    sh = ps + (prev - ps) * mu
    r, k, v = sh[:, :C], sh[:, C:2 * C], sh[:, 2 * C:3 * C]
    wl, al = sh[:, 3 * C:3 * C + L], sh[:, 3 * C + L:3 * C + 2 * L]
    wd = w0 + _mm(jnp.tanh(wl), wdu, False, False, "bf")
    w_pre = -_softplus(-wd) - 0.5
    lw = -jnp.exp(w_pre)
    alpha = _sigmoid(a0 + _mm(al, wiu, False, False, "bf"))
    kk = k * k_k
    ss = _seg(kk * kk, E)
    kk = kk * _seg(lax.rsqrt(jnp.maximum(ss, 1e-24)), ET)
    k_mod = k * (1.0 + (alpha - 1.0) * k_a)
    return r, lw, k_mod, v, -kk, kk * alpha


def _prep_fwd(p, prev, params, C, L, tr=128):
    T = p.shape[0]
    SH = 3 * C + 2 * L
    tr = _tile(T, tr)

    def body(ps_ref, prev_ref, mu, w0, a0, kk_, ka_, wdu, wiu, E, ET, *outs):
        vals = _prep_fn(C, L, ps_ref[...], prev_ref[...], mu[...], w0[...], a0[...], kk_[...], ka_[...],
                        wdu[...], wiu[...], E[...], ET[...])
        for o, v in zip(outs, vals):
            o[...] = v

    pspecs = [_full_spec(a.shape) for a in params]
    return _pcall(body, name="rwkv_prep_fwd", grid=(T // tr,),
                  in_specs=[_row_spec(tr, SH), _row_spec(tr, SH)] + pspecs,
                  out_specs=[_row_spec(tr, C)] * 6,
                  out_shape=[jax.ShapeDtypeStruct((T, C), F32)] * 6,
                  compiler_params=_cp(("arbitrary",)))(p, prev, *params)


def _prep_bwd(p, prev, params, cts, C, L, tr=128):
    T = p.shape[0]
    SH = 3 * C + 2 * L
    tr = _tile(T, tr)
    nparam = 7

    def body(ps_ref, prev_ref, mu, w0, a0, kk_, ka_, wdu, wiu, E, ET, c0, c1, c2, c3, c4, c5, e0, e2, e3,
             dps_ref, dprev_ref, *dpar):
        f = functools.partial(_prep_fn, C, L)
        fe = lambda ps, prev, *par: f(ps, prev, *par, E[...], ET[...])
        _, vjp = jax.vjp(fe, ps_ref[...], prev_ref[...], mu[...], w0[...], a0[...], kk_[...], ka_[...], wdu[...], wiu[...])
        grads = vjp((c0[...] + e0[...], c1[...], c2[...] + e2[...], c3[...] + e3[...], c4[...], c5[...]))
        dps_ref[...] = grads[0]
        dprev_ref[...] = grads[1]

        @pl.when(pl.program_id(0) == 0)
        def _():
            for d in dpar:
                d[...] = jnp.zeros_like(d)

        for d, gval in zip(dpar, grads[2:]):
            d[...] += gval

    pspecs = [_full_spec(a.shape) for a in params]
    par_shapes = [a.shape for a in params[:nparam]]
    return _pcall(body, name="rwkv_prep_bwd", grid=(T // tr,),
                  in_specs=[_row_spec(tr, SH), _row_spec(tr, SH)] + pspecs + [_row_spec(tr, C)] * 9,
                  out_specs=[_row_spec(tr, SH), _row_spec(tr, SH)] + [_full_spec(s) for s in par_shapes],
                  out_shape=[jax.ShapeDtypeStruct((T, SH), F32)] * 2 + [jax.ShapeDtypeStruct(s, F32) for s in par_shapes],
                  compiler_params=_cp(("arbitrary",)))(p, prev, *params, *cts)


def _shift_combine(d_direct, d_prev_up, tr=256):
    T, W = d_direct.shape
    tr = _tile(T, tr)

    def body(a_ref, b_ref, o_ref):
        o_ref[...] = (a_ref[...] + b_ref[...]).astype(BF16)

    return _pcall(body, name="shift_combine", grid=(T // tr,),
                  in_specs=[_row_spec(tr, W)] * 2, out_specs=_row_spec(tr, W),
                  out_shape=jax.ShapeDtypeStruct((T, W), BF16), compiler_params=_cp(("arbitrary",)))(d_direct, d_prev_up)


def _mix_fn(y, r, kmod, v, g_rwkv, yfox, g_fox, ymem, g_mq, lnw, lnb, rk, E, ET):
    inv = 1.0 / HEAD
    mean = _seg(y, E) * inv
    yc = y - _seg(mean, ET)
    var = _seg(yc * yc, E) * inv
    yn = yc * _seg(lax.rsqrt(var + GN_EPS), ET) * lnw + lnb
    bonus = _seg(_seg(r * kmod * rk, E), ET) * v
    o1 = (yn + bonus) * _silu(g_rwkv)
    return jnp.concatenate([o1, yfox * _silu(g_fox), ymem * _silu(g_mq)], axis=1)


def _mix_specs(tr, C, MW):
    return [_row_spec(tr, C)] * 7 + [_row_spec(tr, MW)] * 2


def _mix_fwd(acts, params, C, MW, tr=128):
    T = acts[0].shape[0]
    D = 2 * C + MW
    tr = _tile(T, tr)

    def body(y_, r_, k_, v_, g1, yf, g2, ym, g3, lnw, lnb, rk, E, ET, o_ref):
        o_ref[...] = _mix_fn(y_[...], r_[...], k_[...], v_[...], g1[...], yf[...], g2[...], ym[...], g3[...],
                             lnw[...], lnb[...], rk[...], E[...], ET[...]).astype(BF16)

    return _pcall(body, name="mix_fwd", grid=(T // tr,),
                  in_specs=_mix_specs(tr, C, MW) + [_full_spec(a.shape) for a in params],
                  out_specs=_row_spec(tr, D), out_shape=jax.ShapeDtypeStruct((T, D), BF16),
                  compiler_params=_cp(("arbitrary",)))(*acts, *params)


def _mix_bwd(acts, params, dycat, C, MW, tr=128):
    T = acts[0].shape[0]
    D = 2 * C + MW
    tr = _tile(T, tr)

    def body(y_, r_, k_, v_, g1, yf, g2, ym, g3, lnw, lnb, rk, E, ET, dy_ref, *outs):
        fe = lambda *a: _mix_fn(*a, E[...], ET[...])
        _, vjp = jax.vjp(fe, y_[...], r_[...], k_[...], v_[...], g1[...], yf[...], g2[...], ym[...], g3[...],
                         lnw[...], lnb[...], rk[...])
        grads = vjp(dy_ref[...])
        for o, gval in zip(outs[:9], grads[:9]):
            o[...] = gval

        @pl.when(pl.program_id(0) == 0)
        def _():
            for o in outs[9:]:
                o[...] = jnp.zeros_like(o)

        for o, gval in zip(outs[9:], grads[9:]):
            o[...] += gval

    widths = [C, C, C, C, C, C, C, MW, MW]
    return _pcall(body, name="mix_bwd", grid=(T // tr,),
                  in_specs=_mix_specs(tr, C, MW) + [_full_spec(a.shape) for a in params] + [_row_spec(tr, D)],
                  out_specs=[_row_spec(tr, w) for w in widths] + [_full_spec((1, C))] * 3,
                  out_shape=[jax.ShapeDtypeStruct((T, w), F32) for w in widths] + [jax.ShapeDtypeStruct((1, C), F32)] * 3,
                  compiler_params=_cp(("arbitrary",)))(*acts, *params, dycat)


def _post(yo, x, tgt, g_post, tr=128):
    T, D = x.shape
    tr = _tile(T, tr)

    def body(yo_ref, x_ref, t_ref, g_ref, dyo_ref, dout_ref, loss_ref, dg_ref):
        n, vjp = jax.vjp(_rms, yo_ref[...], g_ref[...])
        diff = (x_ref[...] + n) - t_ref[...]
        part = 0.5 * jnp.sum(jnp.mean(diff * diff, axis=-1, keepdims=True), axis=0, keepdims=True)
        d_out = diff * (1.0 / D)
        dyo, dg = vjp(d_out)
        dyo_ref[...] = dyo.astype(BF16)
        dout_ref[...] = d_out

        @pl.when(pl.program_id(0) == 0)
        def _():
            loss_ref[...] = jnp.zeros_like(loss_ref)
            dg_ref[...] = jnp.zeros_like(dg_ref)

        loss_ref[...] += jnp.broadcast_to(part, loss_ref.shape)
        dg_ref[...] += dg

    return _pcall(body, name="post_loss", grid=(T // tr,),
                  in_specs=[_row_spec(tr, D)] * 3 + [_full_spec((1, D))],
                  out_specs=[_row_spec(tr, D), _row_spec(tr, D), _full_spec((1, LANE)), _full_spec((1, D))],
                  out_shape=[jax.ShapeDtypeStruct((T, D), BF16), jax.ShapeDtypeStruct((T, D), F32),
                             jax.ShapeDtypeStruct((1, LANE), F32), jax.ShapeDtypeStruct((1, D), F32)],
                  compiler_params=_cp(("arbitrary",)))(yo, x, tgt, g_post)


def _memattn_fn(MW, q, mkv):
    hd = MW // MEM_HEADS
    scale = hd ** -0.5
    outs = []
    for h in range(MEM_HEADS):
        qh = q[:, h * hd:(h + 1) * hd]
        kh = mkv[:, h * hd:(h + 1) * hd]
        vh = mkv[:, MW + h * hd:MW + (h + 1) * hd]
        s = _mm(qh, kh, False, True, "bf") * scale
        e = jnp.exp(s - lax.stop_gradient(jnp.max(s, axis=-1, keepdims=True)))
        pr = e / jnp.sum(e, axis=-1, keepdims=True)
        outs.append(_mm(pr, vh, False, False, "bf"))
    return jnp.concatenate(outs, axis=1)


def _memattn_fwd(p, mkv, MW, tr=256):
    T = p.shape[0]
    tr = _tile(T, tr)

    def body(q_ref, kv_ref, o_ref):
        o_ref[...] = _memattn_fn(MW, q_ref[...], kv_ref[...])

    return _pcall(body, name="memattn_fwd", grid=(T // tr,),
                  in_specs=[_row_spec(tr, MW), _full_spec(mkv.shape)], out_specs=_row_spec(tr, MW),
                  out_shape=jax.ShapeDtypeStruct((T, MW), F32), compiler_params=_cp(("arbitrary",)))(p, mkv)


def _memattn_bwd(p, mkv, do, MW, tr=256):
    T = p.shape[0]
    tr = _tile(T, tr)

    def body(q_ref, kv_ref, do_ref, dq_ref, dkv_ref):
        _, vjp = jax.vjp(functools.partial(_memattn_fn, MW), q_ref[...], kv_ref[...])
        dq, dkv = vjp(do_ref[...])
        dq_ref[...] = dq

        @pl.when(pl.program_id(0) == 0)
        def _():
            dkv_ref[...] = jnp.zeros_like(dkv_ref)

        dkv_ref[...] += dkv

    return _pcall(body, name="memattn_bwd", grid=(T // tr,),
                  in_specs=[_row_spec(tr, MW), _full_spec(mkv.shape), _row_spec(tr, MW)],
                  out_specs=[_row_spec(tr, MW), _full_spec(mkv.shape)],
                  out_shape=[jax.ShapeDtypeStruct((T, MW), F32), jax.ShapeDtypeStruct(mkv.shape, F32)],
                  compiler_params=_cp(("arbitrary",)))(p, mkv, do)


def _fox_cum_fwd(p, b_f_pad, off, blk=512):
    T = p.shape[0]
    blk = _tile(T, blk)

    def body(f_ref, b_ref, cum_ref, cumt_ref, carry):
        @pl.when(pl.program_id(0) == 0)
        def _():
            carry[...] = jnp.zeros_like(carry)

        z = f_ref[...] + b_ref[...]
        logf = -_softplus(-z)
        row = lax.broadcasted_iota(jnp.int32, (blk, blk), 0)
        col = lax.broadcasted_iota(jnp.int32, (blk, blk), 1)
        tri = (col <= row).astype(F32)
        c = _dg(tri, logf, False, False, "hi") + carry[...]
        cum_ref[...] = c
        cumt_ref[...] = c.T
        carry[...] += jnp.sum(logf, axis=0, keepdims=True)

    return _pcall(body, name="fox_cum_fwd", grid=(T // blk,),
                  in_specs=[_row_spec(blk, LANE, off // LANE), _full_spec((1, LANE))],
                  out_specs=[_row_spec(blk, LANE), pl.BlockSpec((LANE, blk), lambda i: (0, i))],
                  out_shape=[jax.ShapeDtypeStruct((T, LANE), F32), jax.ShapeDtypeStruct((LANE, T), F32)],
                  scratch_shapes=[pltpu.VMEM((1, LANE), F32)], compiler_params=_cp(("arbitrary",)))(p, b_f_pad)


def _fox_cum_bwd(p, b_f_pad, dcum, off, blk=512):
    T = p.shape[0]
    blk = _tile(T, blk)
    nb = T // blk

    def body(f_ref, b_ref, dc_ref, df_ref, db_ref, carry):
        @pl.when(pl.program_id(0) == 0)
        def _():
            carry[...] = jnp.zeros_like(carry)
            db_ref[...] = jnp.zeros_like(db_ref)

        row = lax.broadcasted_iota(jnp.int32, (blk, blk), 0)
        col = lax.broadcasted_iota(jnp.int32, (blk, blk), 1)
        tri = (col >= row).astype(F32)
        dlogf = _dg(tri, dc_ref[...], False, False, "hi") + carry[...]
        carry[...] += jnp.sum(dc_ref[...], axis=0, keepdims=True)
        z = f_ref[...] + b_ref[...]
        dz = dlogf * (1.0 - _sigmoid(z))
        df_ref[...] = dz
        db_ref[...] += jnp.sum(dz, axis=0, keepdims=True)

    rev = lambda i: (nb - 1 - i, 0)
    return _pcall(body, name="fox_cum_bwd", grid=(nb,),
                  in_specs=[pl.BlockSpec((blk, LANE), lambda i: (nb - 1 - i, off // LANE)), _full_spec((1, LANE)),
                            pl.BlockSpec((blk, LANE), rev)],
                  out_specs=[pl.BlockSpec((blk, LANE), rev), _full_spec((1, LANE))],
                  out_shape=[jax.ShapeDtypeStruct((T, LANE), F32), jax.ShapeDtypeStruct((1, LANE), F32)],
                  scratch_shapes=[pltpu.VMEM((1, LANE), F32)], compiler_params=_cp(("arbitrary",)))(p, b_f_pad, dcum)


FOX_SCALE = HEAD ** -0.5
FOX_TQ, FOX_TK = 512, 512


def _fox_scores(q, k, ck, q0=None, k0=None):
    s = _dg(q, k, False, True, "bf") - ck
    if q0 is None:
        return s
    qpos = q0 + lax.broadcasted_iota(jnp.int32, s.shape, 0)
    kpos = k0 + lax.broadcasted_iota(jnp.int32, s.shape, 1)
    return jnp.where(kpos <= qpos, s, NEG)


def _fox_c0(ck_ref, hh, pos):
    return ck_ref[0, hh:hh + 1, pl.ds(pl.multiple_of(pos, LANE), LANE)][:, 0:1]


def _fox_tiles(T):
    assert T % LANE == 0
    tq, tk = _tile(T, FOX_TQ), _tile(T, FOX_TK)
    shift = (tk // tq).bit_length() - 1
    assert tk == tq << shift
    return tq, tk, shift


def _fox_fwd(p, ck, C, offs):
    T = p.shape[0]
    tq, tk, shift = _fox_tiles(T)
    npair = C // LANE
    cb = lambda name: offs[name] // LANE

    def body(q_ref, k_ref, v_ref, ck_ref, o_ref, lse_ref):
        i = pl.program_id(1)
        nfull = i >> shift
        lse_ref[...] = jnp.zeros_like(lse_ref)
        heads = [slice(hh * HEAD, (hh + 1) * HEAD) for hh in range(2)]
        qs = [(q_ref[:, sl] * FOX_SCALE).astype(BF16) for sl in heads]
        c0s = [_fox_c0(ck_ref, hh, i * tq) for hh in range(2)]

        def step(j, carry, masked):
            off = pl.multiple_of(j * tk, tk)
            out = []
            for hh, sl in enumerate(heads):
                m, l, acc = carry[hh]
                k = k_ref[pl.ds(off, tk), sl].astype(BF16)
                v = v_ref[pl.ds(off, tk), sl].astype(BF16)
                ckv = ck_ref[0, hh:hh + 1, pl.ds(off, tk)] - c0s[hh]
                s = _fox_scores(qs[hh], k, ckv, i * tq, off) if masked else _fox_scores(qs[hh], k, ckv)
                m_new = jnp.maximum(m, jnp.max(s, axis=-1, keepdims=True))
                pr = jnp.exp(s - m_new)
                al = jnp.exp(m - m_new)
                l = al * l + jnp.sum(pr, axis=-1, keepdims=True)
                acc = al * acc + _dg(pr, v, False, False, "bf")
                out.append((m_new, l, acc))
            return tuple(out)

        one = (jnp.full((tq, 1), NEG, F32), jnp.zeros((tq, 1), F32), jnp.zeros((tq, HEAD), F32))
        carry = lax.fori_loop(0, nfull, lambda j, c: step(j, c, False), (one, one))
        for hh, (m, l, acc) in enumerate(step(nfull, carry, True)):
            o_ref[:, heads[hh]] = acc / l
            lse_ref[0, :, hh:hh + 1] = m + jnp.log(l)

    return _pcall(body, name="fox_fwd", grid=(npair, T // tq),
                  in_specs=[pl.BlockSpec((tq, LANE), lambda h, i: (i, cb("fq") + h)),
                            pl.BlockSpec((T, LANE), lambda h, i: (0, cb("fk") + h)),
                            pl.BlockSpec((T, LANE), lambda h, i: (0, cb("fv") + h)),
                            pl.BlockSpec((1, 8, T), lambda h, i: (h, 0, 0))],
                  out_specs=[pl.BlockSpec((tq, LANE), lambda h, i: (i, h)),
                             pl.BlockSpec((1, tq, 8), lambda h, i: (h, i, 0))],
                  out_shape=[jax.ShapeDtypeStruct((T, C), F32), jax.ShapeDtypeStruct((npair, T, 8), F32)],
                  compiler_params=_cp(("arbitrary", "arbitrary")))(p, p, p, ck)


def _fox_delta(p, ck, do, lse, C, offs):
    T = p.shape[0]
    tq, tk, shift = _fox_tiles(T)
    npair = C // LANE
    cb = lambda name: offs[name] // LANE

    def body(q_ref, k_ref, v_ref, ck_ref, do_ref, lse_ref, d_ref):
        i = pl.program_id(1)
        nfull = i >> shift
        d_ref[...] = jnp.zeros_like(d_ref)
        heads = [slice(hh * HEAD, (hh + 1) * HEAD) for hh in range(2)]
        qs = [(q_ref[:, sl] * FOX_SCALE).astype(BF16) for sl in heads]
        dos = [do_ref[:, sl].astype(BF16) for sl in heads]
        lses = [lse_ref[0, :, hh:hh + 1] for hh in range(2)]
        c0s = [_fox_c0(ck_ref, hh, i * tq) for hh in range(2)]

        def step(j, accs, masked):
            off = pl.multiple_of(j * tk, tk)
            out = []
            for hh, sl in enumerate(heads):
                k = k_ref[pl.ds(off, tk), sl].astype(BF16)
                v = v_ref[pl.ds(off, tk), sl].astype(BF16)
                ckv = ck_ref[0, hh:hh + 1, pl.ds(off, tk)] - c0s[hh]
                s = _fox_scores(qs[hh], k, ckv, i * tq, off) if masked else _fox_scores(qs[hh], k, ckv)
                pr = jnp.exp(s - lses[hh])
                dp = _dg(dos[hh], v, False, True, "bf")
                out.append(accs[hh] + jnp.sum(pr * dp, axis=-1, keepdims=True))
            return tuple(out)

        z = jnp.zeros((tq, 1), F32)
        accs = lax.fori_loop(0, nfull, lambda j, c: step(j, c, False), (z, z))
        for hh, acc in enumerate(step(nfull, accs, True)):
            d_ref[0, :, hh:hh + 1] = acc

    return _pcall(body, name="fox_delta", grid=(npair, T // tq),
                  in_specs=[pl.BlockSpec((tq, LANE), lambda h, i: (i, cb("fq") + h)),
                            pl.BlockSpec((T, LANE), lambda h, i: (0, cb("fk") + h)),
                            pl.BlockSpec((T, LANE), lambda h, i: (0, cb("fv") + h)),
                            pl.BlockSpec((1, 8, T), lambda h, i: (h, 0, 0)),
                            pl.BlockSpec((tq, LANE), lambda h, i: (i, h)),
                            pl.BlockSpec((1, tq, 8), lambda h, i: (h, i, 0))],
                  out_specs=pl.BlockSpec((1, tq, 8), lambda h, i: (h, i, 0)),
                  out_shape=jax.ShapeDtypeStruct((npair, T, 8), F32),
                  compiler_params=_cp(("arbitrary", "arbitrary")))(p, p, p, ck, do, lse)


def _fox_bwd(p, ck, delta, do, lse, C, offs):
    T = p.shape[0]
    tq, tk, shift = _fox_tiles(T)
    ratio = tk // tq
    nq = T // tq
    npair = C // LANE
    cb = lambda name: offs[name] // LANE

    def body(q_ref, k_ref, v_ref, ck_ref, ckall_ref, dl_ref, do_ref, lse_ref, dq_ref, dk_ref, dv_ref, dck_ref):
        j = pl.program_id(1)

        @pl.when(j == 0)
        def _():
            dq_ref[...] = jnp.zeros_like(dq_ref)

        dck_ref[...] = jnp.zeros_like(dck_ref)
        heads = [slice(hh * HEAD, (hh + 1) * HEAD) for hh in range(2)]
        ks = [k_ref[:, sl].astype(BF16) for sl in heads]
        vs = [v_ref[:, sl].astype(BF16) for sl in heads]
        cks = [ck_ref[0, hh:hh + 1, :] for hh in range(2)]

        def step(i, carry, masked):
            off = pl.multiple_of(i * tq, tq)
            out = []
            for hh, sl in enumerate(heads):
                dk, dv, dck = carry[hh]
                q = (q_ref[pl.ds(off, tq), sl] * FOX_SCALE).astype(BF16)
                dov = do_ref[pl.ds(off, tq), sl]
                lsev = lse_ref[0, pl.ds(off, tq), hh:hh + 1]
                ckv = cks[hh] - _fox_c0(ckall_ref, hh, off)
                s = _fox_scores(q, ks[hh], ckv, off, j * tk) if masked else _fox_scores(q, ks[hh], ckv)
                pr = jnp.exp(s - lsev)
                dv = dv + _dg(pr, dov, True, False, "bf")
                dp = _dg(dov, vs[hh], False, True, "bf")
                ds = pr * (dp - dl_ref[0, pl.ds(off, tq), hh:hh + 1])
                dk = dk + _dg(ds, q, True, False, "bf")
                dq_ref[pl.ds(off, tq), sl] += _dg(ds, ks[hh], False, False, "bf") * FOX_SCALE
                out.append((dk, dv, dck - jnp.sum(ds, axis=0, keepdims=True)))
            return tuple(out)

        z = jnp.zeros((tk, HEAD), F32)
        carry = ((z, z, jnp.zeros((1, tk), F32)),) * 2
        for r in range(ratio):
            carry = step(j * ratio + r, carry, True)
        carry = lax.fori_loop((j + 1) * ratio, nq, lambda i, c: step(i, c, False), carry)
        for hh, (dk, dv, dck) in enumerate(carry):
            dk_ref[:, heads[hh]] = dk
            dv_ref[:, heads[hh]] = dv
            dck_ref[0, hh:hh + 1, :] = dck

    full = lambda h, j: (0, h)
    return _pcall(body, name="fox_bwd", grid=(npair, T // tk),
                  in_specs=[pl.BlockSpec((T, LANE), lambda h, j: (0, cb("fq") + h)),
                            pl.BlockSpec((tk, LANE), lambda h, j: (j, cb("fk") + h)),
                            pl.BlockSpec((tk, LANE), lambda h, j: (j, cb("fv") + h)),
                            pl.BlockSpec((1, 8, tk), lambda h, j: (h, 0, j)),
                            pl.BlockSpec((1, 8, T), lambda h, j: (h, 0, 0)),
                            pl.BlockSpec((1, T, 8), lambda h, j: (h, 0, 0)), pl.BlockSpec((T, LANE), full),
                            pl.BlockSpec((1, T, 8), lambda h, j: (h, 0, 0))],
                  out_specs=[pl.BlockSpec((T, LANE), full),
                             pl.BlockSpec((tk, LANE), lambda h, j: (j, h)),
                             pl.BlockSpec((tk, LANE), lambda h, j: (j, h)),
                             pl.BlockSpec((1, 8, tk), lambda h, j: (h, 0, j))],
                  out_shape=[jax.ShapeDtypeStruct((T, C), F32)] * 3 + [jax.ShapeDtypeStruct((npair, 8, T), F32)],
                  compiler_params=_cp(("arbitrary", "arbitrary")))(p, p, p, ck, ck, delta, do, lse)


def _chunk_fn(S0, r, lw, k, v, a, b):
    nh, n = r.shape[0], r.shape[1]
    row = lax.broadcasted_iota(jnp.int32, (nh, n, n), 1)
    col = lax.broadcasted_iota(jnp.int32, (nh, n, n), 2)
    incl, strict = col <= row, col < row
    mm = lambda x, y, ta=False, tb=False: _mm(x, y, ta, tb, "x3")
    g = _mm(incl.astype(F32), lw, False, False, "hi")
    einv = jnp.exp(-g)
    rt, at, bt, kt = r * jnp.exp(g), a * jnp.exp(g - lw), b * einv, k * einv
    a_ab = jnp.where(strict, mm(at, bt, tb=True), 0.0)
    a_ak = jnp.where(strict, mm(at, kt, tb=True), 0.0)
    r_b = jnp.where(incl, mm(rt, bt, tb=True), 0.0)
    r_k = jnp.where(incl, mm(rt, kt, tb=True), 0.0)
    u = mm(at, S0, tb=True) + mm(a_ak, v)
    pw = a_ab
    for it in range(6):
        u = u + mm(pw, u)
        if it < 5:
            pw = mm(pw, pw)
    y = mm(rt, S0, tb=True) + mm(r_b, u) + mm(r_k, v)
    g_end = jnp.sum(lw, axis=1, keepdims=True)
    s_end = (S0 + mm(u, bt, ta=True) + mm(v, kt, ta=True)) * jnp.exp(g_end)
    return y, s_end


SCAN_HEADS = 12
SCAN_ROWS = 64


def _scan_group(C):
    nh = SCAN_HEADS
    while C % (nh * HEAD):
        nh -= 2
    return nh, nh * HEAD


def _scan_fwd(r, lw, k, v, a, b):
    T, C = r.shape
    tc = _tile(T, SCAN_ROWS)
    ncs = tc // CHUNK
    nh, GW = _scan_group(C)
    ngroup = C // GW

    def body(r_ref, lw_ref, k_ref, v_ref, a_ref, b_ref, y_ref, ck_ref, state):
        @pl.when(pl.program_id(1) == 0)
        def _():
            state[...] = jnp.zeros_like(state)

        heads = [slice(hh * HEAD, (hh + 1) * HEAD) for hh in range(nh)]
        split = lambda ref, rows: jnp.stack([ref[rows, sl] for sl in heads])
        st = split(state, slice(None))
        for c in range(ncs):
            rows = slice(c * CHUNK, (c + 1) * CHUNK)
            for hh, sl in enumerate(heads):
                ck_ref[0, c, :, sl] = st[hh]
            y, st = _chunk_fn(st, *[split(ref, rows) for ref in (r_ref, lw_ref, k_ref, v_ref, a_ref, b_ref)])
            for hh, sl in enumerate(heads):
                y_ref[rows, sl] = y[hh]
        for hh, sl in enumerate(heads):
            state[:, sl] = st[hh]

    spec = pl.BlockSpec((tc, GW), lambda h, t: (t, h))
    return _pcall(body, name="rwkv_scan_fwd", grid=(ngroup, T // tc),
                  in_specs=[spec] * 6,
                  out_specs=[spec, pl.BlockSpec((1, ncs, HEAD, GW), lambda h, t: (h, t, 0, 0))],
                  out_shape=[jax.ShapeDtypeStruct((T, C), F32),
                             jax.ShapeDtypeStruct((ngroup, T // CHUNK, HEAD, GW), F32)],
                  scratch_shapes=[pltpu.VMEM((HEAD, GW), F32)],
                  compiler_params=_cp(("arbitrary", "arbitrary")))(r, lw, k, v, a, b)


def _scan_bwd(r, lw, k, v, a, b, ckpt, dy):
    T, C = r.shape
    tc = _tile(T, SCAN_ROWS)
    ncs = tc // CHUNK
    nh, GW = _scan_group(C)
    ngroup = C // GW
    nt = T // tc

    def body(r_ref, lw_ref, k_ref, v_ref, a_ref, b_ref, ck_ref, dy_ref, dr, dlw, dk, dv, da, db, dstate):
        @pl.when(pl.program_id(1) == 0)
        def _():
            dstate[...] = jnp.zeros_like(dstate)

        outs = (dr, dlw, dk, dv, da, db)
        heads = [slice(hh * HEAD, (hh + 1) * HEAD) for hh in range(nh)]
        split = lambda ref, rows: jnp.stack([ref[rows, sl] for sl in heads])
        dst = split(dstate, slice(None))
        for c in reversed(range(ncs)):
            rows = slice(c * CHUNK, (c + 1) * CHUNK)
            s0 = jnp.stack([ck_ref[0, c, :, sl] for sl in heads])
            _, vjp = jax.vjp(_chunk_fn, s0, *[split(ref, rows) for ref in (r_ref, lw_ref, k_ref, v_ref, a_ref, b_ref)])
            grads = vjp((split(dy_ref, rows), dst))
            dst = grads[0]
            for o, gval in zip(outs, grads[1:]):
                for hh, sl in enumerate(heads):
                    o[rows, sl] = gval[hh]
        for hh, sl in enumerate(heads):
            dstate[:, sl] = dst[hh]

    spec = pl.BlockSpec((tc, GW), lambda h, t: (nt - 1 - t, h))
    return _pcall(body, name="rwkv_scan_bwd", grid=(ngroup, nt),
                  in_specs=[spec] * 6 + [pl.BlockSpec((1, ncs, HEAD, GW), lambda h, t: (h, nt - 1 - t, 0, 0)), spec],
                  out_specs=[spec] * 6, out_shape=[jax.ShapeDtypeStruct((T, C), F32)] * 6,
                  scratch_shapes=[pltpu.VMEM((HEAD, GW), F32)],
                  compiler_params=_cp(("arbitrary", "arbitrary")))(r, lw, k, v, a, b, ckpt, dy)


def _adam(w, g, m, v):
    m = ADAM_B1 * m + (1.0 - ADAM_B1) * g
    v = ADAM_B2 * v + (1.0 - ADAM_B2) * (g * g)
    m_hat = m / (1.0 - ADAM_B1 ** ADAM_STEP)
    v_hat = v / (1.0 - ADAM_B2 ** ADAM_STEP)
    return -ADAM_LR * (m_hat / (jnp.sqrt(v_hat) + ADAM_EPS) + ADAM_WD * w), m, v


def _sum_adam(parts, w, m, v, name):
    n, R, W = parts.shape
    tr = _tile(R, max(8, min(256, (1 << 20) // (n * W))))

    def body(p_ref, w_ref, m_ref, v_ref, g_ref, d_ref, nm_ref, nv_ref):
        g = p_ref[0].astype(F32)
        for s in range(1, n):
            g = g + p_ref[s].astype(F32)
        d, nm, nv = _adam(w_ref[...], g, m_ref[...], v_ref[...])
        g_ref[...] = g
        d_ref[...] = d
        nm_ref[...] = nm
        nv_ref[...] = nv

    return _pcall(body, name=name, grid=(R // tr,),
                  in_specs=[pl.BlockSpec((n, tr, W), lambda i: (0, i, 0))] + [_row_spec(tr, W)] * 3,
                  out_specs=[_row_spec(tr, W)] * 4, out_shape=[jax.ShapeDtypeStruct((R, W), F32)] * 4,
                  compiler_params=_cp(("arbitrary",)))(parts, w, m, v)


def _pad_lanes(vec, width):
    return jnp.pad(vec, ((0, 0), (0, width - vec.shape[1])))


def _logical_cols(blocks, lo, hi):
    B, out = blocks.shape[2], []
    while lo < hi:
        j, o = divmod(lo, B)
        n = min(hi - lo, B - o)
        out.append(blocks[j, :, o:o + n])
        lo += n
    return out


def kernel(x, mem, g_pre, w_in, mu_rwkv, w0, w_decay_up, a0, w_iclr_up, k_k, k_a, r_k, ln_x_w, ln_x_b, b_f, g_mem, w_mem_kv, w_out, g_post, loss_target, m_g_pre, m_w_in, m_mu_rwkv, m_w0, m_w_decay_up, m_a0, m_w_iclr_up, m_k_k, m_k_a, m_r_k, m_ln_x_w, m_ln_x_b, m_b_f, m_g_mem, m_w_mem_kv, m_w_out, m_g_post, v_g_pre, v_w_in, v_mu_rwkv, v_w0, v_w_decay_up, v_a0, v_w_iclr_up, v_k_k, v_k_a, v_r_k, v_ln_x_w, v_ln_x_b, v_b_f, v_g_mem, v_w_mem_kv, v_w_out, v_g_post):
    T, D = x.shape[1], x.shape[2]
    C = w0.shape[1]
    L = w_decay_up.shape[1]
    H = C // HEAD
    MW = w_mem_kv.shape[2] // 2
    SH = 3 * C + 2 * L
    IN = NDEV * w_in.shape[2]
    assert IN == SH + 5 * C + H + 2 * MW and D == 2 * C + MW and H % 2 == 0 and H <= LANE
    assert C % LANE == 0 and L % LANE == 0 and MW % (MEM_HEADS * HEAD) == 0 and T % CHUNK == 0
    offs = dict(grw=SH, fq=SH + C, fk=SH + 2 * C, fv=SH + 3 * C, gfx=SH + 4 * C, mq=SH + 5 * C, gmq=SH + 5 * C + MW,
                fl=SH + 5 * C + 2 * MW)
    NI = -(-(offs["fl"] + LANE) // 1024) * 1024
    l_fl = SH + 4 * C

    x2, mem2, tgt2 = x[0], mem[0], loss_target[0]

    wg = _exchange(w_in[0].astype(BF16), "gather_w_in", True)
    w_perm = jnp.concatenate(_logical_cols(wg, 0, l_fl) + _logical_cols(wg, l_fl + H, IN)
                             + _logical_cols(wg, l_fl, l_fl + H) + [jnp.zeros((D, NI - IN), BF16)], axis=1)

    h = _rmsnorm_fwd(x2, g_pre, "rmsnorm_pre")
    p, w_out_f, w_kv_f, lora = _matmul(
        h, w_perm, False, False, "in_proj",
        riders=[(w_out[0].astype(BF16), True), (w_mem_kv[0].astype(BF16), True),
                (jnp.concatenate([w_decay_up[0], w_iclr_up[0]], axis=0), True)])
    w_out_f, w_kv_f = w_out_f.reshape(D, D), w_kv_f.reshape(D, 2 * MW)
    lora = jnp.transpose(lora, (1, 0, 2)).reshape(2 * L, C)
    wdu_f, wiu_f = lora[:L], lora[L:]

    E, ET = _head_indicator(C, LANE)
    prep_params = [mu_rwkv, w0, a0, k_k, k_a, wdu_f, wiu_f, E, ET]
    mix_params = [ln_x_w, ln_x_b, r_k.reshape(1, C), E, ET]
    b_f_pad = _pad_lanes(b_f, LANE)
    ps = p[:, :SH]
    col = lambda name, w: p[:, offs[name]:offs[name] + w]
    g_rwkv, g_fox, mq, g_mq = col("grw", C), col("gfx", C), col("mq", MW), col("gmq", MW)
    prev = jnp.concatenate([jnp.zeros((1, SH), F32), ps[:-1]], axis=0)
    r, lw, kmod, v, a, b = _prep_fwd(p, prev, prep_params, C, L)
    y_scan, ckpt = _scan_fwd(r, lw, kmod, v, a, b)

    cum, cum_t = _fox_cum_fwd(p, b_f_pad, offs["fl"])
    ck = jnp.pad(cum_t[:H].reshape(H // 2, 2, T), ((0, 0), (0, 6), (0, 0)))
    y_fox, lse = _fox_fwd(p, ck, C, offs)

    memn = _rmsnorm_fwd(mem2, g_mem, "rmsnorm_mem")
    mkv = _matmul(memn, w_kv_f, False, False, "mem_kv_proj")
    y_mem = _memattn_fwd(mq, mkv, MW)

    acts = [y_scan, r, kmod, v, g_rwkv, y_fox, g_fox, y_mem, g_mq]
    ycat = _mix_fwd(acts, mix_params, C, MW)
    yo = _matmul(ycat, w_out_f, False, False, "out_proj")
    d_yo, d_out, loss_part, dg_post = _post(yo, x2, tgt2, g_post)
    loss = lax.psum(loss_part[0, 0], AXES)

    g_w_out = _matmul(ycat, d_yo, True, False, "grad_w_out", BF16)
    d_ycat = _matmul(d_yo, w_out_f, False, True, "d_ycat")
    (d_y, d_r1, d_k1, d_v1, d_grw, d_yfox, d_gfx, d_ymem, d_gmq, dg_lnw, dg_lnb, dg_rk) = _mix_bwd(
        acts, mix_params, d_ycat, C, MW)

    d_mq, d_mkv = _memattn_bwd(mq, mkv, d_ymem, MW)
    g_w_kv = _matmul(memn, d_mkv, True, False, "grad_w_mem_kv", BF16)
    d_memn = _matmul(d_mkv, w_kv_f, False, True, "d_memn")
    _, dg_mem = _rmsnorm_bwd(mem2, g_mem, d_memn, None, "rmsnorm_mem_bwd")

    delta = _fox_delta(p, ck, d_yfox, lse, C, offs)
    d_fq, d_fk, d_fv, d_ck = _fox_bwd(p, ck, delta, d_yfox, lse, C, offs)
    d_cum = _pad_lanes(d_ck[:, :2, :].reshape(H, T).T, LANE)
    d_fl, dg_bf = _fox_cum_bwd(p, b_f_pad, d_cum, offs["fl"])

    d_r, d_lw, d_k, d_v, d_a, d_b = _scan_bwd(r, lw, kmod, v, a, b, ckpt, d_y)
    cts = [d_r, d_lw, d_k, d_v, d_a, d_b, d_r1, d_k1, d_v1]
    (d_ps, d_prev, dg_mu, dg_w0, dg_a0, dg_kk, dg_ka, dg_wdu, dg_wiu) = _prep_bwd(p, prev, prep_params, cts, C, L)
    d_prev_up = jnp.concatenate([d_prev[1:], jnp.zeros((1, SH), F32)], axis=0)
    d_sh = _shift_combine(d_ps, d_prev_up)

    tobf = lambda z: z.astype(BF16)
    dp = jnp.concatenate([d_sh, tobf(d_grw), tobf(d_fq), tobf(d_fk), tobf(d_fv), tobf(d_gfx), tobf(d_mq), tobf(d_gmq),
                          tobf(d_fl), jnp.zeros((T, NI - offs["fl"] - LANE), BF16)], axis=1)
    g_lora = jnp.concatenate([dg_wdu, dg_wiu], axis=0)
    g_w_perm, parts_out, parts_kv, parts_lora = _matmul(
        h, dp, True, False, "grad_w_in", BF16,
        riders=[(g_w_out.reshape(NDEV, D // NDEV, D), False), (g_w_kv.reshape(NDEV, D // NDEV, 2 * MW), False),
                (jnp.transpose(g_lora.reshape(2 * L, NDEV, C // NDEV), (1, 0, 2)), False)])

    def internal_cols(lo, hi):
        out = []
        for first, last, shift in ((0, l_fl, 0), (l_fl, l_fl + H, offs["fl"] - l_fl), (l_fl + H, IN, -H)):
            s0, s1 = max(lo, first), min(hi, last)
            if s0 < s1:
                out.append(g_w_perm[:, s0 + shift:s1 + shift])
        return out

    blk = IN // NDEV
    g_blocks = jnp.stack([jnp.concatenate(internal_cols(j * blk, (j + 1) * blk), axis=1) for j in range(NDEV)])
    d_h, parts_in = _matmul(dp, w_perm, False, True, "d_h", tk=5120, riders=[(g_blocks, False)])
    grad_x, dg_pre = _rmsnorm_bwd(x2, g_pre, d_h, d_out, "rmsnorm_pre_bwd")

    gw_in, dw_in, nm_w_in, nv_w_in = _sum_adam(parts_in, w_in[0], m_w_in[0], v_w_in[0], "adam_w_in")
    gw_out, dw_out, nm_w_out, nv_w_out = _sum_adam(parts_out, w_out[0], m_w_out[0], v_w_out[0], "adam_w_out")
    gw_kv, dw_kv, nm_w_kv, nv_w_kv = _sum_adam(parts_kv, w_mem_kv[0], m_w_mem_kv[0], v_w_mem_kv[0], "adam_w_mem_kv")
    cat2 = lambda u, w_: jnp.concatenate([u[0], w_[0]], axis=0)
    lora_res = _sum_adam(parts_lora, cat2(w_decay_up, w_iclr_up), cat2(m_w_decay_up, m_w_iclr_up),
                         cat2(v_w_decay_up, v_w_iclr_up), "adam_lora")

    small = [("g_pre", g_pre, m_g_pre, v_g_pre, dg_pre), ("mu_rwkv", mu_rwkv, m_mu_rwkv, v_mu_rwkv, dg_mu),
             ("w0", w0, m_w0, v_w0, dg_w0), ("a0", a0, m_a0, v_a0, dg_a0), ("k_k", k_k, m_k_k, v_k_k, dg_kk),
             ("k_a", k_a, m_k_a, v_k_a, dg_ka), ("r_k", r_k.reshape(1, C), m_r_k.reshape(1, C), v_r_k.reshape(1, C), dg_rk),
             ("ln_x_w", ln_x_w, m_ln_x_w, v_ln_x_w, dg_lnw), ("ln_x_b", ln_x_b, m_ln_x_b, v_ln_x_b, dg_lnb),
             ("b_f", _pad_lanes(b_f, LANE), _pad_lanes(m_b_f, LANE), _pad_lanes(v_b_f, LANE), dg_bf),
             ("g_mem", g_mem, m_g_mem, v_g_mem, dg_mem), ("g_post", g_post, m_g_post, v_g_post, dg_post)]
    widths = [s[1].shape[1] for s in small]
    pack = lambda idx: jnp.concatenate([s[idx] for s in small], axis=1).reshape(-1, LANE)
    parts_small = _exchange(pack(4), "gather_small_grads", True)
    res_small = _sum_adam(parts_small, pack(1), pack(2), pack(3), "adam_small")

    def unpack(flat):
        flat = flat.reshape(1, -1)
        out, o = {}, 0
        for (name, *_), wd in zip(small, widths):
            out[name] = flat[:, o:o + wd]
            o += wd
        out["b_f"] = out["b_f"][:, :H]
        out["r_k"] = out["r_k"].reshape(1, H, HEAD)
        return out

    sg, sd, sm, sv = [unpack(z) for z in res_small]
    big = {"w_in": (gw_in, dw_in, nm_w_in, nv_w_in), "w_out": (gw_out, dw_out, nm_w_out, nv_w_out),
           "w_mem_kv": (gw_kv, dw_kv, nm_w_kv, nv_w_kv),
           "w_decay_up": tuple(z[:L] for z in lora_res), "w_iclr_up": tuple(z[L:] for z in lora_res)}
    order = ["g_pre", "w_in", "mu_rwkv", "w0", "w_decay_up", "a0", "w_iclr_up", "k_k", "k_a", "r_k", "ln_x_w", "ln_x_b",
             "b_f", "g_mem", "w_mem_kv", "w_out", "g_post"]

    def pick(name, idx):
        if name in big:
            return big[name][idx][None]
        return (sg, sd, sm, sv)[idx][name]

    outs = [loss, grad_x[None]]
    for idx in range(4):
        outs += [pick(n, idx) for n in order]
    return tuple(outs)
```

```python
import functools

import jax
import jax.numpy as jnp
from jax import lax
from jax.experimental import pallas as pl
from jax.experimental.pallas import tpu as pltpu

F32, BF16 = jnp.float32, jnp.bfloat16
HI = lax.Precision.HIGHEST
NDEV = 8
AXES = ("x", "y", "c")
HEAD = 64
CHUNK = 64
MEM_HEADS = 4
LANE = 128
RMS_EPS = 1e-6
GN_EPS = 64e-5
NEG = -1e30
ADAM_LR, ADAM_B1, ADAM_B2, ADAM_EPS, ADAM_WD, ADAM_STEP = 0.001, 0.9, 0.999, 1e-08, 0.01, 10
VMEM_LIMIT = 56 * 1024 * 1024


def _pcall(body, **kw):
    return pl.pallas_call(body, **kw)


def _cp(sem=None, vmem=VMEM_LIMIT):
    return pltpu.CompilerParams(dimension_semantics=sem, vmem_limit_bytes=vmem)


def _tile(n, pref):
    for t in (pref, 1024, 512, 256, 128, 64, 32, 16, 8):
        if t <= pref and n % t == 0:
            return t
    return n


def _dg(a, b, ta, tb, mode):
    nb = a.ndim - 2
    ca = nb + (0 if ta else 1)
    cb = nb + (1 if tb else 0)
    dims = (((ca,), (cb,)), (tuple(range(nb)), tuple(range(nb))))
    if mode == "x3":
        a_hi, b_hi = a.astype(BF16), b.astype(BF16)
        a_lo, b_lo = (a - a_hi.astype(F32)).astype(BF16), (b - b_hi.astype(F32)).astype(BF16)
        dot = lambda u, w: lax.dot_general(u, w, dims, preferred_element_type=F32)
        return dot(a_hi, b_hi) + (dot(a_hi, b_lo) + dot(a_lo, b_hi))
    if mode == "x2":
        a_hi, b = a.astype(BF16), b.astype(BF16)
        a_lo = (a - a_hi.astype(F32)).astype(BF16)
        dot = lambda u: lax.dot_general(u, b, dims, preferred_element_type=F32)
        return dot(a_hi) + dot(a_lo)
    if mode == "bf":
        a, b, prec = a.astype(BF16), b.astype(BF16), None
    else:
        prec = HI
    return lax.dot_general(a, b, dims, preferred_element_type=F32, precision=prec)


@functools.partial(jax.custom_vjp, nondiff_argnums=(2, 3, 4))
def _mm(a, b, ta, tb, mode):
    return _dg(a, b, ta, tb, mode)


def _mm_fwd(a, b, ta, tb, mode):
    return _dg(a, b, ta, tb, mode), (a, b)


def _mm_bwd(ta, tb, mode, res, g):
    a, b = res
    da = _dg(g, b, False, not tb, mode) if not ta else _dg(b, g, tb, True, mode)
    db = _dg(a, g, not ta, False, mode) if not tb else _dg(g, a, True, ta, mode)
    return da, db


_mm.defvjp(_mm_fwd, _mm_bwd)


@jax.custom_vjp
def _seg(a, e):
    return _dg(a, e, False, False, "x2")


def _seg_fwd(a, e):
    return _dg(a, e, False, False, "x2"), e


def _seg_bwd(e, g):
    return _dg(g, e, False, True, "x2"), jnp.zeros_like(e)


_seg.defvjp(_seg_fwd, _seg_bwd)


def _sigmoid(z):
    return 1.0 / (1.0 + jnp.exp(-z))


def _softplus(z):
    return jnp.maximum(z, 0.0) + jnp.log(1.0 + jnp.exp(-jnp.abs(z)))


def _silu(z):
    return z * _sigmoid(z)


def _rms(x, g):
    return x * lax.rsqrt(jnp.mean(x * x, axis=-1, keepdims=True) + RMS_EPS) * g


HBM_SPEC = pl.BlockSpec(memory_space=pltpu.HBM)
EXCHANGE_SEMS = [pltpu.SemaphoreType.DMA((NDEV - 1,)), pltpu.SemaphoreType.DMA((NDEV - 1,)), pltpu.SemaphoreType.DMA(())]


def _exchange_copies(gather, x_ref, o_ref, send_sems, recv_sems, local_sem, arrivals):
    ix, iy, ic = lax.axis_index("x"), lax.axis_index("y"), lax.axis_index("c")
    me = 4 * ix + 2 * iy + ic

    def src(dest):
        return x_ref if gather else x_ref.at[dest]

    mine = pltpu.make_async_copy(src(me), o_ref.at[me], local_sem)
    pairs = []
    for k in range(1, NDEV):
        px = 1 - ix if (k >> 2) & 1 else ix
        py = 1 - iy if (k >> 1) & 1 else iy
        pc = 1 - ic if k & 1 else ic
        peer = 4 * px + 2 * py + pc
        send = pltpu.make_async_remote_copy(
            src_ref=src(peer), dst_ref=o_ref.at[me], send_sem=send_sems.at[k - 1], recv_sem=recv_sems.at[k - 1],
            device_id=(px, py, pc), device_id_type=pl.DeviceIdType.MESH)
        arrival = arrivals and pltpu.make_async_remote_copy(
            src_ref=src(peer), dst_ref=o_ref.at[peer], send_sem=send_sems.at[k - 1], recv_sem=recv_sems.at[k - 1],
            device_id=(ix, iy, ic), device_id_type=pl.DeviceIdType.MESH)
        pairs.append((send, arrival))
    return mine, pairs


def _exchange_start(*args):
    mine, pairs = _exchange_copies(*args, arrivals=False)
    mine.start()
    for send, _ in pairs:
        send.start()


def _exchange_wait(*args):
    mine, pairs = _exchange_copies(*args, arrivals=True)
    for send, arrival in pairs:
        send.wait_send()
        arrival.wait_recv()
    mine.wait()


def _exchange_shape(x, gather):
    return jax.ShapeDtypeStruct((NDEV,) + tuple(x.shape if gather else x.shape[1:]), x.dtype)


def _exchange(x, name, gather):
    def body(x_ref, o_ref, *sems):
        _exchange_start(gather, x_ref, o_ref, *sems)
        _exchange_wait(gather, x_ref, o_ref, *sems)

    return _pcall(body, name=name, out_shape=_exchange_shape(x, gather), in_specs=[HBM_SPEC], out_specs=HBM_SPEC,
                  scratch_shapes=list(EXCHANGE_SEMS))(x)


MAX_FULL_K = 4096


def _matmul(a, b, ta, tb, name, out_dtype=F32, tm=1024, tn=1024, tk=3072, riders=()):
    M, K = (a.shape[1], a.shape[0]) if ta else a.shape
    N = b.shape[0] if tb else b.shape[1]
    assert (b.shape[1] if tb else b.shape[0]) == K
    if K <= MAX_FULL_K:
        tk = K
    else:
        tm, tk = min(tm, 512), _tile(K, tk)
    tm, tn = _tile(M, tm), _tile(N, tn)
    grid = (M // tm, N // tn, K // tk)
    nk, nr = grid[2], len(riders)

    def body(*refs):
        a_ref, b_ref, x_refs = refs[0], refs[1], refs[2:2 + nr]
        o_ref, xo_refs, rest = refs[2 + nr], refs[3 + nr:3 + 2 * nr], refs[3 + 2 * nr:]
        sems = rest[1:] if nk > 1 else rest
        ids = [pl.program_id(d) for d in range(3)]
        jobs = [(riders[q][1], x_refs[q], xo_refs[q]) + tuple(sems[3 * q:3 * q + 3]) for q in range(nr)]

        if nr:
            @pl.when((ids[0] == 0) & (ids[1] == 0) & (ids[2] == 0))
            def _():
                for job in jobs:
                    _exchange_start(*job)

        if nk == 1:
            o_ref[...] = _dg(a_ref[...], b_ref[...], ta, tb, "bf").astype(o_ref.dtype)
        else:
            acc = rest[0]

            @pl.when(ids[2] == 0)
            def _():
                acc[...] = jnp.zeros_like(acc)

            acc[...] += _dg(a_ref[...], b_ref[...], ta, tb, "bf")

            @pl.when(ids[2] == nk - 1)
            def _():
                o_ref[...] = acc[...].astype(o_ref.dtype)

        if nr:
            @pl.when((ids[0] == grid[0] - 1) & (ids[1] == grid[1] - 1) & (ids[2] == nk - 1))
            def _():
                for job in jobs:
                    _exchange_wait(*job)

    a_spec = pl.BlockSpec((tk, tm), lambda i, j, k: (k, i)) if ta else pl.BlockSpec((tm, tk), lambda i, j, k: (i, k))
    b_spec = pl.BlockSpec((tn, tk), lambda i, j, k: (j, k)) if tb else pl.BlockSpec((tk, tn), lambda i, j, k: (k, j))
    out = _pcall(
        body, name=name, grid=grid,
        in_specs=[a_spec, b_spec] + [HBM_SPEC] * nr,
        out_specs=[pl.BlockSpec((tm, tn), lambda i, j, k: (i, j))] + [HBM_SPEC] * nr,
        out_shape=[jax.ShapeDtypeStruct((M, N), out_dtype)] + [_exchange_shape(x, g) for x, g in riders],
        scratch_shapes=([pltpu.VMEM((tm, tn), F32)] if nk > 1 else []) + list(EXCHANGE_SEMS) * nr,
        compiler_params=_cp(("arbitrary", "arbitrary", "arbitrary")),
    )(a, b, *[x for x, _ in riders])
    return tuple(out) if nr else out[0]


def _row_spec(tr, width, col_block=0):
    return pl.BlockSpec((tr, width), lambda i: (i, col_block))


def _full_spec(shape):
    nd = len(shape)
    return pl.BlockSpec(tuple(shape), lambda i: (0,) * nd)


def _rmsnorm_fwd(x, g, name, tr=256):
    R, D = x.shape
    tr = _tile(R, tr)

    def body(x_ref, g_ref, o_ref):
        o_ref[...] = _rms(x_ref[...], g_ref[...]).astype(BF16)

    return _pcall(body, name=name, grid=(R // tr,),
                  in_specs=[_row_spec(tr, D), _full_spec((1, D))], out_specs=_row_spec(tr, D),
                  out_shape=jax.ShapeDtypeStruct((R, D), BF16), compiler_params=_cp(("arbitrary",)))(x, g)


def _rmsnorm_bwd(x, g, dy, extra, name, tr=128):
    R, D = x.shape
    tr = _tile(R, tr)
    has_extra = extra is not None

    def body(*refs):
        if has_extra:
            x_ref, g_ref, dy_ref, e_ref, dx_ref, dg_ref = refs
        else:
            x_ref, g_ref, dy_ref, dx_ref, dg_ref = refs
        _, vjp = jax.vjp(_rms, x_ref[...], g_ref[...])
        dx, dg = vjp(dy_ref[...])
        dx_ref[...] = dx + e_ref[...] if has_extra else dx

        @pl.when(pl.program_id(0) == 0)
        def _():
            dg_ref[...] = jnp.zeros_like(dg_ref)

        dg_ref[...] += dg

    ins = [x, g, dy] + ([extra] if has_extra else [])
    specs = [_row_spec(tr, D), _full_spec((1, D)), _row_spec(tr, D)] + ([_row_spec(tr, D)] if has_extra else [])
    return _pcall(body, name=name, grid=(R // tr,), in_specs=specs,
                  out_specs=[_row_spec(tr, D), _full_spec((1, D))],
                  out_shape=[jax.ShapeDtypeStruct((R, D), F32), jax.ShapeDtypeStruct((1, D), F32)],
                  compiler_params=_cp(("arbitrary",)))(*ins)


def _head_indicator(C, hp):
    e = (jnp.arange(C)[:, None] // HEAD == jnp.arange(hp)[None, :]).astype(F32)
    return e, e.T


def _prep_fn(C, L, ps, prev, mu, w0, a0, k_k, k_a, wdu, wiu, E, ET):
    sh = ps + (prev - ps) * mu
    r, k, v = sh[:, :C], sh[:, C:2 * C], sh[:, 2 * C:3 * C]
    wl, al = sh[:, 3 * C:3 * C + L], sh[:, 3 * C + L:3 * C + 2 * L]
    wd = w0 + _mm(jnp.tanh(wl), wdu, False, False, "bf")
    w_pre = -_softplus(-wd) - 0.5
    lw = -jnp.exp(w_pre)
    alpha = _sigmoid(a0 + _mm(al, wiu, False, False, "bf"))
    kk = k * k_k
    ss = _seg(kk * kk, E)
    kk = kk * _seg(lax.rsqrt(jnp.maximum(ss, 1e-24)), ET)
    k_mod = k * (1.0 + (alpha - 1.0) * k_a)
    return r, lw, k_mod, v, -kk, kk * alpha


def _prep_fwd(p, prev, params, C, L, tr=128):
    T = p.shape[0]
    SH = 3 * C + 2 * L
    tr = _tile(T, tr)

    def body(ps_ref, prev_ref, mu, w0, a0, kk_, ka_, wdu, wiu, E, ET, *outs):
        vals = _prep_fn(C, L, ps_ref[...], prev_ref[...], mu[...], w0[...], a0[...], kk_[...], ka_[...],
                        wdu[...], wiu[...], E[...], ET[...])
        for o, v in zip(outs, vals):
            o[...] = v

    pspecs = [_full_spec(a.shape) for a in params]
    return _pcall(body, name="rwkv_prep_fwd", grid=(T // tr,),
                  in_specs=[_row_spec(tr, SH), _row_spec(tr, SH)] + pspecs,
                  out_specs=[_row_spec(tr, C)] * 6,
                  out_shape=[jax.ShapeDtypeStruct((T, C), F32)] * 6,
                  compiler_params=_cp(("arbitrary",)))(p, prev, *params)


def _prep_bwd(p, prev, params, cts, C, L, tr=128):
    T = p.shape[0]
    SH = 3 * C + 2 * L
    tr = _tile(T, tr)
    nparam = 7

    def body(ps_ref, prev_ref, mu, w0, a0, kk_, ka_, wdu, wiu, E, ET, c0, c1, c2, c3, c4, c5, e0, e2, e3,
             dps_ref, dprev_ref, *dpar):
        f = functools.partial(_prep_fn, C, L)
        fe = lambda ps, prev, *par: f(ps, prev, *par, E[...], ET[...])
        _, vjp = jax.vjp(fe, ps_ref[...], prev_ref[...], mu[...], w0[...], a0[...], kk_[...], ka_[...], wdu[...], wiu[...])
        grads = vjp((c0[...] + e0[...], c1[...], c2[...] + e2[...], c3[...] + e3[...], c4[...], c5[...]))
        dps_ref[...] = grads[0]
        dprev_ref[...] = grads[1]

        @pl.when(pl.program_id(0) == 0)
        def _():
            for d in dpar:
                d[...] = jnp.zeros_like(d)

        for d, gval in zip(dpar, grads[2:]):
            d[...] += gval

    pspecs = [_full_spec(a.shape) for a in params]
    par_shapes = [a.shape for a in params[:nparam]]
    return _pcall(body, name="rwkv_prep_bwd", grid=(T // tr,),
                  in_specs=[_row_spec(tr, SH), _row_spec(tr, SH)] + pspecs + [_row_spec(tr, C)] * 9,
                  out_specs=[_row_spec(tr, SH), _row_spec(tr, SH)] + [_full_spec(s) for s in par_shapes],
                  out_shape=[jax.ShapeDtypeStruct((T, SH), F32)] * 2 + [jax.ShapeDtypeStruct(s, F32) for s in par_shapes],
                  compiler_params=_cp(("arbitrary",)))(p, prev, *params, *cts)


def _shift_combine(d_direct, d_prev_up, tr=256):
    T, W = d_direct.shape
    tr = _tile(T, tr)

    def body(a_ref, b_ref, o_ref):
        o_ref[...] = (a_ref[...] + b_ref[...]).astype(BF16)

    return _pcall(body, name="shift_combine", grid=(T // tr,),
                  in_specs=[_row_spec(tr, W)] * 2, out_specs=_row_spec(tr, W),
                  out_shape=jax.ShapeDtypeStruct((T, W), BF16), compiler_params=_cp(("arbitrary",)))(d_direct, d_prev_up)


def _mix_fn(y, r, kmod, v, g_rwkv, yfox, g_fox, ymem, g_mq, lnw, lnb, rk, E, ET):
    inv = 1.0 / HEAD
    mean = _seg(y, E) * inv
    yc = y - _seg(mean, ET)
    var = _seg(yc * yc, E) * inv
    yn = yc * _seg(lax.rsqrt(var + GN_EPS), ET) * lnw + lnb
    bonus = _seg(_seg(r * kmod * rk, E), ET) * v
    o1 = (yn + bonus) * _silu(g_rwkv)
    return jnp.concatenate([o1, yfox * _silu(g_fox), ymem * _silu(g_mq)], axis=1)


def _mix_specs(tr, C, MW):
    return [_row_spec(tr, C)] * 7 + [_row_spec(tr, MW)] * 2


def _mix_fwd(acts, params, C, MW, tr=128):
    T = acts[0].shape[0]
    D = 2 * C + MW
    tr = _tile(T, tr)

    def body(y_, r_, k_, v_, g1, yf, g2, ym, g3, lnw, lnb, rk, E, ET, o_ref):
        o_ref[...] = _mix_fn(y_[...], r_[...], k_[...], v_[...], g1[...], yf[...], g2[...], ym[...], g3[...],
                             lnw[...], lnb[...], rk[...], E[...], ET[...]).astype(BF16)

    return _pcall(body, name="mix_fwd", grid=(T // tr,),
                  in_specs=_mix_specs(tr, C, MW) + [_full_spec(a.shape) for a in params],
                  out_specs=_row_spec(tr, D), out_shape=jax.ShapeDtypeStruct((T, D), BF16),
                  compiler_params=_cp(("arbitrary",)))(*acts, *params)


def _mix_bwd(acts, params, dycat, C, MW, tr=128):
    T = acts[0].shape[0]
    D = 2 * C + MW
    tr = _tile(T, tr)

    def body(y_, r_, k_, v_, g1, yf, g2, ym, g3, lnw, lnb, rk, E, ET, dy_ref, *outs):
        fe = lambda *a: _mix_fn(*a, E[...], ET[...])
        _, vjp = jax.vjp(fe, y_[...], r_[...], k_[...], v_[...], g1[...], yf[...], g2[...], ym[...], g3[...],
                         lnw[...], lnb[...], rk[...])
        grads = vjp(dy_ref[...])
        for o, gval in zip(outs[:9], grads[:9]):
            o[...] = gval

        @pl.when(pl.program_id(0) == 0)
        def _():
            for o in outs[9:]:
                o[...] = jnp.zeros_like(o)

        for o, gval in zip(outs[9:], grads[9:]):
            o[...] += gval

    widths = [C, C, C, C, C, C, C, MW, MW]
    return _pcall(body, name="mix_bwd", grid=(T // tr,),
                  in_specs=_mix_specs(tr, C, MW) + [_full_spec(a.shape) for a in params] + [_row_spec(tr, D)],
                  out_specs=[_row_spec(tr, w) for w in widths] + [_full_spec((1, C))] * 3,
                  out_shape=[jax.ShapeDtypeStruct((T, w), F32) for w in widths] + [jax.ShapeDtypeStruct((1, C), F32)] * 3,
                  compiler_params=_cp(("arbitrary",)))(*acts, *params, dycat)


def _post(yo, x, tgt, g_post, tr=128):
    T, D = x.shape
    tr = _tile(T, tr)

    def body(yo_ref, x_ref, t_ref, g_ref, dyo_ref, dout_ref, loss_ref, dg_ref):
        n, vjp = jax.vjp(_rms, yo_ref[...], g_ref[...])
        diff = (x_ref[...] + n) - t_ref[...]
        part = 0.5 * jnp.sum(jnp.mean(diff * diff, axis=-1, keepdims=True), axis=0, keepdims=True)
        d_out = diff * (1.0 / D)
        dyo, dg = vjp(d_out)
        dyo_ref[...] = dyo.astype(BF16)
        dout_ref[...] = d_out

        @pl.when(pl.program_id(0) == 0)
        def _():
            loss_ref[...] = jnp.zeros_like(loss_ref)
            dg_ref[...] = jnp.zeros_like(dg_ref)

        loss_ref[...] += jnp.broadcast_to(part, loss_ref.shape)
        dg_ref[...] += dg

    return _pcall(body, name="post_loss", grid=(T // tr,),
                  in_specs=[_row_spec(tr, D)] * 3 + [_full_spec((1, D))],
                  out_specs=[_row_spec(tr, D), _row_spec(tr, D), _full_spec((1, LANE)), _full_spec((1, D))],
                  out_shape=[jax.ShapeDtypeStruct((T, D), BF16), jax.ShapeDtypeStruct((T, D), F32),
                             jax.ShapeDtypeStruct((1, LANE), F32), jax.ShapeDtypeStruct((1, D), F32)],
                  compiler_params=_cp(("arbitrary",)))(yo, x, tgt, g_post)


def _memattn_fn(MW, q, mkv):
    hd = MW // MEM_HEADS
    scale = hd ** -0.5
    outs = []
    for h in range(MEM_HEADS):
        qh = q[:, h * hd:(h + 1) * hd]
        kh = mkv[:, h * hd:(h + 1) * hd]
        vh = mkv[:, MW + h * hd:MW + (h + 1) * hd]
        s = _mm(qh, kh, False, True, "bf") * scale
        e = jnp.exp(s - lax.stop_gradient(jnp.max(s, axis=-1, keepdims=True)))
        pr = e / jnp.sum(e, axis=-1, keepdims=True)
        outs.append(_mm(pr, vh, False, False, "bf"))
    return jnp.concatenate(outs, axis=1)


def _memattn_fwd(p, mkv, MW, tr=256):
    T = p.shape[0]
    tr = _tile(T, tr)

    def body(q_ref, kv_ref, o_ref):
        o_ref[...] = _memattn_fn(MW, q_ref[...], kv_ref[...])

    return _pcall(body, name="memattn_fwd", grid=(T // tr,),
                  in_specs=[_row_spec(tr, MW), _full_spec(mkv.shape)], out_specs=_row_spec(tr, MW),
                  out_shape=jax.ShapeDtypeStruct((T, MW), F32), compiler_params=_cp(("arbitrary",)))(p, mkv)


def _memattn_bwd(p, mkv, do, MW, tr=256):
    T = p.shape[0]
    tr = _tile(T, tr)

    def body(q_ref, kv_ref, do_ref, dq_ref, dkv_ref):
        _, vjp = jax.vjp(functools.partial(_memattn_fn, MW), q_ref[...], kv_ref[...])
        dq, dkv = vjp(do_ref[...])
        dq_ref[...] = dq

        @pl.when(pl.program_id(0) == 0)
        def _():
            dkv_ref[...] = jnp.zeros_like(dkv_ref)

        dkv_ref[...] += dkv

    return _pcall(body, name="memattn_bwd", grid=(T // tr,),
                  in_specs=[_row_spec(tr, MW), _full_spec(mkv.shape), _row_spec(tr, MW)],
                  out_specs=[_row_spec(tr, MW), _full_spec(mkv.shape)],
                  out_shape=[jax.ShapeDtypeStruct((T, MW), F32), jax.ShapeDtypeStruct(mkv.shape, F32)],
                  compiler_params=_cp(("arbitrary",)))(p, mkv, do)


def _fox_cum_fwd(p, b_f_pad, off, blk=512):
    T = p.shape[0]
    blk = _tile(T, blk)

    def body(f_ref, b_ref, cum_ref, cumt_ref, carry):
        @pl.when(pl.program_id(0) == 0)
        def _():
            carry[...] = jnp.zeros_like(carry)

        z = f_ref[...] + b_ref[...]
        logf = -_softplus(-z)
        row = lax.broadcasted_iota(jnp.int32, (blk, blk), 0)
        col = lax.broadcasted_iota(jnp.int32, (blk, blk), 1)
        tri = (col <= row).astype(F32)
        c = _dg(tri, logf, False, False, "hi") + carry[...]
        cum_ref[...] = c
        cumt_ref[...] = c.T
        carry[...] += jnp.sum(logf, axis=0, keepdims=True)

    return _pcall(body, name="fox_cum_fwd", grid=(T // blk,),
                  in_specs=[_row_spec(blk, LANE, off // LANE), _full_spec((1, LANE))],
                  out_specs=[_row_spec(blk, LANE), pl.BlockSpec((LANE, blk), lambda i: (0, i))],
                  out_shape=[jax.ShapeDtypeStruct((T, LANE), F32), jax.ShapeDtypeStruct((LANE, T), F32)],
                  scratch_shapes=[pltpu.VMEM((1, LANE), F32)], compiler_params=_cp(("arbitrary",)))(p, b_f_pad)


def _fox_cum_bwd(p, b_f_pad, dcum, off, blk=512):
    T = p.shape[0]
    blk = _tile(T, blk)
    nb = T // blk

    def body(f_ref, b_ref, dc_ref, df_ref, db_ref, carry):
        @pl.when(pl.program_id(0) == 0)
        def _():
            carry[...] = jnp.zeros_like(carry)
            db_ref[...] = jnp.zeros_like(db_ref)

        row = lax.broadcasted_iota(jnp.int32, (blk, blk), 0)
        col = lax.broadcasted_iota(jnp.int32, (blk, blk), 1)
        tri = (col >= row).astype(F32)
        dlogf = _dg(tri, dc_ref[...], False, False, "hi") + carry[...]
        carry[...] += jnp.sum(dc_ref[...], axis=0, keepdims=True)
        z = f_ref[...] + b_ref[...]
        dz = dlogf * (1.0 - _sigmoid(z))
        df_ref[...] = dz
        db_ref[...] += jnp.sum(dz, axis=0, keepdims=True)

    rev = lambda i: (nb - 1 - i, 0)
    return _pcall(body, name="fox_cum_bwd", grid=(nb,),
                  in_specs=[pl.BlockSpec((blk, LANE), lambda i: (nb - 1 - i, off // LANE)), _full_spec((1, LANE)),
                            pl.BlockSpec((blk, LANE), rev)],
                  out_specs=[pl.BlockSpec((blk, LANE), rev), _full_spec((1, LANE))],
                  out_shape=[jax.ShapeDtypeStruct((T, LANE), F32), jax.ShapeDtypeStruct((1, LANE), F32)],
                  scratch_shapes=[pltpu.VMEM((1, LANE), F32)], compiler_params=_cp(("arbitrary",)))(p, b_f_pad, dcum)


FOX_SCALE = HEAD ** -0.5
FOX_TQ, FOX_TK = 512, 512


def _fox_scores(q, k, ck, q0=None, k0=None):
    s = _dg(q, k, False, True, "bf") - ck
    if q0 is None:
        return s
    qpos = q0 + lax.broadcasted_iota(jnp.int32, s.shape, 0)
    kpos = k0 + lax.broadcasted_iota(jnp.int32, s.shape, 1)
    return jnp.where(kpos <= qpos, s, NEG)


def _fox_c0(ck_ref, hh, pos):
    return ck_ref[0, hh:hh + 1, pl.ds(pl.multiple_of(pos, LANE), LANE)][:, 0:1]


def _fox_tiles(T):
    assert T % LANE == 0
    tq, tk = _tile(T, FOX_TQ), _tile(T, FOX_TK)
    shift = (tk // tq).bit_length() - 1
    assert tk == tq << shift
    return tq, tk, shift


def _fox_fwd(p, ck, C, offs):
    T = p.shape[0]
    tq, tk, shift = _fox_tiles(T)
    npair = C // LANE
    cb = lambda name: offs[name] // LANE

    def body(q_ref, k_ref, v_ref, ck_ref, o_ref, lse_ref):
        i = pl.program_id(1)
        nfull = i >> shift
        lse_ref[...] = jnp.zeros_like(lse_ref)
        heads = [slice(hh * HEAD, (hh + 1) * HEAD) for hh in range(2)]
        qs = [(q_ref[:, sl] * FOX_SCALE).astype(BF16) for sl in heads]
        c0s = [_fox_c0(ck_ref, hh, i * tq) for hh in range(2)]

        def step(j, carry, masked):
            off = pl.multiple_of(j * tk, tk)
            out = []
            for hh, sl in enumerate(heads):
                m, l, acc = carry[hh]
                k = k_ref[pl.ds(off, tk), sl].astype(BF16)
                v = v_ref[pl.ds(off, tk), sl].astype(BF16)
                ckv = ck_ref[0, hh:hh + 1, pl.ds(off, tk)] - c0s[hh]
                s = _fox_scores(qs[hh], k, ckv, i * tq, off) if masked else _fox_scores(qs[hh], k, ckv)
                m_new = jnp.maximum(m, jnp.max(s, axis=-1, keepdims=True))
                pr = jnp.exp(s - m_new)
                al = jnp.exp(m - m_new)
                l = al * l + jnp.sum(pr, axis=-1, keepdims=True)
                acc = al * acc + _dg(pr, v, False, False, "bf")
                out.append((m_new, l, acc))
            return tuple(out)

        one = (jnp.full((tq, 1), NEG, F32), jnp.zeros((tq, 1), F32), jnp.zeros((tq, HEAD), F32))
        carry = lax.fori_loop(0, nfull, lambda j, c: step(j, c, False), (one, one))
        for hh, (m, l, acc) in enumerate(step(nfull, carry, True)):
            o_ref[:, heads[hh]] = acc / l
            lse_ref[0, :, hh:hh + 1] = m + jnp.log(l)

    return _pcall(body, name="fox_fwd", grid=(npair, T // tq),
                  in_specs=[pl.BlockSpec((tq, LANE), lambda h, i: (i, cb("fq") + h)),
                            pl.BlockSpec((T, LANE), lambda h, i: (0, cb("fk") + h)),
                            pl.BlockSpec((T, LANE), lambda h, i: (0, cb("fv") + h)),
                            pl.BlockSpec((1, 8, T), lambda h, i: (h, 0, 0))],
                  out_specs=[pl.BlockSpec((tq, LANE), lambda h, i: (i, h)),
                             pl.BlockSpec((1, tq, 8), lambda h, i: (h, i, 0))],
                  out_shape=[jax.ShapeDtypeStruct((T, C), F32), jax.ShapeDtypeStruct((npair, T, 8), F32)],
                  compiler_params=_cp(("arbitrary", "arbitrary")))(p, p, p, ck)


def _fox_delta(p, ck, do, lse, C, offs):
    T = p.shape[0]
    tq, tk, shift = _fox_tiles(T)
    npair = C // LANE
    cb = lambda name: offs[name] // LANE

    def body(q_ref, k_ref, v_ref, ck_ref, do_ref, lse_ref, d_ref):
        i = pl.program_id(1)
        nfull = i >> shift
        d_ref[...] = jnp.zeros_like(d_ref)
        heads = [slice(hh * HEAD, (hh + 1) * HEAD) for hh in range(2)]
        qs = [(q_ref[:, sl] * FOX_SCALE).astype(BF16) for sl in heads]
        dos = [do_ref[:, sl].astype(BF16) for sl in heads]
        lses = [lse_ref[0, :, hh:hh + 1] for hh in range(2)]
        c0s = [_fox_c0(ck_ref, hh, i * tq) for hh in range(2)]

        def step(j, accs, masked):
            off = pl.multiple_of(j * tk, tk)
            out = []
            for hh, sl in enumerate(heads):
                k = k_ref[pl.ds(off, tk), sl].astype(BF16)
                v = v_ref[pl.ds(off, tk), sl].astype(BF16)
                ckv = ck_ref[0, hh:hh + 1, pl.ds(off, tk)] - c0s[hh]
                s = _fox_scores(qs[hh], k, ckv, i * tq, off) if masked else _fox_scores(qs[hh], k, ckv)
                pr = jnp.exp(s - lses[hh])
                dp = _dg(dos[hh], v, False, True, "bf")
                out.append(accs[hh] + jnp.sum(pr * dp, axis=-1, keepdims=True))
            return tuple(out)

        z = jnp.zeros((tq, 1), F32)
        accs = lax.fori_loop(0, nfull, lambda j, c: step(j, c, False), (z, z))
        for hh, acc in enumerate(step(nfull, accs, True)):
            d_ref[0, :, hh:hh + 1] = acc

    return _pcall(body, name="fox_delta", grid=(npair, T // tq),
                  in_specs=[pl.BlockSpec((tq, LANE), lambda h, i: (i, cb("fq") + h)),
                            pl.BlockSpec((T, LANE), lambda h, i: (0, cb("fk") + h)),
                            pl.BlockSpec((T, LANE), lambda h, i: (0, cb("fv") + h)),
                            pl.BlockSpec((1, 8, T), lambda h, i: (h, 0, 0)),
                            pl.BlockSpec((tq, LANE), lambda h, i: (i, h)),
                            pl.BlockSpec((1, tq, 8), lambda h, i: (h, i, 0))],
                  out_specs=pl.BlockSpec((1, tq, 8), lambda h, i: (h, i, 0)),
                  out_shape=jax.ShapeDtypeStruct((npair, T, 8), F32),
                  compiler_params=_cp(("arbitrary", "arbitrary")))(p, p, p, ck, do, lse)


def _fox_bwd(p, ck, delta, do, lse, C, offs):
    T = p.shape[0]
    tq, tk, shift = _fox_tiles(T)
    ratio = tk // tq
    nq = T // tq
    npair = C // LANE
    cb = lambda name: offs[name] // LANE

    def body(q_ref, k_ref, v_ref, ck_ref, ckall_ref, dl_ref, do_ref, lse_ref, dq_ref, dk_ref, dv_ref, dck_ref):
        j = pl.program_id(1)

        @pl.when(j == 0)
        def _():
            dq_ref[...] = jnp.zeros_like(dq_ref)

        dck_ref[...] = jnp.zeros_like(dck_ref)
        heads = [slice(hh * HEAD, (hh + 1) * HEAD) for hh in range(2)]
        ks = [k_ref[:, sl].astype(BF16) for sl in heads]
        vs = [v_ref[:, sl].astype(BF16) for sl in heads]
        cks = [ck_ref[0, hh:hh + 1, :] for hh in range(2)]

        def step(i, carry, masked):
            off = pl.multiple_of(i * tq, tq)
            out = []
            for hh, sl in enumerate(heads):
                dk, dv, dck = carry[hh]
                q = (q_ref[pl.ds(off, tq), sl] * FOX_SCALE).astype(BF16)
                dov = do_ref[pl.ds(off, tq), sl]
                lsev = lse_ref[0, pl.ds(off, tq), hh:hh + 1]
                ckv = cks[hh] - _fox_c0(ckall_ref, hh, off)
                s = _fox_scores(q, ks[hh], ckv, off, j * tk) if masked else _fox_scores(q, ks[hh], ckv)
                pr = jnp.exp(s - lsev)
                dv = dv + _dg(pr, dov, True, False, "bf")
                dp = _dg(dov, vs[hh], False, True, "bf")
                ds = pr * (dp - dl_ref[0, pl.ds(off, tq), hh:hh + 1])
                dk = dk + _dg(ds, q, True, False, "bf")
                dq_ref[pl.ds(off, tq), sl] += _dg(ds, ks[hh], False, False, "bf") * FOX_SCALE
                out.append((dk, dv, dck - jnp.sum(ds, axis=0, keepdims=True)))
            return tuple(out)

        z = jnp.zeros((tk, HEAD), F32)
        carry = ((z, z, jnp.zeros((1, tk), F32)),) * 2
        for r in range(ratio):
            carry = step(j * ratio + r, carry, True)
        carry = lax.fori_loop((j + 1) * ratio, nq, lambda i, c: step(i, c, False), carry)
        for hh, (dk, dv, dck) in enumerate(carry):
            dk_ref[:, heads[hh]] = dk
            dv_ref[:, heads[hh]] = dv
            dck_ref[0, hh:hh + 1, :] = dck

    full = lambda h, j: (0, h)
    return _pcall(body, name="fox_bwd", grid=(npair, T // tk),
                  in_specs=[pl.BlockSpec((T, LANE), lambda h, j: (0, cb("fq") + h)),
                            pl.BlockSpec((tk, LANE), lambda h, j: (j, cb("fk") + h)),
                            pl.BlockSpec((tk, LANE), lambda h, j: (j, cb("fv") + h)),
                            pl.BlockSpec((1, 8, tk), lambda h, j: (h, 0, j)),
                            pl.BlockSpec((1, 8, T), lambda h, j: (h, 0, 0)),
                            pl.BlockSpec((1, T, 8), lambda h, j: (h, 0, 0)), pl.BlockSpec((T, LANE), full),
                            pl.BlockSpec((1, T, 8), lambda h, j: (h, 0, 0))],
                  out_specs=[pl.BlockSpec((T, LANE), full),
                             pl.BlockSpec((tk, LANE), lambda h, j: (j, h)),
                             pl.BlockSpec((tk, LANE), lambda h, j: (j, h)),
                             pl.BlockSpec((1, 8, tk), lambda h, j: (h, 0, j))],
                  out_shape=[jax.ShapeDtypeStruct((T, C), F32)] * 3 + [jax.ShapeDtypeStruct((npair, 8, T), F32)],
                  compiler_params=_cp(("arbitrary", "arbitrary")))(p, p, p, ck, ck, delta, do, lse)


@jax.custom_vjp
def _solve(a, rhs):
    return _solve_fwd(a, rhs)[0]


def _solve_fwd(a, rhs):
    powers = [a]
    for _ in range(CHUNK.bit_length() - 2):
        powers.append(_dg(powers[-1], powers[-1], False, False, "x3"))
    u = rhs
    for pw in powers:
        u = u + _dg(pw, u, False, False, "x3")
    return u, (powers, u)


def _solve_bwd(res, g):
    powers, u = res
    w = g
    for pw in powers:
        w = w + _dg(pw, w, True, False, "x3")
    return _dg(w, u, False, True, "x3"), w


_solve.defvjp(_solve_fwd, _solve_bwd)


def _chunk_fn(S0, r, lw, k, v, a, b):
    nh, n = r.shape[0], r.shape[1]
    row = lax.broadcasted_iota(jnp.int32, (nh, n, n), 1)
    col = lax.broadcasted_iota(jnp.int32, (nh, n, n), 2)
    incl, strict = col <= row, col < row
    mm = lambda x, y, ta=False, tb=False: _mm(x, y, ta, tb, "x3")
    g = _mm(incl.astype(F32), lw, False, False, "hi")
    einv = jnp.exp(-g)
    rt, at, bt, kt = r * jnp.exp(g), a * jnp.exp(g - lw), b * einv, k * einv
    a_ab = jnp.where(strict, mm(at, bt, tb=True), 0.0)
    a_ak = jnp.where(strict, mm(at, kt, tb=True), 0.0)
    r_b = jnp.where(incl, mm(rt, bt, tb=True), 0.0)
    r_k = jnp.where(incl, mm(rt, kt, tb=True), 0.0)
    u = _solve(a_ab, mm(at, S0, tb=True) + mm(a_ak, v))
    y = mm(rt, S0, tb=True) + mm(r_b, u) + mm(r_k, v)
    g_end = jnp.sum(lw, axis=1, keepdims=True)
    s_end = (S0 + mm(u, bt, ta=True) + mm(v, kt, ta=True)) * jnp.exp(g_end)
    return y, s_end


SCAN_HEADS = 12
SCAN_ROWS = 64


def _scan_group(C):
    nh = SCAN_HEADS
    while C % (nh * HEAD):
        nh -= 2
    return nh, nh * HEAD


def _scan_fwd(r, lw, k, v, a, b):
    T, C = r.shape
    tc = _tile(T, SCAN_ROWS)
    ncs = tc // CHUNK
    nh, GW = _scan_group(C)
    ngroup = C // GW

    def body(r_ref, lw_ref, k_ref, v_ref, a_ref, b_ref, y_ref, ck_ref, state):
        @pl.when(pl.program_id(1) == 0)
        def _():
            state[...] = jnp.zeros_like(state)

        heads = [slice(hh * HEAD, (hh + 1) * HEAD) for hh in range(nh)]
        split = lambda ref, rows: jnp.stack([ref[rows, sl] for sl in heads])
        st = split(state, slice(None))
        for c in range(ncs):
            rows = slice(c * CHUNK, (c + 1) * CHUNK)
            for hh, sl in enumerate(heads):
                ck_ref[0, c, :, sl] = st[hh]
            y, st = _chunk_fn(st, *[split(ref, rows) for ref in (r_ref, lw_ref, k_ref, v_ref, a_ref, b_ref)])
            for hh, sl in enumerate(heads):
                y_ref[rows, sl] = y[hh]
        for hh, sl in enumerate(heads):
            state[:, sl] = st[hh]

    spec = pl.BlockSpec((tc, GW), lambda h, t: (t, h))
    return _pcall(body, name="rwkv_scan_fwd", grid=(ngroup, T // tc),
                  in_specs=[spec] * 6,
                  out_specs=[spec, pl.BlockSpec((1, ncs, HEAD, GW), lambda h, t: (h, t, 0, 0))],
                  out_shape=[jax.ShapeDtypeStruct((T, C), F32),
                             jax.ShapeDtypeStruct((ngroup, T // CHUNK, HEAD, GW), F32)],
                  scratch_shapes=[pltpu.VMEM((HEAD, GW), F32)],
                  compiler_params=_cp(("arbitrary", "arbitrary")))(r, lw, k, v, a, b)


def _scan_bwd(r, lw, k, v, a, b, ckpt, dy):
    T, C = r.shape
    tc = _tile(T, SCAN_ROWS)
    ncs = tc // CHUNK
    nh, GW = _scan_group(C)
    ngroup = C // GW
    nt = T // tc

    def body(r_ref, lw_ref, k_ref, v_ref, a_ref, b_ref, ck_ref, dy_ref, dr, dlw, dk, dv, da, db, dstate):
        @pl.when(pl.program_id(1) == 0)
        def _():
            dstate[...] = jnp.zeros_like(dstate)

        outs = (dr, dlw, dk, dv, da, db)
        heads = [slice(hh * HEAD, (hh + 1) * HEAD) for hh in range(nh)]
        split = lambda ref, rows: jnp.stack([ref[rows, sl] for sl in heads])
        dst = split(dstate, slice(None))
        for c in reversed(range(ncs)):
            rows = slice(c * CHUNK, (c + 1) * CHUNK)
            s0 = jnp.stack([ck_ref[0, c, :, sl] for sl in heads])
            _, vjp = jax.vjp(_chunk_fn, s0, *[split(ref, rows) for ref in (r_ref, lw_ref, k_ref, v_ref, a_ref, b_ref)])
            grads = vjp((split(dy_ref, rows), dst))
            dst = grads[0]
            for o, gval in zip(outs, grads[1:]):
                for hh, sl in enumerate(heads):
                    o[rows, sl] = gval[hh]
        for hh, sl in enumerate(heads):
            dstate[:, sl] = dst[hh]

    spec = pl.BlockSpec((tc, GW), lambda h, t: (nt - 1 - t, h))
    return _pcall(body, name="rwkv_scan_bwd", grid=(ngroup, nt),
                  in_specs=[spec] * 6 + [pl.BlockSpec((1, ncs, HEAD, GW), lambda h, t: (h, nt - 1 - t, 0, 0)), spec],
                  out_specs=[spec] * 6, out_shape=[jax.ShapeDtypeStruct((T, C), F32)] * 6,
                  scratch_shapes=[pltpu.VMEM((HEAD, GW), F32)],
                  compiler_params=_cp(("arbitrary", "arbitrary")))(r, lw, k, v, a, b, ckpt, dy)


def _adam(w, g, m, v):
    m = ADAM_B1 * m + (1.0 - ADAM_B1) * g
    v = ADAM_B2 * v + (1.0 - ADAM_B2) * (g * g)
    m_hat = m / (1.0 - ADAM_B1 ** ADAM_STEP)
    v_hat = v / (1.0 - ADAM_B2 ** ADAM_STEP)
    return -ADAM_LR * (m_hat / (jnp.sqrt(v_hat) + ADAM_EPS) + ADAM_WD * w), m, v


def _sum_adam(parts, w, m, v, name):
    n, R, W = parts.shape
    tr = _tile(R, max(8, min(256, (1 << 20) // (n * W))))

    def body(p_ref, w_ref, m_ref, v_ref, g_ref, d_ref, nm_ref, nv_ref):
        g = p_ref[0].astype(F32)
        for s in range(1, n):
            g = g + p_ref[s].astype(F32)
        d, nm, nv = _adam(w_ref[...], g, m_ref[...], v_ref[...])
        g_ref[...] = g
        d_ref[...] = d
        nm_ref[...] = nm
        nv_ref[...] = nv

    return _pcall(body, name=name, grid=(R // tr,),
                  in_specs=[pl.BlockSpec((n, tr, W), lambda i: (0, i, 0))] + [_row_spec(tr, W)] * 3,
                  out_specs=[_row_spec(tr, W)] * 4, out_shape=[jax.ShapeDtypeStruct((R, W), F32)] * 4,
                  compiler_params=_cp(("arbitrary",)))(parts, w, m, v)


def _pad_lanes(vec, width):
    return jnp.pad(vec, ((0, 0), (0, width - vec.shape[1])))


def _logical_cols(blocks, lo, hi):
    B, out = blocks.shape[2], []
    while lo < hi:
        j, o = divmod(lo, B)
        n = min(hi - lo, B - o)
        out.append(blocks[j, :, o:o + n])
        lo += n
    return out


def kernel(x, mem, g_pre, w_in, mu_rwkv, w0, w_decay_up, a0, w_iclr_up, k_k, k_a, r_k, ln_x_w, ln_x_b, b_f, g_mem, w_mem_kv, w_out, g_post, loss_target, m_g_pre, m_w_in, m_mu_rwkv, m_w0, m_w_decay_up, m_a0, m_w_iclr_up, m_k_k, m_k_a, m_r_k, m_ln_x_w, m_ln_x_b, m_b_f, m_g_mem, m_w_mem_kv, m_w_out, m_g_post, v_g_pre, v_w_in, v_mu_rwkv, v_w0, v_w_decay_up, v_a0, v_w_iclr_up, v_k_k, v_k_a, v_r_k, v_ln_x_w, v_ln_x_b, v_b_f, v_g_mem, v_w_mem_kv, v_w_out, v_g_post):
    T, D = x.shape[1], x.shape[2]
    C = w0.shape[1]
    L = w_decay_up.shape[1]
    H = C // HEAD
    MW = w_mem_kv.shape[2] // 2
    SH = 3 * C + 2 * L
    IN = NDEV * w_in.shape[2]
    assert IN == SH + 5 * C + H + 2 * MW and D == 2 * C + MW and H % 2 == 0 and H <= LANE
    assert C % LANE == 0 and L % LANE == 0 and MW % (MEM_HEADS * HEAD) == 0 and T % CHUNK == 0
    offs = dict(grw=SH, fq=SH + C, fk=SH + 2 * C, fv=SH + 3 * C, gfx=SH + 4 * C, mq=SH + 5 * C, gmq=SH + 5 * C + MW,
                fl=SH + 5 * C + 2 * MW)
    NI = -(-(offs["fl"] + LANE) // 1024) * 1024
    l_fl = SH + 4 * C

    x2, mem2, tgt2 = x[0], mem[0], loss_target[0]

    wg = _exchange(w_in[0].astype(BF16), "gather_w_in", True)
    w_perm = jnp.concatenate(_logical_cols(wg, 0, l_fl) + _logical_cols(wg, l_fl + H, IN)
                             + _logical_cols(wg, l_fl, l_fl + H) + [jnp.zeros((D, NI - IN), BF16)], axis=1)

    h = _rmsnorm_fwd(x2, g_pre, "rmsnorm_pre")
    p, w_out_f, w_kv_f, lora = _matmul(
        h, w_perm, False, False, "in_proj",
        riders=[(w_out[0].astype(BF16), True), (w_mem_kv[0].astype(BF16), True),
                (jnp.concatenate([w_decay_up[0], w_iclr_up[0]], axis=0), True)])
    w_out_f, w_kv_f = w_out_f.reshape(D, D), w_kv_f.reshape(D, 2 * MW)
    lora = jnp.transpose(lora, (1, 0, 2)).reshape(2 * L, C)
    wdu_f, wiu_f = lora[:L], lora[L:]

    E, ET = _head_indicator(C, LANE)
    prep_params = [mu_rwkv, w0, a0, k_k, k_a, wdu_f, wiu_f, E, ET]
    mix_params = [ln_x_w, ln_x_b, r_k.reshape(1, C), E, ET]
    b_f_pad = _pad_lanes(b_f, LANE)
    ps = p[:, :SH]
    col = lambda name, w: p[:, offs[name]:offs[name] + w]
    g_rwkv, g_fox, mq, g_mq = col("grw", C), col("gfx", C), col("mq", MW), col("gmq", MW)
    prev = jnp.concatenate([jnp.zeros((1, SH), F32), ps[:-1]], axis=0)
    r, lw, kmod, v, a, b = _prep_fwd(p, prev, prep_params, C, L)
    y_scan, ckpt = _scan_fwd(r, lw, kmod, v, a, b)

    cum, cum_t = _fox_cum_fwd(p, b_f_pad, offs["fl"])
    ck = jnp.pad(cum_t[:H].reshape(H // 2, 2, T), ((0, 0), (0, 6), (0, 0)))
    y_fox, lse = _fox_fwd(p, ck, C, offs)

    memn = _rmsnorm_fwd(mem2, g_mem, "rmsnorm_mem")
    mkv = _matmul(memn, w_kv_f, False, False, "mem_kv_proj")
    y_mem = _memattn_fwd(mq, mkv, MW)

    acts = [y_scan, r, kmod, v, g_rwkv, y_fox, g_fox, y_mem, g_mq]
    ycat = _mix_fwd(acts, mix_params, C, MW)
    yo = _matmul(ycat, w_out_f, False, False, "out_proj")
    d_yo, d_out, loss_part, dg_post = _post(yo, x2, tgt2, g_post)
    loss = lax.psum(loss_part[0, 0], AXES)

    g_w_out = _matmul(ycat, d_yo, True, False, "grad_w_out", BF16)
    d_ycat = _matmul(d_yo, w_out_f, False, True, "d_ycat")
    (d_y, d_r1, d_k1, d_v1, d_grw, d_yfox, d_gfx, d_ymem, d_gmq, dg_lnw, dg_lnb, dg_rk) = _mix_bwd(
        acts, mix_params, d_ycat, C, MW)

    d_mq, d_mkv = _memattn_bwd(mq, mkv, d_ymem, MW)
    g_w_kv = _matmul(memn, d_mkv, True, False, "grad_w_mem_kv", BF16)
    d_memn = _matmul(d_mkv, w_kv_f, False, True, "d_memn")
    _, dg_mem = _rmsnorm_bwd(mem2, g_mem, d_memn, None, "rmsnorm_mem_bwd")

    delta = _fox_delta(p, ck, d_yfox, lse, C, offs)
    d_fq, d_fk, d_fv, d_ck = _fox_bwd(p, ck, delta, d_yfox, lse, C, offs)
    d_cum = _pad_lanes(d_ck[:, :2, :].reshape(H, T).T, LANE)
    d_fl, dg_bf = _fox_cum_bwd(p, b_f_pad, d_cum, offs["fl"])

    d_r, d_lw, d_k, d_v, d_a, d_b = _scan_bwd(r, lw, kmod, v, a, b, ckpt, d_y)
    cts = [d_r, d_lw, d_k, d_v, d_a, d_b, d_r1, d_k1, d_v1]
    (d_ps, d_prev, dg_mu, dg_w0, dg_a0, dg_kk, dg_ka, dg_wdu, dg_wiu) = _prep_bwd(p, prev, prep_params, cts, C, L)
    d_prev_up = jnp.concatenate([d_prev[1:], jnp.zeros((1, SH), F32)], axis=0)
    d_sh = _shift_combine(d_ps, d_prev_up)

    tobf = lambda z: z.astype(BF16)
    dp = jnp.concatenate([d_sh, tobf(d_grw), tobf(d_fq), tobf(d_fk), tobf(d_fv), tobf(d_gfx), tobf(d_mq), tobf(d_gmq),
                          tobf(d_fl), jnp.zeros((T, NI - offs["fl"] - LANE), BF16)], axis=1)
    g_lora = jnp.concatenate([dg_wdu, dg_wiu], axis=0)
    g_w_perm, parts_out, parts_kv, parts_lora = _matmul(
        h, dp, True, False, "grad_w_in", BF16,
        riders=[(g_w_out.reshape(NDEV, D // NDEV, D), False), (g_w_kv.reshape(NDEV, D // NDEV, 2 * MW), False),
                (jnp.transpose(g_lora.reshape(2 * L, NDEV, C // NDEV), (1, 0, 2)), False)])

    def internal_cols(lo, hi):
        out = []
        for first, last, shift in ((0, l_fl, 0), (l_fl, l_fl + H, offs["fl"] - l_fl), (l_fl + H, IN, -H)):
            s0, s1 = max(lo, first), min(hi, last)
            if s0 < s1:
                out.append(g_w_perm[:, s0 + shift:s1 + shift])
        return out

    blk = IN // NDEV
    g_blocks = jnp.stack([jnp.concatenate(internal_cols(j * blk, (j + 1) * blk), axis=1) for j in range(NDEV)])
    d_h, parts_in = _matmul(dp, w_perm, False, True, "d_h", tk=5120, riders=[(g_blocks, False)])
    grad_x, dg_pre = _rmsnorm_bwd(x2, g_pre, d_h, d_out, "rmsnorm_pre_bwd")

    gw_in, dw_in, nm_w_in, nv_w_in = _sum_adam(parts_in, w_in[0], m_w_in[0], v_w_in[0], "adam_w_in")
    gw_out, dw_out, nm_w_out, nv_w_out = _sum_adam(parts_out, w_out[0], m_w_out[0], v_w_out[0], "adam_w_out")
    gw_kv, dw_kv, nm_w_kv, nv_w_kv = _sum_adam(parts_kv, w_mem_kv[0], m_w_mem_kv[0], v_w_mem_kv[0], "adam_w_mem_kv")
    cat2 = lambda u, w_: jnp.concatenate([u[0], w_[0]], axis=0)
    lora_res = _sum_adam(parts_lora, cat2(w_decay_up, w_iclr_up), cat2(m_w_decay_up, m_w_iclr_up),
                         cat2(v_w_decay_up, v_w_iclr_up), "adam_lora")

    small = [("g_pre", g_pre, m_g_pre, v_g_pre, dg_pre), ("mu_rwkv", mu_rwkv, m_mu_rwkv, v_mu_rwkv, dg_mu),
             ("w0", w0, m_w0, v_w0, dg_w0), ("a0", a0, m_a0, v_a0, dg_a0), ("k_k", k_k, m_k_k, v_k_k, dg_kk),
             ("k_a", k_a, m_k_a, v_k_a, dg_ka), ("r_k", r_k.reshape(1, C), m_r_k.reshape(1, C), v_r_k.reshape(1, C), dg_rk),
             ("ln_x_w", ln_x_w, m_ln_x_w, v_ln_x_w, dg_lnw), ("ln_x_b", ln_x_b, m_ln_x_b, v_ln_x_b, dg_lnb),
             ("b_f", _pad_lanes(b_f, LANE), _pad_lanes(m_b_f, LANE), _pad_lanes(v_b_f, LANE), dg_bf),
             ("g_mem", g_mem, m_g_mem, v_g_mem, dg_mem), ("g_post", g_post, m_g_post, v_g_post, dg_post)]
    widths = [s[1].shape[1] for s in small]
    pack = lambda idx: jnp.concatenate([s[idx] for s in small], axis=1).reshape(-1, LANE)
    parts_small = _exchange(pack(4), "gather_small_grads", True)
    res_small = _sum_adam(parts_small, pack(1), pack(2), pack(3), "adam_small")

    def unpack(flat):
        flat = flat.reshape(1, -1)
        out, o = {}, 0
        for (name, *_), wd in zip(small, widths):
            out[name] = flat[:, o:o + wd]
            o += wd
        out["b_f"] = out["b_f"][:, :H]
        out["r_k"] = out["r_k"].reshape(1, H, HEAD)
        return out

    sg, sd, sm, sv = [unpack(z) for z in res_small]
    big = {"w_in": (gw_in, dw_in, nm_w_in, nv_w_in), "w_out": (gw_out, dw_out, nm_w_out, nv_w_out),
           "w_mem_kv": (gw_kv, dw_kv, nm_w_kv, nv_w_kv),
           "w_decay_up": tuple(z[:L] for z in lora_res), "w_iclr_up": tuple(z[L:] for z in lora_res)}
    order = ["g_pre", "w_in", "mu_rwkv", "w0", "w_decay_up", "a0", "w_iclr_up", "k_k", "k_a", "r_k", "ln_x_w", "ln_x_b",
             "b_f", "g_mem", "w_mem_kv", "w_out", "g_post"]

    def pick(name, idx):
        if name in big:
            return big[name][idx][None]
        return (sg, sd, sm, sv)[idx][name]

    outs = [loss, grad_x[None]]
    for idx in range(4):
        outs += [pick(n, idx) for n in order]
    return tuple(outs)
```

```python
import functools

import jax
import jax.numpy as jnp
from jax import lax
from jax.experimental import pallas as pl
from jax.experimental.pallas import tpu as pltpu

F32, BF16 = jnp.float32, jnp.bfloat16
HI = lax.Precision.HIGHEST
NDEV = 8
AXES = ("x", "y", "c")
HEAD = 64
CHUNK = 64
MEM_HEADS = 4
LANE = 128
RMS_EPS = 1e-6
GN_EPS = 64e-5
NEG = -1e30
ADAM_LR, ADAM_B1, ADAM_B2, ADAM_EPS, ADAM_WD, ADAM_STEP = 0.001, 0.9, 0.999, 1e-08, 0.01, 10
VMEM_LIMIT = 56 * 1024 * 1024


def _pcall(body, **kw):
    return pl.pallas_call(body, **kw)


def _cp(sem=None, vmem=VMEM_LIMIT):
    return pltpu.CompilerParams(dimension_semantics=sem, vmem_limit_bytes=vmem)


def _tile(n, pref):
    for t in (pref, 1024, 512, 256, 128, 64, 32, 16, 8):
        if t <= pref and n % t == 0:
            return t
    return n


def _dg(a, b, ta, tb, mode):
    nb = a.ndim - 2
    ca = nb + (0 if ta else 1)
    cb = nb + (1 if tb else 0)
    dims = (((ca,), (cb,)), (tuple(range(nb)), tuple(range(nb))))
    if mode == "x3":
        a_hi, b_hi = a.astype(BF16), b.astype(BF16)
        a_lo, b_lo = (a - a_hi.astype(F32)).astype(BF16), (b - b_hi.astype(F32)).astype(BF16)
        a3 = jnp.concatenate([a_hi, a_lo, a_hi], axis=ca)
        b3 = jnp.concatenate([b_hi, b_hi, b_lo], axis=cb)
        return lax.dot_general(a3, b3, dims, preferred_element_type=F32)
    if mode == "x2":
        a_hi, b = a.astype(BF16), b.astype(BF16)
        a_lo = (a - a_hi.astype(F32)).astype(BF16)
        dot = lambda u: lax.dot_general(u, b, dims, preferred_element_type=F32)
        return dot(a_hi) + dot(a_lo)
    if mode == "bf":
        a, b, prec = a.astype(BF16), b.astype(BF16), None
    else:
        prec = HI
    return lax.dot_general(a, b, dims, preferred_element_type=F32, precision=prec)


@functools.partial(jax.custom_vjp, nondiff_argnums=(2, 3, 4))
def _mm(a, b, ta, tb, mode):
    return _dg(a, b, ta, tb, mode)


def _mm_fwd(a, b, ta, tb, mode):
    return _dg(a, b, ta, tb, mode), (a, b)


def _mm_bwd(ta, tb, mode, res, g):
    a, b = res
    da = _dg(g, b, False, not tb, mode) if not ta else _dg(b, g, tb, True, mode)
    db = _dg(a, g, not ta, False, mode) if not tb else _dg(g, a, True, ta, mode)
    return da, db


_mm.defvjp(_mm_fwd, _mm_bwd)


@jax.custom_vjp
def _seg(a, e):
    return _dg(a, e, False, False, "x2")


def _seg_fwd(a, e):
    return _dg(a, e, False, False, "x2"), e


def _seg_bwd(e, g):
    return _dg(g, e, False, True, "x2"), jnp.zeros_like(e)


_seg.defvjp(_seg_fwd, _seg_bwd)


def _sigmoid(z):
    return 1.0 / (1.0 + jnp.exp(-z))


def _softplus(z):
    return jnp.maximum(z, 0.0) + jnp.log(1.0 + jnp.exp(-jnp.abs(z)))


def _silu(z):
    return z * _sigmoid(z)


def _rms(x, g):
    return x * lax.rsqrt(jnp.mean(x * x, axis=-1, keepdims=True) + RMS_EPS) * g


HBM_SPEC = pl.BlockSpec(memory_space=pltpu.HBM)
EXCHANGE_SEMS = [pltpu.SemaphoreType.DMA((NDEV - 1,)), pltpu.SemaphoreType.DMA((NDEV - 1,)), pltpu.SemaphoreType.DMA(())]


def _exchange_copies(gather, x_ref, o_ref, send_sems, recv_sems, local_sem, arrivals):
    ix, iy, ic = lax.axis_index("x"), lax.axis_index("y"), lax.axis_index("c")
    me = 4 * ix + 2 * iy + ic

    def src(dest):
        return x_ref if gather else x_ref.at[dest]

    mine = pltpu.make_async_copy(src(me), o_ref.at[me], local_sem)
    pairs = []
    for k in range(1, NDEV):
        px = 1 - ix if (k >> 2) & 1 else ix
        py = 1 - iy if (k >> 1) & 1 else iy
        pc = 1 - ic if k & 1 else ic
        peer = 4 * px + 2 * py + pc
        send = pltpu.make_async_remote_copy(
            src_ref=src(peer), dst_ref=o_ref.at[me], send_sem=send_sems.at[k - 1], recv_sem=recv_sems.at[k - 1],
            device_id=(px, py, pc), device_id_type=pl.DeviceIdType.MESH)
        arrival = arrivals and pltpu.make_async_remote_copy(
            src_ref=src(peer), dst_ref=o_ref.at[peer], send_sem=send_sems.at[k - 1], recv_sem=recv_sems.at[k - 1],
            device_id=(ix, iy, ic), device_id_type=pl.DeviceIdType.MESH)
        pairs.append((send, arrival))
    return mine, pairs


def _exchange_start(*args):
    mine, pairs = _exchange_copies(*args, arrivals=False)
    mine.start()
    for send, _ in pairs:
        send.start()


def _exchange_wait(*args):
    mine, pairs = _exchange_copies(*args, arrivals=True)
    for send, arrival in pairs:
        send.wait_send()
        arrival.wait_recv()
    mine.wait()


def _exchange_shape(x, gather):
    return jax.ShapeDtypeStruct((NDEV,) + tuple(x.shape if gather else x.shape[1:]), x.dtype)


def _exchange(x, name, gather):
    def body(x_ref, o_ref, *sems):
        _exchange_start(gather, x_ref, o_ref, *sems)
        _exchange_wait(gather, x_ref, o_ref, *sems)

    return _pcall(body, name=name, out_shape=_exchange_shape(x, gather), in_specs=[HBM_SPEC], out_specs=HBM_SPEC,
                  scratch_shapes=list(EXCHANGE_SEMS))(x)


MAX_FULL_K = 4096


def _matmul(a, b, ta, tb, name, out_dtype=F32, tm=1024, tn=1024, tk=3072, riders=()):
    M, K = (a.shape[1], a.shape[0]) if ta else a.shape
    N = b.shape[0] if tb else b.shape[1]
    assert (b.shape[1] if tb else b.shape[0]) == K
    if K <= MAX_FULL_K:
        tk = K
    else:
        tm, tk = min(tm, 512), _tile(K, tk)
    tm, tn = _tile(M, tm), _tile(N, tn)
    grid = (M // tm, N // tn, K // tk)
    nk, nr = grid[2], len(riders)

    def body(*refs):
        a_ref, b_ref, x_refs = refs[0], refs[1], refs[2:2 + nr]
        o_ref, xo_refs, rest = refs[2 + nr], refs[3 + nr:3 + 2 * nr], refs[3 + 2 * nr:]
        sems = rest[1:] if nk > 1 else rest
        ids = [pl.program_id(d) for d in range(3)]
        jobs = [(riders[q][1], x_refs[q], xo_refs[q]) + tuple(sems[3 * q:3 * q + 3]) for q in range(nr)]

        if nr:
            @pl.when((ids[0] == 0) & (ids[1] == 0) & (ids[2] == 0))
            def _():
                for job in jobs:
                    _exchange_start(*job)

        if nk == 1:
            o_ref[...] = _dg(a_ref[...], b_ref[...], ta, tb, "bf").astype(o_ref.dtype)
        else:
            acc = rest[0]

            @pl.when(ids[2] == 0)
            def _():
                acc[...] = jnp.zeros_like(acc)

            acc[...] += _dg(a_ref[...], b_ref[...], ta, tb, "bf")

            @pl.when(ids[2] == nk - 1)
            def _():
                o_ref[...] = acc[...].astype(o_ref.dtype)

        if nr:
            @pl.when((ids[0] == grid[0] - 1) & (ids[1] == grid[1] - 1) & (ids[2] == nk - 1))
            def _():
                for job in jobs:
                    _exchange_wait(*job)

    a_spec = pl.BlockSpec((tk, tm), lambda i, j, k: (k, i)) if ta else pl.BlockSpec((tm, tk), lambda i, j, k: (i, k))
    b_spec = pl.BlockSpec((tn, tk), lambda i, j, k: (j, k)) if tb else pl.BlockSpec((tk, tn), lambda i, j, k: (k, j))
    out = _pcall(
        body, name=name, grid=grid,
        in_specs=[a_spec, b_spec] + [HBM_SPEC] * nr,
        out_specs=[pl.BlockSpec((tm, tn), lambda i, j, k: (i, j))] + [HBM_SPEC] * nr,
        out_shape=[jax.ShapeDtypeStruct((M, N), out_dtype)] + [_exchange_shape(x, g) for x, g in riders],
        scratch_shapes=([pltpu.VMEM((tm, tn), F32)] if nk > 1 else []) + list(EXCHANGE_SEMS) * nr,
        compiler_params=_cp(("arbitrary", "arbitrary", "arbitrary")),
    )(a, b, *[x for x, _ in riders])
    return tuple(out) if nr else out[0]


def _row_spec(tr, width, col_block=0):
    return pl.BlockSpec((tr, width), lambda i: (i, col_block))


def _full_spec(shape):
    nd = len(shape)
    return pl.BlockSpec(tuple(shape), lambda i: (0,) * nd)


def _rmsnorm_fwd(x, g, name, tr=256):
    R, D = x.shape
    tr = _tile(R, tr)

    def body(x_ref, g_ref, o_ref):
        o_ref[...] = _rms(x_ref[...], g_ref[...]).astype(BF16)

    return _pcall(body, name=name, grid=(R // tr,),
                  in_specs=[_row_spec(tr, D), _full_spec((1, D))], out_specs=_row_spec(tr, D),
                  out_shape=jax.ShapeDtypeStruct((R, D), BF16), compiler_params=_cp(("arbitrary",)))(x, g)


def _rmsnorm_bwd(x, g, dy, extra, name, tr=128):
    R, D = x.shape
    tr = _tile(R, tr)
    has_extra = extra is not None

    def body(*refs):
        if has_extra:
            x_ref, g_ref, dy_ref, e_ref, dx_ref, dg_ref = refs
        else:
            x_ref, g_ref, dy_ref, dx_ref, dg_ref = refs
        _, vjp = jax.vjp(_rms, x_ref[...], g_ref[...])
        dx, dg = vjp(dy_ref[...])
        dx_ref[...] = dx + e_ref[...] if has_extra else dx

        @pl.when(pl.program_id(0) == 0)
        def _():
            dg_ref[...] = jnp.zeros_like(dg_ref)

        dg_ref[...] += dg

    ins = [x, g, dy] + ([extra] if has_extra else [])
    specs = [_row_spec(tr, D), _full_spec((1, D)), _row_spec(tr, D)] + ([_row_spec(tr, D)] if has_extra else [])
    return _pcall(body, name=name, grid=(R // tr,), in_specs=specs,
                  out_specs=[_row_spec(tr, D), _full_spec((1, D))],
                  out_shape=[jax.ShapeDtypeStruct((R, D), F32), jax.ShapeDtypeStruct((1, D), F32)],
                  compiler_params=_cp(("arbitrary",)))(*ins)


def _head_indicator(C, hp):
    e = (jnp.arange(C)[:, None] // HEAD == jnp.arange(hp)[None, :]).astype(F32)
    return e, e.T


def _prep_fn(C, L, ps, prev, mu, w0, a0, k_k, k_a, wdu, wiu, E, ET):
    sh = ps + (prev - ps) * mu
    r, k, v = sh[:, :C], sh[:, C:2 * C], sh[:, 2 * C:3 * C]
    wl, al = sh[:, 3 * C:3 * C + L], sh[:, 3 * C + L:3 * C + 2 * L]
    wd = w0 + _mm(jnp.tanh(wl), wdu, False, False, "bf")
    w_pre = -_softplus(-wd) - 0.5
    lw = -jnp.exp(w_pre)
    alpha = _sigmoid(a0 + _mm(al, wiu, False, False, "bf"))
    kk = k * k_k
    ss = _seg(kk * kk, E)
    kk = kk * _seg(lax.rsqrt(jnp.maximum(ss, 1e-24)), ET)
    k_mod = k * (1.0 + (alpha - 1.0) * k_a)
    return r, lw, k_mod, v, -kk, kk * alpha


def _prep_fwd(p, prev, params, C, L, tr=128):
    T = p.shape[0]
    SH = 3 * C + 2 * L
    tr = _tile(T, tr)

    def body(ps_ref, prev_ref, mu, w0, a0, kk_, ka_, wdu, wiu, E, ET, *outs):
        vals = _prep_fn(C, L, ps_ref[...], prev_ref[...], mu[...], w0[...], a0[...], kk_[...], ka_[...],
                        wdu[...], wiu[...], E[...], ET[...])
        for o, v in zip(outs, vals):
            o[...] = v

    pspecs = [_full_spec(a.shape) for a in params]
    return _pcall(body, name="rwkv_prep_fwd", grid=(T // tr,),
                  in_specs=[_row_spec(tr, SH), _row_spec(tr, SH)] + pspecs,
                  out_specs=[_row_spec(tr, C)] * 6,
                  out_shape=[jax.ShapeDtypeStruct((T, C), F32)] * 6,
                  compiler_params=_cp(("arbitrary",)))(p, prev, *params)


def _prep_bwd(p, prev, params, cts, C, L, tr=128):
    T = p.shape[0]
    SH = 3 * C + 2 * L
    tr = _tile(T, tr)
    nparam = 7

    def body(ps_ref, prev_ref, mu, w0, a0, kk_, ka_, wdu, wiu, E, ET, c0, c1, c2, c3, c4, c5, e0, e2, e3,
             dps_ref, dprev_ref, *dpar):
        f = functools.partial(_prep_fn, C, L)
        fe = lambda ps, prev, *par: f(ps, prev, *par, E[...], ET[...])
        _, vjp = jax.vjp(fe, ps_ref[...], prev_ref[...], mu[...], w0[...], a0[...], kk_[...], ka_[...], wdu[...], wiu[...])
        grads = vjp((c0[...] + e0[...], c1[...], c2[...] + e2[...], c3[...] + e3[...], c4[...], c5[...]))
        dps_ref[...] = grads[0]
        dprev_ref[...] = grads[1]

        @pl.when(pl.program_id(0) == 0)
        def _():
            for d in dpar:
                d[...] = jnp.zeros_like(d)

        for d, gval in zip(dpar, grads[2:]):
            d[...] += gval

    pspecs = [_full_spec(a.shape) for a in params]
    par_shapes = [a.shape for a in params[:nparam]]
    return _pcall(body, name="rwkv_prep_bwd", grid=(T // tr,),
                  in_specs=[_row_spec(tr, SH), _row_spec(tr, SH)] + pspecs + [_row_spec(tr, C)] * 9,
                  out_specs=[_row_spec(tr, SH), _row_spec(tr, SH)] + [_full_spec(s) for s in par_shapes],
                  out_shape=[jax.ShapeDtypeStruct((T, SH), F32)] * 2 + [jax.ShapeDtypeStruct(s, F32) for s in par_shapes],
                  compiler_params=_cp(("arbitrary",)))(p, prev, *params, *cts)


def _shift_combine(d_direct, d_prev_up, tr=256):
    T, W = d_direct.shape
    tr = _tile(T, tr)

    def body(a_ref, b_ref, o_ref):
        o_ref[...] = (a_ref[...] + b_ref[...]).astype(BF16)

    return _pcall(body, name="shift_combine", grid=(T // tr,),
                  in_specs=[_row_spec(tr, W)] * 2, out_specs=_row_spec(tr, W),
                  out_shape=jax.ShapeDtypeStruct((T, W), BF16), compiler_params=_cp(("arbitrary",)))(d_direct, d_prev_up)


def _mix_fn(y, r, kmod, v, g_rwkv, yfox, g_fox, ymem, g_mq, lnw, lnb, rk, E, ET):
    inv = 1.0 / HEAD
    mean = _seg(y, E) * inv
    yc = y - _seg(mean, ET)
    var = _seg(yc * yc, E) * inv
    yn = yc * _seg(lax.rsqrt(var + GN_EPS), ET) * lnw + lnb
    bonus = _seg(_seg(r * kmod * rk, E), ET) * v
    o1 = (yn + bonus) * _silu(g_rwkv)
    return jnp.concatenate([o1, yfox * _silu(g_fox), ymem * _silu(g_mq)], axis=1)


def _mix_specs(tr, C, MW):
    return [_row_spec(tr, C)] * 7 + [_row_spec(tr, MW)] * 2


def _mix_fwd(acts, params, C, MW, tr=128):
    T = acts[0].shape[0]
    D = 2 * C + MW
    tr = _tile(T, tr)

    def body(y_, r_, k_, v_, g1, yf, g2, ym, g3, lnw, lnb, rk, E, ET, o_ref):
        o_ref[...] = _mix_fn(y_[...], r_[...], k_[...], v_[...], g1[...], yf[...], g2[...], ym[...], g3[...],
                             lnw[...], lnb[...], rk[...], E[...], ET[...]).astype(BF16)

    return _pcall(body, name="mix_fwd", grid=(T // tr,),
                  in_specs=_mix_specs(tr, C, MW) + [_full_spec(a.shape) for a in params],
                  out_specs=_row_spec(tr, D), out_shape=jax.ShapeDtypeStruct((T, D), BF16),
                  compiler_params=_cp(("arbitrary",)))(*acts, *params)


def _mix_bwd(acts, params, dycat, C, MW, tr=128):
    T = acts[0].shape[0]
    D = 2 * C + MW
    tr = _tile(T, tr)

    def body(y_, r_, k_, v_, g1, yf, g2, ym, g3, lnw, lnb, rk, E, ET, dy_ref, *outs):
        fe = lambda *a: _mix_fn(*a, E[...], ET[...])
        _, vjp = jax.vjp(fe, y_[...], r_[...], k_[...], v_[...], g1[...], yf[...], g2[...], ym[...], g3[...],
                         lnw[...], lnb[...], rk[...])
        grads = vjp(dy_ref[...])
        for o, gval in zip(outs[:9], grads[:9]):
            o[...] = gval

        @pl.when(pl.program_id(0) == 0)
        def _():
            for o in outs[9:]:
                o[...] = jnp.zeros_like(o)

        for o, gval in zip(outs[9:], grads[9:]):
            o[...] += gval

    widths = [C, C, C, C, C, C, C, MW, MW]
    return _pcall(body, name="mix_bwd", grid=(T // tr,),
                  in_specs=_mix_specs(tr, C, MW) + [_full_spec(a.shape) for a in params] + [_row_spec(tr, D)],
                  out_specs=[_row_spec(tr, w) for w in widths] + [_full_spec((1, C))] * 3,
                  out_shape=[jax.ShapeDtypeStruct((T, w), F32) for w in widths] + [jax.ShapeDtypeStruct((1, C), F32)] * 3,
                  compiler_params=_cp(("arbitrary",)))(*acts, *params, dycat)


def _post(yo, x, tgt, g_post, tr=128):
    T, D = x.shape
    tr = _tile(T, tr)

    def body(yo_ref, x_ref, t_ref, g_ref, dyo_ref, dout_ref, loss_ref, dg_ref):
        n, vjp = jax.vjp(_rms, yo_ref[...], g_ref[...])
        diff = (x_ref[...] + n) - t_ref[...]
        part = 0.5 * jnp.sum(jnp.mean(diff * diff, axis=-1, keepdims=True), axis=0, keepdims=True)
        d_out = diff * (1.0 / D)
        dyo, dg = vjp(d_out)
        dyo_ref[...] = dyo.astype(BF16)
        dout_ref[...] = d_out

        @pl.when(pl.program_id(0) == 0)
        def _():
            loss_ref[...] = jnp.zeros_like(loss_ref)
            dg_ref[...] = jnp.zeros_like(dg_ref)

        loss_ref[...] += jnp.broadcast_to(part, loss_ref.shape)
        dg_ref[...] += dg

    return _pcall(body, name="post_loss", grid=(T // tr,),
                  in_specs=[_row_spec(tr, D)] * 3 + [_full_spec((1, D))],
                  out_specs=[_row_spec(tr, D), _row_spec(tr, D), _full_spec((1, LANE)), _full_spec((1, D))],
                  out_shape=[jax.ShapeDtypeStruct((T, D), BF16), jax.ShapeDtypeStruct((T, D), F32),
                             jax.ShapeDtypeStruct((1, LANE), F32), jax.ShapeDtypeStruct((1, D), F32)],
                  compiler_params=_cp(("arbitrary",)))(yo, x, tgt, g_post)


def _memattn_fn(MW, q, mkv):
    hd = MW // MEM_HEADS
    scale = hd ** -0.5
    outs = []
    for h in range(MEM_HEADS):
        qh = q[:, h * hd:(h + 1) * hd]
        kh = mkv[:, h * hd:(h + 1) * hd]
        vh = mkv[:, MW + h * hd:MW + (h + 1) * hd]
        s = _mm(qh, kh, False, True, "bf") * scale
        e = jnp.exp(s - lax.stop_gradient(jnp.max(s, axis=-1, keepdims=True)))
        pr = e / jnp.sum(e, axis=-1, keepdims=True)
        outs.append(_mm(pr, vh, False, False, "bf"))
    return jnp.concatenate(outs, axis=1)


def _memattn_fwd(p, mkv, MW, tr=256):
    T = p.shape[0]
    tr = _tile(T, tr)

    def body(q_ref, kv_ref, o_ref):
        o_ref[...] = _memattn_fn(MW, q_ref[...], kv_ref[...])

    return _pcall(body, name="memattn_fwd", grid=(T // tr,),
                  in_specs=[_row_spec(tr, MW), _full_spec(mkv.shape)], out_specs=_row_spec(tr, MW),
                  out_shape=jax.ShapeDtypeStruct((T, MW), F32), compiler_params=_cp(("arbitrary",)))(p, mkv)


def _memattn_bwd(p, mkv, do, MW, tr=256):
    T = p.shape[0]
    tr = _tile(T, tr)

    def body(q_ref, kv_ref, do_ref, dq_ref, dkv_ref):
        _, vjp = jax.vjp(functools.partial(_memattn_fn, MW), q_ref[...], kv_ref[...])
        dq, dkv = vjp(do_ref[...])
        dq_ref[...] = dq

        @pl.when(pl.program_id(0) == 0)
        def _():
            dkv_ref[...] = jnp.zeros_like(dkv_ref)

        dkv_ref[...] += dkv

    return _pcall(body, name="memattn_bwd", grid=(T // tr,),
                  in_specs=[_row_spec(tr, MW), _full_spec(mkv.shape), _row_spec(tr, MW)],
                  out_specs=[_row_spec(tr, MW), _full_spec(mkv.shape)],
                  out_shape=[jax.ShapeDtypeStruct((T, MW), F32), jax.ShapeDtypeStruct(mkv.shape, F32)],
                  compiler_params=_cp(("arbitrary",)))(p, mkv, do)


def _fox_cum_fwd(p, b_f_pad, off, blk=512):
    T = p.shape[0]
    blk = _tile(T, blk)

    def body(f_ref, b_ref, cum_ref, cumt_ref, carry):
        @pl.when(pl.program_id(0) == 0)
        def _():
            carry[...] = jnp.zeros_like(carry)

        z = f_ref[...] + b_ref[...]
        logf = -_softplus(-z)
        row = lax.broadcasted_iota(jnp.int32, (blk, blk), 0)
        col = lax.broadcasted_iota(jnp.int32, (blk, blk), 1)
        tri = (col <= row).astype(F32)
        c = _dg(tri, logf, False, False, "hi") + carry[...]
        cum_ref[...] = c
        cumt_ref[...] = c.T
        carry[...] += jnp.sum(logf, axis=0, keepdims=True)

    return _pcall(body, name="fox_cum_fwd", grid=(T // blk,),
                  in_specs=[_row_spec(blk, LANE, off // LANE), _full_spec((1, LANE))],
                  out_specs=[_row_spec(blk, LANE), pl.BlockSpec((LANE, blk), lambda i: (0, i))],
                  out_shape=[jax.ShapeDtypeStruct((T, LANE), F32), jax.ShapeDtypeStruct((LANE, T), F32)],
                  scratch_shapes=[pltpu.VMEM((1, LANE), F32)], compiler_params=_cp(("arbitrary",)))(p, b_f_pad)


def _fox_cum_bwd(p, b_f_pad, dcum, off, blk=512):
    T = p.shape[0]
    blk = _tile(T, blk)
    nb = T // blk

    def body(f_ref, b_ref, dc_ref, df_ref, db_ref, carry):
        @pl.when(pl.program_id(0) == 0)
        def _():
            carry[...] = jnp.zeros_like(carry)
            db_ref[...] = jnp.zeros_like(db_ref)

        row = lax.broadcasted_iota(jnp.int32, (blk, blk), 0)
        col = lax.broadcasted_iota(jnp.int32, (blk, blk), 1)
        tri = (col >= row).astype(F32)
        dlogf = _dg(tri, dc_ref[...], False, False, "hi") + carry[...]
        carry[...] += jnp.sum(dc_ref[...], axis=0, keepdims=True)
        z = f_ref[...] + b_ref[...]
        dz = dlogf * (1.0 - _sigmoid(z))
        df_ref[...] = dz
        db_ref[...] += jnp.sum(dz, axis=0, keepdims=True)

    rev = lambda i: (nb - 1 - i, 0)
    return _pcall(body, name="fox_cum_bwd", grid=(nb,),
                  in_specs=[pl.BlockSpec((blk, LANE), lambda i: (nb - 1 - i, off // LANE)), _full_spec((1, LANE)),
                            pl.BlockSpec((blk, LANE), rev)],
                  out_specs=[pl.BlockSpec((blk, LANE), rev), _full_spec((1, LANE))],
                  out_shape=[jax.ShapeDtypeStruct((T, LANE), F32), jax.ShapeDtypeStruct((1, LANE), F32)],
                  scratch_shapes=[pltpu.VMEM((1, LANE), F32)], compiler_params=_cp(("arbitrary",)))(p, b_f_pad, dcum)


FOX_SCALE = HEAD ** -0.5
FOX_TQ, FOX_TK = 512, 512


def _fox_scores(q, k, ck, q0=None, k0=None):
    s = _dg(q, k, False, True, "bf") - ck
    if q0 is None:
        return s
    qpos = q0 + lax.broadcasted_iota(jnp.int32, s.shape, 0)
    kpos = k0 + lax.broadcasted_iota(jnp.int32, s.shape, 1)
    return jnp.where(kpos <= qpos, s, NEG)


def _fox_c0(ck_ref, hh, pos):
    return ck_ref[0, hh:hh + 1, pl.ds(pl.multiple_of(pos, LANE), LANE)][:, 0:1]


def _fox_tiles(T):
    assert T % LANE == 0
    tq, tk = _tile(T, FOX_TQ), _tile(T, FOX_TK)
    shift = (tk // tq).bit_length() - 1
    assert tk == tq << shift
    return tq, tk, shift


def _fox_fwd(p, ck, C, offs):
    T = p.shape[0]
    tq, tk, shift = _fox_tiles(T)
    npair = C // LANE
    cb = lambda name: offs[name] // LANE

    def body(q_ref, k_ref, v_ref, ck_ref, o_ref, lse_ref):
        i = pl.program_id(1)
        nfull = i >> shift
        lse_ref[...] = jnp.zeros_like(lse_ref)
        heads = [slice(hh * HEAD, (hh + 1) * HEAD) for hh in range(2)]
        qs = [(q_ref[:, sl] * FOX_SCALE).astype(BF16) for sl in heads]
        c0s = [_fox_c0(ck_ref, hh, i * tq) for hh in range(2)]

        def step(j, carry, masked):
            off = pl.multiple_of(j * tk, tk)
            out = []
            for hh, sl in enumerate(heads):
                m, l, acc = carry[hh]
                k = k_ref[pl.ds(off, tk), sl].astype(BF16)
                v = v_ref[pl.ds(off, tk), sl].astype(BF16)
                ckv = ck_ref[0, hh:hh + 1, pl.ds(off, tk)] - c0s[hh]
                s = _fox_scores(qs[hh], k, ckv, i * tq, off) if masked else _fox_scores(qs[hh], k, ckv)
                m_new = jnp.maximum(m, jnp.max(s, axis=-1, keepdims=True))
                pr = jnp.exp(s - m_new)
                al = jnp.exp(m - m_new)
                l = al * l + jnp.sum(pr, axis=-1, keepdims=True)
                acc = al * acc + _dg(pr, v, False, False, "bf")
                out.append((m_new, l, acc))
            return tuple(out)

        one = (jnp.full((tq, 1), NEG, F32), jnp.zeros((tq, 1), F32), jnp.zeros((tq, HEAD), F32))
        carry = lax.fori_loop(0, nfull, lambda j, c: step(j, c, False), (one, one))
        for hh, (m, l, acc) in enumerate(step(nfull, carry, True)):
            o_ref[:, heads[hh]] = acc / l
            lse_ref[0, :, hh:hh + 1] = m + jnp.log(l)

    return _pcall(body, name="fox_fwd", grid=(npair, T // tq),
                  in_specs=[pl.BlockSpec((tq, LANE), lambda h, i: (i, cb("fq") + h)),
                            pl.BlockSpec((T, LANE), lambda h, i: (0, cb("fk") + h)),
                            pl.BlockSpec((T, LANE), lambda h, i: (0, cb("fv") + h)),
                            pl.BlockSpec((1, 8, T), lambda h, i: (h, 0, 0))],
                  out_specs=[pl.BlockSpec((tq, LANE), lambda h, i: (i, h)),
                             pl.BlockSpec((1, tq, 8), lambda h, i: (h, i, 0))],
                  out_shape=[jax.ShapeDtypeStruct((T, C), F32), jax.ShapeDtypeStruct((npair, T, 8), F32)],
                  compiler_params=_cp(("arbitrary", "arbitrary")))(p, p, p, ck)


def _fox_delta(p, ck, do, lse, C, offs):
    T = p.shape[0]
    tq, tk, shift = _fox_tiles(T)
    npair = C // LANE
    cb = lambda name: offs[name] // LANE

    def body(q_ref, k_ref, v_ref, ck_ref, do_ref, lse_ref, d_ref):
        i = pl.program_id(1)
        nfull = i >> shift
        d_ref[...] = jnp.zeros_like(d_ref)
        heads = [slice(hh * HEAD, (hh + 1) * HEAD) for hh in range(2)]
        qs = [(q_ref[:, sl] * FOX_SCALE).astype(BF16) for sl in heads]
        dos = [do_ref[:, sl].astype(BF16) for sl in heads]
        lses = [lse_ref[0, :, hh:hh + 1] for hh in range(2)]
        c0s = [_fox_c0(ck_ref, hh, i * tq) for hh in range(2)]

        def step(j, accs, masked):
            off = pl.multiple_of(j * tk, tk)
            out = []
            for hh, sl in enumerate(heads):
                k = k_ref[pl.ds(off, tk), sl].astype(BF16)
                v = v_ref[pl.ds(off, tk), sl].astype(BF16)
                ckv = ck_ref[0, hh:hh + 1, pl.ds(off, tk)] - c0s[hh]
                s = _fox_scores(qs[hh], k, ckv, i * tq, off) if masked else _fox_scores(qs[hh], k, ckv)
                pr = jnp.exp(s - lses[hh])
                dp = _dg(dos[hh], v, False, True, "bf")
                out.append(accs[hh] + jnp.sum(pr * dp, axis=-1, keepdims=True))
            return tuple(out)

        z = jnp.zeros((tq, 1), F32)
        accs = lax.fori_loop(0, nfull, lambda j, c: step(j, c, False), (z, z))
        for hh, acc in enumerate(step(nfull, accs, True)):
            d_ref[0, :, hh:hh + 1] = acc

    return _pcall(body, name="fox_delta", grid=(npair, T // tq),
                  in_specs=[pl.BlockSpec((tq, LANE), lambda h, i: (i, cb("fq") + h)),
                            pl.BlockSpec((T, LANE), lambda h, i: (0, cb("fk") + h)),
                            pl.BlockSpec((T, LANE), lambda h, i: (0, cb("fv") + h)),
                            pl.BlockSpec((1, 8, T), lambda h, i: (h, 0, 0)),
                            pl.BlockSpec((tq, LANE), lambda h, i: (i, h)),
                            pl.BlockSpec((1, tq, 8), lambda h, i: (h, i, 0))],
                  out_specs=pl.BlockSpec((1, tq, 8), lambda h, i: (h, i, 0)),
                  out_shape=jax.ShapeDtypeStruct((npair, T, 8), F32),
                  compiler_params=_cp(("arbitrary", "arbitrary")))(p, p, p, ck, do, lse)


def _fox_bwd(p, ck, delta, do, lse, C, offs):
    T = p.shape[0]
    tq, tk, shift = _fox_tiles(T)
    ratio = tk // tq
    nq = T // tq
    npair = C // LANE
    cb = lambda name: offs[name] // LANE

    def body(q_ref, k_ref, v_ref, ck_ref, ckall_ref, dl_ref, do_ref, lse_ref, dq_ref, dk_ref, dv_ref, dck_ref):
        j = pl.program_id(1)

        @pl.when(j == 0)
        def _():
            dq_ref[...] = jnp.zeros_like(dq_ref)

        dck_ref[...] = jnp.zeros_like(dck_ref)
        heads = [slice(hh * HEAD, (hh + 1) * HEAD) for hh in range(2)]
        ks = [k_ref[:, sl].astype(BF16) for sl in heads]
        vs = [v_ref[:, sl].astype(BF16) for sl in heads]
        cks = [ck_ref[0, hh:hh + 1, :] for hh in range(2)]

        def step(i, carry, masked):
            off = pl.multiple_of(i * tq, tq)
            out = []
            for hh, sl in enumerate(heads):
                dk, dv, dck = carry[hh]
                q = (q_ref[pl.ds(off, tq), sl] * FOX_SCALE).astype(BF16)
                dov = do_ref[pl.ds(off, tq), sl]
                lsev = lse_ref[0, pl.ds(off, tq), hh:hh + 1]
                ckv = cks[hh] - _fox_c0(ckall_ref, hh, off)
                s = _fox_scores(q, ks[hh], ckv, off, j * tk) if masked else _fox_scores(q, ks[hh], ckv)
                pr = jnp.exp(s - lsev)
                dv = dv + _dg(pr, dov, True, False, "bf")
                dp = _dg(dov, vs[hh], False, True, "bf")
                ds = pr * (dp - dl_ref[0, pl.ds(off, tq), hh:hh + 1])
                dk = dk + _dg(ds, q, True, False, "bf")
                dq_ref[pl.ds(off, tq), sl] += _dg(ds, ks[hh], False, False, "bf") * FOX_SCALE
                out.append((dk, dv, dck - jnp.sum(ds, axis=0, keepdims=True)))
            return tuple(out)

        z = jnp.zeros((tk, HEAD), F32)
        carry = ((z, z, jnp.zeros((1, tk), F32)),) * 2
        for r in range(ratio):
            carry = step(j * ratio + r, carry, True)
        carry = lax.fori_loop((j + 1) * ratio, nq, lambda i, c: step(i, c, False), carry)
        for hh, (dk, dv, dck) in enumerate(carry):
            dk_ref[:, heads[hh]] = dk
            dv_ref[:, heads[hh]] = dv
            dck_ref[0, hh:hh + 1, :] = dck

    full = lambda h, j: (0, h)
    return _pcall(body, name="fox_bwd", grid=(npair, T // tk),
                  in_specs=[pl.BlockSpec((T, LANE), lambda h, j: (0, cb("fq") + h)),
                            pl.BlockSpec((tk, LANE), lambda h, j: (j, cb("fk") + h)),
                            pl.BlockSpec((tk, LANE), lambda h, j: (j, cb("fv") + h)),
                            pl.BlockSpec((1, 8, tk), lambda h, j: (h, 0, j)),
                            pl.BlockSpec((1, 8, T), lambda h, j: (h, 0, 0)),
                            pl.BlockSpec((1, T, 8), lambda h, j: (h, 0, 0)), pl.BlockSpec((T, LANE), full),
                            pl.BlockSpec((1, T, 8), lambda h, j: (h, 0, 0))],
                  out_specs=[pl.BlockSpec((T, LANE), full),
                             pl.BlockSpec((tk, LANE), lambda h, j: (j, h)),
                             pl.BlockSpec((tk, LANE), lambda h, j: (j, h)),
                             pl.BlockSpec((1, 8, tk), lambda h, j: (h, 0, j))],
                  out_shape=[jax.ShapeDtypeStruct((T, C), F32)] * 3 + [jax.ShapeDtypeStruct((npair, 8, T), F32)],
                  compiler_params=_cp(("arbitrary", "arbitrary")))(p, p, p, ck, ck, delta, do, lse)


@jax.custom_vjp
def _solve(a, rhs):
    return _solve_fwd(a, rhs)[0]


def _solve_fwd(a, rhs):
    powers = [a]
    for _ in range(CHUNK.bit_length() - 2):
        powers.append(_dg(powers[-1], powers[-1], False, False, "x3"))
    u = rhs
    for pw in powers:
        u = u + _dg(pw, u, False, False, "x3")
    return u, (powers, u)


def _solve_bwd(res, g):
    powers, u = res
    w = g
    for pw in powers:
        w = w + _dg(pw, w, True, False, "x3")
    return _dg(w, u, False, True, "x3"), w


_solve.defvjp(_solve_fwd, _solve_bwd)


def _chunk_fn(S0, r, lw, k, v, a, b):
    nh, n = r.shape[0], r.shape[1]
    row = lax.broadcasted_iota(jnp.int32, (nh, n, n), 1)
    col = lax.broadcasted_iota(jnp.int32, (nh, n, n), 2)
    incl, strict = col <= row, col < row
    mm = lambda x, y, ta=False, tb=False: _mm(x, y, ta, tb, "x3")
    g = _mm(incl.astype(F32), lw, False, False, "hi")
    einv = jnp.exp(-g)
    rt, at, bt, kt = r * jnp.exp(g), a * jnp.exp(g - lw), b * einv, k * einv
    a_ab = jnp.where(strict, mm(at, bt, tb=True), 0.0)
    a_ak = jnp.where(strict, mm(at, kt, tb=True), 0.0)
    r_b = jnp.where(incl, mm(rt, bt, tb=True), 0.0)
    r_k = jnp.where(incl, mm(rt, kt, tb=True), 0.0)
    u = _solve(a_ab, mm(at, S0, tb=True) + mm(a_ak, v))
    y = mm(rt, S0, tb=True) + mm(r_b, u) + mm(r_k, v)
    g_end = jnp.sum(lw, axis=1, keepdims=True)
    s_end = (S0 + mm(u, bt, ta=True) + mm(v, kt, ta=True)) * jnp.exp(g_end)
    return y, s_end


SCAN_HEADS = 24
SCAN_ROWS = 64


def _scan_group(C):
    nh = SCAN_HEADS
    while C % (nh * HEAD):
        nh -= 2
    return nh, nh * HEAD


def _scan_fwd(r, lw, k, v, a, b):
    T, C = r.shape
    tc = _tile(T, SCAN_ROWS)
    ncs = tc // CHUNK
    nh, GW = _scan_group(C)
    ngroup = C // GW

    def body(r_ref, lw_ref, k_ref, v_ref, a_ref, b_ref, y_ref, ck_ref, state):
        @pl.when(pl.program_id(1) == 0)
        def _():
            state[...] = jnp.zeros_like(state)

        heads = [slice(hh * HEAD, (hh + 1) * HEAD) for hh in range(nh)]
        split = lambda ref, rows: jnp.stack([ref[rows, sl] for sl in heads])
        st = split(state, slice(None))
        for c in range(ncs):
            rows = slice(c * CHUNK, (c + 1) * CHUNK)
            for hh, sl in enumerate(heads):
                ck_ref[0, c, :, sl] = st[hh]
            y, st = _chunk_fn(st, *[split(ref, rows) for ref in (r_ref, lw_ref, k_ref, v_ref, a_ref, b_ref)])
            for hh, sl in enumerate(heads):
                y_ref[rows, sl] = y[hh]
        for hh, sl in enumerate(heads):
            state[:, sl] = st[hh]

    spec = pl.BlockSpec((tc, GW), lambda h, t: (t, h))
    return _pcall(body, name="rwkv_scan_fwd", grid=(ngroup, T // tc),
                  in_specs=[spec] * 6,
                  out_specs=[spec, pl.BlockSpec((1, ncs, HEAD, GW), lambda h, t: (h, t, 0, 0))],
                  out_shape=[jax.ShapeDtypeStruct((T, C), F32),
                             jax.ShapeDtypeStruct((ngroup, T // CHUNK, HEAD, GW), F32)],
                  scratch_shapes=[pltpu.VMEM((HEAD, GW), F32)],
                  compiler_params=_cp(("arbitrary", "arbitrary")))(r, lw, k, v, a, b)


def _scan_bwd(r, lw, k, v, a, b, ckpt, dy):
    T, C = r.shape
    tc = _tile(T, SCAN_ROWS)
    ncs = tc // CHUNK
    nh, GW = _scan_group(C)
    ngroup = C // GW
    nt = T // tc

    def body(r_ref, lw_ref, k_ref, v_ref, a_ref, b_ref, ck_ref, dy_ref, dr, dlw, dk, dv, da, db, dstate):
        @pl.when(pl.program_id(1) == 0)
        def _():
            dstate[...] = jnp.zeros_like(dstate)

        outs = (dr, dlw, dk, dv, da, db)
        heads = [slice(hh * HEAD, (hh + 1) * HEAD) for hh in range(nh)]
        split = lambda ref, rows: jnp.stack([ref[rows, sl] for sl in heads])
        dst = split(dstate, slice(None))
        for c in reversed(range(ncs)):
            rows = slice(c * CHUNK, (c + 1) * CHUNK)
            s0 = jnp.stack([ck_ref[0, c, :, sl] for sl in heads])
            _, vjp = jax.vjp(_chunk_fn, s0, *[split(ref, rows) for ref in (r_ref, lw_ref, k_ref, v_ref, a_ref, b_ref)])
            grads = vjp((split(dy_ref, rows), dst))
            dst = grads[0]
            for o, gval in zip(outs, grads[1:]):
                for hh, sl in enumerate(heads):
                    o[rows, sl] = gval[hh]
        for hh, sl in enumerate(heads):
            dstate[:, sl] = dst[hh]

    spec = pl.BlockSpec((tc, GW), lambda h, t: (nt - 1 - t, h))
    return _pcall(body, name="rwkv_scan_bwd", grid=(ngroup, nt),
                  in_specs=[spec] * 6 + [pl.BlockSpec((1, ncs, HEAD, GW), lambda h, t: (h, nt - 1 - t, 0, 0)), spec],
                  out_specs=[spec] * 6, out_shape=[jax.ShapeDtypeStruct((T, C), F32)] * 6,
                  scratch_shapes=[pltpu.VMEM((HEAD, GW), F32)],
                  compiler_params=_cp(("arbitrary", "arbitrary")))(r, lw, k, v, a, b, ckpt, dy)


def _adam(w, g, m, v):
    m = ADAM_B1 * m + (1.0 - ADAM_B1) * g
    v = ADAM_B2 * v + (1.0 - ADAM_B2) * (g * g)
    m_hat = m / (1.0 - ADAM_B1 ** ADAM_STEP)
    v_hat = v / (1.0 - ADAM_B2 ** ADAM_STEP)
    return -ADAM_LR * (m_hat / (jnp.sqrt(v_hat) + ADAM_EPS) + ADAM_WD * w), m, v


def _sum_adam(parts, w, m, v, name):
    n, R, W = parts.shape
    tr = _tile(R, max(8, min(256, (1 << 20) // (n * W))))

    def body(p_ref, w_ref, m_ref, v_ref, g_ref, d_ref, nm_ref, nv_ref):
        g = p_ref[0].astype(F32)
        for s in range(1, n):
            g = g + p_ref[s].astype(F32)
        d, nm, nv = _adam(w_ref[...], g, m_ref[...], v_ref[...])
        g_ref[...] = g
        d_ref[...] = d
        nm_ref[...] = nm
        nv_ref[...] = nv

    return _pcall(body, name=name, grid=(R // tr,),
                  in_specs=[pl.BlockSpec((n, tr, W), lambda i: (0, i, 0))] + [_row_spec(tr, W)] * 3,
                  out_specs=[_row_spec(tr, W)] * 4, out_shape=[jax.ShapeDtypeStruct((R, W), F32)] * 4,
                  compiler_params=_cp(("arbitrary",)))(parts, w, m, v)


def _pad_lanes(vec, width):
    return jnp.pad(vec, ((0, 0), (0, width - vec.shape[1])))


def _logical_cols(blocks, lo, hi):
    B, out = blocks.shape[2], []
    while lo < hi:
        j, o = divmod(lo, B)
        n = min(hi - lo, B - o)
        out.append(blocks[j, :, o:o + n])
        lo += n
    return out


def kernel(x, mem, g_pre, w_in, mu_rwkv, w0, w_decay_up, a0, w_iclr_up, k_k, k_a, r_k, ln_x_w, ln_x_b, b_f, g_mem, w_mem_kv, w_out, g_post, loss_target, m_g_pre, m_w_in, m_mu_rwkv, m_w0, m_w_decay_up, m_a0, m_w_iclr_up, m_k_k, m_k_a, m_r_k, m_ln_x_w, m_ln_x_b, m_b_f, m_g_mem, m_w_mem_kv, m_w_out, m_g_post, v_g_pre, v_w_in, v_mu_rwkv, v_w0, v_w_decay_up, v_a0, v_w_iclr_up, v_k_k, v_k_a, v_r_k, v_ln_x_w, v_ln_x_b, v_b_f, v_g_mem, v_w_mem_kv, v_w_out, v_g_post):
    T, D = x.shape[1], x.shape[2]
    C = w0.shape[1]
    L = w_decay_up.shape[1]
    H = C // HEAD
    MW = w_mem_kv.shape[2] // 2
    SH = 3 * C + 2 * L
    IN = NDEV * w_in.shape[2]
    assert IN == SH + 5 * C + H + 2 * MW and D == 2 * C + MW and H % 2 == 0 and H <= LANE
    assert C % LANE == 0 and L % LANE == 0 and MW % (MEM_HEADS * HEAD) == 0 and T % CHUNK == 0
    offs = dict(grw=SH, fq=SH + C, fk=SH + 2 * C, fv=SH + 3 * C, gfx=SH + 4 * C, mq=SH + 5 * C, gmq=SH + 5 * C + MW,
                fl=SH + 5 * C + 2 * MW)
    NI = -(-(offs["fl"] + LANE) // 1024) * 1024
    l_fl = SH + 4 * C

    x2, mem2, tgt2 = x[0], mem[0], loss_target[0]

    wg = _exchange(w_in[0].astype(BF16), "gather_w_in", True)
    w_perm = jnp.concatenate(_logical_cols(wg, 0, l_fl) + _logical_cols(wg, l_fl + H, IN)
                             + _logical_cols(wg, l_fl, l_fl + H) + [jnp.zeros((D, NI - IN), BF16)], axis=1)

    h = _rmsnorm_fwd(x2, g_pre, "rmsnorm_pre")
    p, w_out_f, w_kv_f, lora = _matmul(
        h, w_perm, False, False, "in_proj",
        riders=[(w_out[0].astype(BF16), True), (w_mem_kv[0].astype(BF16), True),
                (jnp.concatenate([w_decay_up[0], w_iclr_up[0]], axis=0), True)])
    w_out_f, w_kv_f = w_out_f.reshape(D, D), w_kv_f.reshape(D, 2 * MW)
    lora = jnp.transpose(lora, (1, 0, 2)).reshape(2 * L, C)
    wdu_f, wiu_f = lora[:L], lora[L:]

    E, ET = _head_indicator(C, LANE)
    prep_params = [mu_rwkv, w0, a0, k_k, k_a, wdu_f, wiu_f, E, ET]
    mix_params = [ln_x_w, ln_x_b, r_k.reshape(1, C), E, ET]
    b_f_pad = _pad_lanes(b_f, LANE)
    ps = p[:, :SH]
    col = lambda name, w: p[:, offs[name]:offs[name] + w]
    g_rwkv, g_fox, mq, g_mq = col("grw", C), col("gfx", C), col("mq", MW), col("gmq", MW)
    prev = jnp.concatenate([jnp.zeros((1, SH), F32), ps[:-1]], axis=0)
    r, lw, kmod, v, a, b = _prep_fwd(p, prev, prep_params, C, L)
    y_scan, ckpt = _scan_fwd(r, lw, kmod, v, a, b)

    cum, cum_t = _fox_cum_fwd(p, b_f_pad, offs["fl"])
    ck = jnp.pad(cum_t[:H].reshape(H // 2, 2, T), ((0, 0), (0, 6), (0, 0)))
    y_fox, lse = _fox_fwd(p, ck, C, offs)

    memn = _rmsnorm_fwd(mem2, g_mem, "rmsnorm_mem")
    mkv = _matmul(memn, w_kv_f, False, False, "mem_kv_proj")
    y_mem = _memattn_fwd(mq, mkv, MW)

    acts = [y_scan, r, kmod, v, g_rwkv, y_fox, g_fox, y_mem, g_mq]
    ycat = _mix_fwd(acts, mix_params, C, MW)
    yo = _matmul(ycat, w_out_f, False, False, "out_proj")
    d_yo, d_out, loss_part, dg_post = _post(yo, x2, tgt2, g_post)
    loss = lax.psum(loss_part[0, 0], AXES)

    g_w_out = _matmul(ycat, d_yo, True, False, "grad_w_out", BF16)
    d_ycat = _matmul(d_yo, w_out_f, False, True, "d_ycat")
    (d_y, d_r1, d_k1, d_v1, d_grw, d_yfox, d_gfx, d_ymem, d_gmq, dg_lnw, dg_lnb, dg_rk) = _mix_bwd(
        acts, mix_params, d_ycat, C, MW)

    d_mq, d_mkv = _memattn_bwd(mq, mkv, d_ymem, MW)
    g_w_kv = _matmul(memn, d_mkv, True, False, "grad_w_mem_kv", BF16)
    d_memn = _matmul(d_mkv, w_kv_f, False, True, "d_memn")
    _, dg_mem = _rmsnorm_bwd(mem2, g_mem, d_memn, None, "rmsnorm_mem_bwd")

    delta = _fox_delta(p, ck, d_yfox, lse, C, offs)
    d_fq, d_fk, d_fv, d_ck = _fox_bwd(p, ck, delta, d_yfox, lse, C, offs)
    d_cum = _pad_lanes(d_ck[:, :2, :].reshape(H, T).T, LANE)
    d_fl, dg_bf = _fox_cum_bwd(p, b_f_pad, d_cum, offs["fl"])

    d_r, d_lw, d_k, d_v, d_a, d_b = _scan_bwd(r, lw, kmod, v, a, b, ckpt, d_y)
    cts = [d_r, d_lw, d_k, d_v, d_a, d_b, d_r1, d_k1, d_v1]
    (d_ps, d_prev, dg_mu, dg_w0, dg_a0, dg_kk, dg_ka, dg_wdu, dg_wiu) = _prep_bwd(p, prev, prep_params, cts, C, L)
    d_prev_up = jnp.concatenate([d_prev[1:], jnp.zeros((1, SH), F32)], axis=0)
    d_sh = _shift_combine(d_ps, d_prev_up)

    tobf = lambda z: z.astype(BF16)
    dp = jnp.concatenate([d_sh, tobf(d_grw), tobf(d_fq), tobf(d_fk), tobf(d_fv), tobf(d_gfx), tobf(d_mq), tobf(d_gmq),
                          tobf(d_fl), jnp.zeros((T, NI - offs["fl"] - LANE), BF16)], axis=1)
    g_lora = jnp.concatenate([dg_wdu, dg_wiu], axis=0)
    g_w_perm, parts_out, parts_kv, parts_lora = _matmul(
        h, dp, True, False, "grad_w_in", BF16,
        riders=[(g_w_out.reshape(NDEV, D // NDEV, D), False), (g_w_kv.reshape(NDEV, D // NDEV, 2 * MW), False),
                (jnp.transpose(g_lora.reshape(2 * L, NDEV, C // NDEV), (1, 0, 2)), False)])

    def internal_cols(lo, hi):
        out = []
        for first, last, shift in ((0, l_fl, 0), (l_fl, l_fl + H, offs["fl"] - l_fl), (l_fl + H, IN, -H)):
            s0, s1 = max(lo, first), min(hi, last)
            if s0 < s1:
                out.append(g_w_perm[:, s0 + shift:s1 + shift])
        return out

    blk = IN // NDEV
    g_blocks = jnp.stack([jnp.concatenate(internal_cols(j * blk, (j + 1) * blk), axis=1) for j in range(NDEV)])
    d_h, parts_in = _matmul(dp, w_perm, False, True, "d_h", tk=5120, riders=[(g_blocks, False)])
    grad_x, dg_pre = _rmsnorm_bwd(x2, g_pre, d_h, d_out, "rmsnorm_pre_bwd")

    gw_in, dw_in, nm_w_in, nv_w_in = _sum_adam(parts_in, w_in[0], m_w_in[0], v_w_in[0], "adam_w_in")
    gw_out, dw_out, nm_w_out, nv_w_out = _sum_adam(parts_out, w_out[0], m_w_out[0], v_w_out[0], "adam_w_out")
    gw_kv, dw_kv, nm_w_kv, nv_w_kv = _sum_adam(parts_kv, w_mem_kv[0], m_w_mem_kv[0], v_w_mem_kv[0], "adam_w_mem_kv")
    cat2 = lambda u, w_: jnp.concatenate([u[0], w_[0]], axis=0)
    lora_res = _sum_adam(parts_lora, cat2(w_decay_up, w_iclr_up), cat2(m_w_decay_up, m_w_iclr_up),
                         cat2(v_w_decay_up, v_w_iclr_up), "adam_lora")

    small = [("g_pre", g_pre, m_g_pre, v_g_pre, dg_pre), ("mu_rwkv", mu_rwkv, m_mu_rwkv, v_mu_rwkv, dg_mu),
             ("w0", w0, m_w0, v_w0, dg_w0), ("a0", a0, m_a0, v_a0, dg_a0), ("k_k", k_k, m_k_k, v_k_k, dg_kk),
             ("k_a", k_a, m_k_a, v_k_a, dg_ka), ("r_k", r_k.reshape(1, C), m_r_k.reshape(1, C), v_r_k.reshape(1, C), dg_rk),
             ("ln_x_w", ln_x_w, m_ln_x_w, v_ln_x_w, dg_lnw), ("ln_x_b", ln_x_b, m_ln_x_b, v_ln_x_b, dg_lnb),
             ("b_f", _pad_lanes(b_f, LANE), _pad_lanes(m_b_f, LANE), _pad_lanes(v_b_f, LANE), dg_bf),
             ("g_mem", g_mem, m_g_mem, v_g_mem, dg_mem), ("g_post", g_post, m_g_post, v_g_post, dg_post)]
    widths = [s[1].shape[1] for s in small]
    pack = lambda idx: jnp.concatenate([s[idx] for s in small], axis=1).reshape(-1, LANE)
    parts_small = _exchange(pack(4), "gather_small_grads", True)
    res_small = _sum_adam(parts_small, pack(1), pack(2), pack(3), "adam_small")

    def unpack(flat):
        flat = flat.reshape(1, -1)
        out, o = {}, 0
        for (name, *_), wd in zip(small, widths):
            out[name] = flat[:, o:o + wd]
            o += wd
        out["b_f"] = out["b_f"][:, :H]
        out["r_k"] = out["r_k"].reshape(1, H, HEAD)
        return out

    sg, sd, sm, sv = [unpack(z) for z in res_small]
    big = {"w_in": (gw_in, dw_in, nm_w_in, nv_w_in), "w_out": (gw_out, dw_out, nm_w_out, nv_w_out),
           "w_mem_kv": (gw_kv, dw_kv, nm_w_kv, nv_w_kv),
           "w_decay_up": tuple(z[:L] for z in lora_res), "w_iclr_up": tuple(z[L:] for z in lora_res)}
    order = ["g_pre", "w_in", "mu_rwkv", "w0", "w_decay_up", "a0", "w_iclr_up", "k_k", "k_a", "r_k", "ln_x_w", "ln_x_b",
             "b_f", "g_mem", "w_mem_kv", "w_out", "g_post"]

    def pick(name, idx):
        if name in big:
            return big[name][idx][None]
        return (sg, sd, sm, sv)[idx][name]

    outs = [loss, grad_x[None]]
    for idx in range(4):
        outs += [pick(n, idx) for n in order]
    return tuple(outs)
```

```python
import functools

import jax
import jax.numpy as jnp
from jax import lax
from jax.experimental import pallas as pl
from jax.experimental.pallas import tpu as pltpu

F32, BF16 = jnp.float32, jnp.bfloat16
HI = lax.Precision.HIGHEST
NDEV = 8
AXES = ("x", "y", "c")
HEAD = 64
CHUNK = 64
MEM_HEADS = 4
LANE = 128
RMS_EPS = 1e-6
GN_EPS = 64e-5
NEG = -1e30
ADAM_LR, ADAM_B1, ADAM_B2, ADAM_EPS, ADAM_WD, ADAM_STEP = 0.001, 0.9, 0.999, 1e-08, 0.01, 10
VMEM_LIMIT = 56 * 1024 * 1024


def _pcall(body, **kw):
    return pl.pallas_call(body, **kw)


def _cp(sem=None, vmem=VMEM_LIMIT):
    return pltpu.CompilerParams(dimension_semantics=sem, vmem_limit_bytes=vmem)


def _tile(n, pref):
    for t in (pref, 1024, 512, 256, 128, 64, 32, 16, 8):
        if t <= pref and n % t == 0:
            return t
    return n


def _dg(a, b, ta, tb, mode):
    nb = a.ndim - 2
    ca = nb + (0 if ta else 1)
    cb = nb + (1 if tb else 0)
    dims = (((ca,), (cb,)), (tuple(range(nb)), tuple(range(nb))))
    if mode == "x3":
        a_hi, b_hi = a.astype(BF16), b.astype(BF16)
        a_lo, b_lo = (a - a_hi.astype(F32)).astype(BF16), (b - b_hi.astype(F32)).astype(BF16)
        a3 = jnp.concatenate([a_hi, a_lo, a_hi], axis=ca)
        b3 = jnp.concatenate([b_hi, b_hi, b_lo], axis=cb)
        return lax.dot_general(a3, b3, dims, preferred_element_type=F32)
    if mode == "x2":
        a_hi, b = a.astype(BF16), b.astype(BF16)
        a_lo = (a - a_hi.astype(F32)).astype(BF16)
        dot = lambda u: lax.dot_general(u, b, dims, preferred_element_type=F32)
        return dot(a_hi) + dot(a_lo)
    if mode == "bf":
        a, b, prec = a.astype(BF16), b.astype(BF16), None
    else:
        prec = HI
    return lax.dot_general(a, b, dims, preferred_element_type=F32, precision=prec)


@functools.partial(jax.custom_vjp, nondiff_argnums=(2, 3, 4))
def _mm(a, b, ta, tb, mode):
    return _dg(a, b, ta, tb, mode)


def _mm_fwd(a, b, ta, tb, mode):
    return _dg(a, b, ta, tb, mode), (a, b)


def _mm_bwd(ta, tb, mode, res, g):
    a, b = res
    da = _dg(g, b, False, not tb, mode) if not ta else _dg(b, g, tb, True, mode)
    db = _dg(a, g, not ta, False, mode) if not tb else _dg(g, a, True, ta, mode)
    return da, db


_mm.defvjp(_mm_fwd, _mm_bwd)


@jax.custom_vjp
def _seg(a, e):
    return _dg(a, e, False, False, "x2")


def _seg_fwd(a, e):
    return _dg(a, e, False, False, "x2"), e


def _seg_bwd(e, g):
    return _dg(g, e, False, True, "x2"), jnp.zeros_like(e)


_seg.defvjp(_seg_fwd, _seg_bwd)


def _sigmoid(z):
    return 1.0 / (1.0 + jnp.exp(-z))


def _softplus(z):
    return jnp.maximum(z, 0.0) + jnp.log(1.0 + jnp.exp(-jnp.abs(z)))


def _silu(z):
    return z * _sigmoid(z)


def _rms(x, g):
    return x * lax.rsqrt(jnp.mean(x * x, axis=-1, keepdims=True) + RMS_EPS) * g


HBM_SPEC = pl.BlockSpec(memory_space=pltpu.HBM)
EXCHANGE_SEMS = [pltpu.SemaphoreType.DMA((NDEV - 1,)), pltpu.SemaphoreType.DMA((NDEV - 1,)), pltpu.SemaphoreType.DMA(())]


def _exchange_copies(gather, x_ref, o_ref, send_sems, recv_sems, local_sem, arrivals):
    ix, iy, ic = lax.axis_index("x"), lax.axis_index("y"), lax.axis_index("c")
    me = 4 * ix + 2 * iy + ic

    def src(dest):
        return x_ref if gather else x_ref.at[dest]

    mine = pltpu.make_async_copy(src(me), o_ref.at[me], local_sem)
    pairs = []
    for k in range(1, NDEV):
        px = 1 - ix if (k >> 2) & 1 else ix
        py = 1 - iy if (k >> 1) & 1 else iy
        pc = 1 - ic if k & 1 else ic
        peer = 4 * px + 2 * py + pc
        send = pltpu.make_async_remote_copy(
            src_ref=src(peer), dst_ref=o_ref.at[me], send_sem=send_sems.at[k - 1], recv_sem=recv_sems.at[k - 1],
            device_id=(px, py, pc), device_id_type=pl.DeviceIdType.MESH)
        arrival = arrivals and pltpu.make_async_remote_copy(
            src_ref=src(peer), dst_ref=o_ref.at[peer], send_sem=send_sems.at[k - 1], recv_sem=recv_sems.at[k - 1],
            device_id=(ix, iy, ic), device_id_type=pl.DeviceIdType.MESH)
        pairs.append((send, arrival))
    return mine, pairs


def _exchange_start(*args):
    mine, pairs = _exchange_copies(*args, arrivals=False)
    mine.start()
    for send, _ in pairs:
        send.start()


def _exchange_wait(*args):
    mine, pairs = _exchange_copies(*args, arrivals=True)
    for send, arrival in pairs:
        send.wait_send()
        arrival.wait_recv()
    mine.wait()


def _exchange_shape(x, gather):
    return jax.ShapeDtypeStruct((NDEV,) + tuple(x.shape if gather else x.shape[1:]), x.dtype)


def _exchange(x, name, gather):
    def body(x_ref, o_ref, *sems):
        _exchange_start(gather, x_ref, o_ref, *sems)
        _exchange_wait(gather, x_ref, o_ref, *sems)

    return _pcall(body, name=name, out_shape=_exchange_shape(x, gather), in_specs=[HBM_SPEC], out_specs=HBM_SPEC,
                  scratch_shapes=list(EXCHANGE_SEMS))(x)


MAX_FULL_K = 4096


def _matmul(a, b, ta, tb, name, out_dtype=F32, tm=1024, tn=1024, tk=3072, riders=()):
    M, K = (a.shape[1], a.shape[0]) if ta else a.shape
    N = b.shape[0] if tb else b.shape[1]
    assert (b.shape[1] if tb else b.shape[0]) == K
    if K <= MAX_FULL_K:
        tk = K
    else:
        tm, tk = min(tm, 512), _tile(K, tk)
    tm, tn = _tile(M, tm), _tile(N, tn)
    grid = (M // tm, N // tn, K // tk)
    nk, nr = grid[2], len(riders)

    def body(*refs):
        a_ref, b_ref, x_refs = refs[0], refs[1], refs[2:2 + nr]
        o_ref, xo_refs, rest = refs[2 + nr], refs[3 + nr:3 + 2 * nr], refs[3 + 2 * nr:]
        sems = rest[1:] if nk > 1 else rest
        ids = [pl.program_id(d) for d in range(3)]
        jobs = [(riders[q][1], x_refs[q], xo_refs[q]) + tuple(sems[3 * q:3 * q + 3]) for q in range(nr)]

        if nr:
            @pl.when((ids[0] == 0) & (ids[1] == 0) & (ids[2] == 0))
            def _():
                for job in jobs:
                    _exchange_start(*job)

        if nk == 1:
            o_ref[...] = _dg(a_ref[...], b_ref[...], ta, tb, "bf").astype(o_ref.dtype)
        else:
            acc = rest[0]

            @pl.when(ids[2] == 0)
            def _():
                acc[...] = jnp.zeros_like(acc)

            acc[...] += _dg(a_ref[...], b_ref[...], ta, tb, "bf")

            @pl.when(ids[2] == nk - 1)
            def _():
                o_ref[...] = acc[...].astype(o_ref.dtype)

        if nr:
            @pl.when((ids[0] == grid[0] - 1) & (ids[1] == grid[1] - 1) & (ids[2] == nk - 1))
            def _():
                for job in jobs:
                    _exchange_wait(*job)

    a_spec = pl.BlockSpec((tk, tm), lambda i, j, k: (k, i)) if ta else pl.BlockSpec((tm, tk), lambda i, j, k: (i, k))
    b_spec = pl.BlockSpec((tn, tk), lambda i, j, k: (j, k)) if tb else pl.BlockSpec((tk, tn), lambda i, j, k: (k, j))
    out = _pcall(
        body, name=name, grid=grid,
        in_specs=[a_spec, b_spec] + [HBM_SPEC] * nr,
        out_specs=[pl.BlockSpec((tm, tn), lambda i, j, k: (i, j))] + [HBM_SPEC] * nr,
        out_shape=[jax.ShapeDtypeStruct((M, N), out_dtype)] + [_exchange_shape(x, g) for x, g in riders],
        scratch_shapes=([pltpu.VMEM((tm, tn), F32)] if nk > 1 else []) + list(EXCHANGE_SEMS) * nr,
        compiler_params=_cp(("arbitrary", "arbitrary", "arbitrary")),
    )(a, b, *[x for x, _ in riders])
    return tuple(out) if nr else out[0]


def _row_spec(tr, width, col_block=0):
    return pl.BlockSpec((tr, width), lambda i: (i, col_block))


def _full_spec(shape):
    nd = len(shape)
    return pl.BlockSpec(tuple(shape), lambda i: (0,) * nd)


def _rmsnorm_fwd(x, g, name, tr=256):
    R, D = x.shape
    tr = _tile(R, tr)

    def body(x_ref, g_ref, o_ref):
        o_ref[...] = _rms(x_ref[...], g_ref[...]).astype(BF16)

    return _pcall(body, name=name, grid=(R // tr,),
                  in_specs=[_row_spec(tr, D), _full_spec((1, D))], out_specs=_row_spec(tr, D),
                  out_shape=jax.ShapeDtypeStruct((R, D), BF16), compiler_params=_cp(("arbitrary",)))(x, g)


def _rmsnorm_bwd(x, g, dy, extra, name, tr=128):
    R, D = x.shape
    tr = _tile(R, tr)
    has_extra = extra is not None

    def body(*refs):
        if has_extra:
            x_ref, g_ref, dy_ref, e_ref, dx_ref, dg_ref = refs
        else:
            x_ref, g_ref, dy_ref, dx_ref, dg_ref = refs
        _, vjp = jax.vjp(_rms, x_ref[...], g_ref[...])
        dx, dg = vjp(dy_ref[...])
        dx_ref[...] = dx + e_ref[...] if has_extra else dx

        @pl.when(pl.program_id(0) == 0)
        def _():
            dg_ref[...] = jnp.zeros_like(dg_ref)

        dg_ref[...] += dg

    ins = [x, g, dy] + ([extra] if has_extra else [])
    specs = [_row_spec(tr, D), _full_spec((1, D)), _row_spec(tr, D)] + ([_row_spec(tr, D)] if has_extra else [])
    return _pcall(body, name=name, grid=(R // tr,), in_specs=specs,
                  out_specs=[_row_spec(tr, D), _full_spec((1, D))],
                  out_shape=[jax.ShapeDtypeStruct((R, D), F32), jax.ShapeDtypeStruct((1, D), F32)],
                  compiler_params=_cp(("arbitrary",)))(*ins)


def _head_indicator(C, hp):
    e = (jnp.arange(C)[:, None] // HEAD == jnp.arange(hp)[None, :]).astype(F32)
    return e, e.T


def _prep_fn(C, L, ps, prev, mu, w0, a0, k_k, k_a, wdu, wiu, E, ET):
    sh = ps + (prev - ps) * mu
    r, k, v = sh[:, :C], sh[:, C:2 * C], sh[:, 2 * C:3 * C]
    wl, al = sh[:, 3 * C:3 * C + L], sh[:, 3 * C + L:3 * C + 2 * L]
    wd = w0 + _mm(jnp.tanh(wl), wdu, False, False, "bf")
    w_pre = -_softplus(-wd) - 0.5
    lw = -jnp.exp(w_pre)
    alpha = _sigmoid(a0 + _mm(al, wiu, False, False, "bf"))
    kk = k * k_k
    ss = _seg(kk * kk, E)
    kk = kk * _seg(lax.rsqrt(jnp.maximum(ss, 1e-24)), ET)
    k_mod = k * (1.0 + (alpha - 1.0) * k_a)
    return r, lw, k_mod, v, -kk, kk * alpha


def _prep_fwd(p, prev, params, C, L, tr=128):
    T = p.shape[0]
    SH = 3 * C + 2 * L
    tr = _tile(T, tr)

    def body(ps_ref, prev_ref, mu, w0, a0, kk_, ka_, wdu, wiu, E, ET, *outs):
        vals = _prep_fn(C, L, ps_ref[...], prev_ref[...], mu[...], w0[...], a0[...], kk_[...], ka_[...],
                        wdu[...], wiu[...], E[...], ET[...])
        for o, v in zip(outs, vals):
            o[...] = v

    pspecs = [_full_spec(a.shape) for a in params]
    return _pcall(body, name="rwkv_prep_fwd", grid=(T // tr,),
                  in_specs=[_row_spec(tr, SH), _row_spec(tr, SH)] + pspecs,
                  out_specs=[_row_spec(tr, C)] * 6,
                  out_shape=[jax.ShapeDtypeStruct((T, C), F32)] * 6,
                  compiler_params=_cp(("arbitrary",)))(p, prev, *params)


def _prep_bwd(p, prev, params, cts, C, L, tr=128):
    T = p.shape[0]
    SH = 3 * C + 2 * L
    tr = _tile(T, tr)
    nparam = 7

    def body(ps_ref, prev_ref, mu, w0, a0, kk_, ka_, wdu, wiu, E, ET, c0, c1, c2, c3, c4, c5, e0, e2, e3,
             dps_ref, dprev_ref, *dpar):
        f = functools.partial(_prep_fn, C, L)
        fe = lambda ps, prev, *par: f(ps, prev, *par, E[...], ET[...])
        _, vjp = jax.vjp(fe, ps_ref[...], prev_ref[...], mu[...], w0[...], a0[...], kk_[...], ka_[...], wdu[...], wiu[...])
        grads = vjp((c0[...] + e0[...], c1[...], c2[...] + e2[...], c3[...] + e3[...], c4[...], c5[...]))
        dps_ref[...] = grads[0]
        dprev_ref[...] = grads[1]

        @pl.when(pl.program_id(0) == 0)
        def _():
            for d in dpar:
                d[...] = jnp.zeros_like(d)

        for d, gval in zip(dpar, grads[2:]):
            d[...] += gval

    pspecs = [_full_spec(a.shape) for a in params]
    par_shapes = [a.shape for a in params[:nparam]]
    return _pcall(body, name="rwkv_prep_bwd", grid=(T // tr,),
                  in_specs=[_row_spec(tr, SH), _row_spec(tr, SH)] + pspecs + [_row_spec(tr, C)] * 9,
                  out_specs=[_row_spec(tr, SH), _row_spec(tr, SH)] + [_full_spec(s) for s in par_shapes],
                  out_shape=[jax.ShapeDtypeStruct((T, SH), F32)] * 2 + [jax.ShapeDtypeStruct(s, F32) for s in par_shapes],
                  compiler_params=_cp(("arbitrary",)))(p, prev, *params, *cts)


def _shift_combine(d_direct, d_prev_up, tr=256):
    T, W = d_direct.shape
    tr = _tile(T, tr)

    def body(a_ref, b_ref, o_ref):
        o_ref[...] = (a_ref[...] + b_ref[...]).astype(BF16)

    return _pcall(body, name="shift_combine", grid=(T // tr,),
                  in_specs=[_row_spec(tr, W)] * 2, out_specs=_row_spec(tr, W),
                  out_shape=jax.ShapeDtypeStruct((T, W), BF16), compiler_params=_cp(("arbitrary",)))(d_direct, d_prev_up)


def _mix_fn(y, r, kmod, v, g_rwkv, yfox, g_fox, ymem, g_mq, lnw, lnb, rk, E, ET):
    inv = 1.0 / HEAD
    mean = _seg(y, E) * inv
    yc = y - _seg(mean, ET)
    var = _seg(yc * yc, E) * inv
    yn = yc * _seg(lax.rsqrt(var + GN_EPS), ET) * lnw + lnb
    bonus = _seg(_seg(r * kmod * rk, E), ET) * v
    o1 = (yn + bonus) * _silu(g_rwkv)
    return jnp.concatenate([o1, yfox * _silu(g_fox), ymem * _silu(g_mq)], axis=1)


def _mix_specs(tr, C, MW):
    return [_row_spec(tr, C)] * 7 + [_row_spec(tr, MW)] * 2


def _mix_fwd(acts, params, C, MW, tr=128):
    T = acts[0].shape[0]
    D = 2 * C + MW
    tr = _tile(T, tr)

    def body(y_, r_, k_, v_, g1, yf, g2, ym, g3, lnw, lnb, rk, E, ET, o_ref):
        o_ref[...] = _mix_fn(y_[...], r_[...], k_[...], v_[...], g1[...], yf[...], g2[...], ym[...], g3[...],
                             lnw[...], lnb[...], rk[...], E[...], ET[...]).astype(BF16)

    return _pcall(body, name="mix_fwd", grid=(T // tr,),
                  in_specs=_mix_specs(tr, C, MW) + [_full_spec(a.shape) for a in params],
                  out_specs=_row_spec(tr, D), out_shape=jax.ShapeDtypeStruct((T, D), BF16),
                  compiler_params=_cp(("arbitrary",)))(*acts, *params)


def _mix_bwd(acts, params, dycat, C, MW, tr=128):
    T = acts[0].shape[0]
    D = 2 * C + MW
    tr = _tile(T, tr)

    def body(y_, r_, k_, v_, g1, yf, g2, ym, g3, lnw, lnb, rk, E, ET, dy_ref, *outs):
        fe = lambda *a: _mix_fn(*a, E[...], ET[...])
        _, vjp = jax.vjp(fe, y_[...], r_[...], k_[...], v_[...], g1[...], yf[...], g2[...], ym[...], g3[...],
                         lnw[...], lnb[...], rk[...])
        grads = vjp(dy_ref[...])
        for o, gval in zip(outs[:9], grads[:9]):
            o[...] = gval

        @pl.when(pl.program_id(0) == 0)
        def _():
            for o in outs[9:]:
                o[...] = jnp.zeros_like(o)

        for o, gval in zip(outs[9:], grads[9:]):
            o[...] += gval

    widths = [C, C, C, C, C, C, C, MW, MW]
    return _pcall(body, name="mix_bwd", grid=(T // tr,),
                  in_specs=_mix_specs(tr, C, MW) + [_full_spec(a.shape) for a in params] + [_row_spec(tr, D)],
                  out_specs=[_row_spec(tr, w) for w in widths] + [_full_spec((1, C))] * 3,
                  out_shape=[jax.ShapeDtypeStruct((T, w), F32) for w in widths] + [jax.ShapeDtypeStruct((1, C), F32)] * 3,
                  compiler_params=_cp(("arbitrary",)))(*acts, *params, dycat)


def _post(yo, x, tgt, g_post, tr=128):
    T, D = x.shape
    tr = _tile(T, tr)

    def body(yo_ref, x_ref, t_ref, g_ref, dyo_ref, dout_ref, loss_ref, dg_ref):
        n, vjp = jax.vjp(_rms, yo_ref[...], g_ref[...])
        diff = (x_ref[...] + n) - t_ref[...]
        part = 0.5 * jnp.sum(jnp.mean(diff * diff, axis=-1, keepdims=True), axis=0, keepdims=True)
        d_out = diff * (1.0 / D)
        dyo, dg = vjp(d_out)
        dyo_ref[...] = dyo.astype(BF16)
        dout_ref[...] = d_out

        @pl.when(pl.program_id(0) == 0)
        def _():
            loss_ref[...] = jnp.zeros_like(loss_ref)
            dg_ref[...] = jnp.zeros_like(dg_ref)

        loss_ref[...] += jnp.broadcast_to(part, loss_ref.shape)
        dg_ref[...] += dg

    return _pcall(body, name="post_loss", grid=(T // tr,),
                  in_specs=[_row_spec(tr, D)] * 3 + [_full_spec((1, D))],
                  out_specs=[_row_spec(tr, D), _row_spec(tr, D), _full_spec((1, LANE)), _full_spec((1, D))],
                  out_shape=[jax.ShapeDtypeStruct((T, D), BF16), jax.ShapeDtypeStruct((T, D), F32),
                             jax.ShapeDtypeStruct((1, LANE), F32), jax.ShapeDtypeStruct((1, D), F32)],
                  compiler_params=_cp(("arbitrary",)))(yo, x, tgt, g_post)


def _memattn_fn(MW, q, mkv):
    hd = MW // MEM_HEADS
    scale = hd ** -0.5
    outs = []
    for h in range(MEM_HEADS):
        qh = q[:, h * hd:(h + 1) * hd]
        kh = mkv[:, h * hd:(h + 1) * hd]
        vh = mkv[:, MW + h * hd:MW + (h + 1) * hd]
        s = _mm(qh, kh, False, True, "bf") * scale
        e = jnp.exp(s - lax.stop_gradient(jnp.max(s, axis=-1, keepdims=True)))
        pr = e / jnp.sum(e, axis=-1, keepdims=True)
        outs.append(_mm(pr, vh, False, False, "bf"))
    return jnp.concatenate(outs, axis=1)


def _memattn_fwd(p, mkv, MW, tr=256):
    T = p.shape[0]
    tr = _tile(T, tr)

    def body(q_ref, kv_ref, o_ref):
        o_ref[...] = _memattn_fn(MW, q_ref[...], kv_ref[...])

    return _pcall(body, name="memattn_fwd", grid=(T // tr,),
                  in_specs=[_row_spec(tr, MW), _full_spec(mkv.shape)], out_specs=_row_spec(tr, MW),
                  out_shape=jax.ShapeDtypeStruct((T, MW), F32), compiler_params=_cp(("arbitrary",)))(p, mkv)


def _memattn_bwd(p, mkv, do, MW, tr=256):
    T = p.shape[0]
    tr = _tile(T, tr)

    def body(q_ref, kv_ref, do_ref, dq_ref, dkv_ref):
        _, vjp = jax.vjp(functools.partial(_memattn_fn, MW), q_ref[...], kv_ref[...])
        dq, dkv = vjp(do_ref[...])
        dq_ref[...] = dq

        @pl.when(pl.program_id(0) == 0)
        def _():
            dkv_ref[...] = jnp.zeros_like(dkv_ref)

        dkv_ref[...] += dkv

    return _pcall(body, name="memattn_bwd", grid=(T // tr,),
                  in_specs=[_row_spec(tr, MW), _full_spec(mkv.shape), _row_spec(tr, MW)],
                  out_specs=[_row_spec(tr, MW), _full_spec(mkv.shape)],
                  out_shape=[jax.ShapeDtypeStruct((T, MW), F32), jax.ShapeDtypeStruct(mkv.shape, F32)],
                  compiler_params=_cp(("arbitrary",)))(p, mkv, do)


def _fox_cum_fwd(p, b_f_pad, off, blk=512):
    T = p.shape[0]
    blk = _tile(T, blk)

    def body(f_ref, b_ref, cum_ref, cumt_ref, carry):
        @pl.when(pl.program_id(0) == 0)
        def _():
            carry[...] = jnp.zeros_like(carry)

        z = f_ref[...] + b_ref[...]
        logf = -_softplus(-z)
        row = lax.broadcasted_iota(jnp.int32, (blk, blk), 0)
        col = lax.broadcasted_iota(jnp.int32, (blk, blk), 1)
        tri = (col <= row).astype(F32)
        c = _dg(tri, logf, False, False, "hi") + carry[...]
        cum_ref[...] = c
        cumt_ref[...] = c.T
        carry[...] += jnp.sum(logf, axis=0, keepdims=True)

    return _pcall(body, name="fox_cum_fwd", grid=(T // blk,),
                  in_specs=[_row_spec(blk, LANE, off // LANE), _full_spec((1, LANE))],
                  out_specs=[_row_spec(blk, LANE), pl.BlockSpec((LANE, blk), lambda i: (0, i))],
                  out_shape=[jax.ShapeDtypeStruct((T, LANE), F32), jax.ShapeDtypeStruct((LANE, T), F32)],
                  scratch_shapes=[pltpu.VMEM((1, LANE), F32)], compiler_params=_cp(("arbitrary",)))(p, b_f_pad)


def _fox_cum_bwd(p, b_f_pad, dcum, off, blk=512):
    T = p.shape[0]
    blk = _tile(T, blk)
    nb = T // blk

    def body(f_ref, b_ref, dc_ref, df_ref, db_ref, carry):
        @pl.when(pl.program_id(0) == 0)
        def _():
            carry[...] = jnp.zeros_like(carry)
            db_ref[...] = jnp.zeros_like(db_ref)

        row = lax.broadcasted_iota(jnp.int32, (blk, blk), 0)
        col = lax.broadcasted_iota(jnp.int32, (blk, blk), 1)
        tri = (col >= row).astype(F32)
        dlogf = _dg(tri, dc_ref[...], False, False, "hi") + carry[...]
        carry[...] += jnp.sum(dc_ref[...], axis=0, keepdims=True)
        z = f_ref[...] + b_ref[...]
        dz = dlogf * (1.0 - _sigmoid(z))
        df_ref[...] = dz
        db_ref[...] += jnp.sum(dz, axis=0, keepdims=True)

    rev = lambda i: (nb - 1 - i, 0)
    return _pcall(body, name="fox_cum_bwd", grid=(nb,),
                  in_specs=[pl.BlockSpec((blk, LANE), lambda i: (nb - 1 - i, off // LANE)), _full_spec((1, LANE)),
                            pl.BlockSpec((blk, LANE), rev)],
                  out_specs=[pl.BlockSpec((blk, LANE), rev), _full_spec((1, LANE))],
                  out_shape=[jax.ShapeDtypeStruct((T, LANE), F32), jax.ShapeDtypeStruct((1, LANE), F32)],
                  scratch_shapes=[pltpu.VMEM((1, LANE), F32)], compiler_params=_cp(("arbitrary",)))(p, b_f_pad, dcum)


FOX_SCALE = HEAD ** -0.5
FOX_TQ, FOX_TK = 512, 1024


def _fox_scores(q, k, ck, q0=None, k0=None):
    s = _dg(q, k, False, True, "bf") - ck
    if q0 is None:
        return s
    qpos = q0 + lax.broadcasted_iota(jnp.int32, s.shape, 0)
    kpos = k0 + lax.broadcasted_iota(jnp.int32, s.shape, 1)
    return jnp.where(kpos <= qpos, s, NEG)


def _fox_c0(ck_ref, hh, pos):
    return ck_ref[0, hh:hh + 1, pl.ds(pl.multiple_of(pos, LANE), LANE)][:, 0:1]


def _fox_tiles(T):
    assert T % LANE == 0
    tq, tk = _tile(T, FOX_TQ), _tile(T, FOX_TK)
    shift = (tk // tq).bit_length() - 1
    assert tk == tq << shift
    return tq, tk, shift


def _fox_fwd(p, ck, C, offs):
    T = p.shape[0]
    tq, tk, shift = _fox_tiles(T)
    npair = C // LANE
    cb = lambda name: offs[name] // LANE

    def body(q_ref, k_ref, v_ref, ck_ref, o_ref, lse_ref):
        i = pl.program_id(1)
        nfull = i >> shift
        lse_ref[...] = jnp.zeros_like(lse_ref)
        heads = [slice(hh * HEAD, (hh + 1) * HEAD) for hh in range(2)]
        qs = [(q_ref[:, sl] * FOX_SCALE).astype(BF16) for sl in heads]
        c0s = [_fox_c0(ck_ref, hh, i * tq) for hh in range(2)]

        def step(j, carry, masked):
            off = pl.multiple_of(j * tk, tk)
            out = []
            for hh, sl in enumerate(heads):
                m, l, acc = carry[hh]
                k = k_ref[pl.ds(off, tk), sl].astype(BF16)
                v = v_ref[pl.ds(off, tk), sl].astype(BF16)
                ckv = ck_ref[0, hh:hh + 1, pl.ds(off, tk)] - c0s[hh]
                s = _fox_scores(qs[hh], k, ckv, i * tq, off) if masked else _fox_scores(qs[hh], k, ckv)
                m_new = jnp.maximum(m, jnp.max(s, axis=-1, keepdims=True))
                pr = jnp.exp(s - m_new)
                al = jnp.exp(m - m_new)
                l = al * l + jnp.sum(pr, axis=-1, keepdims=True)
                acc = al * acc + _dg(pr, v, False, False, "bf")
                out.append((m_new, l, acc))
            return tuple(out)

        one = (jnp.full((tq, 1), NEG, F32), jnp.zeros((tq, 1), F32), jnp.zeros((tq, HEAD), F32))
        carry = lax.fori_loop(0, nfull, lambda j, c: step(j, c, False), (one, one))
        for hh, (m, l, acc) in enumerate(step(nfull, carry, True)):
            o_ref[:, heads[hh]] = acc / l
            lse_ref[0, :, hh:hh + 1] = m + jnp.log(l)

    return _pcall(body, name="fox_fwd", grid=(npair, T // tq),
                  in_specs=[pl.BlockSpec((tq, LANE), lambda h, i: (i, cb("fq") + h)),
                            pl.BlockSpec((T, LANE), lambda h, i: (0, cb("fk") + h)),
                            pl.BlockSpec((T, LANE), lambda h, i: (0, cb("fv") + h)),
                            pl.BlockSpec((1, 8, T), lambda h, i: (h, 0, 0))],
                  out_specs=[pl.BlockSpec((tq, LANE), lambda h, i: (i, h)),
                             pl.BlockSpec((1, tq, 8), lambda h, i: (h, i, 0))],
                  out_shape=[jax.ShapeDtypeStruct((T, C), F32), jax.ShapeDtypeStruct((npair, T, 8), F32)],
                  compiler_params=_cp(("arbitrary", "arbitrary")))(p, p, p, ck)


def _fox_delta(p, ck, do, lse, C, offs):
    T = p.shape[0]
    tq, tk, shift = _fox_tiles(T)
    npair = C // LANE
    cb = lambda name: offs[name] // LANE

    def body(q_ref, k_ref, v_ref, ck_ref, do_ref, lse_ref, d_ref):
        i = pl.program_id(1)
        nfull = i >> shift
        d_ref[...] = jnp.zeros_like(d_ref)
        heads = [slice(hh * HEAD, (hh + 1) * HEAD) for hh in range(2)]
        qs = [(q_ref[:, sl] * FOX_SCALE).astype(BF16) for sl in heads]
        dos = [do_ref[:, sl].astype(BF16) for sl in heads]
        lses = [lse_ref[0, :, hh:hh + 1] for hh in range(2)]
        c0s = [_fox_c0(ck_ref, hh, i * tq) for hh in range(2)]

        def step(j, accs, masked):
            off = pl.multiple_of(j * tk, tk)
            out = []
            for hh, sl in enumerate(heads):
                k = k_ref[pl.ds(off, tk), sl].astype(BF16)
                v = v_ref[pl.ds(off, tk), sl].astype(BF16)
                ckv = ck_ref[0, hh:hh + 1, pl.ds(off, tk)] - c0s[hh]
                s = _fox_scores(qs[hh], k, ckv, i * tq, off) if masked else _fox_scores(qs[hh], k, ckv)
                pr = jnp.exp(s - lses[hh])
                dp = _dg(dos[hh], v, False, True, "bf")
                out.append(accs[hh] + jnp.sum(pr * dp, axis=-1, keepdims=True))
            return tuple(out)

        z = jnp.zeros((tq, 1), F32)
        accs = lax.fori_loop(0, nfull, lambda j, c: step(j, c, False), (z, z))
        for hh, acc in enumerate(step(nfull, accs, True)):
            d_ref[0, :, hh:hh + 1] = acc

    return _pcall(body, name="fox_delta", grid=(npair, T // tq),
                  in_specs=[pl.BlockSpec((tq, LANE), lambda h, i: (i, cb("fq") + h)),
                            pl.BlockSpec((T, LANE), lambda h, i: (0, cb("fk") + h)),
                            pl.BlockSpec((T, LANE), lambda h, i: (0, cb("fv") + h)),
                            pl.BlockSpec((1, 8, T), lambda h, i: (h, 0, 0)),
                            pl.BlockSpec((tq, LANE), lambda h, i: (i, h)),
                            pl.BlockSpec((1, tq, 8), lambda h, i: (h, i, 0))],
                  out_specs=pl.BlockSpec((1, tq, 8), lambda h, i: (h, i, 0)),
                  out_shape=jax.ShapeDtypeStruct((npair, T, 8), F32),
                  compiler_params=_cp(("arbitrary", "arbitrary")))(p, p, p, ck, do, lse)


def _fox_bwd(p, ck, delta, do, lse, C, offs):
    T = p.shape[0]
    tq, tk, shift = _fox_tiles(T)
    ratio = tk // tq
    nq = T // tq
    npair = C // LANE
    cb = lambda name: offs[name] // LANE

    def body(q_ref, k_ref, v_ref, ck_ref, ckall_ref, dl_ref, do_ref, lse_ref, dq_ref, dk_ref, dv_ref, dck_ref):
        j = pl.program_id(1)

        @pl.when(j == 0)
        def _():
            dq_ref[...] = jnp.zeros_like(dq_ref)

        dck_ref[...] = jnp.zeros_like(dck_ref)
        heads = [slice(hh * HEAD, (hh + 1) * HEAD) for hh in range(2)]
        ks = [k_ref[:, sl].astype(BF16) for sl in heads]
        vs = [v_ref[:, sl].astype(BF16) for sl in heads]
        cks = [ck_ref[0, hh:hh + 1, :] for hh in range(2)]

        def step(i, carry, masked):
            off = pl.multiple_of(i * tq, tq)
            out = []
            for hh, sl in enumerate(heads):
                dk, dv, dck = carry[hh]
                q = (q_ref[pl.ds(off, tq), sl] * FOX_SCALE).astype(BF16)
                dov = do_ref[pl.ds(off, tq), sl]
                lsev = lse_ref[0, pl.ds(off, tq), hh:hh + 1]
                ckv = cks[hh] - _fox_c0(ckall_ref, hh, off)
                s = _fox_scores(q, ks[hh], ckv, off, j * tk) if masked else _fox_scores(q, ks[hh], ckv)
                pr = jnp.exp(s - lsev)
                dv = dv + _dg(pr, dov, True, False, "bf")
                dp = _dg(dov, vs[hh], False, True, "bf")
                ds = pr * (dp - dl_ref[0, pl.ds(off, tq), hh:hh + 1])
                dk = dk + _dg(ds, q, True, False, "bf")
                dq_ref[pl.ds(off, tq), sl] += _dg(ds, ks[hh], False, False, "bf") * FOX_SCALE
                out.append((dk, dv, dck - jnp.sum(ds, axis=0, keepdims=True)))
            return tuple(out)

        z = jnp.zeros((tk, HEAD), F32)
        carry = ((z, z, jnp.zeros((1, tk), F32)),) * 2
        for r in range(ratio):
            carry = step(j * ratio + r, carry, True)
        carry = lax.fori_loop((j + 1) * ratio, nq, lambda i, c: step(i, c, False), carry)
        for hh, (dk, dv, dck) in enumerate(carry):
            dk_ref[:, heads[hh]] = dk
            dv_ref[:, heads[hh]] = dv
            dck_ref[0, hh:hh + 1, :] = dck

    full = lambda h, j: (0, h)
    return _pcall(body, name="fox_bwd", grid=(npair, T // tk),
                  in_specs=[pl.BlockSpec((T, LANE), lambda h, j: (0, cb("fq") + h)),
                            pl.BlockSpec((tk, LANE), lambda h, j: (j, cb("fk") + h)),
                            pl.BlockSpec((tk, LANE), lambda h, j: (j, cb("fv") + h)),
                            pl.BlockSpec((1, 8, tk), lambda h, j: (h, 0, j)),
                            pl.BlockSpec((1, 8, T), lambda h, j: (h, 0, 0)),
                            pl.BlockSpec((1, T, 8), lambda h, j: (h, 0, 0)), pl.BlockSpec((T, LANE), full),
                            pl.BlockSpec((1, T, 8), lambda h, j: (h, 0, 0))],
                  out_specs=[pl.BlockSpec((T, LANE), full),
                             pl.BlockSpec((tk, LANE), lambda h, j: (j, h)),
                             pl.BlockSpec((tk, LANE), lambda h, j: (j, h)),
                             pl.BlockSpec((1, 8, tk), lambda h, j: (h, 0, j))],
                  out_shape=[jax.ShapeDtypeStruct((T, C), F32)] * 3 + [jax.ShapeDtypeStruct((npair, 8, T), F32)],
                  compiler_params=_cp(("arbitrary", "arbitrary")))(p, p, p, ck, ck, delta, do, lse)


@jax.custom_vjp
def _solve(a, rhs):
    return _solve_fwd(a, rhs)[0]


def _solve_fwd(a, rhs):
    powers = [a]
    for _ in range(CHUNK.bit_length() - 2):
        powers.append(_dg(powers[-1], powers[-1], False, False, "x3"))
    u = rhs
    for pw in powers:
        u = u + _dg(pw, u, False, False, "x3")
    return u, (powers, u)


def _solve_bwd(res, g):
    powers, u = res
    w = g
    for pw in powers:
        w = w + _dg(pw, w, True, False, "x3")
    return _dg(w, u, False, True, "x3"), w


_solve.defvjp(_solve_fwd, _solve_bwd)


def _chunk_fn(S0, r, lw, k, v, a, b):
    nh, n = r.shape[0], r.shape[1]
    row = lax.broadcasted_iota(jnp.int32, (nh, n, n), 1)
    col = lax.broadcasted_iota(jnp.int32, (nh, n, n), 2)
    incl, strict = col <= row, col < row
    mm = lambda x, y, ta=False, tb=False: _mm(x, y, ta, tb, "x3")
    g = _mm(incl.astype(F32), lw, False, False, "hi")
    einv = jnp.exp(-g)
    rt, at, bt, kt = r * jnp.exp(g), a * jnp.exp(g - lw), b * einv, k * einv
    a_ab = jnp.where(strict, mm(at, bt, tb=True), 0.0)
    a_ak = jnp.where(strict, mm(at, kt, tb=True), 0.0)
    r_b = jnp.where(incl, mm(rt, bt, tb=True), 0.0)
    r_k = jnp.where(incl, mm(rt, kt, tb=True), 0.0)
    u = _solve(a_ab, mm(at, S0, tb=True) + mm(a_ak, v))
    y = mm(rt, S0, tb=True) + mm(r_b, u) + mm(r_k, v)
    g_end = jnp.sum(lw, axis=1, keepdims=True)
    s_end = (S0 + mm(u, bt, ta=True) + mm(v, kt, ta=True)) * jnp.exp(g_end)
    return y, s_end


SCAN_HEADS = 24
SCAN_ROWS = 64


def _scan_group(C):
    nh = SCAN_HEADS
    while C % (nh * HEAD):
        nh -= 2
    return nh, nh * HEAD


def _scan_fwd(r, lw, k, v, a, b):
    T, C = r.shape
    tc = _tile(T, SCAN_ROWS)
    ncs = tc // CHUNK
    nh, GW = _scan_group(C)
    ngroup = C // GW

    def body(r_ref, lw_ref, k_ref, v_ref, a_ref, b_ref, y_ref, ck_ref, state):
        @pl.when(pl.program_id(1) == 0)
        def _():
            state[...] = jnp.zeros_like(state)

        heads = [slice(hh * HEAD, (hh + 1) * HEAD) for hh in range(nh)]
        split = lambda ref, rows: jnp.stack([ref[rows, sl] for sl in heads])
        st = split(state, slice(None))
        for c in range(ncs):
            rows = slice(c * CHUNK, (c + 1) * CHUNK)
            for hh, sl in enumerate(heads):
                ck_ref[0, c, :, sl] = st[hh]
            y, st = _chunk_fn(st, *[split(ref, rows) for ref in (r_ref, lw_ref, k_ref, v_ref, a_ref, b_ref)])
            for hh, sl in enumerate(heads):
                y_ref[rows, sl] = y[hh]
        for hh, sl in enumerate(heads):
            state[:, sl] = st[hh]

    spec = pl.BlockSpec((tc, GW), lambda h, t: (t, h))
    return _pcall(body, name="rwkv_scan_fwd", grid=(ngroup, T // tc),
                  in_specs=[spec] * 6,
                  out_specs=[spec, pl.BlockSpec((1, ncs, HEAD, GW), lambda h, t: (h, t, 0, 0))],
                  out_shape=[jax.ShapeDtypeStruct((T, C), F32),
                             jax.ShapeDtypeStruct((ngroup, T // CHUNK, HEAD, GW), F32)],
                  scratch_shapes=[pltpu.VMEM((HEAD, GW), F32)],
                  compiler_params=_cp(("arbitrary", "arbitrary")))(r, lw, k, v, a, b)


def _scan_bwd(r, lw, k, v, a, b, ckpt, dy):
    T, C = r.shape
    tc = _tile(T, SCAN_ROWS)
    ncs = tc // CHUNK
    nh, GW = _scan_group(C)
    ngroup = C // GW
    nt = T // tc

    def body(r_ref, lw_ref, k_ref, v_ref, a_ref, b_ref, ck_ref, dy_ref, dr, dlw, dk, dv, da, db, dstate):
        @pl.when(pl.program_id(1) == 0)
        def _():
            dstate[...] = jnp.zeros_like(dstate)

        outs = (dr, dlw, dk, dv, da, db)
        heads = [slice(hh * HEAD, (hh + 1) * HEAD) for hh in range(nh)]
        split = lambda ref, rows: jnp.stack([ref[rows, sl] for sl in heads])
        dst = split(dstate, slice(None))
        for c in reversed(range(ncs)):
            rows = slice(c * CHUNK, (c + 1) * CHUNK)
            s0 = jnp.stack([ck_ref[0, c, :, sl] for sl in heads])
            _, vjp = jax.vjp(_chunk_fn, s0, *[split(ref, rows) for ref in (r_ref, lw_ref, k_ref, v_ref, a_ref, b_ref)])
            grads = vjp((split(dy_ref, rows), dst))
            dst = grads[0]
            for o, gval in zip(outs, grads[1:]):
                for hh, sl in enumerate(heads):
                    o[rows, sl] = gval[hh]
        for hh, sl in enumerate(heads):
            dstate[:, sl] = dst[hh]

    spec = pl.BlockSpec((tc, GW), lambda h, t: (nt - 1 - t, h))
    return _pcall(body, name="rwkv_scan_bwd", grid=(ngroup, nt),
                  in_specs=[spec] * 6 + [pl.BlockSpec((1, ncs, HEAD, GW), lambda h, t: (h, nt - 1 - t, 0, 0)), spec],
                  out_specs=[spec] * 6, out_shape=[jax.ShapeDtypeStruct((T, C), F32)] * 6,
                  scratch_shapes=[pltpu.VMEM((HEAD, GW), F32)],
                  compiler_params=_cp(("arbitrary", "arbitrary")))(r, lw, k, v, a, b, ckpt, dy)


def _adam(w, g, m, v):
    m = ADAM_B1 * m + (1.0 - ADAM_B1) * g
    v = ADAM_B2 * v + (1.0 - ADAM_B2) * (g * g)
    m_hat = m / (1.0 - ADAM_B1 ** ADAM_STEP)
    v_hat = v / (1.0 - ADAM_B2 ** ADAM_STEP)
    return -ADAM_LR * (m_hat / (jnp.sqrt(v_hat) + ADAM_EPS) + ADAM_WD * w), m, v


def _sum_adam(parts, w, m, v, name):
    n, R, W = parts.shape
    tr = _tile(R, max(8, min(256, (1 << 20) // (n * W))))

    def body(p_ref, w_ref, m_ref, v_ref, g_ref, d_ref, nm_ref, nv_ref):
        g = p_ref[0].astype(F32)
        for s in range(1, n):
            g = g + p_ref[s].astype(F32)
        d, nm, nv = _adam(w_ref[...], g, m_ref[...], v_ref[...])
        g_ref[...] = g
        d_ref[...] = d
        nm_ref[...] = nm
        nv_ref[...] = nv

    return _pcall(body, name=name, grid=(R // tr,),
                  in_specs=[pl.BlockSpec((n, tr, W), lambda i: (0, i, 0))] + [_row_spec(tr, W)] * 3,
                  out_specs=[_row_spec(tr, W)] * 4, out_shape=[jax.ShapeDtypeStruct((R, W), F32)] * 4,
                  compiler_params=_cp(("arbitrary",)))(parts, w, m, v)


def _pad_lanes(vec, width):
    return jnp.pad(vec, ((0, 0), (0, width - vec.shape[1])))


def _logical_cols(blocks, lo, hi):
    B, out = blocks.shape[2], []
    while lo < hi:
        j, o = divmod(lo, B)
        n = min(hi - lo, B - o)
        out.append(blocks[j, :, o:o + n])
        lo += n
    return out


def kernel(x, mem, g_pre, w_in, mu_rwkv, w0, w_decay_up, a0, w_iclr_up, k_k, k_a, r_k, ln_x_w, ln_x_b, b_f, g_mem, w_mem_kv, w_out, g_post, loss_target, m_g_pre, m_w_in, m_mu_rwkv, m_w0, m_w_decay_up, m_a0, m_w_iclr_up, m_k_k, m_k_a, m_r_k, m_ln_x_w, m_ln_x_b, m_b_f, m_g_mem, m_w_mem_kv, m_w_out, m_g_post, v_g_pre, v_w_in, v_mu_rwkv, v_w0, v_w_decay_up, v_a0, v_w_iclr_up, v_k_k, v_k_a, v_r_k, v_ln_x_w, v_ln_x_b, v_b_f, v_g_mem, v_w_mem_kv, v_w_out, v_g_post):
    T, D = x.shape[1], x.shape[2]
    C = w0.shape[1]
    L = w_decay_up.shape[1]
    H = C // HEAD
    MW = w_mem_kv.shape[2] // 2
    SH = 3 * C + 2 * L
    IN = NDEV * w_in.shape[2]
    assert IN == SH + 5 * C + H + 2 * MW and D == 2 * C + MW and H % 2 == 0 and H <= LANE
    assert C % LANE == 0 and L % LANE == 0 and MW % (MEM_HEADS * HEAD) == 0 and T % CHUNK == 0
    offs = dict(grw=SH, fq=SH + C, fk=SH + 2 * C, fv=SH + 3 * C, gfx=SH + 4 * C, mq=SH + 5 * C, gmq=SH + 5 * C + MW,
                fl=SH + 5 * C + 2 * MW)
    NI = -(-(offs["fl"] + LANE) // 1024) * 1024
    l_fl = SH + 4 * C

    x2, mem2, tgt2 = x[0], mem[0], loss_target[0]

    wg = _exchange(w_in[0].astype(BF16), "gather_w_in", True)
    w_perm = jnp.concatenate(_logical_cols(wg, 0, l_fl) + _logical_cols(wg, l_fl + H, IN)
                             + _logical_cols(wg, l_fl, l_fl + H) + [jnp.zeros((D, NI - IN), BF16)], axis=1)

    h = _rmsnorm_fwd(x2, g_pre, "rmsnorm_pre")
    p, w_out_f, w_kv_f, lora = _matmul(
        h, w_perm, False, False, "in_proj",
        riders=[(w_out[0].astype(BF16), True), (w_mem_kv[0].astype(BF16), True),
                (jnp.concatenate([w_decay_up[0], w_iclr_up[0]], axis=0), True)])
    w_out_f, w_kv_f = w_out_f.reshape(D, D), w_kv_f.reshape(D, 2 * MW)
    lora = jnp.transpose(lora, (1, 0, 2)).reshape(2 * L, C)
    wdu_f, wiu_f = lora[:L], lora[L:]

    E, ET = _head_indicator(C, LANE)
    prep_params = [mu_rwkv, w0, a0, k_k, k_a, wdu_f, wiu_f, E, ET]
    mix_params = [ln_x_w, ln_x_b, r_k.reshape(1, C), E, ET]
    b_f_pad = _pad_lanes(b_f, LANE)
    ps = p[:, :SH]
    col = lambda name, w: p[:, offs[name]:offs[name] + w]
    g_rwkv, g_fox, mq, g_mq = col("grw", C), col("gfx", C), col("mq", MW), col("gmq", MW)
    prev = jnp.concatenate([jnp.zeros((1, SH), F32), ps[:-1]], axis=0)
    r, lw, kmod, v, a, b = _prep_fwd(p, prev, prep_params, C, L)
    y_scan, ckpt = _scan_fwd(r, lw, kmod, v, a, b)

    cum, cum_t = _fox_cum_fwd(p, b_f_pad, offs["fl"])
    ck = jnp.pad(cum_t[:H].reshape(H // 2, 2, T), ((0, 0), (0, 6), (0, 0)))
    y_fox, lse = _fox_fwd(p, ck, C, offs)

    memn = _rmsnorm_fwd(mem2, g_mem, "rmsnorm_mem")
    mkv = _matmul(memn, w_kv_f, False, False, "mem_kv_proj")
    y_mem = _memattn_fwd(mq, mkv, MW)

    acts = [y_scan, r, kmod, v, g_rwkv, y_fox, g_fox, y_mem, g_mq]
    ycat = _mix_fwd(acts, mix_params, C, MW)
    yo = _matmul(ycat, w_out_f, False, False, "out_proj")
    d_yo, d_out, loss_part, dg_post = _post(yo, x2, tgt2, g_post)
    loss = lax.psum(loss_part[0, 0], AXES)

    g_w_out = _matmul(ycat, d_yo, True, False, "grad_w_out", BF16)
    d_ycat = _matmul(d_yo, w_out_f, False, True, "d_ycat")
    (d_y, d_r1, d_k1, d_v1, d_grw, d_yfox, d_gfx, d_ymem, d_gmq, dg_lnw, dg_lnb, dg_rk) = _mix_bwd(
        acts, mix_params, d_ycat, C, MW)

    d_mq, d_mkv = _memattn_bwd(mq, mkv, d_ymem, MW)
    g_w_kv = _matmul(memn, d_mkv, True, False, "grad_w_mem_kv", BF16)
    d_memn = _matmul(d_mkv, w_kv_f, False, True, "d_memn")
    _, dg_mem = _rmsnorm_bwd(mem2, g_mem, d_memn, None, "rmsnorm_mem_bwd")

    delta = _fox_delta(p, ck, d_yfox, lse, C, offs)
    d_fq, d_fk, d_fv, d_ck = _fox_bwd(p, ck, delta, d_yfox, lse, C, offs)
    d_cum = _pad_lanes(d_ck[:, :2, :].reshape(H, T).T, LANE)
    d_fl, dg_bf = _fox_cum_bwd(p, b_f_pad, d_cum, offs["fl"])

    d_r, d_lw, d_k, d_v, d_a, d_b = _scan_bwd(r, lw, kmod, v, a, b, ckpt, d_y)
    cts = [d_r, d_lw, d_k, d_v, d_a, d_b, d_r1, d_k1, d_v1]
    (d_ps, d_prev, dg_mu, dg_w0, dg_a0, dg_kk, dg_ka, dg_wdu, dg_wiu) = _prep_bwd(p, prev, prep_params, cts, C, L)
    d_prev_up = jnp.concatenate([d_prev[1:], jnp.zeros((1, SH), F32)], axis=0)
    d_sh = _shift_combine(d_ps, d_prev_up)

    tobf = lambda z: z.astype(BF16)
    dp = jnp.concatenate([d_sh, tobf(d_grw), tobf(d_fq), tobf(d_fk), tobf(d_fv), tobf(d_gfx), tobf(d_mq), tobf(d_gmq),
                          tobf(d_fl), jnp.zeros((T, NI - offs["fl"] - LANE), BF16)], axis=1)
    g_lora = jnp.concatenate([dg_wdu, dg_wiu], axis=0)
    g_w_perm, parts_out, parts_kv, parts_lora = _matmul(
        h, dp, True, False, "grad_w_in", BF16,
        riders=[(g_w_out.reshape(NDEV, D // NDEV, D), False), (g_w_kv.reshape(NDEV, D // NDEV, 2 * MW), False),
                (jnp.transpose(g_lora.reshape(2 * L, NDEV, C // NDEV), (1, 0, 2)), False)])

    def internal_cols(lo, hi):
        out = []
        for first, last, shift in ((0, l_fl, 0), (l_fl, l_fl + H, offs["fl"] - l_fl), (l_fl + H, IN, -H)):
            s0, s1 = max(lo, first), min(hi, last)
            if s0 < s1:
                out.append(g_w_perm[:, s0 + shift:s1 + shift])
        return out

    blk = IN // NDEV
    g_blocks = jnp.stack([jnp.concatenate(internal_cols(j * blk, (j + 1) * blk), axis=1) for j in range(NDEV)])
    d_h, parts_in = _matmul(dp, w_perm, False, True, "d_h", tk=5120, riders=[(g_blocks, False)])
    grad_x, dg_pre = _rmsnorm_bwd(x2, g_pre, d_h, d_out, "rmsnorm_pre_bwd")

    gw_in, dw_in, nm_w_in, nv_w_in = _sum_adam(parts_in, w_in[0], m_w_in[0], v_w_in[0], "adam_w_in")
    gw_out, dw_out, nm_w_out, nv_w_out = _sum_adam(parts_out, w_out[0], m_w_out[0], v_w_out[0], "adam_w_out")
    gw_kv, dw_kv, nm_w_kv, nv_w_kv = _sum_adam(parts_kv, w_mem_kv[0], m_w_mem_kv[0], v_w_mem_kv[0], "adam_w_mem_kv")
    cat2 = lambda u, w_: jnp.concatenate([u[0], w_[0]], axis=0)
    lora_res = _sum_adam(parts_lora, cat2(w_decay_up, w_iclr_up), cat2(m_w_decay_up, m_w_iclr_up),
                         cat2(v_w_decay_up, v_w_iclr_up), "adam_lora")

    small = [("g_pre", g_pre, m_g_pre, v_g_pre, dg_pre), ("mu_rwkv", mu_rwkv, m_mu_rwkv, v_mu_rwkv, dg_mu),
             ("w0", w0, m_w0, v_w0, dg_w0), ("a0", a0, m_a0, v_a0, dg_a0), ("k_k", k_k, m_k_k, v_k_k, dg_kk),
             ("k_a", k_a, m_k_a, v_k_a, dg_ka), ("r_k", r_k.reshape(1, C), m_r_k.reshape(1, C), v_r_k.reshape(1, C), dg_rk),
             ("ln_x_w", ln_x_w, m_ln_x_w, v_ln_x_w, dg_lnw), ("ln_x_b", ln_x_b, m_ln_x_b, v_ln_x_b, dg_lnb),
             ("b_f", _pad_lanes(b_f, LANE), _pad_lanes(m_b_f, LANE), _pad_lanes(v_b_f, LANE), dg_bf),
             ("g_mem", g_mem, m_g_mem, v_g_mem, dg_mem), ("g_post", g_post, m_g_post, v_g_post, dg_post)]
    widths = [s[1].shape[1] for s in small]
    pack = lambda idx: jnp.concatenate([s[idx] for s in small], axis=1).reshape(-1, LANE)
    parts_small = _exchange(pack(4), "gather_small_grads", True)
    res_small = _sum_adam(parts_small, pack(1), pack(2), pack(3), "adam_small")

    def unpack(flat):
        flat = flat.reshape(1, -1)
        out, o = {}, 0
        for (name, *_), wd in zip(small, widths):
            out[name] = flat[:, o:o + wd]
            o += wd
        out["b_f"] = out["b_f"][:, :H]
        out["r_k"] = out["r_k"].reshape(1, H, HEAD)
        return out

    sg, sd, sm, sv = [unpack(z) for z in res_small]
    big = {"w_in": (gw_in, dw_in, nm_w_in, nv_w_in), "w_out": (gw_out, dw_out, nm_w_out, nv_w_out),
           "w_mem_kv": (gw_kv, dw_kv, nm_w_kv, nv_w_kv),
           "w_decay_up": tuple(z[:L] for z in lora_res), "w_iclr_up": tuple(z[L:] for z in lora_res)}
    order = ["g_pre", "w_in", "mu_rwkv", "w0", "w_decay_up", "a0", "w_iclr_up", "k_k", "k_a", "r_k", "ln_x_w", "ln_x_b",
             "b_f", "g_mem", "w_mem_kv", "w_out", "g_post"]

    def pick(name, idx):
        if name in big:
            return big[name][idx][None]
        return (sg, sd, sm, sv)[idx][name]

    outs = [loss, grad_x[None]]
    for idx in range(4):
        outs += [pick(n, idx) for n in order]
    return tuple(outs)
```

```python
import functools

import jax
import jax.numpy as jnp
from jax import lax
from jax.experimental import pallas as pl
from jax.experimental.pallas import tpu as pltpu

F32, BF16 = jnp.float32, jnp.bfloat16
HI = lax.Precision.HIGHEST
NDEV = 8
AXES = ("x", "y", "c")
HEAD = 64
CHUNK = 64
MEM_HEADS = 4
LANE = 128
RMS_EPS = 1e-6
GN_EPS = 64e-5
NEG = -1e30
ADAM_LR, ADAM_B1, ADAM_B2, ADAM_EPS, ADAM_WD, ADAM_STEP = 0.001, 0.9, 0.999, 1e-08, 0.01, 10
VMEM_LIMIT = 56 * 1024 * 1024


def _pcall(body, **kw):
    return pl.pallas_call(body, **kw)


def _cp(sem=None, vmem=VMEM_LIMIT):
    return pltpu.CompilerParams(dimension_semantics=sem, vmem_limit_bytes=vmem)


def _tile(n, pref):
    for t in (pref, 1024, 512, 256, 128, 64, 32, 16, 8):
        if t <= pref and n % t == 0:
            return t
    return n


def _dg(a, b, ta, tb, mode):
    nb = a.ndim - 2
    ca = nb + (0 if ta else 1)
    cb = nb + (1 if tb else 0)
    dims = (((ca,), (cb,)), (tuple(range(nb)), tuple(range(nb))))
    if mode == "x3":
        a_hi, b_hi = a.astype(BF16), b.astype(BF16)
        a_lo, b_lo = (a - a_hi.astype(F32)).astype(BF16), (b - b_hi.astype(F32)).astype(BF16)
        a3 = jnp.concatenate([a_hi, a_lo, a_hi], axis=ca)
        b3 = jnp.concatenate([b_hi, b_hi, b_lo], axis=cb)
        return lax.dot_general(a3, b3, dims, preferred_element_type=F32)
    if mode == "e3":
        b1 = b.astype(BF16)
        rest = b - b1.astype(F32)
        b2 = rest.astype(BF16)
        b3 = (rest - b2.astype(F32)).astype(BF16)
        a = a.astype(BF16)
        return lax.dot_general(jnp.concatenate([a, a, a], axis=ca), jnp.concatenate([b1, b2, b3], axis=cb), dims,
                               preferred_element_type=F32)
    if mode == "x2":
        a_hi, b = a.astype(BF16), b.astype(BF16)
        a_lo = (a - a_hi.astype(F32)).astype(BF16)
        dot = lambda u: lax.dot_general(u, b, dims, preferred_element_type=F32)
        return dot(a_hi) + dot(a_lo)
    if mode == "bf":
        a, b, prec = a.astype(BF16), b.astype(BF16), None
    else:
        prec = HI
    return lax.dot_general(a, b, dims, preferred_element_type=F32, precision=prec)


@functools.partial(jax.custom_vjp, nondiff_argnums=(2, 3, 4))
def _mm(a, b, ta, tb, mode):
    return _dg(a, b, ta, tb, mode)


def _mm_fwd(a, b, ta, tb, mode):
    return _dg(a, b, ta, tb, mode), (a, b)


def _mm_bwd(ta, tb, mode, res, g):
    a, b = res
    da = _dg(g, b, False, not tb, mode) if not ta else _dg(b, g, tb, True, mode)
    db = _dg(a, g, not ta, False, mode) if not tb else _dg(g, a, True, ta, mode)
    return da, db


_mm.defvjp(_mm_fwd, _mm_bwd)


@jax.custom_vjp
def _seg(a, e):
    return _dg(a, e, False, False, "x2")


def _seg_fwd(a, e):
    return _dg(a, e, False, False, "x2"), e


def _seg_bwd(e, g):
    return _dg(g, e, False, True, "x2"), jnp.zeros_like(e)


_seg.defvjp(_seg_fwd, _seg_bwd)


def _sigmoid(z):
    return 1.0 / (1.0 + jnp.exp(-z))


def _softplus(z):
    return jnp.maximum(z, 0.0) + jnp.log(1.0 + jnp.exp(-jnp.abs(z)))


def _silu(z):
    return z * _sigmoid(z)


def _rms(x, g):
    return x * lax.rsqrt(jnp.mean(x * x, axis=-1, keepdims=True) + RMS_EPS) * g


HBM_SPEC = pl.BlockSpec(memory_space=pltpu.HBM)
EXCHANGE_SEMS = [pltpu.SemaphoreType.DMA((NDEV - 1,)), pltpu.SemaphoreType.DMA((NDEV - 1,)), pltpu.SemaphoreType.DMA(())]


def _exchange_copies(gather, x_ref, o_ref, send_sems, recv_sems, local_sem, arrivals):
    ix, iy, ic = lax.axis_index("x"), lax.axis_index("y"), lax.axis_index("c")
    me = 4 * ix + 2 * iy + ic

    def src(dest):
        return x_ref if gather else x_ref.at[dest]

    mine = pltpu.make_async_copy(src(me), o_ref.at[me], local_sem)
    pairs = []
    for k in range(1, NDEV):
        px = 1 - ix if (k >> 2) & 1 else ix
        py = 1 - iy if (k >> 1) & 1 else iy
        pc = 1 - ic if k & 1 else ic
        peer = 4 * px + 2 * py + pc
        send = pltpu.make_async_remote_copy(
            src_ref=src(peer), dst_ref=o_ref.at[me], send_sem=send_sems.at[k - 1], recv_sem=recv_sems.at[k - 1],
            device_id=(px, py, pc), device_id_type=pl.DeviceIdType.MESH)
        arrival = arrivals and pltpu.make_async_remote_copy(
            src_ref=src(peer), dst_ref=o_ref.at[peer], send_sem=send_sems.at[k - 1], recv_sem=recv_sems.at[k - 1],
            device_id=(ix, iy, ic), device_id_type=pl.DeviceIdType.MESH)
        pairs.append((send, arrival))
    return mine, pairs


def _exchange_start(*args):
    mine, pairs = _exchange_copies(*args, arrivals=False)
    mine.start()
    for send, _ in pairs:
        send.start()


def _exchange_wait(*args):
    mine, pairs = _exchange_copies(*args, arrivals=True)
    for send, arrival in pairs:
        send.wait_send()
        arrival.wait_recv()
    mine.wait()


def _exchange_shape(x, gather):
    return jax.ShapeDtypeStruct((NDEV,) + tuple(x.shape if gather else x.shape[1:]), x.dtype)


def _exchange(x, name, gather):
    def body(x_ref, o_ref, *sems):
        _exchange_start(gather, x_ref, o_ref, *sems)
        _exchange_wait(gather, x_ref, o_ref, *sems)

    return _pcall(body, name=name, out_shape=_exchange_shape(x, gather), in_specs=[HBM_SPEC], out_specs=HBM_SPEC,
                  scratch_shapes=list(EXCHANGE_SEMS))(x)


MAX_FULL_K = 4096


def _matmul(a, b, ta, tb, name, out_dtype=F32, tm=1024, tn=1024, tk=3072, riders=()):
    M, K = (a.shape[1], a.shape[0]) if ta else a.shape
    N = b.shape[0] if tb else b.shape[1]
    assert (b.shape[1] if tb else b.shape[0]) == K
    if K <= MAX_FULL_K:
        tk = K
    else:
        tm, tk = min(tm, 512), _tile(K, tk)
    tm, tn = _tile(M, tm), _tile(N, tn)
    grid = (M // tm, N // tn, K // tk)
    nk, nr = grid[2], len(riders)

    def body(*refs):
        a_ref, b_ref, x_refs = refs[0], refs[1], refs[2:2 + nr]
        o_ref, xo_refs, rest = refs[2 + nr], refs[3 + nr:3 + 2 * nr], refs[3 + 2 * nr:]
        sems = rest[1:] if nk > 1 else rest
        ids = [pl.program_id(d) for d in range(3)]
        jobs = [(riders[q][1], x_refs[q], xo_refs[q]) + tuple(sems[3 * q:3 * q + 3]) for q in range(nr)]

        if nr:
            @pl.when((ids[0] == 0) & (ids[1] == 0) & (ids[2] == 0))
            def _():
                for job in jobs:
                    _exchange_start(*job)

        if nk == 1:
            o_ref[...] = _dg(a_ref[...], b_ref[...], ta, tb, "bf").astype(o_ref.dtype)
        else:
            acc = rest[0]

            @pl.when(ids[2] == 0)
            def _():
                acc[...] = jnp.zeros_like(acc)

            acc[...] += _dg(a_ref[...], b_ref[...], ta, tb, "bf")

            @pl.when(ids[2] == nk - 1)
            def _():
                o_ref[...] = acc[...].astype(o_ref.dtype)

        if nr:
            @pl.when((ids[0] == grid[0] - 1) & (ids[1] == grid[1] - 1) & (ids[2] == nk - 1))
            def _():
                for job in jobs:
                    _exchange_wait(*job)

    a_spec = pl.BlockSpec((tk, tm), lambda i, j, k: (k, i)) if ta else pl.BlockSpec((tm, tk), lambda i, j, k: (i, k))
    b_spec = pl.BlockSpec((tn, tk), lambda i, j, k: (j, k)) if tb else pl.BlockSpec((tk, tn), lambda i, j, k: (k, j))
    out = _pcall(
        body, name=name, grid=grid,
        in_specs=[a_spec, b_spec] + [HBM_SPEC] * nr,
        out_specs=[pl.BlockSpec((tm, tn), lambda i, j, k: (i, j))] + [HBM_SPEC] * nr,
        out_shape=[jax.ShapeDtypeStruct((M, N), out_dtype)] + [_exchange_shape(x, g) for x, g in riders],
        scratch_shapes=([pltpu.VMEM((tm, tn), F32)] if nk > 1 else []) + list(EXCHANGE_SEMS) * nr,
        compiler_params=_cp(("arbitrary", "arbitrary", "arbitrary")),
    )(a, b, *[x for x, _ in riders])
    return tuple(out) if nr else out[0]


def _row_spec(tr, width, col_block=0):
    return pl.BlockSpec((tr, width), lambda i: (i, col_block))


def _full_spec(shape):
    nd = len(shape)
    return pl.BlockSpec(tuple(shape), lambda i: (0,) * nd)


def _rmsnorm_fwd(x, g, name, tr=256):
    R, D = x.shape
    tr = _tile(R, tr)

    def body(x_ref, g_ref, o_ref):
        o_ref[...] = _rms(x_ref[...], g_ref[...]).astype(BF16)

    return _pcall(body, name=name, grid=(R // tr,),
                  in_specs=[_row_spec(tr, D), _full_spec((1, D))], out_specs=_row_spec(tr, D),
                  out_shape=jax.ShapeDtypeStruct((R, D), BF16), compiler_params=_cp(("arbitrary",)))(x, g)


def _rmsnorm_bwd(x, g, dy, extra, name, tr=128):
    R, D = x.shape
    tr = _tile(R, tr)
    has_extra = extra is not None

    def body(*refs):
        if has_extra:
            x_ref, g_ref, dy_ref, e_ref, dx_ref, dg_ref = refs
        else:
            x_ref, g_ref, dy_ref, dx_ref, dg_ref = refs
        _, vjp = jax.vjp(_rms, x_ref[...], g_ref[...])
        dx, dg = vjp(dy_ref[...])
        dx_ref[...] = dx + e_ref[...] if has_extra else dx

        @pl.when(pl.program_id(0) == 0)
        def _():
            dg_ref[...] = jnp.zeros_like(dg_ref)

        dg_ref[...] += dg

    ins = [x, g, dy] + ([extra] if has_extra else [])
    specs = [_row_spec(tr, D), _full_spec((1, D)), _row_spec(tr, D)] + ([_row_spec(tr, D)] if has_extra else [])
    return _pcall(body, name=name, grid=(R // tr,), in_specs=specs,
                  out_specs=[_row_spec(tr, D), _full_spec((1, D))],
                  out_shape=[jax.ShapeDtypeStruct((R, D), F32), jax.ShapeDtypeStruct((1, D), F32)],
                  compiler_params=_cp(("arbitrary",)))(*ins)


def _head_indicator(C, hp):
    e = (jnp.arange(C)[:, None] // HEAD == jnp.arange(hp)[None, :]).astype(F32)
    return e, e.T


def _prep_fn(C, L, ps, prev, mu, w0, a0, k_k, k_a, wdu, wiu, E, ET):
    sh = ps + (prev - ps) * mu
    r, k, v = sh[:, :C], sh[:, C:2 * C], sh[:, 2 * C:3 * C]
    wl, al = sh[:, 3 * C:3 * C + L], sh[:, 3 * C + L:3 * C + 2 * L]
    wd = w0 + _mm(jnp.tanh(wl), wdu, False, False, "bf")
    w_pre = -_softplus(-wd) - 0.5
    lw = -jnp.exp(w_pre)
    alpha = _sigmoid(a0 + _mm(al, wiu, False, False, "bf"))
    kk = k * k_k
    ss = _seg(kk * kk, E)
    kk = kk * _seg(lax.rsqrt(jnp.maximum(ss, 1e-24)), ET)
    k_mod = k * (1.0 + (alpha - 1.0) * k_a)
    return r, lw, k_mod, v, -kk, kk * alpha


def _prep_fwd(p, prev, params, C, L, tr=128):
    T = p.shape[0]
    SH = 3 * C + 2 * L
    tr = _tile(T, tr)

    def body(ps_ref, prev_ref, mu, w0, a0, kk_, ka_, wdu, wiu, E, ET, *outs):
        vals = _prep_fn(C, L, ps_ref[...], prev_ref[...], mu[...], w0[...], a0[...], kk_[...], ka_[...],
                        wdu[...], wiu[...], E[...], ET[...])
        for o, v in zip(outs, vals):
            o[...] = v

    pspecs = [_full_spec(a.shape) for a in params]
    return _pcall(body, name="rwkv_prep_fwd", grid=(T // tr,),
                  in_specs=[_row_spec(tr, SH), _row_spec(tr, SH)] + pspecs,
                  out_specs=[_row_spec(tr, C)] * 6,
                  out_shape=[jax.ShapeDtypeStruct((T, C), F32)] * 6,
                  compiler_params=_cp(("arbitrary",)))(p, prev, *params)


def _prep_bwd(p, prev, params, cts, C, L, tr=128):
    T = p.shape[0]
    SH = 3 * C + 2 * L
    tr = _tile(T, tr)
    nparam = 7

    def body(ps_ref, prev_ref, mu, w0, a0, kk_, ka_, wdu, wiu, E, ET, c0, c1, c2, c3, c4, c5, e0, e2, e3,
             dps_ref, dprev_ref, *dpar):
        f = functools.partial(_prep_fn, C, L)
        fe = lambda ps, prev, *par: f(ps, prev, *par, E[...], ET[...])
        _, vjp = jax.vjp(fe, ps_ref[...], prev_ref[...], mu[...], w0[...], a0[...], kk_[...], ka_[...], wdu[...], wiu[...])
        grads = vjp((c0[...] + e0[...], c1[...], c2[...] + e2[...], c3[...] + e3[...], c4[...], c5[...]))
        dps_ref[...] = grads[0]
        dprev_ref[...] = grads[1]

        @pl.when(pl.program_id(0) == 0)
        def _():
            for d in dpar:
                d[...] = jnp.zeros_like(d)

        for d, gval in zip(dpar, grads[2:]):
            d[...] += gval

    pspecs = [_full_spec(a.shape) for a in params]
    par_shapes = [a.shape for a in params[:nparam]]
    return _pcall(body, name="rwkv_prep_bwd", grid=(T // tr,),
                  in_specs=[_row_spec(tr, SH), _row_spec(tr, SH)] + pspecs + [_row_spec(tr, C)] * 9,
                  out_specs=[_row_spec(tr, SH), _row_spec(tr, SH)] + [_full_spec(s) for s in par_shapes],
                  out_shape=[jax.ShapeDtypeStruct((T, SH), F32)] * 2 + [jax.ShapeDtypeStruct(s, F32) for s in par_shapes],
                  compiler_params=_cp(("arbitrary",)))(p, prev, *params, *cts)


def _shift_combine(d_direct, d_prev_up, tr=256):
    T, W = d_direct.shape
    tr = _tile(T, tr)

    def body(a_ref, b_ref, o_ref):
        o_ref[...] = (a_ref[...] + b_ref[...]).astype(BF16)

    return _pcall(body, name="shift_combine", grid=(T // tr,),
                  in_specs=[_row_spec(tr, W)] * 2, out_specs=_row_spec(tr, W),
                  out_shape=jax.ShapeDtypeStruct((T, W), BF16), compiler_params=_cp(("arbitrary",)))(d_direct, d_prev_up)


def _mix_fn(y, r, kmod, v, g_rwkv, yfox, g_fox, ymem, g_mq, lnw, lnb, rk, E, ET):
    inv = 1.0 / HEAD
    mean = _seg(y, E) * inv
    yc = y - _seg(mean, ET)
    var = _seg(yc * yc, E) * inv
    yn = yc * _seg(lax.rsqrt(var + GN_EPS), ET) * lnw + lnb
    bonus = _seg(_seg(r * kmod * rk, E), ET) * v
    o1 = (yn + bonus) * _silu(g_rwkv)
    return jnp.concatenate([o1, yfox * _silu(g_fox), ymem * _silu(g_mq)], axis=1)


def _mix_specs(tr, C, MW):
    return [_row_spec(tr, C)] * 7 + [_row_spec(tr, MW)] * 2


def _mix_fwd(acts, params, C, MW, tr=128):
    T = acts[0].shape[0]
    D = 2 * C + MW
    tr = _tile(T, tr)

    def body(y_, r_, k_, v_, g1, yf, g2, ym, g3, lnw, lnb, rk, E, ET, o_ref):
        o_ref[...] = _mix_fn(y_[...], r_[...], k_[...], v_[...], g1[...], yf[...], g2[...], ym[...], g3[...],
                             lnw[...], lnb[...], rk[...], E[...], ET[...]).astype(BF16)

    return _pcall(body, name="mix_fwd", grid=(T // tr,),
                  in_specs=_mix_specs(tr, C, MW) + [_full_spec(a.shape) for a in params],
                  out_specs=_row_spec(tr, D), out_shape=jax.ShapeDtypeStruct((T, D), BF16),
                  compiler_params=_cp(("arbitrary",)))(*acts, *params)


def _mix_bwd(acts, params, dycat, C, MW, tr=128):
    T = acts[0].shape[0]
    D = 2 * C + MW
    tr = _tile(T, tr)

    def body(y_, r_, k_, v_, g1, yf, g2, ym, g3, lnw, lnb, rk, E, ET, dy_ref, *outs):
        fe = lambda *a: _mix_fn(*a, E[...], ET[...])
        _, vjp = jax.vjp(fe, y_[...], r_[...], k_[...], v_[...], g1[...], yf[...], g2[...], ym[...], g3[...],
                         lnw[...], lnb[...], rk[...])
        grads = vjp(dy_ref[...])
        for o, gval in zip(outs[:9], grads[:9]):
            o[...] = gval

        @pl.when(pl.program_id(0) == 0)
        def _():
            for o in outs[9:]:
                o[...] = jnp.zeros_like(o)

        for o, gval in zip(outs[9:], grads[9:]):
            o[...] += gval

    widths = [C, C, C, C, C, C, C, MW, MW]
    return _pcall(body, name="mix_bwd", grid=(T // tr,),
                  in_specs=_mix_specs(tr, C, MW) + [_full_spec(a.shape) for a in params] + [_row_spec(tr, D)],
                  out_specs=[_row_spec(tr, w) for w in widths] + [_full_spec((1, C))] * 3,
                  out_shape=[jax.ShapeDtypeStruct((T, w), F32) for w in widths] + [jax.ShapeDtypeStruct((1, C), F32)] * 3,
                  compiler_params=_cp(("arbitrary",)))(*acts, *params, dycat)


def _post(yo, x, tgt, g_post, tr=128):
    T, D = x.shape
    tr = _tile(T, tr)

    def body(yo_ref, x_ref, t_ref, g_ref, dyo_ref, dout_ref, loss_ref, dg_ref):
        n, vjp = jax.vjp(_rms, yo_ref[...], g_ref[...])
        diff = (x_ref[...] + n) - t_ref[...]
        part = 0.5 * jnp.sum(jnp.mean(diff * diff, axis=-1, keepdims=True), axis=0, keepdims=True)
        d_out = diff * (1.0 / D)
        dyo, dg = vjp(d_out)
        dyo_ref[...] = dyo.astype(BF16)
        dout_ref[...] = d_out

        @pl.when(pl.program_id(0) == 0)
        def _():
            loss_ref[...] = jnp.zeros_like(loss_ref)
            dg_ref[...] = jnp.zeros_like(dg_ref)

        loss_ref[...] += jnp.broadcast_to(part, loss_ref.shape)
        dg_ref[...] += dg

    return _pcall(body, name="post_loss", grid=(T // tr,),
                  in_specs=[_row_spec(tr, D)] * 3 + [_full_spec((1, D))],
                  out_specs=[_row_spec(tr, D), _row_spec(tr, D), _full_spec((1, LANE)), _full_spec((1, D))],
                  out_shape=[jax.ShapeDtypeStruct((T, D), BF16), jax.ShapeDtypeStruct((T, D), F32),
                             jax.ShapeDtypeStruct((1, LANE), F32), jax.ShapeDtypeStruct((1, D), F32)],
                  compiler_params=_cp(("arbitrary",)))(yo, x, tgt, g_post)


def _memattn_fn(MW, q, mkv):
    hd = MW // MEM_HEADS
    scale = hd ** -0.5
    outs = []
    for h in range(MEM_HEADS):
        qh = q[:, h * hd:(h + 1) * hd]
        kh = mkv[:, h * hd:(h + 1) * hd]
        vh = mkv[:, MW + h * hd:MW + (h + 1) * hd]
        s = _mm(qh, kh, False, True, "bf") * scale
        e = jnp.exp(s - lax.stop_gradient(jnp.max(s, axis=-1, keepdims=True)))
        pr = e / jnp.sum(e, axis=-1, keepdims=True)
        outs.append(_mm(pr, vh, False, False, "bf"))
    return jnp.concatenate(outs, axis=1)


def _memattn_fwd(p, mkv, MW, tr=256):
    T = p.shape[0]
    tr = _tile(T, tr)

    def body(q_ref, kv_ref, o_ref):
        o_ref[...] = _memattn_fn(MW, q_ref[...], kv_ref[...])

    return _pcall(body, name="memattn_fwd", grid=(T // tr,),
                  in_specs=[_row_spec(tr, MW), _full_spec(mkv.shape)], out_specs=_row_spec(tr, MW),
                  out_shape=jax.ShapeDtypeStruct((T, MW), F32), compiler_params=_cp(("arbitrary",)))(p, mkv)


def _memattn_bwd(p, mkv, do, MW, tr=256):
    T = p.shape[0]
    tr = _tile(T, tr)

    def body(q_ref, kv_ref, do_ref, dq_ref, dkv_ref):
        _, vjp = jax.vjp(functools.partial(_memattn_fn, MW), q_ref[...], kv_ref[...])
        dq, dkv = vjp(do_ref[...])
        dq_ref[...] = dq

        @pl.when(pl.program_id(0) == 0)
        def _():
            dkv_ref[...] = jnp.zeros_like(dkv_ref)

        dkv_ref[...] += dkv

    return _pcall(body, name="memattn_bwd", grid=(T // tr,),
                  in_specs=[_row_spec(tr, MW), _full_spec(mkv.shape), _row_spec(tr, MW)],
                  out_specs=[_row_spec(tr, MW), _full_spec(mkv.shape)],
                  out_shape=[jax.ShapeDtypeStruct((T, MW), F32), jax.ShapeDtypeStruct(mkv.shape, F32)],
                  compiler_params=_cp(("arbitrary",)))(p, mkv, do)


def _fox_cum_fwd(p, b_f_pad, off, blk=512):
    T = p.shape[0]
    blk = _tile(T, blk)

    def body(f_ref, b_ref, cum_ref, cumt_ref, carry):
        @pl.when(pl.program_id(0) == 0)
        def _():
            carry[...] = jnp.zeros_like(carry)

        z = f_ref[...] + b_ref[...]
        logf = -_softplus(-z)
        row = lax.broadcasted_iota(jnp.int32, (blk, blk), 0)
        col = lax.broadcasted_iota(jnp.int32, (blk, blk), 1)
        tri = (col <= row).astype(F32)
        c = _dg(tri, logf, False, False, "hi") + carry[...]
        cum_ref[...] = c
        cumt_ref[...] = c.T
        carry[...] += jnp.sum(logf, axis=0, keepdims=True)

    return _pcall(body, name="fox_cum_fwd", grid=(T // blk,),
                  in_specs=[_row_spec(blk, LANE, off // LANE), _full_spec((1, LANE))],
                  out_specs=[_row_spec(blk, LANE), pl.BlockSpec((LANE, blk), lambda i: (0, i))],
                  out_shape=[jax.ShapeDtypeStruct((T, LANE), F32), jax.ShapeDtypeStruct((LANE, T), F32)],
                  scratch_shapes=[pltpu.VMEM((1, LANE), F32)], compiler_params=_cp(("arbitrary",)))(p, b_f_pad)


def _fox_cum_bwd(p, b_f_pad, dcum, off, blk=512):
    T = p.shape[0]
    blk = _tile(T, blk)
    nb = T // blk

    def body(f_ref, b_ref, dc_ref, df_ref, db_ref, carry):
        @pl.when(pl.program_id(0) == 0)
        def _():
            carry[...] = jnp.zeros_like(carry)
            db_ref[...] = jnp.zeros_like(db_ref)

        row = lax.broadcasted_iota(jnp.int32, (blk, blk), 0)
        col = lax.broadcasted_iota(jnp.int32, (blk, blk), 1)
        tri = (col >= row).astype(F32)
        dlogf = _dg(tri, dc_ref[...], False, False, "hi") + carry[...]
        carry[...] += jnp.sum(dc_ref[...], axis=0, keepdims=True)
        z = f_ref[...] + b_ref[...]
        dz = dlogf * (1.0 - _sigmoid(z))
        df_ref[...] = dz
        db_ref[...] += jnp.sum(dz, axis=0, keepdims=True)

    rev = lambda i: (nb - 1 - i, 0)
    return _pcall(body, name="fox_cum_bwd", grid=(nb,),
                  in_specs=[pl.BlockSpec((blk, LANE), lambda i: (nb - 1 - i, off // LANE)), _full_spec((1, LANE)),
                            pl.BlockSpec((blk, LANE), rev)],
                  out_specs=[pl.BlockSpec((blk, LANE), rev), _full_spec((1, LANE))],
                  out_shape=[jax.ShapeDtypeStruct((T, LANE), F32), jax.ShapeDtypeStruct((1, LANE), F32)],
                  scratch_shapes=[pltpu.VMEM((1, LANE), F32)], compiler_params=_cp(("arbitrary",)))(p, b_f_pad, dcum)


FOX_SCALE = HEAD ** -0.5
FOX_TQ, FOX_TK = 1024, 1024


def _fox_scores(q, k, ck, q0=None, k0=None):
    s = _dg(q, k, False, True, "bf") - ck
    if q0 is None:
        return s
    qpos = q0 + lax.broadcasted_iota(jnp.int32, s.shape, 0)
    kpos = k0 + lax.broadcasted_iota(jnp.int32, s.shape, 1)
    return jnp.where(kpos <= qpos, s, NEG)


def _fox_c0(ck_ref, hh, pos):
    return ck_ref[0, hh:hh + 1, pl.ds(pl.multiple_of(pos, LANE), LANE)][:, 0:1]


def _fox_tiles(T):
    assert T % LANE == 0
    tq, tk = _tile(T, FOX_TQ), _tile(T, FOX_TK)
    shift = (tk // tq).bit_length() - 1
    assert tk == tq << shift
    return tq, tk, shift


def _fox_fwd(p, ck, C, offs):
    T = p.shape[0]
    tq, tk, shift = _fox_tiles(T)
    npair = C // LANE
    cb = lambda name: offs[name] // LANE

    def body(q_ref, k_ref, v_ref, ck_ref, o_ref, lse_ref):
        i = pl.program_id(1)
        nfull = i >> shift
        lse_ref[...] = jnp.zeros_like(lse_ref)
        heads = [slice(hh * HEAD, (hh + 1) * HEAD) for hh in range(2)]
        qs = [(q_ref[:, sl] * FOX_SCALE).astype(BF16) for sl in heads]
        c0s = [_fox_c0(ck_ref, hh, i * tq) for hh in range(2)]

        def step(j, carry, masked):
            off = pl.multiple_of(j * tk, tk)
            out = []
            for hh, sl in enumerate(heads):
                m, l, acc = carry[hh]
                k = k_ref[pl.ds(off, tk), sl].astype(BF16)
                v = v_ref[pl.ds(off, tk), sl].astype(BF16)
                ckv = ck_ref[0, hh:hh + 1, pl.ds(off, tk)] - c0s[hh]
                s = _fox_scores(qs[hh], k, ckv, i * tq, off) if masked else _fox_scores(qs[hh], k, ckv)
                m_new = jnp.maximum(m, jnp.max(s, axis=-1, keepdims=True))
                pr = jnp.exp(s - m_new)
                al = jnp.exp(m - m_new)
                l = al * l + jnp.sum(pr, axis=-1, keepdims=True)
                acc = al * acc + _dg(pr, v, False, False, "bf")
                out.append((m_new, l, acc))
            return tuple(out)

        one = (jnp.full((tq, 1), NEG, F32), jnp.zeros((tq, 1), F32), jnp.zeros((tq, HEAD), F32))
        carry = lax.fori_loop(0, nfull, lambda j, c: step(j, c, False), (one, one))
        for hh, (m, l, acc) in enumerate(step(nfull, carry, True)):
            o_ref[:, heads[hh]] = acc / l
            lse_ref[0, :, hh:hh + 1] = m + jnp.log(l)

    return _pcall(body, name="fox_fwd", grid=(npair, T // tq),
                  in_specs=[pl.BlockSpec((tq, LANE), lambda h, i: (i, cb("fq") + h)),
                            pl.BlockSpec((T, LANE), lambda h, i: (0, cb("fk") + h)),
                            pl.BlockSpec((T, LANE), lambda h, i: (0, cb("fv") + h)),
                            pl.BlockSpec((1, 8, T), lambda h, i: (h, 0, 0))],
                  out_specs=[pl.BlockSpec((tq, LANE), lambda h, i: (i, h)),
                             pl.BlockSpec((1, tq, 8), lambda h, i: (h, i, 0))],
                  out_shape=[jax.ShapeDtypeStruct((T, C), F32), jax.ShapeDtypeStruct((npair, T, 8), F32)],
                  compiler_params=_cp(("arbitrary", "arbitrary")))(p, p, p, ck)


def _fox_delta(p, ck, do, lse, C, offs):
    T = p.shape[0]
    tq, tk, shift = _fox_tiles(T)
    npair = C // LANE
    cb = lambda name: offs[name] // LANE

    def body(q_ref, k_ref, v_ref, ck_ref, do_ref, lse_ref, d_ref):
        i = pl.program_id(1)
        nfull = i >> shift
        d_ref[...] = jnp.zeros_like(d_ref)
        heads = [slice(hh * HEAD, (hh + 1) * HEAD) for hh in range(2)]
        qs = [(q_ref[:, sl] * FOX_SCALE).astype(BF16) for sl in heads]
        dos = [do_ref[:, sl].astype(BF16) for sl in heads]
        lses = [lse_ref[0, :, hh:hh + 1] for hh in range(2)]
        c0s = [_fox_c0(ck_ref, hh, i * tq) for hh in range(2)]

        def step(j, accs, masked):
            off = pl.multiple_of(j * tk, tk)
            out = []
            for hh, sl in enumerate(heads):
                k = k_ref[pl.ds(off, tk), sl].astype(BF16)
                v = v_ref[pl.ds(off, tk), sl].astype(BF16)
                ckv = ck_ref[0, hh:hh + 1, pl.ds(off, tk)] - c0s[hh]
                s = _fox_scores(qs[hh], k, ckv, i * tq, off) if masked else _fox_scores(qs[hh], k, ckv)
                pr = jnp.exp(s - lses[hh])
                dp = _dg(dos[hh], v, False, True, "bf")
                out.append(accs[hh] + jnp.sum(pr * dp, axis=-1, keepdims=True))
            return tuple(out)

        z = jnp.zeros((tq, 1), F32)
        accs = lax.fori_loop(0, nfull, lambda j, c: step(j, c, False), (z, z))
        for hh, acc in enumerate(step(nfull, accs, True)):
            d_ref[0, :, hh:hh + 1] = acc

    return _pcall(body, name="fox_delta", grid=(npair, T // tq),
                  in_specs=[pl.BlockSpec((tq, LANE), lambda h, i: (i, cb("fq") + h)),
                            pl.BlockSpec((T, LANE), lambda h, i: (0, cb("fk") + h)),
                            pl.BlockSpec((T, LANE), lambda h, i: (0, cb("fv") + h)),
                            pl.BlockSpec((1, 8, T), lambda h, i: (h, 0, 0)),
                            pl.BlockSpec((tq, LANE), lambda h, i: (i, h)),
                            pl.BlockSpec((1, tq, 8), lambda h, i: (h, i, 0))],
                  out_specs=pl.BlockSpec((1, tq, 8), lambda h, i: (h, i, 0)),
                  out_shape=jax.ShapeDtypeStruct((npair, T, 8), F32),
                  compiler_params=_cp(("arbitrary", "arbitrary")))(p, p, p, ck, do, lse)


def _fox_bwd(p, ck, delta, do, lse, C, offs):
    T = p.shape[0]
    tq, tk, shift = _fox_tiles(T)
    ratio = tk // tq
    nq = T // tq
    npair = C // LANE
    cb = lambda name: offs[name] // LANE

    def body(q_ref, k_ref, v_ref, ck_ref, ckall_ref, dl_ref, do_ref, lse_ref, dq_ref, dk_ref, dv_ref, dck_ref):
        j = pl.program_id(1)

        @pl.when(j == 0)
        def _():
            dq_ref[...] = jnp.zeros_like(dq_ref)

        dck_ref[...] = jnp.zeros_like(dck_ref)
        heads = [slice(hh * HEAD, (hh + 1) * HEAD) for hh in range(2)]
        ks = [k_ref[:, sl].astype(BF16) for sl in heads]
        vs = [v_ref[:, sl].astype(BF16) for sl in heads]
        cks = [ck_ref[0, hh:hh + 1, :] for hh in range(2)]

        def step(i, carry, masked):
            off = pl.multiple_of(i * tq, tq)
            out = []
            for hh, sl in enumerate(heads):
                dk, dv, dck = carry[hh]
                q = (q_ref[pl.ds(off, tq), sl] * FOX_SCALE).astype(BF16)
                dov = do_ref[pl.ds(off, tq), sl]
                lsev = lse_ref[0, pl.ds(off, tq), hh:hh + 1]
                ckv = cks[hh] - _fox_c0(ckall_ref, hh, off)
                s = _fox_scores(q, ks[hh], ckv, off, j * tk) if masked else _fox_scores(q, ks[hh], ckv)
                pr = jnp.exp(s - lsev)
                dv = dv + _dg(pr, dov, True, False, "bf")
                dp = _dg(dov, vs[hh], False, True, "bf")
                ds = pr * (dp - dl_ref[0, pl.ds(off, tq), hh:hh + 1])
                dk = dk + _dg(ds, q, True, False, "bf")
                dq_ref[pl.ds(off, tq), sl] += _dg(ds, ks[hh], False, False, "bf") * FOX_SCALE
                out.append((dk, dv, dck - jnp.sum(ds, axis=0, keepdims=True)))
            return tuple(out)

        z = jnp.zeros((tk, HEAD), F32)
        carry = ((z, z, jnp.zeros((1, tk), F32)),) * 2
        for r in range(ratio):
            carry = step(j * ratio + r, carry, True)
        carry = lax.fori_loop((j + 1) * ratio, nq, lambda i, c: step(i, c, False), carry)
        for hh, (dk, dv, dck) in enumerate(carry):
            dk_ref[:, heads[hh]] = dk
            dv_ref[:, heads[hh]] = dv
            dck_ref[0, hh:hh + 1, :] = dck

    full = lambda h, j: (0, h)
    return _pcall(body, name="fox_bwd", grid=(npair, T // tk),
                  in_specs=[pl.BlockSpec((T, LANE), lambda h, j: (0, cb("fq") + h)),
                            pl.BlockSpec((tk, LANE), lambda h, j: (j, cb("fk") + h)),
                            pl.BlockSpec((tk, LANE), lambda h, j: (j, cb("fv") + h)),
                            pl.BlockSpec((1, 8, tk), lambda h, j: (h, 0, j)),
                            pl.BlockSpec((1, 8, T), lambda h, j: (h, 0, 0)),
                            pl.BlockSpec((1, T, 8), lambda h, j: (h, 0, 0)), pl.BlockSpec((T, LANE), full),
                            pl.BlockSpec((1, T, 8), lambda h, j: (h, 0, 0))],
                  out_specs=[pl.BlockSpec((T, LANE), full),
                             pl.BlockSpec((tk, LANE), lambda h, j: (j, h)),
                             pl.BlockSpec((tk, LANE), lambda h, j: (j, h)),
                             pl.BlockSpec((1, 8, tk), lambda h, j: (h, 0, j))],
                  out_shape=[jax.ShapeDtypeStruct((T, C), F32)] * 3 + [jax.ShapeDtypeStruct((npair, 8, T), F32)],
                  compiler_params=_cp(("arbitrary", "arbitrary")))(p, p, p, ck, ck, delta, do, lse)


@jax.custom_vjp
def _tri_sum(tri, x):
    return _dg(tri, x, False, False, "e3")


def _tri_sum_fwd(tri, x):
    return _dg(tri, x, False, False, "e3"), tri


def _tri_sum_bwd(tri, g):
    return jnp.zeros_like(tri), _dg(tri, g, True, False, "e3")


_tri_sum.defvjp(_tri_sum_fwd, _tri_sum_bwd)


@jax.custom_vjp
def _solve(a, rhs):
    return _solve_fwd(a, rhs)[0]


def _solve_fwd(a, rhs):
    powers = [a]
    for _ in range(CHUNK.bit_length() - 2):
        powers.append(_dg(powers[-1], powers[-1], False, False, "x3"))
    u = rhs
    for pw in powers:
        u = u + _dg(pw, u, False, False, "x3")
    return u, (powers, u)


def _solve_bwd(res, g):
    powers, u = res
    w = g
    for pw in powers:
        w = w + _dg(pw, w, True, False, "x3")
    return _dg(w, u, False, True, "x3"), w


_solve.defvjp(_solve_fwd, _solve_bwd)


def _chunk_fn(S0, r, lw, k, v, a, b):
    nh, n = r.shape[0], r.shape[1]
    row = lax.broadcasted_iota(jnp.int32, (nh, n, n), 1)
    col = lax.broadcasted_iota(jnp.int32, (nh, n, n), 2)
    incl, strict = col <= row, col < row
    mm = lambda x, y, ta=False, tb=False: _mm(x, y, ta, tb, "x3")
    g = _tri_sum(incl.astype(F32), lw)
    einv = jnp.exp(-g)
    rt, at, bt, kt = r * jnp.exp(g), a * jnp.exp(g - lw), b * einv, k * einv
    a_ab = jnp.where(strict, mm(at, bt, tb=True), 0.0)
    a_ak = jnp.where(strict, mm(at, kt, tb=True), 0.0)
    r_b = jnp.where(incl, mm(rt, bt, tb=True), 0.0)
    r_k = jnp.where(incl, mm(rt, kt, tb=True), 0.0)
    u = _solve(a_ab, mm(at, S0, tb=True) + mm(a_ak, v))
    y = mm(rt, S0, tb=True) + mm(r_b, u) + mm(r_k, v)
    g_end = jnp.sum(lw, axis=1, keepdims=True)
    s_end = (S0 + mm(u, bt, ta=True) + mm(v, kt, ta=True)) * jnp.exp(g_end)
    return y, s_end


SCAN_HEADS = 24
SCAN_ROWS = 64


def _scan_group(C):
    nh = SCAN_HEADS
    while C % (nh * HEAD):
        nh -= 2
    return nh, nh * HEAD


def _scan_fwd(r, lw, k, v, a, b):
    T, C = r.shape
    tc = _tile(T, SCAN_ROWS)
    ncs = tc // CHUNK
    nh, GW = _scan_group(C)
    ngroup = C // GW

    def body(r_ref, lw_ref, k_ref, v_ref, a_ref, b_ref, y_ref, ck_ref, state):
        @pl.when(pl.program_id(1) == 0)
        def _():
            state[...] = jnp.zeros_like(state)

        heads = [slice(hh * HEAD, (hh + 1) * HEAD) for hh in range(nh)]
        split = lambda ref, rows: jnp.stack([ref[rows, sl] for sl in heads])
        st = split(state, slice(None))
        for c in range(ncs):
            rows = slice(c * CHUNK, (c + 1) * CHUNK)
            for hh, sl in enumerate(heads):
                ck_ref[0, c, :, sl] = st[hh]
            y, st = _chunk_fn(st, *[split(ref, rows) for ref in (r_ref, lw_ref, k_ref, v_ref, a_ref, b_ref)])
            for hh, sl in enumerate(heads):
                y_ref[rows, sl] = y[hh]
        for hh, sl in enumerate(heads):
            state[:, sl] = st[hh]

    spec = pl.BlockSpec((tc, GW), lambda h, t: (t, h))
    return _pcall(body, name="rwkv_scan_fwd", grid=(ngroup, T // tc),
                  in_specs=[spec] * 6,
                  out_specs=[spec, pl.BlockSpec((1, ncs, HEAD, GW), lambda h, t: (h, t, 0, 0))],
                  out_shape=[jax.ShapeDtypeStruct((T, C), F32),
                             jax.ShapeDtypeStruct((ngroup, T // CHUNK, HEAD, GW), F32)],
                  scratch_shapes=[pltpu.VMEM((HEAD, GW), F32)],
                  compiler_params=_cp(("arbitrary", "arbitrary")))(r, lw, k, v, a, b)


def _scan_bwd(r, lw, k, v, a, b, ckpt, dy):
    T, C = r.shape
    tc = _tile(T, SCAN_ROWS)
    ncs = tc // CHUNK
    nh, GW = _scan_group(C)
    ngroup = C // GW
    nt = T // tc

    def body(r_ref, lw_ref, k_ref, v_ref, a_ref, b_ref, ck_ref, dy_ref, dr, dlw, dk, dv, da, db, dstate):
        @pl.when(pl.program_id(1) == 0)
        def _():
            dstate[...] = jnp.zeros_like(dstate)

        outs = (dr, dlw, dk, dv, da, db)
        heads = [slice(hh * HEAD, (hh + 1) * HEAD) for hh in range(nh)]
        split = lambda ref, rows: jnp.stack([ref[rows, sl] for sl in heads])
        dst = split(dstate, slice(None))
        for c in reversed(range(ncs)):
            rows = slice(c * CHUNK, (c + 1) * CHUNK)
            s0 = jnp.stack([ck_ref[0, c, :, sl] for sl in heads])
            _, vjp = jax.vjp(_chunk_fn, s0, *[split(ref, rows) for ref in (r_ref, lw_ref, k_ref, v_ref, a_ref, b_ref)])
            grads = vjp((split(dy_ref, rows), dst))
            dst = grads[0]
            for o, gval in zip(outs, grads[1:]):
                for hh, sl in enumerate(heads):
                    o[rows, sl] = gval[hh]
        for hh, sl in enumerate(heads):
            dstate[:, sl] = dst[hh]

    spec = pl.BlockSpec((tc, GW), lambda h, t: (nt - 1 - t, h))
    return _pcall(body, name="rwkv_scan_bwd", grid=(ngroup, nt),
                  in_specs=[spec] * 6 + [pl.BlockSpec((1, ncs, HEAD, GW), lambda h, t: (h, nt - 1 - t, 0, 0)), spec],
                  out_specs=[spec] * 6, out_shape=[jax.ShapeDtypeStruct((T, C), F32)] * 6,
                  scratch_shapes=[pltpu.VMEM((HEAD, GW), F32)],
                  compiler_params=_cp(("arbitrary", "arbitrary")))(r, lw, k, v, a, b, ckpt, dy)


def _adam(w, g, m, v):
    m = ADAM_B1 * m + (1.0 - ADAM_B1) * g
    v = ADAM_B2 * v + (1.0 - ADAM_B2) * (g * g)
    m_hat = m / (1.0 - ADAM_B1 ** ADAM_STEP)
    v_hat = v / (1.0 - ADAM_B2 ** ADAM_STEP)
    return -ADAM_LR * (m_hat / (jnp.sqrt(v_hat) + ADAM_EPS) + ADAM_WD * w), m, v


def _sum_adam(parts, w, m, v, name):
    n, R, W = parts.shape
    tr = _tile(R, max(8, min(256, (1 << 20) // (n * W))))

    def body(p_ref, w_ref, m_ref, v_ref, g_ref, d_ref, nm_ref, nv_ref):
        g = p_ref[0].astype(F32)
        for s in range(1, n):
            g = g + p_ref[s].astype(F32)
        d, nm, nv = _adam(w_ref[...], g, m_ref[...], v_ref[...])
        g_ref[...] = g
        d_ref[...] = d
        nm_ref[...] = nm
        nv_ref[...] = nv

    return _pcall(body, name=name, grid=(R // tr,),
                  in_specs=[pl.BlockSpec((n, tr, W), lambda i: (0, i, 0))] + [_row_spec(tr, W)] * 3,
                  out_specs=[_row_spec(tr, W)] * 4, out_shape=[jax.ShapeDtypeStruct((R, W), F32)] * 4,
                  compiler_params=_cp(("arbitrary",)))(parts, w, m, v)


def _pad_lanes(vec, width):
    return jnp.pad(vec, ((0, 0), (0, width - vec.shape[1])))


def _logical_cols(blocks, lo, hi):
    B, out = blocks.shape[2], []
    while lo < hi:
        j, o = divmod(lo, B)
        n = min(hi - lo, B - o)
        out.append(blocks[j, :, o:o + n])
        lo += n
    return out


def kernel(x, mem, g_pre, w_in, mu_rwkv, w0, w_decay_up, a0, w_iclr_up, k_k, k_a, r_k, ln_x_w, ln_x_b, b_f, g_mem, w_mem_kv, w_out, g_post, loss_target, m_g_pre, m_w_in, m_mu_rwkv, m_w0, m_w_decay_up, m_a0, m_w_iclr_up, m_k_k, m_k_a, m_r_k, m_ln_x_w, m_ln_x_b, m_b_f, m_g_mem, m_w_mem_kv, m_w_out, m_g_post, v_g_pre, v_w_in, v_mu_rwkv, v_w0, v_w_decay_up, v_a0, v_w_iclr_up, v_k_k, v_k_a, v_r_k, v_ln_x_w, v_ln_x_b, v_b_f, v_g_mem, v_w_mem_kv, v_w_out, v_g_post):
    T, D = x.shape[1], x.shape[2]
    C = w0.shape[1]
    L = w_decay_up.shape[1]
    H = C // HEAD
    MW = w_mem_kv.shape[2] // 2
    SH = 3 * C + 2 * L
    IN = NDEV * w_in.shape[2]
    assert IN == SH + 5 * C + H + 2 * MW and D == 2 * C + MW and H % 2 == 0 and H <= LANE
    assert C % LANE == 0 and L % LANE == 0 and MW % (MEM_HEADS * HEAD) == 0 and T % CHUNK == 0
    offs = dict(grw=SH, fq=SH + C, fk=SH + 2 * C, fv=SH + 3 * C, gfx=SH + 4 * C, mq=SH + 5 * C, gmq=SH + 5 * C + MW,
                fl=SH + 5 * C + 2 * MW)
    NI = -(-(offs["fl"] + LANE) // 1024) * 1024
    l_fl = SH + 4 * C

    x2, mem2, tgt2 = x[0], mem[0], loss_target[0]

    wg = _exchange(w_in[0].astype(BF16), "gather_w_in", True)
    w_perm = jnp.concatenate(_logical_cols(wg, 0, l_fl) + _logical_cols(wg, l_fl + H, IN)
                             + _logical_cols(wg, l_fl, l_fl + H) + [jnp.zeros((D, NI - IN), BF16)], axis=1)

    h = _rmsnorm_fwd(x2, g_pre, "rmsnorm_pre")
    p, w_out_f, w_kv_f, lora = _matmul(
        h, w_perm, False, False, "in_proj",
        riders=[(w_out[0].astype(BF16), True), (w_mem_kv[0].astype(BF16), True),
                (jnp.concatenate([w_decay_up[0], w_iclr_up[0]], axis=0), True)])
    w_out_f, w_kv_f = w_out_f.reshape(D, D), w_kv_f.reshape(D, 2 * MW)
    lora = jnp.transpose(lora, (1, 0, 2)).reshape(2 * L, C)
    wdu_f, wiu_f = lora[:L], lora[L:]

    E, ET = _head_indicator(C, LANE)
    prep_params = [mu_rwkv, w0, a0, k_k, k_a, wdu_f, wiu_f, E, ET]
    mix_params = [ln_x_w, ln_x_b, r_k.reshape(1, C), E, ET]
    b_f_pad = _pad_lanes(b_f, LANE)
    ps = p[:, :SH]
    col = lambda name, w: p[:, offs[name]:offs[name] + w]
    g_rwkv, g_fox, mq, g_mq = col("grw", C), col("gfx", C), col("mq", MW), col("gmq", MW)
    prev = jnp.concatenate([jnp.zeros((1, SH), F32), ps[:-1]], axis=0)
    r, lw, kmod, v, a, b = _prep_fwd(p, prev, prep_params, C, L)
    y_scan, ckpt = _scan_fwd(r, lw, kmod, v, a, b)

    cum, cum_t = _fox_cum_fwd(p, b_f_pad, offs["fl"])
    ck = jnp.pad(cum_t[:H].reshape(H // 2, 2, T), ((0, 0), (0, 6), (0, 0)))
    y_fox, lse = _fox_fwd(p, ck, C, offs)

    memn = _rmsnorm_fwd(mem2, g_mem, "rmsnorm_mem")
    mkv = _matmul(memn, w_kv_f, False, False, "mem_kv_proj")
    y_mem = _memattn_fwd(mq, mkv, MW)

    acts = [y_scan, r, kmod, v, g_rwkv, y_fox, g_fox, y_mem, g_mq]
    ycat = _mix_fwd(acts, mix_params, C, MW)
    yo = _matmul(ycat, w_out_f, False, False, "out_proj")
    d_yo, d_out, loss_part, dg_post = _post(yo, x2, tgt2, g_post)
    loss = lax.psum(loss_part[0, 0], AXES)

    g_w_out = _matmul(ycat, d_yo, True, False, "grad_w_out", BF16)
    d_ycat = _matmul(d_yo, w_out_f, False, True, "d_ycat")
    (d_y, d_r1, d_k1, d_v1, d_grw, d_yfox, d_gfx, d_ymem, d_gmq, dg_lnw, dg_lnb, dg_rk) = _mix_bwd(
        acts, mix_params, d_ycat, C, MW)

    d_mq, d_mkv = _memattn_bwd(mq, mkv, d_ymem, MW)
    g_w_kv = _matmul(memn, d_mkv, True, False, "grad_w_mem_kv", BF16)
    d_memn = _matmul(d_mkv, w_kv_f, False, True, "d_memn")
    _, dg_mem = _rmsnorm_bwd(mem2, g_mem, d_memn, None, "rmsnorm_mem_bwd")

    delta = _fox_delta(p, ck, d_yfox, lse, C, offs)
    d_fq, d_fk, d_fv, d_ck = _fox_bwd(p, ck, delta, d_yfox, lse, C, offs)
    d_cum = _pad_lanes(d_ck[:, :2, :].reshape(H, T).T, LANE)
    d_fl, dg_bf = _fox_cum_bwd(p, b_f_pad, d_cum, offs["fl"])

    d_r, d_lw, d_k, d_v, d_a, d_b = _scan_bwd(r, lw, kmod, v, a, b, ckpt, d_y)
    cts = [d_r, d_lw, d_k, d_v, d_a, d_b, d_r1, d_k1, d_v1]
    (d_ps, d_prev, dg_mu, dg_w0, dg_a0, dg_kk, dg_ka, dg_wdu, dg_wiu) = _prep_bwd(p, prev, prep_params, cts, C, L)
    d_prev_up = jnp.concatenate([d_prev[1:], jnp.zeros((1, SH), F32)], axis=0)
    d_sh = _shift_combine(d_ps, d_prev_up)

    tobf = lambda z: z.astype(BF16)
    dp = jnp.concatenate([d_sh, tobf(d_grw), tobf(d_fq), tobf(d_fk), tobf(d_fv), tobf(d_gfx), tobf(d_mq), tobf(d_gmq),
                          tobf(d_fl), jnp.zeros((T, NI - offs["fl"] - LANE), BF16)], axis=1)
    g_lora = jnp.concatenate([dg_wdu, dg_wiu], axis=0)
    g_w_perm, parts_out, parts_kv, parts_lora = _matmul(
        h, dp, True, False, "grad_w_in", BF16,
        riders=[(g_w_out.reshape(NDEV, D // NDEV, D), False), (g_w_kv.reshape(NDEV, D // NDEV, 2 * MW), False),
                (jnp.transpose(g_lora.reshape(2 * L, NDEV, C // NDEV), (1, 0, 2)), False)])

    def internal_cols(lo, hi):
        out = []
        for first, last, shift in ((0, l_fl, 0), (l_fl, l_fl + H, offs["fl"] - l_fl), (l_fl + H, IN, -H)):
            s0, s1 = max(lo, first), min(hi, last)
            if s0 < s1:
                out.append(g_w_perm[:, s0 + shift:s1 + shift])
        return out

    blk = IN // NDEV
    g_blocks = jnp.stack([jnp.concatenate(internal_cols(j * blk, (j + 1) * blk), axis=1) for j in range(NDEV)])
    d_h, parts_in = _matmul(dp, w_perm, False, True, "d_h", tk=5120, riders=[(g_blocks, False)])
    grad_x, dg_pre = _rmsnorm_bwd(x2, g_pre, d_h, d_out, "rmsnorm_pre_bwd")

    gw_in, dw_in, nm_w_in, nv_w_in = _sum_adam(parts_in, w_in[0], m_w_in[0], v_w_in[0], "adam_w_in")
    gw_out, dw_out, nm_w_out, nv_w_out = _sum_adam(parts_out, w_out[0], m_w_out[0], v_w_out[0], "adam_w_out")
    gw_kv, dw_kv, nm_w_kv, nv_w_kv = _sum_adam(parts_kv, w_mem_kv[0], m_w_mem_kv[0], v_w_mem_kv[0], "adam_w_mem_kv")
    cat2 = lambda u, w_: jnp.concatenate([u[0], w_[0]], axis=0)
    lora_res = _sum_adam(parts_lora, cat2(w_decay_up, w_iclr_up), cat2(m_w_decay_up, m_w_iclr_up),
                         cat2(v_w_decay_up, v_w_iclr_up), "adam_lora")

    small = [("g_pre", g_pre, m_g_pre, v_g_pre, dg_pre), ("mu_rwkv", mu_rwkv, m_mu_rwkv, v_mu_rwkv, dg_mu),
             ("w0", w0, m_w0, v_w0, dg_w0), ("a0", a0, m_a0, v_a0, dg_a0), ("k_k", k_k, m_k_k, v_k_k, dg_kk),
             ("k_a", k_a, m_k_a, v_k_a, dg_ka), ("r_k", r_k.reshape(1, C), m_r_k.reshape(1, C), v_r_k.reshape(1, C), dg_rk),
             ("ln_x_w", ln_x_w, m_ln_x_w, v_ln_x_w, dg_lnw), ("ln_x_b", ln_x_b, m_ln_x_b, v_ln_x_b, dg_lnb),
             ("b_f", _pad_lanes(b_f, LANE), _pad_lanes(m_b_f, LANE), _pad_lanes(v_b_f, LANE), dg_bf),
             ("g_mem", g_mem, m_g_mem, v_g_mem, dg_mem), ("g_post", g_post, m_g_post, v_g_post, dg_post)]
    widths = [s[1].shape[1] for s in small]
    pack = lambda idx: jnp.concatenate([s[idx] for s in small], axis=1).reshape(-1, LANE)
    parts_small = _exchange(pack(4), "gather_small_grads", True)
    res_small = _sum_adam(parts_small, pack(1), pack(2), pack(3), "adam_small")

    def unpack(flat):
        flat = flat.reshape(1, -1)
        out, o = {}, 0
        for (name, *_), wd in zip(small, widths):
            out[name] = flat[:, o:o + wd]
            o += wd
        out["b_f"] = out["b_f"][:, :H]
        out["r_k"] = out["r_k"].reshape(1, H, HEAD)
        return out

    sg, sd, sm, sv = [unpack(z) for z in res_small]
    big = {"w_in": (gw_in, dw_in, nm_w_in, nv_w_in), "w_out": (gw_out, dw_out, nm_w_out, nv_w_out),
           "w_mem_kv": (gw_kv, dw_kv, nm_w_kv, nv_w_kv),
           "w_decay_up": tuple(z[:L] for z in lora_res), "w_iclr_up": tuple(z[L:] for z in lora_res)}
    order = ["g_pre", "w_in", "mu_rwkv", "w0", "w_decay_up", "a0", "w_iclr_up", "k_k", "k_a", "r_k", "ln_x_w", "ln_x_b",
             "b_f", "g_mem", "w_mem_kv", "w_out", "g_post"]

    def pick(name, idx):
        if name in big:
            return big[name][idx][None]
        return (sg, sd, sm, sv)[idx][name]

    outs = [loss, grad_x[None]]
    for idx in range(4):
        outs += [pick(n, idx) for n in order]
    return tuple(outs)
```

```python
import functools

import jax
import jax.numpy as jnp
from jax import lax
from jax.experimental import pallas as pl
from jax.experimental.pallas import tpu as pltpu

F32, BF16 = jnp.float32, jnp.bfloat16
HI = lax.Precision.HIGHEST
NDEV = 8
AXES = ("x", "y", "c")
HEAD = 64
CHUNK = 64
MEM_HEADS = 4
LANE = 128
RMS_EPS = 1e-6
GN_EPS = 64e-5
NEG = -1e30
ADAM_LR, ADAM_B1, ADAM_B2, ADAM_EPS, ADAM_WD, ADAM_STEP = 0.001, 0.9, 0.999, 1e-08, 0.01, 10
VMEM_LIMIT = 56 * 1024 * 1024


def _pcall(body, **kw):
    return pl.pallas_call(body, **kw)


def _cp(sem=None, vmem=VMEM_LIMIT):
    return pltpu.CompilerParams(dimension_semantics=sem, vmem_limit_bytes=vmem)


def _tile(n, pref):
    for t in (pref, 1024, 512, 256, 128, 64, 32, 16, 8):
        if t <= pref and n % t == 0:
            return t
    return n


def _dg(a, b, ta, tb, mode):
    nb = a.ndim - 2
    ca = nb + (0 if ta else 1)
    cb = nb + (1 if tb else 0)
    dims = (((ca,), (cb,)), (tuple(range(nb)), tuple(range(nb))))
    if mode == "x3":
        a_hi, b_hi = a.astype(BF16), b.astype(BF16)
        a_lo, b_lo = (a - a_hi.astype(F32)).astype(BF16), (b - b_hi.astype(F32)).astype(BF16)
        a3 = jnp.concatenate([a_hi, a_lo, a_hi], axis=ca)
        b3 = jnp.concatenate([b_hi, b_hi, b_lo], axis=cb)
        return lax.dot_general(a3, b3, dims, preferred_element_type=F32)
    if mode == "e3":
        b1 = b.astype(BF16)
        rest = b - b1.astype(F32)
        b2 = rest.astype(BF16)
        b3 = (rest - b2.astype(F32)).astype(BF16)
        a = a.astype(BF16)
        return lax.dot_general(jnp.concatenate([a, a, a], axis=ca), jnp.concatenate([b1, b2, b3], axis=cb), dims,
                               preferred_element_type=F32)
    if mode == "x2":
        a_hi, b = a.astype(BF16), b.astype(BF16)
        a_lo = (a - a_hi.astype(F32)).astype(BF16)
        dot = lambda u: lax.dot_general(u, b, dims, preferred_element_type=F32)
        return dot(a_hi) + dot(a_lo)
    if mode == "bf":
        a, b, prec = a.astype(BF16), b.astype(BF16), None
    else:
        prec = HI
    return lax.dot_general(a, b, dims, preferred_element_type=F32, precision=prec)


@functools.partial(jax.custom_vjp, nondiff_argnums=(2, 3, 4))
def _mm(a, b, ta, tb, mode):
    return _dg(a, b, ta, tb, mode)


def _mm_fwd(a, b, ta, tb, mode):
    return _dg(a, b, ta, tb, mode), (a, b)


def _mm_bwd(ta, tb, mode, res, g):
    a, b = res
    da = _dg(g, b, False, not tb, mode) if not ta else _dg(b, g, tb, True, mode)
    db = _dg(a, g, not ta, False, mode) if not tb else _dg(g, a, True, ta, mode)
    return da, db


_mm.defvjp(_mm_fwd, _mm_bwd)


@jax.custom_vjp
def _seg(a, e):
    return _dg(a, e, False, False, "x2")


def _seg_fwd(a, e):
    return _dg(a, e, False, False, "x2"), e


def _seg_bwd(e, g):
    return _dg(g, e, False, True, "x2"), jnp.zeros_like(e)


_seg.defvjp(_seg_fwd, _seg_bwd)


def _sigmoid(z):
    return 1.0 / (1.0 + jnp.exp(-z))


def _softplus(z):
    return jnp.maximum(z, 0.0) + jnp.log(1.0 + jnp.exp(-jnp.abs(z)))


def _silu(z):
    return z * _sigmoid(z)


def _rms(x, g):
    return x * lax.rsqrt(jnp.mean(x * x, axis=-1, keepdims=True) + RMS_EPS) * g


HBM_SPEC = pl.BlockSpec(memory_space=pltpu.HBM)
EXCHANGE_SEMS = [pltpu.SemaphoreType.DMA((NDEV - 1,)), pltpu.SemaphoreType.DMA((NDEV - 1,)), pltpu.SemaphoreType.DMA(())]


def _exchange_copies(gather, x_ref, o_ref, send_sems, recv_sems, local_sem, arrivals):
    ix, iy, ic = lax.axis_index("x"), lax.axis_index("y"), lax.axis_index("c")
    me = 4 * ix + 2 * iy + ic

    def src(dest):
        return x_ref if gather else x_ref.at[dest]

    mine = pltpu.make_async_copy(src(me), o_ref.at[me], local_sem)
    pairs = []
    for k in range(1, NDEV):
        px = 1 - ix if (k >> 2) & 1 else ix
        py = 1 - iy if (k >> 1) & 1 else iy
        pc = 1 - ic if k & 1 else ic
        peer = 4 * px + 2 * py + pc
        send = pltpu.make_async_remote_copy(
            src_ref=src(peer), dst_ref=o_ref.at[me], send_sem=send_sems.at[k - 1], recv_sem=recv_sems.at[k - 1],
            device_id=(px, py, pc), device_id_type=pl.DeviceIdType.MESH)
        arrival = arrivals and pltpu.make_async_remote_copy(
            src_ref=src(peer), dst_ref=o_ref.at[peer], send_sem=send_sems.at[k - 1], recv_sem=recv_sems.at[k - 1],
            device_id=(ix, iy, ic), device_id_type=pl.DeviceIdType.MESH)
        pairs.append((send, arrival))
    return mine, pairs


def _exchange_start(*args):
    mine, pairs = _exchange_copies(*args, arrivals=False)
    mine.start()
    for send, _ in pairs:
        send.start()


def _exchange_wait(*args):
    mine, pairs = _exchange_copies(*args, arrivals=True)
    for send, arrival in pairs:
        send.wait_send()
        arrival.wait_recv()
    mine.wait()


def _exchange_shape(x, gather):
    return jax.ShapeDtypeStruct((NDEV,) + tuple(x.shape if gather else x.shape[1:]), x.dtype)


def _exchange(x, name, gather):
    def body(x_ref, o_ref, *sems):
        _exchange_start(gather, x_ref, o_ref, *sems)
        _exchange_wait(gather, x_ref, o_ref, *sems)

    return _pcall(body, name=name, out_shape=_exchange_shape(x, gather), in_specs=[HBM_SPEC], out_specs=HBM_SPEC,
                  scratch_shapes=list(EXCHANGE_SEMS))(x)


MAX_FULL_K = 4096


def _matmul(a, b, ta, tb, name, out_dtype=F32, tm=1024, tn=1024, tk=3072, riders=()):
    M, K = (a.shape[1], a.shape[0]) if ta else a.shape
    N = b.shape[0] if tb else b.shape[1]
    assert (b.shape[1] if tb else b.shape[0]) == K
    if K <= MAX_FULL_K:
        tk = K
    else:
        tm, tk = min(tm, 512), _tile(K, tk)
    tm, tn = _tile(M, tm), _tile(N, tn)
    grid = (M // tm, N // tn, K // tk)
    nk, nr = grid[2], len(riders)

    def body(*refs):
        a_ref, b_ref, x_refs = refs[0], refs[1], refs[2:2 + nr]
        o_ref, xo_refs, rest = refs[2 + nr], refs[3 + nr:3 + 2 * nr], refs[3 + 2 * nr:]
        sems = rest[1:] if nk > 1 else rest
        ids = [pl.program_id(d) for d in range(3)]
        jobs = [(riders[q][1], x_refs[q], xo_refs[q]) + tuple(sems[3 * q:3 * q + 3]) for q in range(nr)]

        if nr:
            @pl.when((ids[0] == 0) & (ids[1] == 0) & (ids[2] == 0))
            def _():
                for job in jobs:
                    _exchange_start(*job)

        if nk == 1:
            o_ref[...] = _dg(a_ref[...], b_ref[...], ta, tb, "bf").astype(o_ref.dtype)
        else:
            acc = rest[0]

            @pl.when(ids[2] == 0)
            def _():
                acc[...] = jnp.zeros_like(acc)

            acc[...] += _dg(a_ref[...], b_ref[...], ta, tb, "bf")

            @pl.when(ids[2] == nk - 1)
            def _():
                o_ref[...] = acc[...].astype(o_ref.dtype)

        if nr:
            @pl.when((ids[0] == grid[0] - 1) & (ids[1] == grid[1] - 1) & (ids[2] == nk - 1))
            def _():
                for job in jobs:
                    _exchange_wait(*job)

    a_spec = pl.BlockSpec((tk, tm), lambda i, j, k: (k, i)) if ta else pl.BlockSpec((tm, tk), lambda i, j, k: (i, k))
    b_spec = pl.BlockSpec((tn, tk), lambda i, j, k: (j, k)) if tb else pl.BlockSpec((tk, tn), lambda i, j, k: (k, j))
    out = _pcall(
        body, name=name, grid=grid,
        in_specs=[a_spec, b_spec] + [HBM_SPEC] * nr,
        out_specs=[pl.BlockSpec((tm, tn), lambda i, j, k: (i, j))] + [HBM_SPEC] * nr,
        out_shape=[jax.ShapeDtypeStruct((M, N), out_dtype)] + [_exchange_shape(x, g) for x, g in riders],
        scratch_shapes=([pltpu.VMEM((tm, tn), F32)] if nk > 1 else []) + list(EXCHANGE_SEMS) * nr,
        compiler_params=_cp(("arbitrary", "arbitrary", "arbitrary")),
    )(a, b, *[x for x, _ in riders])
    return tuple(out) if nr else out[0]


def _row_spec(tr, width, col_block=0):
    return pl.BlockSpec((tr, width), lambda i: (i, col_block))


def _full_spec(shape):
    nd = len(shape)
    return pl.BlockSpec(tuple(shape), lambda i: (0,) * nd)


def _rmsnorm_fwd(x, g, name, tr=256):
    R, D = x.shape
    tr = _tile(R, tr)

    def body(x_ref, g_ref, o_ref):
        o_ref[...] = _rms(x_ref[...], g_ref[...]).astype(BF16)

    return _pcall(body, name=name, grid=(R // tr,),
                  in_specs=[_row_spec(tr, D), _full_spec((1, D))], out_specs=_row_spec(tr, D),
                  out_shape=jax.ShapeDtypeStruct((R, D), BF16), compiler_params=_cp(("arbitrary",)))(x, g)


def _rmsnorm_bwd(x, g, dy, extra, name, tr=128):
    R, D = x.shape
    tr = _tile(R, tr)
    has_extra = extra is not None

    def body(*refs):
        if has_extra:
            x_ref, g_ref, dy_ref, e_ref, dx_ref, dg_ref = refs
        else:
            x_ref, g_ref, dy_ref, dx_ref, dg_ref = refs
        _, vjp = jax.vjp(_rms, x_ref[...], g_ref[...])
        dx, dg = vjp(dy_ref[...])
        dx_ref[...] = dx + e_ref[...] if has_extra else dx

        @pl.when(pl.program_id(0) == 0)
        def _():
            dg_ref[...] = jnp.zeros_like(dg_ref)

        dg_ref[...] += dg

    ins = [x, g, dy] + ([extra] if has_extra else [])
    specs = [_row_spec(tr, D), _full_spec((1, D)), _row_spec(tr, D)] + ([_row_spec(tr, D)] if has_extra else [])
    return _pcall(body, name=name, grid=(R // tr,), in_specs=specs,
                  out_specs=[_row_spec(tr, D), _full_spec((1, D))],
                  out_shape=[jax.ShapeDtypeStruct((R, D), F32), jax.ShapeDtypeStruct((1, D), F32)],
                  compiler_params=_cp(("arbitrary",)))(*ins)


def _head_indicator(C, hp):
    e = (jnp.arange(C)[:, None] // HEAD == jnp.arange(hp)[None, :]).astype(F32)
    return e, e.T


def _prep_fn(C, L, ps, prev, mu, w0, a0, k_k, k_a, wdu, wiu, E, ET):
    sh = ps + (prev - ps) * mu
    r, k, v = sh[:, :C], sh[:, C:2 * C], sh[:, 2 * C:3 * C]
    wl, al = sh[:, 3 * C:3 * C + L], sh[:, 3 * C + L:3 * C + 2 * L]
    wd = w0 + _mm(jnp.tanh(wl), wdu, False, False, "bf")
    w_pre = -_softplus(-wd) - 0.5
    lw = -jnp.exp(w_pre)
    alpha = _sigmoid(a0 + _mm(al, wiu, False, False, "bf"))
    kk = k * k_k
    ss = _seg(kk * kk, E)
    kk = kk * _seg(lax.rsqrt(jnp.maximum(ss, 1e-24)), ET)
    k_mod = k * (1.0 + (alpha - 1.0) * k_a)
    return r, lw, k_mod, v, -kk, kk * alpha


SUBLANES = 8


def _edge_spec(tr, width, nrows, before):
    per, last = tr // SUBLANES, nrows // SUBLANES - 1
    if before:
        return pl.BlockSpec((SUBLANES, width), lambda i: (jnp.maximum(i * per - 1, 0), 0))
    return pl.BlockSpec((SUBLANES, width), lambda i: (jnp.minimum((i + 1) * per, last), 0))


def _previous_rows(cur, before_ref):
    edge = before_ref[SUBLANES - 1:SUBLANES, :] * (pl.program_id(0) > 0).astype(F32)
    row = lax.broadcasted_iota(jnp.int32, cur.shape, 0)
    return jnp.where(row == 0, edge, pltpu.roll(cur, 1, 0))


def _prep_fwd(p, params, C, L, tr=128):
    T = p.shape[0]
    SH = 3 * C + 2 * L
    tr = _tile(T, tr)

    def body(ps_ref, before_ref, mu, w0, a0, kk_, ka_, wdu, wiu, E, ET, *outs):
        ps = ps_ref[...]
        vals = _prep_fn(C, L, ps, _previous_rows(ps, before_ref), mu[...], w0[...], a0[...], kk_[...], ka_[...],
                        wdu[...], wiu[...], E[...], ET[...])
        for o, v in zip(outs, vals):
            o[...] = v

    pspecs = [_full_spec(a.shape) for a in params]
    return _pcall(body, name="rwkv_prep_fwd", grid=(T // tr,),
                  in_specs=[_row_spec(tr, SH), _edge_spec(tr, SH, T, True)] + pspecs,
                  out_specs=[_row_spec(tr, C)] * 6,
                  out_shape=[jax.ShapeDtypeStruct((T, C), F32)] * 6,
                  compiler_params=_cp(("arbitrary",)))(p, p, *params)


def _prep_bwd(p, params, cts, C, L, tr=128):
    T = p.shape[0]
    SH = 3 * C + 2 * L
    tr = _tile(T, tr)
    nparam = 7

    def body(ps_ref, before_ref, mu, w0, a0, kk_, ka_, wdu, wiu, E, ET, c0, c1, c2, c3, c4, c5, e0, e2, e3,
             dps_ref, dprev_ref, *dpar):
        f = functools.partial(_prep_fn, C, L)
        fe = lambda ps, prev, *par: f(ps, prev, *par, E[...], ET[...])
        ps = ps_ref[...]
        _, vjp = jax.vjp(fe, ps, _previous_rows(ps, before_ref), mu[...], w0[...], a0[...], kk_[...], ka_[...], wdu[...], wiu[...])
        grads = vjp((c0[...] + e0[...], c1[...], c2[...] + e2[...], c3[...] + e3[...], c4[...], c5[...]))
        dps_ref[...] = grads[0]
        dprev_ref[...] = grads[1]

        @pl.when(pl.program_id(0) == 0)
        def _():
            for d in dpar:
                d[...] = jnp.zeros_like(d)

        for d, gval in zip(dpar, grads[2:]):
            d[...] += gval

    pspecs = [_full_spec(a.shape) for a in params]
    par_shapes = [a.shape for a in params[:nparam]]
    return _pcall(body, name="rwkv_prep_bwd", grid=(T // tr,),
                  in_specs=[_row_spec(tr, SH), _edge_spec(tr, SH, T, True)] + pspecs + [_row_spec(tr, C)] * 9,
                  out_specs=[_row_spec(tr, SH), _row_spec(tr, SH)] + [_full_spec(s) for s in par_shapes],
                  out_shape=[jax.ShapeDtypeStruct((T, SH), F32)] * 2 + [jax.ShapeDtypeStruct(s, F32) for s in par_shapes],
                  compiler_params=_cp(("arbitrary",)))(p, p, *params, *cts)


def _shift_combine(d_direct, d_prev, tr=256):
    T, W = d_direct.shape
    tr = _tile(T, tr)
    nt = T // tr

    def body(a_ref, b_ref, after_ref, o_ref):
        cur = b_ref[...]
        edge = after_ref[0:1, :] * (pl.program_id(0) < nt - 1).astype(F32)
        row = lax.broadcasted_iota(jnp.int32, cur.shape, 0)
        nxt = jnp.where(row == tr - 1, edge, pltpu.roll(cur, tr - 1, 0))
        o_ref[...] = (a_ref[...] + nxt).astype(BF16)

    return _pcall(body, name="shift_combine", grid=(nt,),
                  in_specs=[_row_spec(tr, W)] * 2 + [_edge_spec(tr, W, T, False)], out_specs=_row_spec(tr, W),
                  out_shape=jax.ShapeDtypeStruct((T, W), BF16), compiler_params=_cp(("arbitrary",)))(d_direct, d_prev, d_prev)


def _mix_fn(y, r, kmod, v, g_rwkv, yfox, g_fox, ymem, g_mq, lnw, lnb, rk, E, ET):
    inv = 1.0 / HEAD
    mean = _seg(y, E) * inv
    yc = y - _seg(mean, ET)
    var = _seg(yc * yc, E) * inv
    yn = yc * _seg(lax.rsqrt(var + GN_EPS), ET) * lnw + lnb
    bonus = _seg(_seg(r * kmod * rk, E), ET) * v
    o1 = (yn + bonus) * _silu(g_rwkv)
    return jnp.concatenate([o1, yfox * _silu(g_fox), ymem * _silu(g_mq)], axis=1)


def _mix_specs(tr, C, MW):
    return [_row_spec(tr, C)] * 7 + [_row_spec(tr, MW)] * 2


def _mix_fwd(acts, params, C, MW, tr=128):
    T = acts[0].shape[0]
    D = 2 * C + MW
    tr = _tile(T, tr)

    def body(y_, r_, k_, v_, g1, yf, g2, ym, g3, lnw, lnb, rk, E, ET, o_ref):
        o_ref[...] = _mix_fn(y_[...], r_[...], k_[...], v_[...], g1[...], yf[...], g2[...], ym[...], g3[...],
                             lnw[...], lnb[...], rk[...], E[...], ET[...]).astype(BF16)

    return _pcall(body, name="mix_fwd", grid=(T // tr,),
                  in_specs=_mix_specs(tr, C, MW) + [_full_spec(a.shape) for a in params],
                  out_specs=_row_spec(tr, D), out_shape=jax.ShapeDtypeStruct((T, D), BF16),
                  compiler_params=_cp(("arbitrary",)))(*acts, *params)


def _mix_bwd(acts, params, dycat, C, MW, tr=128):
    T = acts[0].shape[0]
    D = 2 * C + MW
    tr = _tile(T, tr)

    def body(y_, r_, k_, v_, g1, yf, g2, ym, g3, lnw, lnb, rk, E, ET, dy_ref, *outs):
        fe = lambda *a: _mix_fn(*a, E[...], ET[...])
        _, vjp = jax.vjp(fe, y_[...], r_[...], k_[...], v_[...], g1[...], yf[...], g2[...], ym[...], g3[...],
                         lnw[...], lnb[...], rk[...])
        grads = vjp(dy_ref[...])
        for o, gval in zip(outs[:9], grads[:9]):
            o[...] = gval

        @pl.when(pl.program_id(0) == 0)
        def _():
            for o in outs[9:]:
                o[...] = jnp.zeros_like(o)

        for o, gval in zip(outs[9:], grads[9:]):
            o[...] += gval

    widths = [C, C, C, C, C, C, C, MW, MW]
    return _pcall(body, name="mix_bwd", grid=(T // tr,),
                  in_specs=_mix_specs(tr, C, MW) + [_full_spec(a.shape) for a in params] + [_row_spec(tr, D)],
                  out_specs=[_row_spec(tr, w) for w in widths] + [_full_spec((1, C))] * 3,
                  out_shape=[jax.ShapeDtypeStruct((T, w), F32) for w in widths] + [jax.ShapeDtypeStruct((1, C), F32)] * 3,
                  compiler_params=_cp(("arbitrary",)))(*acts, *params, dycat)


def _post(yo, x, tgt, g_post, tr=128):
    T, D = x.shape
    tr = _tile(T, tr)

    def body(yo_ref, x_ref, t_ref, g_ref, dyo_ref, dout_ref, loss_ref, dg_ref):
        n, vjp = jax.vjp(_rms, yo_ref[...], g_ref[...])
        diff = (x_ref[...] + n) - t_ref[...]
        part = 0.5 * jnp.sum(jnp.mean(diff * diff, axis=-1, keepdims=True), axis=0, keepdims=True)
        d_out = diff * (1.0 / D)
        dyo, dg = vjp(d_out)
        dyo_ref[...] = dyo.astype(BF16)
        dout_ref[...] = d_out

        @pl.when(pl.program_id(0) == 0)
        def _():
            loss_ref[...] = jnp.zeros_like(loss_ref)
            dg_ref[...] = jnp.zeros_like(dg_ref)

        loss_ref[...] += jnp.broadcast_to(part, loss_ref.shape)
        dg_ref[...] += dg

    return _pcall(body, name="post_loss", grid=(T // tr,),
                  in_specs=[_row_spec(tr, D)] * 3 + [_full_spec((1, D))],
                  out_specs=[_row_spec(tr, D), _row_spec(tr, D), _full_spec((1, LANE)), _full_spec((1, D))],
                  out_shape=[jax.ShapeDtypeStruct((T, D), BF16), jax.ShapeDtypeStruct((T, D), F32),
                             jax.ShapeDtypeStruct((1, LANE), F32), jax.ShapeDtypeStruct((1, D), F32)],
                  compiler_params=_cp(("arbitrary",)))(yo, x, tgt, g_post)


def _memattn_fn(MW, q, mkv):
    hd = MW // MEM_HEADS
    scale = hd ** -0.5
    outs = []
    for h in range(MEM_HEADS):
        qh = q[:, h * hd:(h + 1) * hd]
        kh = mkv[:, h * hd:(h + 1) * hd]
        vh = mkv[:, MW + h * hd:MW + (h + 1) * hd]
        s = _mm(qh, kh, False, True, "bf") * scale
        e = jnp.exp(s - lax.stop_gradient(jnp.max(s, axis=-1, keepdims=True)))
        pr = e / jnp.sum(e, axis=-1, keepdims=True)
        outs.append(_mm(pr, vh, False, False, "bf"))
    return jnp.concatenate(outs, axis=1)


def _memattn_fwd(p, mkv, MW, tr=256):
    T = p.shape[0]
    tr = _tile(T, tr)

    def body(q_ref, kv_ref, o_ref):
        o_ref[...] = _memattn_fn(MW, q_ref[...], kv_ref[...])

    return _pcall(body, name="memattn_fwd", grid=(T // tr,),
                  in_specs=[_row_spec(tr, MW), _full_spec(mkv.shape)], out_specs=_row_spec(tr, MW),
                  out_shape=jax.ShapeDtypeStruct((T, MW), F32), compiler_params=_cp(("arbitrary",)))(p, mkv)


def _memattn_bwd(p, mkv, do, MW, tr=256):
    T = p.shape[0]
    tr = _tile(T, tr)

    def body(q_ref, kv_ref, do_ref, dq_ref, dkv_ref):
        _, vjp = jax.vjp(functools.partial(_memattn_fn, MW), q_ref[...], kv_ref[...])
        dq, dkv = vjp(do_ref[...])
        dq_ref[...] = dq

        @pl.when(pl.program_id(0) == 0)
        def _():
            dkv_ref[...] = jnp.zeros_like(dkv_ref)

        dkv_ref[...] += dkv

    return _pcall(body, name="memattn_bwd", grid=(T // tr,),
                  in_specs=[_row_spec(tr, MW), _full_spec(mkv.shape), _row_spec(tr, MW)],
                  out_specs=[_row_spec(tr, MW), _full_spec(mkv.shape)],
                  out_shape=[jax.ShapeDtypeStruct((T, MW), F32), jax.ShapeDtypeStruct(mkv.shape, F32)],
                  compiler_params=_cp(("arbitrary",)))(p, mkv, do)


def _fox_cum_fwd(p, b_f_pad, off, blk=512):
    T = p.shape[0]
    blk = _tile(T, blk)

    def body(f_ref, b_ref, cum_ref, cumt_ref, carry):
        @pl.when(pl.program_id(0) == 0)
        def _():
            carry[...] = jnp.zeros_like(carry)

        z = f_ref[...] + b_ref[...]
        logf = -_softplus(-z)
        row = lax.broadcasted_iota(jnp.int32, (blk, blk), 0)
        col = lax.broadcasted_iota(jnp.int32, (blk, blk), 1)
        tri = (col <= row).astype(F32)
        c = _dg(tri, logf, False, False, "hi") + carry[...]
        cum_ref[...] = c
        cumt_ref[...] = c.T
        carry[...] += jnp.sum(logf, axis=0, keepdims=True)

    return _pcall(body, name="fox_cum_fwd", grid=(T // blk,),
                  in_specs=[_row_spec(blk, LANE, off // LANE), _full_spec((1, LANE))],
                  out_specs=[_row_spec(blk, LANE), pl.BlockSpec((LANE, blk), lambda i: (0, i))],
                  out_shape=[jax.ShapeDtypeStruct((T, LANE), F32), jax.ShapeDtypeStruct((LANE, T), F32)],
                  scratch_shapes=[pltpu.VMEM((1, LANE), F32)], compiler_params=_cp(("arbitrary",)))(p, b_f_pad)


def _fox_cum_bwd(p, b_f_pad, dcum, off, blk=512):
    T = p.shape[0]
    blk = _tile(T, blk)
    nb = T // blk

    def body(f_ref, b_ref, dc_ref, df_ref, db_ref, carry):
        @pl.when(pl.program_id(0) == 0)
        def _():
            carry[...] = jnp.zeros_like(carry)
            db_ref[...] = jnp.zeros_like(db_ref)

        row = lax.broadcasted_iota(jnp.int32, (blk, blk), 0)
        col = lax.broadcasted_iota(jnp.int32, (blk, blk), 1)
        tri = (col >= row).astype(F32)
        dlogf = _dg(tri, dc_ref[...], False, False, "hi") + carry[...]
        carry[...] += jnp.sum(dc_ref[...], axis=0, keepdims=True)
        z = f_ref[...] + b_ref[...]
        dz = dlogf * (1.0 - _sigmoid(z))
        df_ref[...] = dz
        db_ref[...] += jnp.sum(dz, axis=0, keepdims=True)

    rev = lambda i: (nb - 1 - i, 0)
    return _pcall(body, name="fox_cum_bwd", grid=(nb,),
                  in_specs=[pl.BlockSpec((blk, LANE), lambda i: (nb - 1 - i, off // LANE)), _full_spec((1, LANE)),
                            pl.BlockSpec((blk, LANE), rev)],
                  out_specs=[pl.BlockSpec((blk, LANE), rev), _full_spec((1, LANE))],
                  out_shape=[jax.ShapeDtypeStruct((T, LANE), F32), jax.ShapeDtypeStruct((1, LANE), F32)],
                  scratch_shapes=[pltpu.VMEM((1, LANE), F32)], compiler_params=_cp(("arbitrary",)))(p, b_f_pad, dcum)


FOX_SCALE = HEAD ** -0.5
FOX_TQ, FOX_TK = 1024, 1024


def _fox_scores(q, k, ck, q0=None, k0=None):
    s = _dg(q, k, False, True, "bf") - ck
    if q0 is None:
        return s
    qpos = q0 + lax.broadcasted_iota(jnp.int32, s.shape, 0)
    kpos = k0 + lax.broadcasted_iota(jnp.int32, s.shape, 1)
    return jnp.where(kpos <= qpos, s, NEG)


def _fox_c0(ck_ref, hh, pos):
    return ck_ref[0, hh:hh + 1, pl.ds(pl.multiple_of(pos, LANE), LANE)][:, 0:1]


def _fox_tiles(T):
    assert T % LANE == 0
    tq, tk = _tile(T, FOX_TQ), _tile(T, FOX_TK)
    shift = (tk // tq).bit_length() - 1
    assert tk == tq << shift
    return tq, tk, shift


def _fox_fwd(p, ck, C, offs):
    T = p.shape[0]
    tq, tk, shift = _fox_tiles(T)
    npair = C // LANE
    cb = lambda name: offs[name] // LANE

    def body(q_ref, k_ref, v_ref, ck_ref, o_ref, lse_ref):
        i = pl.program_id(1)
        nfull = i >> shift
        lse_ref[...] = jnp.zeros_like(lse_ref)
        heads = [slice(hh * HEAD, (hh + 1) * HEAD) for hh in range(2)]
        qs = [(q_ref[:, sl] * FOX_SCALE).astype(BF16) for sl in heads]
        c0s = [_fox_c0(ck_ref, hh, i * tq) for hh in range(2)]

        def step(j, carry, masked):
            off = pl.multiple_of(j * tk, tk)
            out = []
            for hh, sl in enumerate(heads):
                m, l, acc = carry[hh]
                k = k_ref[pl.ds(off, tk), sl].astype(BF16)
                v = v_ref[pl.ds(off, tk), sl].astype(BF16)
                ckv = ck_ref[0, hh:hh + 1, pl.ds(off, tk)] - c0s[hh]
                s = _fox_scores(qs[hh], k, ckv, i * tq, off) if masked else _fox_scores(qs[hh], k, ckv)
                m_new = jnp.maximum(m, jnp.max(s, axis=-1, keepdims=True))
                pr = jnp.exp(s - m_new)
                al = jnp.exp(m - m_new)
                l = al * l + jnp.sum(pr, axis=-1, keepdims=True)
                acc = al * acc + _dg(pr, v, False, False, "bf")
                out.append((m_new, l, acc))
            return tuple(out)

        one = (jnp.full((tq, 1), NEG, F32), jnp.zeros((tq, 1), F32), jnp.zeros((tq, HEAD), F32))
        carry = lax.fori_loop(0, nfull, lambda j, c: step(j, c, False), (one, one))
        for hh, (m, l, acc) in enumerate(step(nfull, carry, True)):
            o_ref[:, heads[hh]] = acc / l
            lse_ref[0, :, hh:hh + 1] = m + jnp.log(l)

    return _pcall(body, name="fox_fwd", grid=(npair, T // tq),
                  in_specs=[pl.BlockSpec((tq, LANE), lambda h, i: (i, cb("fq") + h)),
                            pl.BlockSpec((T, LANE), lambda h, i: (0, cb("fk") + h)),
                            pl.BlockSpec((T, LANE), lambda h, i: (0, cb("fv") + h)),
                            pl.BlockSpec((1, 8, T), lambda h, i: (h, 0, 0))],
                  out_specs=[pl.BlockSpec((tq, LANE), lambda h, i: (i, h)),
                             pl.BlockSpec((1, tq, 8), lambda h, i: (h, i, 0))],
                  out_shape=[jax.ShapeDtypeStruct((T, C), F32), jax.ShapeDtypeStruct((npair, T, 8), F32)],
                  compiler_params=_cp(("arbitrary", "arbitrary")))(p, p, p, ck)


def _fox_delta(p, ck, do, lse, C, offs):
    T = p.shape[0]
    tq, tk, shift = _fox_tiles(T)
    npair = C // LANE
    cb = lambda name: offs[name] // LANE

    def body(q_ref, k_ref, v_ref, ck_ref, do_ref, lse_ref, d_ref):
        i = pl.program_id(1)
        nfull = i >> shift
        d_ref[...] = jnp.zeros_like(d_ref)
        heads = [slice(hh * HEAD, (hh + 1) * HEAD) for hh in range(2)]
        qs = [(q_ref[:, sl] * FOX_SCALE).astype(BF16) for sl in heads]
        dos = [do_ref[:, sl].astype(BF16) for sl in heads]
        lses = [lse_ref[0, :, hh:hh + 1] for hh in range(2)]
        c0s = [_fox_c0(ck_ref, hh, i * tq) for hh in range(2)]

        def step(j, accs, masked):
            off = pl.multiple_of(j * tk, tk)
            out = []
            for hh, sl in enumerate(heads):
                k = k_ref[pl.ds(off, tk), sl].astype(BF16)
                v = v_ref[pl.ds(off, tk), sl].astype(BF16)
                ckv = ck_ref[0, hh:hh + 1, pl.ds(off, tk)] - c0s[hh]
                s = _fox_scores(qs[hh], k, ckv, i * tq, off) if masked else _fox_scores(qs[hh], k, ckv)
                pr = jnp.exp(s - lses[hh])
                dp = _dg(dos[hh], v, False, True, "bf")
                out.append(accs[hh] + jnp.sum(pr * dp, axis=-1, keepdims=True))
            return tuple(out)

        z = jnp.zeros((tq, 1), F32)
        accs = lax.fori_loop(0, nfull, lambda j, c: step(j, c, False), (z, z))
        for hh, acc in enumerate(step(nfull, accs, True)):
            d_ref[0, :, hh:hh + 1] = acc

    return _pcall(body, name="fox_delta", grid=(npair, T // tq),
                  in_specs=[pl.BlockSpec((tq, LANE), lambda h, i: (i, cb("fq") + h)),
                            pl.BlockSpec((T, LANE), lambda h, i: (0, cb("fk") + h)),
                            pl.BlockSpec((T, LANE), lambda h, i: (0, cb("fv") + h)),
                            pl.BlockSpec((1, 8, T), lambda h, i: (h, 0, 0)),
                            pl.BlockSpec((tq, LANE), lambda h, i: (i, h)),
                            pl.BlockSpec((1, tq, 8), lambda h, i: (h, i, 0))],
                  out_specs=pl.BlockSpec((1, tq, 8), lambda h, i: (h, i, 0)),
                  out_shape=jax.ShapeDtypeStruct((npair, T, 8), F32),
                  compiler_params=_cp(("arbitrary", "arbitrary")))(p, p, p, ck, do, lse)


def _fox_bwd(p, ck, delta, do, lse, C, offs):
    T = p.shape[0]
    tq, tk, shift = _fox_tiles(T)
    ratio = tk // tq
    nq = T // tq
    npair = C // LANE
    cb = lambda name: offs[name] // LANE

    def body(q_ref, k_ref, v_ref, ck_ref, ckall_ref, dl_ref, do_ref, lse_ref, dq_ref, dk_ref, dv_ref, dck_ref):
        j = pl.program_id(1)

        @pl.when(j == 0)
        def _():
            dq_ref[...] = jnp.zeros_like(dq_ref)

        dck_ref[...] = jnp.zeros_like(dck_ref)
        heads = [slice(hh * HEAD, (hh + 1) * HEAD) for hh in range(2)]
        ks = [k_ref[:, sl].astype(BF16) for sl in heads]
        vs = [v_ref[:, sl].astype(BF16) for sl in heads]
        cks = [ck_ref[0, hh:hh + 1, :] for hh in range(2)]

        def step(i, carry, masked):
            off = pl.multiple_of(i * tq, tq)
            out = []
            for hh, sl in enumerate(heads):
                dk, dv, dck = carry[hh]
                q = (q_ref[pl.ds(off, tq), sl] * FOX_SCALE).astype(BF16)
                dov = do_ref[pl.ds(off, tq), sl]
                lsev = lse_ref[0, pl.ds(off, tq), hh:hh + 1]
                ckv = cks[hh] - _fox_c0(ckall_ref, hh, off)
                s = _fox_scores(q, ks[hh], ckv, off, j * tk) if masked else _fox_scores(q, ks[hh], ckv)
                pr = jnp.exp(s - lsev)
                dv = dv + _dg(pr, dov, True, False, "bf")
                dp = _dg(dov, vs[hh], False, True, "bf")
                ds = pr * (dp - dl_ref[0, pl.ds(off, tq), hh:hh + 1])
                dk = dk + _dg(ds, q, True, False, "bf")
                dq_ref[pl.ds(off, tq), sl] += _dg(ds, ks[hh], False, False, "bf") * FOX_SCALE
                out.append((dk, dv, dck - jnp.sum(ds, axis=0, keepdims=True)))
            return tuple(out)

        z = jnp.zeros((tk, HEAD), F32)
        carry = ((z, z, jnp.zeros((1, tk), F32)),) * 2
        for r in range(ratio):
            carry = step(j * ratio + r, carry, True)
        carry = lax.fori_loop((j + 1) * ratio, nq, lambda i, c: step(i, c, False), carry)
        for hh, (dk, dv, dck) in enumerate(carry):
            dk_ref[:, heads[hh]] = dk
            dv_ref[:, heads[hh]] = dv
            dck_ref[0, hh:hh + 1, :] = dck

    full = lambda h, j: (0, h)
    return _pcall(body, name="fox_bwd", grid=(npair, T // tk),
                  in_specs=[pl.BlockSpec((T, LANE), lambda h, j: (0, cb("fq") + h)),
                            pl.BlockSpec((tk, LANE), lambda h, j: (j, cb("fk") + h)),
                            pl.BlockSpec((tk, LANE), lambda h, j: (j, cb("fv") + h)),
                            pl.BlockSpec((1, 8, tk), lambda h, j: (h, 0, j)),
                            pl.BlockSpec((1, 8, T), lambda h, j: (h, 0, 0)),
                            pl.BlockSpec((1, T, 8), lambda h, j: (h, 0, 0)), pl.BlockSpec((T, LANE), full),
                            pl.BlockSpec((1, T, 8), lambda h, j: (h, 0, 0))],
                  out_specs=[pl.BlockSpec((T, LANE), full),
                             pl.BlockSpec((tk, LANE), lambda h, j: (j, h)),
                             pl.BlockSpec((tk, LANE), lambda h, j: (j, h)),
                             pl.BlockSpec((1, 8, tk), lambda h, j: (h, 0, j))],
                  out_shape=[jax.ShapeDtypeStruct((T, C), F32)] * 3 + [jax.ShapeDtypeStruct((npair, 8, T), F32)],
                  compiler_params=_cp(("arbitrary", "arbitrary")))(p, p, p, ck, ck, delta, do, lse)


@jax.custom_vjp
def _tri_sum(tri, x):
    return _dg(tri, x, False, False, "e3")


def _tri_sum_fwd(tri, x):
    return _dg(tri, x, False, False, "e3"), tri


def _tri_sum_bwd(tri, g):
    return jnp.zeros_like(tri), _dg(tri, g, True, False, "e3")


_tri_sum.defvjp(_tri_sum_fwd, _tri_sum_bwd)


@jax.custom_vjp
def _solve(a, rhs):
    return _solve_fwd(a, rhs)[0]


def _solve_fwd(a, rhs):
    powers = [a]
    for _ in range(CHUNK.bit_length() - 2):
        powers.append(_dg(powers[-1], powers[-1], False, False, "x3"))
    u = rhs
    for pw in powers:
        u = u + _dg(pw, u, False, False, "x3")
    return u, (powers, u)


def _solve_bwd(res, g):
    powers, u = res
    w = g
    for pw in powers:
        w = w + _dg(pw, w, True, False, "x3")
    return _dg(w, u, False, True, "x3"), w


_solve.defvjp(_solve_fwd, _solve_bwd)


def _chunk_fn(S0, r, lw, k, v, a, b):
    nh, n = r.shape[0], r.shape[1]
    row = lax.broadcasted_iota(jnp.int32, (nh, n, n), 1)
    col = lax.broadcasted_iota(jnp.int32, (nh, n, n), 2)
    incl, strict = col <= row, col < row
    mm = lambda x, y, ta=False, tb=False: _mm(x, y, ta, tb, "x3")
    g = _tri_sum(incl.astype(F32), lw)
    einv = jnp.exp(-g)
    rt, at, bt, kt = r * jnp.exp(g), a * jnp.exp(g - lw), b * einv, k * einv
    a_ab = jnp.where(strict, mm(at, bt, tb=True), 0.0)
    a_ak = jnp.where(strict, mm(at, kt, tb=True), 0.0)
    r_b = jnp.where(incl, mm(rt, bt, tb=True), 0.0)
    r_k = jnp.where(incl, mm(rt, kt, tb=True), 0.0)
    u = _solve(a_ab, mm(at, S0, tb=True) + mm(a_ak, v))
    y = mm(rt, S0, tb=True) + mm(r_b, u) + mm(r_k, v)
    g_end = jnp.sum(lw, axis=1, keepdims=True)
    s_end = (S0 + mm(u, bt, ta=True) + mm(v, kt, ta=True)) * jnp.exp(g_end)
    return y, s_end


SCAN_HEADS = 24
SCAN_ROWS = 64


def _scan_group(C):
    nh = SCAN_HEADS
    while C % (nh * HEAD):
        nh -= 2
    return nh, nh * HEAD


def _scan_fwd(r, lw, k, v, a, b):
    T, C = r.shape
    tc = _tile(T, SCAN_ROWS)
    ncs = tc // CHUNK
    nh, GW = _scan_group(C)
    ngroup = C // GW

    def body(r_ref, lw_ref, k_ref, v_ref, a_ref, b_ref, y_ref, ck_ref, state):
        @pl.when(pl.program_id(1) == 0)
        def _():
            state[...] = jnp.zeros_like(state)

        heads = [slice(hh * HEAD, (hh + 1) * HEAD) for hh in range(nh)]
        split = lambda ref, rows: jnp.stack([ref[rows, sl] for sl in heads])
        st = split(state, slice(None))
        for c in range(ncs):
            rows = slice(c * CHUNK, (c + 1) * CHUNK)
            for hh, sl in enumerate(heads):
                ck_ref[0, c, :, sl] = st[hh]
            y, st = _chunk_fn(st, *[split(ref, rows) for ref in (r_ref, lw_ref, k_ref, v_ref, a_ref, b_ref)])
            for hh, sl in enumerate(heads):
                y_ref[rows, sl] = y[hh]
        for hh, sl in enumerate(heads):
            state[:, sl] = st[hh]

    spec = pl.BlockSpec((tc, GW), lambda h, t: (t, h))
    return _pcall(body, name="rwkv_scan_fwd", grid=(ngroup, T // tc),
                  in_specs=[spec] * 6,
                  out_specs=[spec, pl.BlockSpec((1, ncs, HEAD, GW), lambda h, t: (h, t, 0, 0))],
                  out_shape=[jax.ShapeDtypeStruct((T, C), F32),
                             jax.ShapeDtypeStruct((ngroup, T // CHUNK, HEAD, GW), F32)],
                  scratch_shapes=[pltpu.VMEM((HEAD, GW), F32)],
                  compiler_params=_cp(("arbitrary", "arbitrary")))(r, lw, k, v, a, b)


def _scan_bwd(r, lw, k, v, a, b, ckpt, dy):
    T, C = r.shape
    tc = _tile(T, SCAN_ROWS)
    ncs = tc // CHUNK
    nh, GW = _scan_group(C)
    ngroup = C // GW
    nt = T // tc

    def body(r_ref, lw_ref, k_ref, v_ref, a_ref, b_ref, ck_ref, dy_ref, dr, dlw, dk, dv, da, db, dstate):
        @pl.when(pl.program_id(1) == 0)
        def _():
            dstate[...] = jnp.zeros_like(dstate)

        outs = (dr, dlw, dk, dv, da, db)
        heads = [slice(hh * HEAD, (hh + 1) * HEAD) for hh in range(nh)]
        split = lambda ref, rows: jnp.stack([ref[rows, sl] for sl in heads])
        dst = split(dstate, slice(None))
        for c in reversed(range(ncs)):
            rows = slice(c * CHUNK, (c + 1) * CHUNK)
            s0 = jnp.stack([ck_ref[0, c, :, sl] for sl in heads])
            _, vjp = jax.vjp(_chunk_fn, s0, *[split(ref, rows) for ref in (r_ref, lw_ref, k_ref, v_ref, a_ref, b_ref)])
            grads = vjp((split(dy_ref, rows), dst))
            dst = grads[0]
            for o, gval in zip(outs, grads[1:]):
                for hh, sl in enumerate(heads):
                    o[rows, sl] = gval[hh]
        for hh, sl in enumerate(heads):
            dstate[:, sl] = dst[hh]

    spec = pl.BlockSpec((tc, GW), lambda h, t: (nt - 1 - t, h))
    return _pcall(body, name="rwkv_scan_bwd", grid=(ngroup, nt),
                  in_specs=[spec] * 6 + [pl.BlockSpec((1, ncs, HEAD, GW), lambda h, t: (h, nt - 1 - t, 0, 0)), spec],
                  out_specs=[spec] * 6, out_shape=[jax.ShapeDtypeStruct((T, C), F32)] * 6,
                  scratch_shapes=[pltpu.VMEM((HEAD, GW), F32)],
                  compiler_params=_cp(("arbitrary", "arbitrary")))(r, lw, k, v, a, b, ckpt, dy)


def _adam(w, g, m, v):
    m = ADAM_B1 * m + (1.0 - ADAM_B1) * g
    v = ADAM_B2 * v + (1.0 - ADAM_B2) * (g * g)
    m_hat = m / (1.0 - ADAM_B1 ** ADAM_STEP)
    v_hat = v / (1.0 - ADAM_B2 ** ADAM_STEP)
    return -ADAM_LR * (m_hat / (jnp.sqrt(v_hat) + ADAM_EPS) + ADAM_WD * w), m, v


def _sum_adam(parts, w, m, v, name):
    n, R, W = parts.shape
    tr = _tile(R, max(8, min(256, (1 << 20) // (n * W))))

    def body(p_ref, w_ref, m_ref, v_ref, g_ref, d_ref, nm_ref, nv_ref):
        g = p_ref[0].astype(F32)
        for s in range(1, n):
            g = g + p_ref[s].astype(F32)
        d, nm, nv = _adam(w_ref[...], g, m_ref[...], v_ref[...])
        g_ref[...] = g
        d_ref[...] = d
        nm_ref[...] = nm
        nv_ref[...] = nv

    return _pcall(body, name=name, grid=(R // tr,),
                  in_specs=[pl.BlockSpec((n, tr, W), lambda i: (0, i, 0))] + [_row_spec(tr, W)] * 3,
                  out_specs=[_row_spec(tr, W)] * 4, out_shape=[jax.ShapeDtypeStruct((R, W), F32)] * 4,
                  compiler_params=_cp(("arbitrary",)))(parts, w, m, v)


def _pad_lanes(vec, width):
    return jnp.pad(vec, ((0, 0), (0, width - vec.shape[1])))


def _logical_cols(blocks, lo, hi):
    B, out = blocks.shape[2], []
    while lo < hi:
        j, o = divmod(lo, B)
        n = min(hi - lo, B - o)
        out.append(blocks[j, :, o:o + n])
        lo += n
    return out


def kernel(x, mem, g_pre, w_in, mu_rwkv, w0, w_decay_up, a0, w_iclr_up, k_k, k_a, r_k, ln_x_w, ln_x_b, b_f, g_mem, w_mem_kv, w_out, g_post, loss_target, m_g_pre, m_w_in, m_mu_rwkv, m_w0, m_w_decay_up, m_a0, m_w_iclr_up, m_k_k, m_k_a, m_r_k, m_ln_x_w, m_ln_x_b, m_b_f, m_g_mem, m_w_mem_kv, m_w_out, m_g_post, v_g_pre, v_w_in, v_mu_rwkv, v_w0, v_w_decay_up, v_a0, v_w_iclr_up, v_k_k, v_k_a, v_r_k, v_ln_x_w, v_ln_x_b, v_b_f, v_g_mem, v_w_mem_kv, v_w_out, v_g_post):
    T, D = x.shape[1], x.shape[2]
    C = w0.shape[1]
    L = w_decay_up.shape[1]
    H = C // HEAD
    MW = w_mem_kv.shape[2] // 2
    SH = 3 * C + 2 * L
    IN = NDEV * w_in.shape[2]
    assert IN == SH + 5 * C + H + 2 * MW and D == 2 * C + MW and H % 2 == 0 and H <= LANE
    assert C % LANE == 0 and L % LANE == 0 and MW % (MEM_HEADS * HEAD) == 0 and T % CHUNK == 0
    offs = dict(grw=SH, fq=SH + C, fk=SH + 2 * C, fv=SH + 3 * C, gfx=SH + 4 * C, mq=SH + 5 * C, gmq=SH + 5 * C + MW,
                fl=SH + 5 * C + 2 * MW)
    NI = -(-(offs["fl"] + LANE) // 1024) * 1024
    l_fl = SH + 4 * C

    x2, mem2, tgt2 = x[0], mem[0], loss_target[0]

    wg = _exchange(w_in[0].astype(BF16), "gather_w_in", True)
    w_perm = jnp.concatenate(_logical_cols(wg, 0, l_fl) + _logical_cols(wg, l_fl + H, IN)
                             + _logical_cols(wg, l_fl, l_fl + H) + [jnp.zeros((D, NI - IN), BF16)], axis=1)

    h = _rmsnorm_fwd(x2, g_pre, "rmsnorm_pre")
    p, w_out_f, w_kv_f, lora = _matmul(
        h, w_perm, False, False, "in_proj",
        riders=[(w_out[0].astype(BF16), True), (w_mem_kv[0].astype(BF16), True),
                (jnp.concatenate([w_decay_up[0], w_iclr_up[0]], axis=0), True)])
    w_out_f, w_kv_f = w_out_f.reshape(D, D), w_kv_f.reshape(D, 2 * MW)
    lora = jnp.transpose(lora, (1, 0, 2)).reshape(2 * L, C)
    wdu_f, wiu_f = lora[:L], lora[L:]

    E, ET = _head_indicator(C, LANE)
    prep_params = [mu_rwkv, w0, a0, k_k, k_a, wdu_f, wiu_f, E, ET]
    mix_params = [ln_x_w, ln_x_b, r_k.reshape(1, C), E, ET]
    b_f_pad = _pad_lanes(b_f, LANE)
    col = lambda name, w: p[:, offs[name]:offs[name] + w]
    g_rwkv, g_fox, mq, g_mq = col("grw", C), col("gfx", C), col("mq", MW), col("gmq", MW)
    r, lw, kmod, v, a, b = _prep_fwd(p, prep_params, C, L)
    y_scan, ckpt = _scan_fwd(r, lw, kmod, v, a, b)

    cum, cum_t = _fox_cum_fwd(p, b_f_pad, offs["fl"])
    ck = jnp.pad(cum_t[:H].reshape(H // 2, 2, T), ((0, 0), (0, 6), (0, 0)))
    y_fox, lse = _fox_fwd(p, ck, C, offs)

    memn = _rmsnorm_fwd(mem2, g_mem, "rmsnorm_mem")
    mkv = _matmul(memn, w_kv_f, False, False, "mem_kv_proj")
    y_mem = _memattn_fwd(mq, mkv, MW)

    acts = [y_scan, r, kmod, v, g_rwkv, y_fox, g_fox, y_mem, g_mq]
    ycat = _mix_fwd(acts, mix_params, C, MW)
    yo = _matmul(ycat, w_out_f, False, False, "out_proj")
    d_yo, d_out, loss_part, dg_post = _post(yo, x2, tgt2, g_post)
    loss = lax.psum(loss_part[0, 0], AXES)

    g_w_out = _matmul(ycat, d_yo, True, False, "grad_w_out", BF16)
    d_ycat = _matmul(d_yo, w_out_f, False, True, "d_ycat")
    (d_y, d_r1, d_k1, d_v1, d_grw, d_yfox, d_gfx, d_ymem, d_gmq, dg_lnw, dg_lnb, dg_rk) = _mix_bwd(
        acts, mix_params, d_ycat, C, MW)

    d_mq, d_mkv = _memattn_bwd(mq, mkv, d_ymem, MW)
    g_w_kv = _matmul(memn, d_mkv, True, False, "grad_w_mem_kv", BF16)
    d_memn = _matmul(d_mkv, w_kv_f, False, True, "d_memn")
    _, dg_mem = _rmsnorm_bwd(mem2, g_mem, d_memn, None, "rmsnorm_mem_bwd")

    delta = _fox_delta(p, ck, d_yfox, lse, C, offs)
    d_fq, d_fk, d_fv, d_ck = _fox_bwd(p, ck, delta, d_yfox, lse, C, offs)
    d_cum = _pad_lanes(d_ck[:, :2, :].reshape(H, T).T, LANE)
    d_fl, dg_bf = _fox_cum_bwd(p, b_f_pad, d_cum, offs["fl"])

    d_r, d_lw, d_k, d_v, d_a, d_b = _scan_bwd(r, lw, kmod, v, a, b, ckpt, d_y)
    cts = [d_r, d_lw, d_k, d_v, d_a, d_b, d_r1, d_k1, d_v1]
    (d_ps, d_prev, dg_mu, dg_w0, dg_a0, dg_kk, dg_ka, dg_wdu, dg_wiu) = _prep_bwd(p, prep_params, cts, C, L)
    d_sh = _shift_combine(d_ps, d_prev)

    tobf = lambda z: z.astype(BF16)
    dp = jnp.concatenate([d_sh, tobf(d_grw), tobf(d_fq), tobf(d_fk), tobf(d_fv), tobf(d_gfx), tobf(d_mq), tobf(d_gmq),
                          tobf(d_fl), jnp.zeros((T, NI - offs["fl"] - LANE), BF16)], axis=1)
    g_lora = jnp.concatenate([dg_wdu, dg_wiu], axis=0)
    g_w_perm, parts_out, parts_kv, parts_lora = _matmul(
        h, dp, True, False, "grad_w_in", BF16,
        riders=[(g_w_out.reshape(NDEV, D // NDEV, D), False), (g_w_kv.reshape(NDEV, D // NDEV, 2 * MW), False),
                (jnp.transpose(g_lora.reshape(2 * L, NDEV, C // NDEV), (1, 0, 2)), False)])

    def internal_cols(lo, hi):
        out = []
        for first, last, shift in ((0, l_fl, 0), (l_fl, l_fl + H, offs["fl"] - l_fl), (l_fl + H, IN, -H)):
            s0, s1 = max(lo, first), min(hi, last)
            if s0 < s1:
                out.append(g_w_perm[:, s0 + shift:s1 + shift])
        return out

    blk = IN // NDEV
    g_blocks = jnp.stack([jnp.concatenate(internal_cols(j * blk, (j + 1) * blk), axis=1) for j in range(NDEV)])
    d_h, parts_in = _matmul(dp, w_perm, False, True, "d_h", tk=5120, riders=[(g_blocks, False)])
    grad_x, dg_pre = _rmsnorm_bwd(x2, g_pre, d_h, d_out, "rmsnorm_pre_bwd")

    gw_in, dw_in, nm_w_in, nv_w_in = _sum_adam(parts_in, w_in[0], m_w_in[0], v_w_in[0], "adam_w_in")
    gw_out, dw_out, nm_w_out, nv_w_out = _sum_adam(parts_out, w_out[0], m_w_out[0], v_w_out[0], "adam_w_out")
    gw_kv, dw_kv, nm_w_kv, nv_w_kv = _sum_adam(parts_kv, w_mem_kv[0], m_w_mem_kv[0], v_w_mem_kv[0], "adam_w_mem_kv")
    cat2 = lambda u, w_: jnp.concatenate([u[0], w_[0]], axis=0)
    lora_res = _sum_adam(parts_lora, cat2(w_decay_up, w_iclr_up), cat2(m_w_decay_up, m_w_iclr_up),
                         cat2(v_w_decay_up, v_w_iclr_up), "adam_lora")

    small = [("g_pre", g_pre, m_g_pre, v_g_pre, dg_pre), ("mu_rwkv", mu_rwkv, m_mu_rwkv, v_mu_rwkv, dg_mu),
             ("w0", w0, m_w0, v_w0, dg_w0), ("a0", a0, m_a0, v_a0, dg_a0), ("k_k", k_k, m_k_k, v_k_k, dg_kk),
             ("k_a", k_a, m_k_a, v_k_a, dg_ka), ("r_k", r_k.reshape(1, C), m_r_k.reshape(1, C), v_r_k.reshape(1, C), dg_rk),
             ("ln_x_w", ln_x_w, m_ln_x_w, v_ln_x_w, dg_lnw), ("ln_x_b", ln_x_b, m_ln_x_b, v_ln_x_b, dg_lnb),
             ("b_f", _pad_lanes(b_f, LANE), _pad_lanes(m_b_f, LANE), _pad_lanes(v_b_f, LANE), dg_bf),
             ("g_mem", g_mem, m_g_mem, v_g_mem, dg_mem), ("g_post", g_post, m_g_post, v_g_post, dg_post)]
    widths = [s[1].shape[1] for s in small]
    pack = lambda idx: jnp.concatenate([s[idx] for s in small], axis=1).reshape(-1, LANE)
    parts_small = _exchange(pack(4), "gather_small_grads", True)
    res_small = _sum_adam(parts_small, pack(1), pack(2), pack(3), "adam_small")

    def unpack(flat):
        flat = flat.reshape(1, -1)
        out, o = {}, 0
        for (name, *_), wd in zip(small, widths):
            out[name] = flat[:, o:o + wd]
            o += wd
        out["b_f"] = out["b_f"][:, :H]
        out["r_k"] = out["r_k"].reshape(1, H, HEAD)
        return out

    sg, sd, sm, sv = [unpack(z) for z in res_small]
    big = {"w_in": (gw_in, dw_in, nm_w_in, nv_w_in), "w_out": (gw_out, dw_out, nm_w_out, nv_w_out),
           "w_mem_kv": (gw_kv, dw_kv, nm_w_kv, nv_w_kv),
           "w_decay_up": tuple(z[:L] for z in lora_res), "w_iclr_up": tuple(z[L:] for z in lora_res)}
    order = ["g_pre", "w_in", "mu_rwkv", "w0", "w_decay_up", "a0", "w_iclr_up", "k_k", "k_a", "r_k", "ln_x_w", "ln_x_b",
             "b_f", "g_mem", "w_mem_kv", "w_out", "g_post"]

    def pick(name, idx):
        if name in big:
            return big[name][idx][None]
        return (sg, sd, sm, sv)[idx][name]

    outs = [loss, grad_x[None]]
    for idx in range(4):
        outs += [pick(n, idx) for n in order]
    return tuple(outs)
```

```python
import functools

import jax
import jax.numpy as jnp
from jax import lax
from jax.experimental import pallas as pl
from jax.experimental.pallas import tpu as pltpu

F32, BF16 = jnp.float32, jnp.bfloat16
HI = lax.Precision.HIGHEST
NDEV = 8
AXES = ("x", "y", "c")
HEAD = 64
CHUNK = 64
MEM_HEADS = 4
LANE = 128
RMS_EPS = 1e-6
GN_EPS = 64e-5
NEG = -1e30
ADAM_LR, ADAM_B1, ADAM_B2, ADAM_EPS, ADAM_WD, ADAM_STEP = 0.001, 0.9, 0.999, 1e-08, 0.01, 10
VMEM_LIMIT = 56 * 1024 * 1024


def _pcall(body, **kw):
    return pl.pallas_call(body, **kw)


def _cp(sem=None, vmem=VMEM_LIMIT):
    return pltpu.CompilerParams(dimension_semantics=sem, vmem_limit_bytes=vmem)


def _tile(n, pref):
    for t in (pref, 1024, 512, 256, 128, 64, 32, 16, 8):
        if t <= pref and n % t == 0:
            return t
    return n


def _dg(a, b, ta, tb, mode):
    nb = a.ndim - 2
    ca = nb + (0 if ta else 1)
    cb = nb + (1 if tb else 0)
    dims = (((ca,), (cb,)), (tuple(range(nb)), tuple(range(nb))))
    if mode == "x3":
        a_hi, b_hi = a.astype(BF16), b.astype(BF16)
        a_lo, b_lo = (a - a_hi.astype(F32)).astype(BF16), (b - b_hi.astype(F32)).astype(BF16)
        a3 = jnp.concatenate([a_hi, a_lo, a_hi], axis=ca)
        b3 = jnp.concatenate([b_hi, b_hi, b_lo], axis=cb)
        return lax.dot_general(a3, b3, dims, preferred_element_type=F32)
    if mode == "e3":
        b1 = b.astype(BF16)
        rest = b - b1.astype(F32)
        b2 = rest.astype(BF16)
        b3 = (rest - b2.astype(F32)).astype(BF16)
        a = a.astype(BF16)
        return lax.dot_general(jnp.concatenate([a, a, a], axis=ca), jnp.concatenate([b1, b2, b3], axis=cb), dims,
                               preferred_element_type=F32)
    if mode == "x2":
        a_hi, b = a.astype(BF16), b.astype(BF16)
        a_lo = (a - a_hi.astype(F32)).astype(BF16)
        dot = lambda u: lax.dot_general(u, b, dims, preferred_element_type=F32)
        return dot(a_hi) + dot(a_lo)
    if mode == "bf":
        a, b, prec = a.astype(BF16), b.astype(BF16), None
    else:
        prec = HI
    return lax.dot_general(a, b, dims, preferred_element_type=F32, precision=prec)


@functools.partial(jax.custom_vjp, nondiff_argnums=(2, 3, 4))
def _mm(a, b, ta, tb, mode):
    return _dg(a, b, ta, tb, mode)


def _mm_fwd(a, b, ta, tb, mode):
    return _dg(a, b, ta, tb, mode), (a, b)


def _mm_bwd(ta, tb, mode, res, g):
    a, b = res
    da = _dg(g, b, False, not tb, mode) if not ta else _dg(b, g, tb, True, mode)
    db = _dg(a, g, not ta, False, mode) if not tb else _dg(g, a, True, ta, mode)
    return da, db


_mm.defvjp(_mm_fwd, _mm_bwd)


@jax.custom_vjp
def _seg(a, e):
    return _dg(a, e, False, False, "x2")


def _seg_fwd(a, e):
    return _dg(a, e, False, False, "x2"), e


def _seg_bwd(e, g):
    return _dg(g, e, False, True, "x2"), jnp.zeros_like(e)


_seg.defvjp(_seg_fwd, _seg_bwd)


def _sigmoid(z):
    return 1.0 / (1.0 + jnp.exp(-z))


def _softplus(z):
    return jnp.maximum(z, 0.0) + jnp.log(1.0 + jnp.exp(-jnp.abs(z)))


def _silu(z):
    return z * _sigmoid(z)


def _rms(x, g):
    return x * lax.rsqrt(jnp.mean(x * x, axis=-1, keepdims=True) + RMS_EPS) * g


HBM_SPEC = pl.BlockSpec(memory_space=pltpu.HBM)
EXCHANGE_SEMS = [pltpu.SemaphoreType.DMA((NDEV - 1,)), pltpu.SemaphoreType.DMA((NDEV - 1,)), pltpu.SemaphoreType.DMA(())]


def _exchange_copies(gather, x_ref, o_ref, send_sems, recv_sems, local_sem, arrivals):
    ix, iy, ic = lax.axis_index("x"), lax.axis_index("y"), lax.axis_index("c")
    me = 4 * ix + 2 * iy + ic

    def src(dest):
        return x_ref if gather else x_ref.at[dest]

    mine = pltpu.make_async_copy(src(me), o_ref.at[me], local_sem)
    pairs = []
    for k in range(1, NDEV):
        px = 1 - ix if (k >> 2) & 1 else ix
        py = 1 - iy if (k >> 1) & 1 else iy
        pc = 1 - ic if k & 1 else ic
        peer = 4 * px + 2 * py + pc
        send = pltpu.make_async_remote_copy(
            src_ref=src(peer), dst_ref=o_ref.at[me], send_sem=send_sems.at[k - 1], recv_sem=recv_sems.at[k - 1],
            device_id=(px, py, pc), device_id_type=pl.DeviceIdType.MESH)
        arrival = arrivals and pltpu.make_async_remote_copy(
            src_ref=src(peer), dst_ref=o_ref.at[peer], send_sem=send_sems.at[k - 1], recv_sem=recv_sems.at[k - 1],
            device_id=(ix, iy, ic), device_id_type=pl.DeviceIdType.MESH)
        pairs.append((send, arrival))
    return mine, pairs


def _exchange_start(*args):
    mine, pairs = _exchange_copies(*args, arrivals=False)
    mine.start()
    for send, _ in pairs:
        send.start()


def _exchange_wait(*args):
    mine, pairs = _exchange_copies(*args, arrivals=True)
    for send, arrival in pairs:
        send.wait_send()
        arrival.wait_recv()
    mine.wait()


def _exchange_shape(x, gather):
    return jax.ShapeDtypeStruct((NDEV,) + tuple(x.shape if gather else x.shape[1:]), x.dtype)


def _exchange(x, name, gather):
    def body(x_ref, o_ref, *sems):
        _exchange_start(gather, x_ref, o_ref, *sems)
        _exchange_wait(gather, x_ref, o_ref, *sems)

    return _pcall(body, name=name, out_shape=_exchange_shape(x, gather), in_specs=[HBM_SPEC], out_specs=HBM_SPEC,
                  scratch_shapes=list(EXCHANGE_SEMS))(x)


MAX_FULL_K = 4096


def _matmul(a, b, ta, tb, name, out_dtype=F32, tm=1024, tn=1024, tk=3072, riders=()):
    M, K = (a.shape[1], a.shape[0]) if ta else a.shape
    N = b.shape[0] if tb else b.shape[1]
    assert (b.shape[1] if tb else b.shape[0]) == K
    if K <= MAX_FULL_K:
        tk = K
    else:
        tm, tk = min(tm, 512), _tile(K, tk)
    tm, tn = _tile(M, tm), _tile(N, tn)
    grid = (M // tm, N // tn, K // tk)
    nk, nr = grid[2], len(riders)

    def body(*refs):
        a_ref, b_ref, x_refs = refs[0], refs[1], refs[2:2 + nr]
        o_ref, xo_refs, rest = refs[2 + nr], refs[3 + nr:3 + 2 * nr], refs[3 + 2 * nr:]
        sems = rest[1:] if nk > 1 else rest
        ids = [pl.program_id(d) for d in range(3)]
        jobs = [(riders[q][1], x_refs[q], xo_refs[q]) + tuple(sems[3 * q:3 * q + 3]) for q in range(nr)]

        if nr:
            @pl.when((ids[0] == 0) & (ids[1] == 0) & (ids[2] == 0))
            def _():
                for job in jobs:
                    _exchange_start(*job)

        if nk == 1:
            o_ref[...] = _dg(a_ref[...], b_ref[...], ta, tb, "bf").astype(o_ref.dtype)
        else:
            acc = rest[0]

            @pl.when(ids[2] == 0)
            def _():
                acc[...] = jnp.zeros_like(acc)

            acc[...] += _dg(a_ref[...], b_ref[...], ta, tb, "bf")

            @pl.when(ids[2] == nk - 1)
            def _():
                o_ref[...] = acc[...].astype(o_ref.dtype)

        if nr:
            @pl.when((ids[0] == grid[0] - 1) & (ids[1] == grid[1] - 1) & (ids[2] == nk - 1))
            def _():
                for job in jobs:
                    _exchange_wait(*job)

    a_spec = pl.BlockSpec((tk, tm), lambda i, j, k: (k, i)) if ta else pl.BlockSpec((tm, tk), lambda i, j, k: (i, k))
    b_spec = pl.BlockSpec((tn, tk), lambda i, j, k: (j, k)) if tb else pl.BlockSpec((tk, tn), lambda i, j, k: (k, j))
    out = _pcall(
        body, name=name, grid=grid,
        in_specs=[a_spec, b_spec] + [HBM_SPEC] * nr,
        out_specs=[pl.BlockSpec((tm, tn), lambda i, j, k: (i, j))] + [HBM_SPEC] * nr,
        out_shape=[jax.ShapeDtypeStruct((M, N), out_dtype)] + [_exchange_shape(x, g) for x, g in riders],
        scratch_shapes=([pltpu.VMEM((tm, tn), F32)] if nk > 1 else []) + list(EXCHANGE_SEMS) * nr,
        compiler_params=_cp(("arbitrary", "arbitrary", "arbitrary")),
    )(a, b, *[x for x, _ in riders])
    return tuple(out) if nr else out[0]


def _row_spec(tr, width, col_block=0):
    return pl.BlockSpec((tr, width), lambda i: (i, col_block))


def _full_spec(shape):
    nd = len(shape)
    return pl.BlockSpec(tuple(shape), lambda i: (0,) * nd)


def _rmsnorm_fwd(x, g, name, tr=256):
    R, D = x.shape
    tr = _tile(R, tr)

    def body(x_ref, g_ref, o_ref):
        o_ref[...] = _rms(x_ref[...], g_ref[...]).astype(BF16)

    return _pcall(body, name=name, grid=(R // tr,),
                  in_specs=[_row_spec(tr, D), _full_spec((1, D))], out_specs=_row_spec(tr, D),
                  out_shape=jax.ShapeDtypeStruct((R, D), BF16), compiler_params=_cp(("arbitrary",)))(x, g)


def _rmsnorm_bwd(x, g, dy, extra, name, tr=128):
    R, D = x.shape
    tr = _tile(R, tr)
    has_extra = extra is not None

    def body(*refs):
        if has_extra:
            x_ref, g_ref, dy_ref, e_ref, dx_ref, dg_ref = refs
        else:
            x_ref, g_ref, dy_ref, dx_ref, dg_ref = refs
        _, vjp = jax.vjp(_rms, x_ref[...], g_ref[...])
        dx, dg = vjp(dy_ref[...])
        dx_ref[...] = dx + e_ref[...] if has_extra else dx

        @pl.when(pl.program_id(0) == 0)
        def _():
            dg_ref[...] = jnp.zeros_like(dg_ref)

        dg_ref[...] += dg

    ins = [x, g, dy] + ([extra] if has_extra else [])
    specs = [_row_spec(tr, D), _full_spec((1, D)), _row_spec(tr, D)] + ([_row_spec(tr, D)] if has_extra else [])
    return _pcall(body, name=name, grid=(R // tr,), in_specs=specs,
                  out_specs=[_row_spec(tr, D), _full_spec((1, D))],
                  out_shape=[jax.ShapeDtypeStruct((R, D), F32), jax.ShapeDtypeStruct((1, D), F32)],
                  compiler_params=_cp(("arbitrary",)))(*ins)


def _head_indicator(C, hp):
    e = (jnp.arange(C)[:, None] // HEAD == jnp.arange(hp)[None, :]).astype(F32)
    return e, e.T


def _prep_fn(C, L, ps, prev, mu, w0, a0, k_k, k_a, wdu, wiu, E, ET):
    sh = ps + (prev - ps) * mu
    r, k, v = sh[:, :C], sh[:, C:2 * C], sh[:, 2 * C:3 * C]
    wl, al = sh[:, 3 * C:3 * C + L], sh[:, 3 * C + L:3 * C + 2 * L]
    wd = w0 + _mm(jnp.tanh(wl), wdu, False, False, "bf")
    w_pre = -_softplus(-wd) - 0.5
    lw = -jnp.exp(w_pre)
    alpha = _sigmoid(a0 + _mm(al, wiu, False, False, "bf"))
    kk = k * k_k
    ss = _seg(kk * kk, E)
    kk = kk * _seg(lax.rsqrt(jnp.maximum(ss, 1e-24)), ET)
    k_mod = k * (1.0 + (alpha - 1.0) * k_a)
    return r, lw, k_mod, v, -kk, kk * alpha


SUBLANES = 8


def _edge_spec(tr, width, nrows, before):
    per, last = tr // SUBLANES, nrows // SUBLANES - 1
    if before:
        return pl.BlockSpec((SUBLANES, width), lambda i: (jnp.maximum(i * per - 1, 0), 0))
    return pl.BlockSpec((SUBLANES, width), lambda i: (jnp.minimum((i + 1) * per, last), 0))


def _previous_rows(cur, before_ref):
    edge = before_ref[SUBLANES - 1:SUBLANES, :] * (pl.program_id(0) > 0).astype(F32)
    row = lax.broadcasted_iota(jnp.int32, cur.shape, 0)
    return jnp.where(row == 0, edge, pltpu.roll(cur, 1, 0))


def _prep_fwd(p, params, C, L, tr=128):
    T = p.shape[0]
    SH = 3 * C + 2 * L
    tr = _tile(T, tr)

    def body(ps_ref, before_ref, mu, w0, a0, kk_, ka_, wdu, wiu, E, ET, *outs):
        ps = ps_ref[...]
        vals = _prep_fn(C, L, ps, _previous_rows(ps, before_ref), mu[...], w0[...], a0[...], kk_[...], ka_[...],
                        wdu[...], wiu[...], E[...], ET[...])
        for o, v in zip(outs, vals):
            o[...] = v

    pspecs = [_full_spec(a.shape) for a in params]
    return _pcall(body, name="rwkv_prep_fwd", grid=(T // tr,),
                  in_specs=[_row_spec(tr, SH), _edge_spec(tr, SH, T, True)] + pspecs,
                  out_specs=[_row_spec(tr, C)] * 6,
                  out_shape=[jax.ShapeDtypeStruct((T, C), F32)] * 6,
                  compiler_params=_cp(("arbitrary",)))(p, p, *params)


def _prep_bwd(p, params, cts, C, L, tr=128):
    T = p.shape[0]
    SH = 3 * C + 2 * L
    tr = _tile(T, tr)
    nparam = 7

    def body(ps_ref, before_ref, mu, w0, a0, kk_, ka_, wdu, wiu, E, ET, c0, c1, c2, c3, c4, c5, e0, e2, e3,
             dps_ref, dprev_ref, *dpar):
        f = functools.partial(_prep_fn, C, L)
        fe = lambda ps, prev, *par: f(ps, prev, *par, E[...], ET[...])
        ps = ps_ref[...]
        _, vjp = jax.vjp(fe, ps, _previous_rows(ps, before_ref), mu[...], w0[...], a0[...], kk_[...], ka_[...], wdu[...], wiu[...])
        grads = vjp((c0[...] + e0[...], c1[...], c2[...] + e2[...], c3[...] + e3[...], c4[...], c5[...]))
        dps_ref[...] = grads[0]
        dprev_ref[...] = grads[1]

        @pl.when(pl.program_id(0) == 0)
        def _():
            for d in dpar:
                d[...] = jnp.zeros_like(d)

        for d, gval in zip(dpar, grads[2:]):
            d[...] += gval

    pspecs = [_full_spec(a.shape) for a in params]
    par_shapes = [a.shape for a in params[:nparam]]
    return _pcall(body, name="rwkv_prep_bwd", grid=(T // tr,),
                  in_specs=[_row_spec(tr, SH), _edge_spec(tr, SH, T, True)] + pspecs + [_row_spec(tr, C)] * 9,
                  out_specs=[_row_spec(tr, SH), _row_spec(tr, SH)] + [_full_spec(s) for s in par_shapes],
                  out_shape=[jax.ShapeDtypeStruct((T, SH), F32)] * 2 + [jax.ShapeDtypeStruct(s, F32) for s in par_shapes],
                  compiler_params=_cp(("arbitrary",)))(p, p, *params, *cts)


def _shift_combine(d_direct, d_prev, tr=256):
    T, W = d_direct.shape
    tr = _tile(T, tr)
    nt = T // tr

    def body(a_ref, b_ref, after_ref, o_ref):
        cur = b_ref[...]
        edge = after_ref[0:1, :] * (pl.program_id(0) < nt - 1).astype(F32)
        row = lax.broadcasted_iota(jnp.int32, cur.shape, 0)
        nxt = jnp.where(row == tr - 1, edge, pltpu.roll(cur, tr - 1, 0))
        o_ref[...] = (a_ref[...] + nxt).astype(BF16)

    return _pcall(body, name="shift_combine", grid=(nt,),
                  in_specs=[_row_spec(tr, W)] * 2 + [_edge_spec(tr, W, T, False)], out_specs=_row_spec(tr, W),
                  out_shape=jax.ShapeDtypeStruct((T, W), BF16), compiler_params=_cp(("arbitrary",)))(d_direct, d_prev, d_prev)


def _mix_fn(y, r, kmod, v, g_rwkv, yfox, g_fox, ymem, g_mq, lnw, lnb, rk, E, ET):
    inv = 1.0 / HEAD
    mean = _seg(y, E) * inv
    yc = y - _seg(mean, ET)
    var = _seg(yc * yc, E) * inv
    yn = yc * _seg(lax.rsqrt(var + GN_EPS), ET) * lnw + lnb
    bonus = _seg(_seg(r * kmod * rk, E), ET) * v
    o1 = (yn + bonus) * _silu(g_rwkv)
    return jnp.concatenate([o1, yfox * _silu(g_fox), ymem * _silu(g_mq)], axis=1)


def _mix_specs(tr, C, MW):
    return [_row_spec(tr, C)] * 7 + [_row_spec(tr, MW)] * 2


def _mix_fwd(acts, params, C, MW, tr=128):
    T = acts[0].shape[0]
    D = 2 * C + MW
    tr = _tile(T, tr)

    def body(y_, r_, k_, v_, g1, yf, g2, ym, g3, lnw, lnb, rk, E, ET, o_ref):
        o_ref[...] = _mix_fn(y_[...], r_[...], k_[...], v_[...], g1[...], yf[...], g2[...], ym[...], g3[...],
                             lnw[...], lnb[...], rk[...], E[...], ET[...]).astype(BF16)

    return _pcall(body, name="mix_fwd", grid=(T // tr,),
                  in_specs=_mix_specs(tr, C, MW) + [_full_spec(a.shape) for a in params],
                  out_specs=_row_spec(tr, D), out_shape=jax.ShapeDtypeStruct((T, D), BF16),
                  compiler_params=_cp(("arbitrary",)))(*acts, *params)


def _mix_bwd(acts, params, dycat, C, MW, tr=128):
    T = acts[0].shape[0]
    D = 2 * C + MW
    tr = _tile(T, tr)

    def body(y_, r_, k_, v_, g1, yf, g2, ym, g3, lnw, lnb, rk, E, ET, dy_ref, *outs):
        fe = lambda *a: _mix_fn(*a, E[...], ET[...])
        _, vjp = jax.vjp(fe, y_[...], r_[...], k_[...], v_[...], g1[...], yf[...], g2[...], ym[...], g3[...],
                         lnw[...], lnb[...], rk[...])
        grads = vjp(dy_ref[...])
        for o, gval in zip(outs[:9], grads[:9]):
            o[...] = gval

        @pl.when(pl.program_id(0) == 0)
        def _():
            for o in outs[9:]:
                o[...] = jnp.zeros_like(o)

        for o, gval in zip(outs[9:], grads[9:]):
            o[...] += gval

    widths = [C, C, C, C, C, C, C, MW, MW]
    return _pcall(body, name="mix_bwd", grid=(T // tr,),
                  in_specs=_mix_specs(tr, C, MW) + [_full_spec(a.shape) for a in params] + [_row_spec(tr, D)],
                  out_specs=[_row_spec(tr, w) for w in widths] + [_full_spec((1, C))] * 3,
                  out_shape=[jax.ShapeDtypeStruct((T, w), F32) for w in widths] + [jax.ShapeDtypeStruct((1, C), F32)] * 3,
                  compiler_params=_cp(("arbitrary",)))(*acts, *params, dycat)


def _post(yo, x, tgt, g_post, tr=128):
    T, D = x.shape
    tr = _tile(T, tr)

    def body(yo_ref, x_ref, t_ref, g_ref, dyo_ref, dout_ref, loss_ref, dg_ref):
        n, vjp = jax.vjp(_rms, yo_ref[...], g_ref[...])
        diff = (x_ref[...] + n) - t_ref[...]
        part = 0.5 * jnp.sum(jnp.mean(diff * diff, axis=-1, keepdims=True), axis=0, keepdims=True)
        d_out = diff * (1.0 / D)
        dyo, dg = vjp(d_out)
        dyo_ref[...] = dyo.astype(BF16)
        dout_ref[...] = d_out

        @pl.when(pl.program_id(0) == 0)
        def _():
            loss_ref[...] = jnp.zeros_like(loss_ref)
            dg_ref[...] = jnp.zeros_like(dg_ref)

        loss_ref[...] += jnp.broadcast_to(part, loss_ref.shape)
        dg_ref[...] += dg

    return _pcall(body, name="post_loss", grid=(T // tr,),
                  in_specs=[_row_spec(tr, D)] * 3 + [_full_spec((1, D))],
                  out_specs=[_row_spec(tr, D), _row_spec(tr, D), _full_spec((1, LANE)), _full_spec((1, D))],
                  out_shape=[jax.ShapeDtypeStruct((T, D), BF16), jax.ShapeDtypeStruct((T, D), F32),
                             jax.ShapeDtypeStruct((1, LANE), F32), jax.ShapeDtypeStruct((1, D), F32)],
                  compiler_params=_cp(("arbitrary",)))(yo, x, tgt, g_post)


def _memattn_fn(MW, q, mkv):
    hd = MW // MEM_HEADS
    scale = hd ** -0.5
    outs = []
    for h in range(MEM_HEADS):
        qh = q[:, h * hd:(h + 1) * hd]
        kh = mkv[:, h * hd:(h + 1) * hd]
        vh = mkv[:, MW + h * hd:MW + (h + 1) * hd]
        s = _mm(qh, kh, False, True, "bf") * scale
        e = jnp.exp(s - lax.stop_gradient(jnp.max(s, axis=-1, keepdims=True)))
        pr = e / jnp.sum(e, axis=-1, keepdims=True)
        outs.append(_mm(pr, vh, False, False, "bf"))
    return jnp.concatenate(outs, axis=1)


def _memattn_fwd(p, mkv, MW, tr=256):
    T = p.shape[0]
    tr = _tile(T, tr)

    def body(q_ref, kv_ref, o_ref):
        o_ref[...] = _memattn_fn(MW, q_ref[...], kv_ref[...])

    return _pcall(body, name="memattn_fwd", grid=(T // tr,),
                  in_specs=[_row_spec(tr, MW), _full_spec(mkv.shape)], out_specs=_row_spec(tr, MW),
                  out_shape=jax.ShapeDtypeStruct((T, MW), F32), compiler_params=_cp(("arbitrary",)))(p, mkv)


def _memattn_bwd(p, mkv, do, MW, tr=256):
    T = p.shape[0]
    tr = _tile(T, tr)

    def body(q_ref, kv_ref, do_ref, dq_ref, dkv_ref):
        _, vjp = jax.vjp(functools.partial(_memattn_fn, MW), q_ref[...], kv_ref[...])
        dq, dkv = vjp(do_ref[...])
        dq_ref[...] = dq

        @pl.when(pl.program_id(0) == 0)
        def _():
            dkv_ref[...] = jnp.zeros_like(dkv_ref)

        dkv_ref[...] += dkv

    return _pcall(body, name="memattn_bwd", grid=(T // tr,),
                  in_specs=[_row_spec(tr, MW), _full_spec(mkv.shape), _row_spec(tr, MW)],
                  out_specs=[_row_spec(tr, MW), _full_spec(mkv.shape)],
                  out_shape=[jax.ShapeDtypeStruct((T, MW), F32), jax.ShapeDtypeStruct(mkv.shape, F32)],
                  compiler_params=_cp(("arbitrary",)))(p, mkv, do)


def _fox_cum_fwd(p, b_f_pad, off, blk=512):
    T = p.shape[0]
    blk = _tile(T, blk)

    def body(f_ref, b_ref, cum_ref, cumt_ref, carry):
        @pl.when(pl.program_id(0) == 0)
        def _():
            carry[...] = jnp.zeros_like(carry)

        z = f_ref[...] + b_ref[...]
        logf = -_softplus(-z)
        row = lax.broadcasted_iota(jnp.int32, (blk, blk), 0)
        col = lax.broadcasted_iota(jnp.int32, (blk, blk), 1)
        tri = (col <= row).astype(F32)
        c = _dg(tri, logf, False, False, "hi") + carry[...]
        cum_ref[...] = c
        cumt_ref[...] = c.T
        carry[...] += jnp.sum(logf, axis=0, keepdims=True)

    return _pcall(body, name="fox_cum_fwd", grid=(T // blk,),
                  in_specs=[_row_spec(blk, LANE, off // LANE), _full_spec((1, LANE))],
                  out_specs=[_row_spec(blk, LANE), pl.BlockSpec((LANE, blk), lambda i: (0, i))],
                  out_shape=[jax.ShapeDtypeStruct((T, LANE), F32), jax.ShapeDtypeStruct((LANE, T), F32)],
                  scratch_shapes=[pltpu.VMEM((1, LANE), F32)], compiler_params=_cp(("arbitrary",)))(p, b_f_pad)


def _fox_cum_bwd(p, b_f_pad, dcum, off, blk=512):
    T = p.shape[0]
    blk = _tile(T, blk)
    nb = T // blk

    def body(f_ref, b_ref, dc_ref, df_ref, db_ref, carry):
        @pl.when(pl.program_id(0) == 0)
        def _():
            carry[...] = jnp.zeros_like(carry)
            db_ref[...] = jnp.zeros_like(db_ref)

        row = lax.broadcasted_iota(jnp.int32, (blk, blk), 0)
        col = lax.broadcasted_iota(jnp.int32, (blk, blk), 1)
        tri = (col >= row).astype(F32)
        dlogf = _dg(tri, dc_ref[...], False, False, "hi") + carry[...]
        carry[...] += jnp.sum(dc_ref[...], axis=0, keepdims=True)
        z = f_ref[...] + b_ref[...]
        dz = dlogf * (1.0 - _sigmoid(z))
        df_ref[...] = dz
        db_ref[...] += jnp.sum(dz, axis=0, keepdims=True)

    rev = lambda i: (nb - 1 - i, 0)
    return _pcall(body, name="fox_cum_bwd", grid=(nb,),
                  in_specs=[pl.BlockSpec((blk, LANE), lambda i: (nb - 1 - i, off // LANE)), _full_spec((1, LANE)),
                            pl.BlockSpec((blk, LANE), rev)],
                  out_specs=[pl.BlockSpec((blk, LANE), rev), _full_spec((1, LANE))],
                  out_shape=[jax.ShapeDtypeStruct((T, LANE), F32), jax.ShapeDtypeStruct((1, LANE), F32)],
                  scratch_shapes=[pltpu.VMEM((1, LANE), F32)], compiler_params=_cp(("arbitrary",)))(p, b_f_pad, dcum)


FOX_SCALE = HEAD ** -0.5
FOX_TQ, FOX_TK = 512, 1024


def _fox_scores(q, k, ck, q0=None, k0=None):
    s = _dg(q, k, False, True, "bf") - ck
    if q0 is None:
        return s
    qpos = q0 + lax.broadcasted_iota(jnp.int32, s.shape, 0)
    kpos = k0 + lax.broadcasted_iota(jnp.int32, s.shape, 1)
    return jnp.where(kpos <= qpos, s, NEG)


def _fox_c0(ck_ref, hh, pos):
    return ck_ref[0, hh:hh + 1, pl.ds(pl.multiple_of(pos, LANE), LANE)][:, 0:1]


def _fox_tiles(T):
    assert T % LANE == 0
    tq, tk = _tile(T, FOX_TQ), _tile(T, FOX_TK)
    shift = (tk // tq).bit_length() - 1
    assert tk == tq << shift
    return tq, tk, shift


def _fox_fwd(p, ck, C, offs):
    T = p.shape[0]
    tq, tk, shift = _fox_tiles(T)
    npair = C // LANE
    cb = lambda name: offs[name] // LANE

    def body(q_ref, k_ref, v_ref, ck_ref, o_ref, lse_ref):
        i = pl.program_id(1)
        nfull = i >> shift
        lse_ref[...] = jnp.zeros_like(lse_ref)
        heads = [slice(hh * HEAD, (hh + 1) * HEAD) for hh in range(2)]
        qs = [(q_ref[:, sl] * FOX_SCALE).astype(BF16) for sl in heads]
        c0s = [_fox_c0(ck_ref, hh, i * tq) for hh in range(2)]

        def step(j, carry, masked):
            off = pl.multiple_of(j * tk, tk)
            out = []
            for hh, sl in enumerate(heads):
                m, l, acc = carry[hh]
                k = k_ref[pl.ds(off, tk), sl].astype(BF16)
                v = v_ref[pl.ds(off, tk), sl].astype(BF16)
                ckv = ck_ref[0, hh:hh + 1, pl.ds(off, tk)] - c0s[hh]
                s = _fox_scores(qs[hh], k, ckv, i * tq, off) if masked else _fox_scores(qs[hh], k, ckv)
                m_new = jnp.maximum(m, jnp.max(s, axis=-1, keepdims=True))
                pr = jnp.exp(s - m_new)
                al = jnp.exp(m - m_new)
                l = al * l + jnp.sum(pr, axis=-1, keepdims=True)
                acc = al * acc + _dg(pr, v, False, False, "bf")
                out.append((m_new, l, acc))
            return tuple(out)

        one = (jnp.full((tq, 1), NEG, F32), jnp.zeros((tq, 1), F32), jnp.zeros((tq, HEAD), F32))
        carry = lax.fori_loop(0, nfull, lambda j, c: step(j, c, False), (one, one))
        for hh, (m, l, acc) in enumerate(step(nfull, carry, True)):
            o_ref[:, heads[hh]] = acc / l
            lse_ref[0, :, hh:hh + 1] = m + jnp.log(l)

    return _pcall(body, name="fox_fwd", grid=(npair, T // tq),
                  in_specs=[pl.BlockSpec((tq, LANE), lambda h, i: (i, cb("fq") + h)),
                            pl.BlockSpec((T, LANE), lambda h, i: (0, cb("fk") + h)),
                            pl.BlockSpec((T, LANE), lambda h, i: (0, cb("fv") + h)),
                            pl.BlockSpec((1, 8, T), lambda h, i: (h, 0, 0))],
                  out_specs=[pl.BlockSpec((tq, LANE), lambda h, i: (i, h)),
                             pl.BlockSpec((1, tq, 8), lambda h, i: (h, i, 0))],
                  out_shape=[jax.ShapeDtypeStruct((T, C), F32), jax.ShapeDtypeStruct((npair, T, 8), F32)],
                  compiler_params=_cp(("arbitrary", "arbitrary")))(p, p, p, ck)


def _fox_delta(p, ck, do, lse, C, offs):
    T = p.shape[0]
    tq, tk, shift = _fox_tiles(T)
    npair = C // LANE
    cb = lambda name: offs[name] // LANE

    def body(q_ref, k_ref, v_ref, ck_ref, do_ref, lse_ref, d_ref):
        i = pl.program_id(1)
        nfull = i >> shift
        d_ref[...] = jnp.zeros_like(d_ref)
        heads = [slice(hh * HEAD, (hh + 1) * HEAD) for hh in range(2)]
        qs = [(q_ref[:, sl] * FOX_SCALE).astype(BF16) for sl in heads]
        dos = [do_ref[:, sl].astype(BF16) for sl in heads]
        lses = [lse_ref[0, :, hh:hh + 1] for hh in range(2)]
        c0s = [_fox_c0(ck_ref, hh, i * tq) for hh in range(2)]

        def step(j, accs, masked):
            off = pl.multiple_of(j * tk, tk)
            out = []
            for hh, sl in enumerate(heads):
                k = k_ref[pl.ds(off, tk), sl].astype(BF16)
                v = v_ref[pl.ds(off, tk), sl].astype(BF16)
                ckv = ck_ref[0, hh:hh + 1, pl.ds(off, tk)] - c0s[hh]
                s = _fox_scores(qs[hh], k, ckv, i * tq, off) if masked else _fox_scores(qs[hh], k, ckv)
                pr = jnp.exp(s - lses[hh])
                dp = _dg(dos[hh], v, False, True, "bf")
                out.append(accs[hh] + jnp.sum(pr * dp, axis=-1, keepdims=True))
            return tuple(out)

        z = jnp.zeros((tq, 1), F32)
        accs = lax.fori_loop(0, nfull, lambda j, c: step(j, c, False), (z, z))
        for hh, acc in enumerate(step(nfull, accs, True)):
            d_ref[0, :, hh:hh + 1] = acc

    return _pcall(body, name="fox_delta", grid=(npair, T // tq),
                  in_specs=[pl.BlockSpec((tq, LANE), lambda h, i: (i, cb("fq") + h)),
                            pl.BlockSpec((T, LANE), lambda h, i: (0, cb("fk") + h)),
                            pl.BlockSpec((T, LANE), lambda h, i: (0, cb("fv") + h)),
                            pl.BlockSpec((1, 8, T), lambda h, i: (h, 0, 0)),
                            pl.BlockSpec((tq, LANE), lambda h, i: (i, h)),
                            pl.BlockSpec((1, tq, 8), lambda h, i: (h, i, 0))],
                  out_specs=pl.BlockSpec((1, tq, 8), lambda h, i: (h, i, 0)),
                  out_shape=jax.ShapeDtypeStruct((npair, T, 8), F32),
                  compiler_params=_cp(("arbitrary", "arbitrary")))(p, p, p, ck, do, lse)


def _fox_bwd(p, ck, delta, do, lse, C, offs):
    T = p.shape[0]
    tq, tk, shift = _fox_tiles(T)
    ratio = tk // tq
    nq = T // tq
    npair = C // LANE
    cb = lambda name: offs[name] // LANE

    def body(q_ref, k_ref, v_ref, ck_ref, ckall_ref, dl_ref, do_ref, lse_ref, dq_ref, dk_ref, dv_ref, dck_ref):
        j = pl.program_id(1)

        @pl.when(j == 0)
        def _():
            dq_ref[...] = jnp.zeros_like(dq_ref)

        dck_ref[...] = jnp.zeros_like(dck_ref)
        heads = [slice(hh * HEAD, (hh + 1) * HEAD) for hh in range(2)]
        ks = [k_ref[:, sl].astype(BF16) for sl in heads]
        vs = [v_ref[:, sl].astype(BF16) for sl in heads]
        cks = [ck_ref[0, hh:hh + 1, :] for hh in range(2)]

        def step(i, carry, masked):
            off = pl.multiple_of(i * tq, tq)
            out = []
            for hh, sl in enumerate(heads):
                dk, dv, dck = carry[hh]
                q = (q_ref[pl.ds(off, tq), sl] * FOX_SCALE).astype(BF16)
                dov = do_ref[pl.ds(off, tq), sl]
                lsev = lse_ref[0, pl.ds(off, tq), hh:hh + 1]
                ckv = cks[hh] - _fox_c0(ckall_ref, hh, off)
                s = _fox_scores(q, ks[hh], ckv, off, j * tk) if masked else _fox_scores(q, ks[hh], ckv)
                pr = jnp.exp(s - lsev)
                dv = dv + _dg(pr, dov, True, False, "bf")
                dp = _dg(dov, vs[hh], False, True, "bf")
                ds = pr * (dp - dl_ref[0, pl.ds(off, tq), hh:hh + 1])
                dk = dk + _dg(ds, q, True, False, "bf")
                dq_ref[pl.ds(off, tq), sl] += _dg(ds, ks[hh], False, False, "bf") * FOX_SCALE
                out.append((dk, dv, dck - jnp.sum(ds, axis=0, keepdims=True)))
            return tuple(out)

        z = jnp.zeros((tk, HEAD), F32)
        carry = ((z, z, jnp.zeros((1, tk), F32)),) * 2
        for r in range(ratio):
            carry = step(j * ratio + r, carry, True)
        carry = lax.fori_loop((j + 1) * ratio, nq, lambda i, c: step(i, c, False), carry)
        for hh, (dk, dv, dck) in enumerate(carry):
            dk_ref[:, heads[hh]] = dk
            dv_ref[:, heads[hh]] = dv
            dck_ref[0, hh:hh + 1, :] = dck

    full = lambda h, j: (0, h)
    return _pcall(body, name="fox_bwd", grid=(npair, T // tk),
                  in_specs=[pl.BlockSpec((T, LANE), lambda h, j: (0, cb("fq") + h)),
                            pl.BlockSpec((tk, LANE), lambda h, j: (j, cb("fk") + h)),
                            pl.BlockSpec((tk, LANE), lambda h, j: (j, cb("fv") + h)),
                            pl.BlockSpec((1, 8, tk), lambda h, j: (h, 0, j)),
                            pl.BlockSpec((1, 8, T), lambda h, j: (h, 0, 0)),
                            pl.BlockSpec((1, T, 8), lambda h, j: (h, 0, 0)), pl.BlockSpec((T, LANE), full),
                            pl.BlockSpec((1, T, 8), lambda h, j: (h, 0, 0))],
                  out_specs=[pl.BlockSpec((T, LANE), full),
                             pl.BlockSpec((tk, LANE), lambda h, j: (j, h)),
                             pl.BlockSpec((tk, LANE), lambda h, j: (j, h)),
                             pl.BlockSpec((1, 8, tk), lambda h, j: (h, 0, j))],
                  out_shape=[jax.ShapeDtypeStruct((T, C), F32)] * 3 + [jax.ShapeDtypeStruct((npair, 8, T), F32)],
                  compiler_params=_cp(("arbitrary", "arbitrary")))(p, p, p, ck, ck, delta, do, lse)


@jax.custom_vjp
def _tri_sum(tri, x):
    return _dg(tri, x, False, False, "e3")


def _tri_sum_fwd(tri, x):
    return _dg(tri, x, False, False, "e3"), tri


def _tri_sum_bwd(tri, g):
    return jnp.zeros_like(tri), _dg(tri, g, True, False, "e3")


_tri_sum.defvjp(_tri_sum_fwd, _tri_sum_bwd)


@jax.custom_vjp
def _solve(a, rhs):
    return _solve_fwd(a, rhs)[0]


def _solve_fwd(a, rhs):
    powers = [a]
    for _ in range(CHUNK.bit_length() - 2):
        powers.append(_dg(powers[-1], powers[-1], False, False, "x3"))
    u = rhs
    for pw in powers:
        u = u + _dg(pw, u, False, False, "x3")
    return u, (powers, u)


def _solve_bwd(res, g):
    powers, u = res
    w = g
    for pw in powers:
        w = w + _dg(pw, w, True, False, "x3")
    return _dg(w, u, False, True, "x3"), w


_solve.defvjp(_solve_fwd, _solve_bwd)


def _chunk_fn(S0, r, lw, k, v, a, b):
    nh, n = r.shape[0], r.shape[1]
    row = lax.broadcasted_iota(jnp.int32, (nh, n, n), 1)
    col = lax.broadcasted_iota(jnp.int32, (nh, n, n), 2)
    incl, strict = col <= row, col < row
    mm = lambda x, y, ta=False, tb=False: _mm(x, y, ta, tb, "x3")
    g = _tri_sum(incl.astype(F32), lw)
    einv = jnp.exp(-g)
    rt, at, bt, kt = r * jnp.exp(g), a * jnp.exp(g - lw), b * einv, k * einv
    a_ab = jnp.where(strict, mm(at, bt, tb=True), 0.0)
    a_ak = jnp.where(strict, mm(at, kt, tb=True), 0.0)
    r_b = jnp.where(incl, mm(rt, bt, tb=True), 0.0)
    r_k = jnp.where(incl, mm(rt, kt, tb=True), 0.0)
    u = _solve(a_ab, mm(at, S0, tb=True) + mm(a_ak, v))
    y = mm(rt, S0, tb=True) + mm(r_b, u) + mm(r_k, v)
    g_end = jnp.sum(lw, axis=1, keepdims=True)
    s_end = (S0 + mm(u, bt, ta=True) + mm(v, kt, ta=True)) * jnp.exp(g_end)
    return y, s_end


SCAN_HEADS = 24
SCAN_ROWS = 64


def _scan_group(C):
    nh = SCAN_HEADS
    while C % (nh * HEAD):
        nh -= 2
    return nh, nh * HEAD


def _scan_fwd(r, lw, k, v, a, b):
    T, C = r.shape
    tc = _tile(T, SCAN_ROWS)
    ncs = tc // CHUNK
    nh, GW = _scan_group(C)
    ngroup = C // GW

    def body(r_ref, lw_ref, k_ref, v_ref, a_ref, b_ref, y_ref, ck_ref, state):
        @pl.when(pl.program_id(1) == 0)
        def _():
            state[...] = jnp.zeros_like(state)

        heads = [slice(hh * HEAD, (hh + 1) * HEAD) for hh in range(nh)]
        split = lambda ref, rows: jnp.stack([ref[rows, sl] for sl in heads])
        st = split(state, slice(None))
        for c in range(ncs):
            rows = slice(c * CHUNK, (c + 1) * CHUNK)
            for hh, sl in enumerate(heads):
                ck_ref[0, c, :, sl] = st[hh]
            y, st = _chunk_fn(st, *[split(ref, rows) for ref in (r_ref, lw_ref, k_ref, v_ref, a_ref, b_ref)])
            for hh, sl in enumerate(heads):
                y_ref[rows, sl] = y[hh]
        for hh, sl in enumerate(heads):
            state[:, sl] = st[hh]

    spec = pl.BlockSpec((tc, GW), lambda h, t: (t, h))
    return _pcall(body, name="rwkv_scan_fwd", grid=(ngroup, T // tc),
                  in_specs=[spec] * 6,
                  out_specs=[spec, pl.BlockSpec((1, ncs, HEAD, GW), lambda h, t: (h, t, 0, 0))],
                  out_shape=[jax.ShapeDtypeStruct((T, C), F32),
                             jax.ShapeDtypeStruct((ngroup, T // CHUNK, HEAD, GW), F32)],
                  scratch_shapes=[pltpu.VMEM((HEAD, GW), F32)],
                  compiler_params=_cp(("arbitrary", "arbitrary")))(r, lw, k, v, a, b)


def _scan_bwd(r, lw, k, v, a, b, ckpt, dy):
    T, C = r.shape
    tc = _tile(T, SCAN_ROWS)
    ncs = tc // CHUNK
    nh, GW = _scan_group(C)
    ngroup = C // GW
    nt = T // tc

    def body(r_ref, lw_ref, k_ref, v_ref, a_ref, b_ref, ck_ref, dy_ref, dr, dlw, dk, dv, da, db, dstate):
        @pl.when(pl.program_id(1) == 0)
        def _():
            dstate[...] = jnp.zeros_like(dstate)

        outs = (dr, dlw, dk, dv, da, db)
        heads = [slice(hh * HEAD, (hh + 1) * HEAD) for hh in range(nh)]
        split = lambda ref, rows: jnp.stack([ref[rows, sl] for sl in heads])
        dst = split(dstate, slice(None))
        for c in reversed(range(ncs)):
            rows = slice(c * CHUNK, (c + 1) * CHUNK)
            s0 = jnp.stack([ck_ref[0, c, :, sl] for sl in heads])
            _, vjp = jax.vjp(_chunk_fn, s0, *[split(ref, rows) for ref in (r_ref, lw_ref, k_ref, v_ref, a_ref, b_ref)])
            grads = vjp((split(dy_ref, rows), dst))
            dst = grads[0]
            for o, gval in zip(outs, grads[1:]):
                for hh, sl in enumerate(heads):
                    o[rows, sl] = gval[hh]
        for hh, sl in enumerate(heads):
            dstate[:, sl] = dst[hh]

    spec = pl.BlockSpec((tc, GW), lambda h, t: (nt - 1 - t, h))
    return _pcall(body, name="rwkv_scan_bwd", grid=(ngroup, nt),
                  in_specs=[spec] * 6 + [pl.BlockSpec((1, ncs, HEAD, GW), lambda h, t: (h, nt - 1 - t, 0, 0)), spec],
                  out_specs=[spec] * 6, out_shape=[jax.ShapeDtypeStruct((T, C), F32)] * 6,
                  scratch_shapes=[pltpu.VMEM((HEAD, GW), F32)],
                  compiler_params=_cp(("arbitrary", "arbitrary")))(r, lw, k, v, a, b, ckpt, dy)


def _adam(w, g, m, v):
    m = ADAM_B1 * m + (1.0 - ADAM_B1) * g
    v = ADAM_B2 * v + (1.0 - ADAM_B2) * (g * g)
    m_hat = m / (1.0 - ADAM_B1 ** ADAM_STEP)
    v_hat = v / (1.0 - ADAM_B2 ** ADAM_STEP)
    return -ADAM_LR * (m_hat / (jnp.sqrt(v_hat) + ADAM_EPS) + ADAM_WD * w), m, v


def _sum_adam(parts, w, m, v, name):
    n, R, W = parts.shape
    tr = _tile(R, max(8, min(256, (1 << 20) // (n * W))))

    def body(p_ref, w_ref, m_ref, v_ref, g_ref, d_ref, nm_ref, nv_ref):
        g = p_ref[0].astype(F32)
        for s in range(1, n):
            g = g + p_ref[s].astype(F32)
        d, nm, nv = _adam(w_ref[...], g, m_ref[...], v_ref[...])
        g_ref[...] = g
        d_ref[...] = d
        nm_ref[...] = nm
        nv_ref[...] = nv

    return _pcall(body, name=name, grid=(R // tr,),
                  in_specs=[pl.BlockSpec((n, tr, W), lambda i: (0, i, 0))] + [_row_spec(tr, W)] * 3,
                  out_specs=[_row_spec(tr, W)] * 4, out_shape=[jax.ShapeDtypeStruct((R, W), F32)] * 4,
                  compiler_params=_cp(("arbitrary",)))(parts, w, m, v)


def _pad_lanes(vec, width):
    return jnp.pad(vec, ((0, 0), (0, width - vec.shape[1])))


def _logical_cols(blocks, lo, hi):
    B, out = blocks.shape[2], []
    while lo < hi:
        j, o = divmod(lo, B)
        n = min(hi - lo, B - o)
        out.append(blocks[j, :, o:o + n])
        lo += n
    return out


def kernel(x, mem, g_pre, w_in, mu_rwkv, w0, w_decay_up, a0, w_iclr_up, k_k, k_a, r_k, ln_x_w, ln_x_b, b_f, g_mem, w_mem_kv, w_out, g_post, loss_target, m_g_pre, m_w_in, m_mu_rwkv, m_w0, m_w_decay_up, m_a0, m_w_iclr_up, m_k_k, m_k_a, m_r_k, m_ln_x_w, m_ln_x_b, m_b_f, m_g_mem, m_w_mem_kv, m_w_out, m_g_post, v_g_pre, v_w_in, v_mu_rwkv, v_w0, v_w_decay_up, v_a0, v_w_iclr_up, v_k_k, v_k_a, v_r_k, v_ln_x_w, v_ln_x_b, v_b_f, v_g_mem, v_w_mem_kv, v_w_out, v_g_post):
    T, D = x.shape[1], x.shape[2]
    C = w0.shape[1]
    L = w_decay_up.shape[1]
    H = C // HEAD
    MW = w_mem_kv.shape[2] // 2
    SH = 3 * C + 2 * L
    IN = NDEV * w_in.shape[2]
    assert IN == SH + 5 * C + H + 2 * MW and D == 2 * C + MW and H % 2 == 0 and H <= LANE
    assert C % LANE == 0 and L % LANE == 0 and MW % (MEM_HEADS * HEAD) == 0 and T % CHUNK == 0
    offs = dict(grw=SH, fq=SH + C, fk=SH + 2 * C, fv=SH + 3 * C, gfx=SH + 4 * C, mq=SH + 5 * C, gmq=SH + 5 * C + MW,
                fl=SH + 5 * C + 2 * MW)
    NI = -(-(offs["fl"] + LANE) // 1024) * 1024
    l_fl = SH + 4 * C

    x2, mem2, tgt2 = x[0], mem[0], loss_target[0]

    wg = _exchange(w_in[0].astype(BF16), "gather_w_in", True)
    w_perm = jnp.concatenate(_logical_cols(wg, 0, l_fl) + _logical_cols(wg, l_fl + H, IN)
                             + _logical_cols(wg, l_fl, l_fl + H) + [jnp.zeros((D, NI - IN), BF16)], axis=1)

    h = _rmsnorm_fwd(x2, g_pre, "rmsnorm_pre")
    p, w_out_f, w_kv_f, lora = _matmul(
        h, w_perm, False, False, "in_proj",
        riders=[(w_out[0].astype(BF16), True), (w_mem_kv[0].astype(BF16), True),
                (jnp.concatenate([w_decay_up[0], w_iclr_up[0]], axis=0), True)])
    w_out_f, w_kv_f = w_out_f.reshape(D, D), w_kv_f.reshape(D, 2 * MW)
    lora = jnp.transpose(lora, (1, 0, 2)).reshape(2 * L, C)
    wdu_f, wiu_f = lora[:L], lora[L:]

    E, ET = _head_indicator(C, LANE)
    prep_params = [mu_rwkv, w0, a0, k_k, k_a, wdu_f, wiu_f, E, ET]
    mix_params = [ln_x_w, ln_x_b, r_k.reshape(1, C), E, ET]
    b_f_pad = _pad_lanes(b_f, LANE)
    col = lambda name, w: p[:, offs[name]:offs[name] + w]
    g_rwkv, g_fox, mq, g_mq = col("grw", C), col("gfx", C), col("mq", MW), col("gmq", MW)
    r, lw, kmod, v, a, b = _prep_fwd(p, prep_params, C, L)
    y_scan, ckpt = _scan_fwd(r, lw, kmod, v, a, b)

    cum, cum_t = _fox_cum_fwd(p, b_f_pad, offs["fl"])
    ck = jnp.pad(cum_t[:H].reshape(H // 2, 2, T), ((0, 0), (0, 6), (0, 0)))
    y_fox, lse = _fox_fwd(p, ck, C, offs)

    memn = _rmsnorm_fwd(mem2, g_mem, "rmsnorm_mem")
    mkv = _matmul(memn, w_kv_f, False, False, "mem_kv_proj")
    y_mem = _memattn_fwd(mq, mkv, MW)

    acts = [y_scan, r, kmod, v, g_rwkv, y_fox, g_fox, y_mem, g_mq]
    ycat = _mix_fwd(acts, mix_params, C, MW)
    yo = _matmul(ycat, w_out_f, False, False, "out_proj")
    d_yo, d_out, loss_part, dg_post = _post(yo, x2, tgt2, g_post)
    loss = lax.psum(loss_part[0, 0], AXES)

    g_w_out = _matmul(ycat, d_yo, True, False, "grad_w_out", BF16)
    d_ycat = _matmul(d_yo, w_out_f, False, True, "d_ycat")
    (d_y, d_r1, d_k1, d_v1, d_grw, d_yfox, d_gfx, d_ymem, d_gmq, dg_lnw, dg_lnb, dg_rk) = _mix_bwd(
        acts, mix_params, d_ycat, C, MW)

    d_mq, d_mkv = _memattn_bwd(mq, mkv, d_ymem, MW)
    g_w_kv = _matmul(memn, d_mkv, True, False, "grad_w_mem_kv", BF16)
    d_memn = _matmul(d_mkv, w_kv_f, False, True, "d_memn")
    _, dg_mem = _rmsnorm_bwd(mem2, g_mem, d_memn, None, "rmsnorm_mem_bwd")

    delta = _fox_delta(p, ck, d_yfox, lse, C, offs)
    d_fq, d_fk, d_fv, d_ck = _fox_bwd(p, ck, delta, d_yfox, lse, C, offs)
    d_cum = _pad_lanes(d_ck[:, :2, :].reshape(H, T).T, LANE)
    d_fl, dg_bf = _fox_cum_bwd(p, b_f_pad, d_cum, offs["fl"])

    d_r, d_lw, d_k, d_v, d_a, d_b = _scan_bwd(r, lw, kmod, v, a, b, ckpt, d_y)
    cts = [d_r, d_lw, d_k, d_v, d_a, d_b, d_r1, d_k1, d_v1]
    (d_ps, d_prev, dg_mu, dg_w0, dg_a0, dg_kk, dg_ka, dg_wdu, dg_wiu) = _prep_bwd(p, prep_params, cts, C, L)
    d_sh = _shift_combine(d_ps, d_prev)

    tobf = lambda z: z.astype(BF16)
    dp = jnp.concatenate([d_sh, tobf(d_grw), tobf(d_fq), tobf(d_fk), tobf(d_fv), tobf(d_gfx), tobf(d_mq), tobf(d_gmq),
                          tobf(d_fl), jnp.zeros((T, NI - offs["fl"] - LANE), BF16)], axis=1)
    g_lora = jnp.concatenate([dg_wdu, dg_wiu], axis=0)
    g_w_perm, parts_out, parts_kv, parts_lora = _matmul(
        h, dp, True, False, "grad_w_in", BF16,
        riders=[(g_w_out.reshape(NDEV, D // NDEV, D), False), (g_w_kv.reshape(NDEV, D // NDEV, 2 * MW), False),
                (jnp.transpose(g_lora.reshape(2 * L, NDEV, C // NDEV), (1, 0, 2)), False)])

    def internal_cols(lo, hi):
        out = []
        for first, last, shift in ((0, l_fl, 0), (l_fl, l_fl + H, offs["fl"] - l_fl), (l_fl + H, IN, -H)):
            s0, s1 = max(lo, first), min(hi, last)
            if s0 < s1:
                out.append(g_w_perm[:, s0 + shift:s1 + shift])
        return out

    blk = IN // NDEV
    g_blocks = jnp.stack([jnp.concatenate(internal_cols(j * blk, (j + 1) * blk), axis=1) for j in range(NDEV)])
    d_h, parts_in = _matmul(dp, w_perm, False, True, "d_h", tk=5120, riders=[(g_blocks, False)])
    grad_x, dg_pre = _rmsnorm_bwd(x2, g_pre, d_h, d_out, "rmsnorm_pre_bwd")

    gw_in, dw_in, nm_w_in, nv_w_in = _sum_adam(parts_in, w_in[0], m_w_in[0], v_w_in[0], "adam_w_in")
    gw_out, dw_out, nm_w_out, nv_w_out = _sum_adam(parts_out, w_out[0], m_w_out[0], v_w_out[0], "adam_w_out")
    gw_kv, dw_kv, nm_w_kv, nv_w_kv = _sum_adam(parts_kv, w_mem_kv[0], m_w_mem_kv[0], v_w_mem_kv[0], "adam_w_mem_kv")
    cat2 = lambda u, w_: jnp.concatenate([u[0], w_[0]], axis=0)
    lora_res = _sum_adam(parts_lora, cat2(w_decay_up, w_iclr_up), cat2(m_w_decay_up, m_w_iclr_up),
                         cat2(v_w_decay_up, v_w_iclr_up), "adam_lora")

    small = [("g_pre", g_pre, m_g_pre, v_g_pre, dg_pre), ("mu_rwkv", mu_rwkv, m_mu_rwkv, v_mu_rwkv, dg_mu),
             ("w0", w0, m_w0, v_w0, dg_w0), ("a0", a0, m_a0, v_a0, dg_a0), ("k_k", k_k, m_k_k, v_k_k, dg_kk),
             ("k_a", k_a, m_k_a, v_k_a, dg_ka), ("r_k", r_k.reshape(1, C), m_r_k.reshape(1, C), v_r_k.reshape(1, C), dg_rk),
             ("ln_x_w", ln_x_w, m_ln_x_w, v_ln_x_w, dg_lnw), ("ln_x_b", ln_x_b, m_ln_x_b, v_ln_x_b, dg_lnb),
             ("b_f", _pad_lanes(b_f, LANE), _pad_lanes(m_b_f, LANE), _pad_lanes(v_b_f, LANE), dg_bf),
             ("g_mem", g_mem, m_g_mem, v_g_mem, dg_mem), ("g_post", g_post, m_g_post, v_g_post, dg_post)]
    widths = [s[1].shape[1] for s in small]
    pack = lambda idx: jnp.concatenate([s[idx] for s in small], axis=1).reshape(-1, LANE)
    parts_small = _exchange(pack(4), "gather_small_grads", True)
    res_small = _sum_adam(parts_small, pack(1), pack(2), pack(3), "adam_small")

    def unpack(flat):
        flat = flat.reshape(1, -1)
        out, o = {}, 0
        for (name, *_), wd in zip(small, widths):
            out[name] = flat[:, o:o + wd]
            o += wd
        out["b_f"] = out["b_f"][:, :H]
        out["r_k"] = out["r_k"].reshape(1, H, HEAD)
        return out

    sg, sd, sm, sv = [unpack(z) for z in res_small]
    big = {"w_in": (gw_in, dw_in, nm_w_in, nv_w_in), "w_out": (gw_out, dw_out, nm_w_out, nv_w_out),
           "w_mem_kv": (gw_kv, dw_kv, nm_w_kv, nv_w_kv),
           "w_decay_up": tuple(z[:L] for z in lora_res), "w_iclr_up": tuple(z[L:] for z in lora_res)}
    order = ["g_pre", "w_in", "mu_rwkv", "w0", "w_decay_up", "a0", "w_iclr_up", "k_k", "k_a", "r_k", "ln_x_w", "ln_x_b",
             "b_f", "g_mem", "w_mem_kv", "w_out", "g_post"]

    def pick(name, idx):
        if name in big:
            return big[name][idx][None]
        return (sg, sd, sm, sv)[idx][name]

    outs = [loss, grad_x[None]]
    for idx in range(4):
        outs += [pick(n, idx) for n in order]
    return tuple(outs)
```

```python
import functools

import jax
import jax.numpy as jnp
from jax import lax
from jax.experimental import pallas as pl
from jax.experimental.pallas import tpu as pltpu

F32, BF16 = jnp.float32, jnp.bfloat16
HI = lax.Precision.HIGHEST
NDEV = 8
AXES = ("x", "y", "c")
HEAD = 64
CHUNK = 64
MEM_HEADS = 4
LANE = 128
RMS_EPS = 1e-6
GN_EPS = 64e-5
NEG = -1e30
ADAM_LR, ADAM_B1, ADAM_B2, ADAM_EPS, ADAM_WD, ADAM_STEP = 0.001, 0.9, 0.999, 1e-08, 0.01, 10
VMEM_LIMIT = 56 * 1024 * 1024


def _pcall(body, **kw):
    return pl.pallas_call(body, **kw)


def _cp(sem=None, vmem=VMEM_LIMIT):
    return pltpu.CompilerParams(dimension_semantics=sem, vmem_limit_bytes=vmem)


def _tile(n, pref):
    for t in (pref, 1024, 512, 256, 128, 64, 32, 16, 8):
        if t <= pref and n % t == 0:
            return t
    return n


def _dg(a, b, ta, tb, mode):
    nb = a.ndim - 2
    ca = nb + (0 if ta else 1)
    cb = nb + (1 if tb else 0)
    dims = (((ca,), (cb,)), (tuple(range(nb)), tuple(range(nb))))
    if mode == "x3":
        a_hi, b_hi = a.astype(BF16), b.astype(BF16)
        a_lo, b_lo = (a - a_hi.astype(F32)).astype(BF16), (b - b_hi.astype(F32)).astype(BF16)
        a3 = jnp.concatenate([a_hi, a_lo, a_hi], axis=ca)
        b3 = jnp.concatenate([b_hi, b_hi, b_lo], axis=cb)
        return lax.dot_general(a3, b3, dims, preferred_element_type=F32)
    if mode == "e3":
        b1 = b.astype(BF16)
        rest = b - b1.astype(F32)
        b2 = rest.astype(BF16)
        b3 = (rest - b2.astype(F32)).astype(BF16)
        a = a.astype(BF16)
        return lax.dot_general(jnp.concatenate([a, a, a], axis=ca), jnp.concatenate([b1, b2, b3], axis=cb), dims,
                               preferred_element_type=F32)
    if mode == "x2":
        a_hi, b = a.astype(BF16), b.astype(BF16)
        a_lo = (a - a_hi.astype(F32)).astype(BF16)
        dot = lambda u: lax.dot_general(u, b, dims, preferred_element_type=F32)
        return dot(a_hi) + dot(a_lo)
    if mode == "bf":
        a, b, prec = a.astype(BF16), b.astype(BF16), None
    else:
        prec = HI
    return lax.dot_general(a, b, dims, preferred_element_type=F32, precision=prec)


@functools.partial(jax.custom_vjp, nondiff_argnums=(2, 3, 4))
def _mm(a, b, ta, tb, mode):
    return _dg(a, b, ta, tb, mode)


def _mm_fwd(a, b, ta, tb, mode):
    return _dg(a, b, ta, tb, mode), (a, b)


def _mm_bwd(ta, tb, mode, res, g):
    a, b = res
    da = _dg(g, b, False, not tb, mode) if not ta else _dg(b, g, tb, True, mode)
    db = _dg(a, g, not ta, False, mode) if not tb else _dg(g, a, True, ta, mode)
    return da, db


_mm.defvjp(_mm_fwd, _mm_bwd)


@jax.custom_vjp
def _seg(a, e):
    return _dg(a, e, False, False, "x2")


def _seg_fwd(a, e):
    return _dg(a, e, False, False, "x2"), e


def _seg_bwd(e, g):
    return _dg(g, e, False, True, "x2"), jnp.zeros_like(e)


_seg.defvjp(_seg_fwd, _seg_bwd)


def _sigmoid(z):
    return 1.0 / (1.0 + jnp.exp(-z))


def _softplus(z):
    return jnp.maximum(z, 0.0) + jnp.log(1.0 + jnp.exp(-jnp.abs(z)))


def _silu(z):
    return z * _sigmoid(z)


def _rms(x, g):
    return x * lax.rsqrt(jnp.mean(x * x, axis=-1, keepdims=True) + RMS_EPS) * g


HBM_SPEC = pl.BlockSpec(memory_space=pltpu.HBM)
EXCHANGE_SEMS = [pltpu.SemaphoreType.DMA((NDEV - 1,)), pltpu.SemaphoreType.DMA((NDEV - 1,)), pltpu.SemaphoreType.DMA(())]


def _exchange_copies(gather, x_ref, o_ref, send_sems, recv_sems, local_sem, arrivals):
    ix, iy, ic = lax.axis_index("x"), lax.axis_index("y"), lax.axis_index("c")
    me = 4 * ix + 2 * iy + ic

    def src(dest):
        return x_ref if gather else x_ref.at[dest]

    mine = pltpu.make_async_copy(src(me), o_ref.at[me], local_sem)
    pairs = []
    for k in range(1, NDEV):
        px = 1 - ix if (k >> 2) & 1 else ix
        py = 1 - iy if (k >> 1) & 1 else iy
        pc = 1 - ic if k & 1 else ic
        peer = 4 * px + 2 * py + pc
        send = pltpu.make_async_remote_copy(
            src_ref=src(peer), dst_ref=o_ref.at[me], send_sem=send_sems.at[k - 1], recv_sem=recv_sems.at[k - 1],
            device_id=(px, py, pc), device_id_type=pl.DeviceIdType.MESH)
        arrival = arrivals and pltpu.make_async_remote_copy(
            src_ref=src(peer), dst_ref=o_ref.at[peer], send_sem=send_sems.at[k - 1], recv_sem=recv_sems.at[k - 1],
            device_id=(ix, iy, ic), device_id_type=pl.DeviceIdType.MESH)
        pairs.append((send, arrival))
    return mine, pairs


def _exchange_start(*args):
    mine, pairs = _exchange_copies(*args, arrivals=False)
    mine.start()
    for send, _ in pairs:
        send.start()


def _exchange_wait(*args):
    mine, pairs = _exchange_copies(*args, arrivals=True)
    for send, arrival in pairs:
        send.wait_send()
        arrival.wait_recv()
    mine.wait()


def _exchange_shape(x, gather):
    return jax.ShapeDtypeStruct((NDEV,) + tuple(x.shape if gather else x.shape[1:]), x.dtype)


def _exchange(x, name, gather):
    def body(x_ref, o_ref, *sems):
        _exchange_start(gather, x_ref, o_ref, *sems)
        _exchange_wait(gather, x_ref, o_ref, *sems)

    return _pcall(body, name=name, out_shape=_exchange_shape(x, gather), in_specs=[HBM_SPEC], out_specs=HBM_SPEC,
                  scratch_shapes=list(EXCHANGE_SEMS))(x)


MAX_FULL_K = 4096


def _matmul(a, b, ta, tb, name, out_dtype=F32, tm=1024, tn=1024, tk=3072, riders=()):
    M, K = (a.shape[1], a.shape[0]) if ta else a.shape
    N = b.shape[0] if tb else b.shape[1]
    assert (b.shape[1] if tb else b.shape[0]) == K
    if K <= MAX_FULL_K:
        tk = K
    else:
        tm, tk = min(tm, 512), _tile(K, tk)
    tm, tn = _tile(M, tm), _tile(N, tn)
    grid = (M // tm, N // tn, K // tk)
    nk, nr = grid[2], len(riders)

    def body(*refs):
        a_ref, b_ref, x_refs = refs[0], refs[1], refs[2:2 + nr]
        o_ref, xo_refs, rest = refs[2 + nr], refs[3 + nr:3 + 2 * nr], refs[3 + 2 * nr:]
        sems = rest[1:] if nk > 1 else rest
        ids = [pl.program_id(d) for d in range(3)]
        jobs = [(riders[q][1], x_refs[q], xo_refs[q]) + tuple(sems[3 * q:3 * q + 3]) for q in range(nr)]

        if nr:
            @pl.when((ids[0] == 0) & (ids[1] == 0) & (ids[2] == 0))
            def _():
                for job in jobs:
                    _exchange_start(*job)

        if nk == 1:
            o_ref[...] = _dg(a_ref[...], b_ref[...], ta, tb, "bf").astype(o_ref.dtype)
        else:
            acc = rest[0]

            @pl.when(ids[2] == 0)
            def _():
                acc[...] = jnp.zeros_like(acc)

            acc[...] += _dg(a_ref[...], b_ref[...], ta, tb, "bf")

            @pl.when(ids[2] == nk - 1)
            def _():
                o_ref[...] = acc[...].astype(o_ref.dtype)

        if nr:
            @pl.when((ids[0] == grid[0] - 1) & (ids[1] == grid[1] - 1) & (ids[2] == nk - 1))
            def _():
                for job in jobs:
                    _exchange_wait(*job)

    a_spec = pl.BlockSpec((tk, tm), lambda i, j, k: (k, i)) if ta else pl.BlockSpec((tm, tk), lambda i, j, k: (i, k))
    b_spec = pl.BlockSpec((tn, tk), lambda i, j, k: (j, k)) if tb else pl.BlockSpec((tk, tn), lambda i, j, k: (k, j))
    out = _pcall(
        body, name=name, grid=grid,
        in_specs=[a_spec, b_spec] + [HBM_SPEC] * nr,
        out_specs=[pl.BlockSpec((tm, tn), lambda i, j, k: (i, j))] + [HBM_SPEC] * nr,
        out_shape=[jax.ShapeDtypeStruct((M, N), out_dtype)] + [_exchange_shape(x, g) for x, g in riders],
        scratch_shapes=([pltpu.VMEM((tm, tn), F32)] if nk > 1 else []) + list(EXCHANGE_SEMS) * nr,
        compiler_params=_cp(("arbitrary", "arbitrary", "arbitrary")),
    )(a, b, *[x for x, _ in riders])
    return tuple(out) if nr else out[0]


def _row_spec(tr, width, col_block=0):
    return pl.BlockSpec((tr, width), lambda i: (i, col_block))


def _full_spec(shape):
    nd = len(shape)
    return pl.BlockSpec(tuple(shape), lambda i: (0,) * nd)


def _rmsnorm_fwd(x, g, name, tr=256):
    R, D = x.shape
    tr = _tile(R, tr)

    def body(x_ref, g_ref, o_ref):
        o_ref[...] = _rms(x_ref[...], g_ref[...]).astype(BF16)

    return _pcall(body, name=name, grid=(R // tr,),
                  in_specs=[_row_spec(tr, D), _full_spec((1, D))], out_specs=_row_spec(tr, D),
                  out_shape=jax.ShapeDtypeStruct((R, D), BF16), compiler_params=_cp(("arbitrary",)))(x, g)


def _rmsnorm_bwd(x, g, dy, extra, name, tr=128):
    R, D = x.shape
    tr = _tile(R, tr)
    has_extra = extra is not None

    def body(*refs):
        if has_extra:
            x_ref, g_ref, dy_ref, e_ref, dx_ref, dg_ref = refs
        else:
            x_ref, g_ref, dy_ref, dx_ref, dg_ref = refs
        _, vjp = jax.vjp(_rms, x_ref[...], g_ref[...])
        dx, dg = vjp(dy_ref[...])
        dx_ref[...] = dx + e_ref[...] if has_extra else dx

        @pl.when(pl.program_id(0) == 0)
        def _():
            dg_ref[...] = jnp.zeros_like(dg_ref)

        dg_ref[...] += dg

    ins = [x, g, dy] + ([extra] if has_extra else [])
    specs = [_row_spec(tr, D), _full_spec((1, D)), _row_spec(tr, D)] + ([_row_spec(tr, D)] if has_extra else [])
    return _pcall(body, name=name, grid=(R // tr,), in_specs=specs,
                  out_specs=[_row_spec(tr, D), _full_spec((1, D))],
                  out_shape=[jax.ShapeDtypeStruct((R, D), F32), jax.ShapeDtypeStruct((1, D), F32)],
                  compiler_params=_cp(("arbitrary",)))(*ins)


def _head_indicator(C, hp):
    e = (jnp.arange(C)[:, None] // HEAD == jnp.arange(hp)[None, :]).astype(F32)
    return e, e.T


def _prep_fn(C, L, ps, prev, mu, w0, a0, k_k, k_a, wdu, wiu, E, ET):
    sh = ps + (prev - ps) * mu
    r, k, v = sh[:, :C], sh[:, C:2 * C], sh[:, 2 * C:3 * C]
    wl, al = sh[:, 3 * C:3 * C + L], sh[:, 3 * C + L:3 * C + 2 * L]
    wd = w0 + _mm(jnp.tanh(wl), wdu, False, False, "bf")
    w_pre = -_softplus(-wd) - 0.5
    lw = -jnp.exp(w_pre)
    alpha = _sigmoid(a0 + _mm(al, wiu, False, False, "bf"))
    kk = k * k_k
    ss = _seg(kk * kk, E)
    kk = kk * _seg(lax.rsqrt(jnp.maximum(ss, 1e-24)), ET)
    k_mod = k * (1.0 + (alpha - 1.0) * k_a)
    return r, lw, k_mod, v, -kk, kk * alpha


SUBLANES = 8


def _edge_spec(tr, width, nrows, before):
    per, last = tr // SUBLANES, nrows // SUBLANES - 1
    if before:
        return pl.BlockSpec((SUBLANES, width), lambda i: (jnp.maximum(i * per - 1, 0), 0))
    return pl.BlockSpec((SUBLANES, width), lambda i: (jnp.minimum((i + 1) * per, last), 0))


def _previous_rows(cur, before_ref):
    edge = before_ref[SUBLANES - 1:SUBLANES, :] * (pl.program_id(0) > 0).astype(F32)
    row = lax.broadcasted_iota(jnp.int32, cur.shape, 0)
    return jnp.where(row == 0, edge, pltpu.roll(cur, 1, 0))


def _prep_fwd(p, params, C, L, tr=128):
    T = p.shape[0]
    SH = 3 * C + 2 * L
    tr = _tile(T, tr)

    def body(ps_ref, before_ref, mu, w0, a0, kk_, ka_, wdu, wiu, E, ET, *outs):
        ps = ps_ref[...]
        vals = _prep_fn(C, L, ps, _previous_rows(ps, before_ref), mu[...], w0[...], a0[...], kk_[...], ka_[...],
                        wdu[...], wiu[...], E[...], ET[...])
        for o, v in zip(outs, vals):
            o[...] = v

    pspecs = [_full_spec(a.shape) for a in params]
    return _pcall(body, name="rwkv_prep_fwd", grid=(T // tr,),
                  in_specs=[_row_spec(tr, SH), _edge_spec(tr, SH, T, True)] + pspecs,
                  out_specs=[_row_spec(tr, C)] * 6,
                  out_shape=[jax.ShapeDtypeStruct((T, C), F32)] * 6,
                  compiler_params=_cp(("arbitrary",)))(p, p, *params)


def _prep_bwd(p, params, cts, C, L, tr=128):
    T = p.shape[0]
    SH = 3 * C + 2 * L
    tr = _tile(T, tr)
    nparam = 7

    def body(ps_ref, before_ref, mu, w0, a0, kk_, ka_, wdu, wiu, E, ET, c0, c1, c2, c3, c4, c5, e0, e2, e3,
             dps_ref, dprev_ref, *dpar):
        f = functools.partial(_prep_fn, C, L)
        fe = lambda ps, prev, *par: f(ps, prev, *par, E[...], ET[...])
        ps = ps_ref[...]
        _, vjp = jax.vjp(fe, ps, _previous_rows(ps, before_ref), mu[...], w0[...], a0[...], kk_[...], ka_[...], wdu[...], wiu[...])
        grads = vjp((c0[...] + e0[...], c1[...], c2[...] + e2[...], c3[...] + e3[...], c4[...], c5[...]))
        dps_ref[...] = grads[0]
        dprev_ref[...] = grads[1]

        @pl.when(pl.program_id(0) == 0)
        def _():
            for d in dpar:
                d[...] = jnp.zeros_like(d)

        for d, gval in zip(dpar, grads[2:]):
            d[...] += gval

    pspecs = [_full_spec(a.shape) for a in params]
    par_shapes = [a.shape for a in params[:nparam]]
    return _pcall(body, name="rwkv_prep_bwd", grid=(T // tr,),
                  in_specs=[_row_spec(tr, SH), _edge_spec(tr, SH, T, True)] + pspecs + [_row_spec(tr, C)] * 9,
                  out_specs=[_row_spec(tr, SH), _row_spec(tr, SH)] + [_full_spec(s) for s in par_shapes],
                  out_shape=[jax.ShapeDtypeStruct((T, SH), F32)] * 2 + [jax.ShapeDtypeStruct(s, F32) for s in par_shapes],
                  compiler_params=_cp(("arbitrary",)))(p, p, *params, *cts)


def _shift_combine(d_direct, d_prev, tr=256):
    T, W = d_direct.shape
    tr = _tile(T, tr)
    nt = T // tr

    def body(a_ref, b_ref, after_ref, o_ref):
        cur = b_ref[...]
        edge = after_ref[0:1, :] * (pl.program_id(0) < nt - 1).astype(F32)
        row = lax.broadcasted_iota(jnp.int32, cur.shape, 0)
        nxt = jnp.where(row == tr - 1, edge, pltpu.roll(cur, tr - 1, 0))
        o_ref[...] = (a_ref[...] + nxt).astype(BF16)

    return _pcall(body, name="shift_combine", grid=(nt,),
                  in_specs=[_row_spec(tr, W)] * 2 + [_edge_spec(tr, W, T, False)], out_specs=_row_spec(tr, W),
                  out_shape=jax.ShapeDtypeStruct((T, W), BF16), compiler_params=_cp(("arbitrary",)))(d_direct, d_prev, d_prev)


def _mix_fn(y, r, kmod, v, g_rwkv, yfox, g_fox, ymem, g_mq, lnw, lnb, rk, E, ET):
    inv = 1.0 / HEAD
    mean = _seg(y, E) * inv
    yc = y - _seg(mean, ET)
    var = _seg(yc * yc, E) * inv
    yn = yc * _seg(lax.rsqrt(var + GN_EPS), ET) * lnw + lnb
    bonus = _seg(_seg(r * kmod * rk, E), ET) * v
    o1 = (yn + bonus) * _silu(g_rwkv)
    return jnp.concatenate([o1, yfox * _silu(g_fox), ymem * _silu(g_mq)], axis=1)


def _mix_specs(tr, C, MW):
    return [_row_spec(tr, C)] * 7 + [_row_spec(tr, MW)] * 2


def _mix_fwd(acts, params, C, MW, tr=128):
    T = acts[0].shape[0]
    D = 2 * C + MW
    tr = _tile(T, tr)

    def body(y_, r_, k_, v_, g1, yf, g2, ym, g3, lnw, lnb, rk, E, ET, o_ref):
        o_ref[...] = _mix_fn(y_[...], r_[...], k_[...], v_[...], g1[...], yf[...], g2[...], ym[...], g3[...],
                             lnw[...], lnb[...], rk[...], E[...], ET[...]).astype(BF16)

    return _pcall(body, name="mix_fwd", grid=(T // tr,),
                  in_specs=_mix_specs(tr, C, MW) + [_full_spec(a.shape) for a in params],
                  out_specs=_row_spec(tr, D), out_shape=jax.ShapeDtypeStruct((T, D), BF16),
                  compiler_params=_cp(("arbitrary",)))(*acts, *params)


def _mix_bwd(acts, params, dycat, C, MW, tr=128):
    T = acts[0].shape[0]
    D = 2 * C + MW
    tr = _tile(T, tr)

    def body(y_, r_, k_, v_, g1, yf, g2, ym, g3, lnw, lnb, rk, E, ET, dy_ref, *outs):
        fe = lambda *a: _mix_fn(*a, E[...], ET[...])
        _, vjp = jax.vjp(fe, y_[...], r_[...], k_[...], v_[...], g1[...], yf[...], g2[...], ym[...], g3[...],
                         lnw[...], lnb[...], rk[...])
        grads = vjp(dy_ref[...])
        for o, gval in zip(outs[:9], grads[:9]):
            o[...] = gval

        @pl.when(pl.program_id(0) == 0)
        def _():
            for o in outs[9:]:
                o[...] = jnp.zeros_like(o)

        for o, gval in zip(outs[9:], grads[9:]):
            o[...] += gval

    widths = [C, C, C, C, C, C, C, MW, MW]
    return _pcall(body, name="mix_bwd", grid=(T // tr,),
                  in_specs=_mix_specs(tr, C, MW) + [_full_spec(a.shape) for a in params] + [_row_spec(tr, D)],
                  out_specs=[_row_spec(tr, w) for w in widths] + [_full_spec((1, C))] * 3,
                  out_shape=[jax.ShapeDtypeStruct((T, w), F32) for w in widths] + [jax.ShapeDtypeStruct((1, C), F32)] * 3,
                  compiler_params=_cp(("arbitrary",)))(*acts, *params, dycat)


def _post(yo, x, tgt, g_post, tr=128):
    T, D = x.shape
    tr = _tile(T, tr)

    def body(yo_ref, x_ref, t_ref, g_ref, dyo_ref, dout_ref, loss_ref, dg_ref):
        n, vjp = jax.vjp(_rms, yo_ref[...], g_ref[...])
        diff = (x_ref[...] + n) - t_ref[...]
        part = 0.5 * jnp.sum(jnp.mean(diff * diff, axis=-1, keepdims=True), axis=0, keepdims=True)
        d_out = diff * (1.0 / D)
        dyo, dg = vjp(d_out)
        dyo_ref[...] = dyo.astype(BF16)
        dout_ref[...] = d_out

        @pl.when(pl.program_id(0) == 0)
        def _():
            loss_ref[...] = jnp.zeros_like(loss_ref)
            dg_ref[...] = jnp.zeros_like(dg_ref)

        loss_ref[...] += jnp.broadcast_to(part, loss_ref.shape)
        dg_ref[...] += dg

    return _pcall(body, name="post_loss", grid=(T // tr,),
                  in_specs=[_row_spec(tr, D)] * 3 + [_full_spec((1, D))],
                  out_specs=[_row_spec(tr, D), _row_spec(tr, D), _full_spec((1, LANE)), _full_spec((1, D))],
                  out_shape=[jax.ShapeDtypeStruct((T, D), BF16), jax.ShapeDtypeStruct((T, D), F32),
                             jax.ShapeDtypeStruct((1, LANE), F32), jax.ShapeDtypeStruct((1, D), F32)],
                  compiler_params=_cp(("arbitrary",)))(yo, x, tgt, g_post)


def _memattn_fn(MW, q, mkv):
    hd = MW // MEM_HEADS
    scale = hd ** -0.5
    outs = []
    for h in range(MEM_HEADS):
        qh = q[:, h * hd:(h + 1) * hd]
        kh = mkv[:, h * hd:(h + 1) * hd]
        vh = mkv[:, MW + h * hd:MW + (h + 1) * hd]
        s = _mm(qh, kh, False, True, "bf") * scale
        e = jnp.exp(s - lax.stop_gradient(jnp.max(s, axis=-1, keepdims=True)))
        pr = e / jnp.sum(e, axis=-1, keepdims=True)
        outs.append(_mm(pr, vh, False, False, "bf"))
    return jnp.concatenate(outs, axis=1)


def _memattn_fwd(p, mkv, MW, tr=256):
    T = p.shape[0]
    tr = _tile(T, tr)

    def body(q_ref, kv_ref, o_ref):
        o_ref[...] = _memattn_fn(MW, q_ref[...], kv_ref[...])

    return _pcall(body, name="memattn_fwd", grid=(T // tr,),
                  in_specs=[_row_spec(tr, MW), _full_spec(mkv.shape)], out_specs=_row_spec(tr, MW),
                  out_shape=jax.ShapeDtypeStruct((T, MW), F32), compiler_params=_cp(("arbitrary",)))(p, mkv)


def _memattn_bwd(p, mkv, do, MW, tr=256):
    T = p.shape[0]
    tr = _tile(T, tr)

    def body(q_ref, kv_ref, do_ref, dq_ref, dkv_ref):
        _, vjp = jax.vjp(functools.partial(_memattn_fn, MW), q_ref[...], kv_ref[...])
        dq, dkv = vjp(do_ref[...])
        dq_ref[...] = dq

        @pl.when(pl.program_id(0) == 0)
        def _():
            dkv_ref[...] = jnp.zeros_like(dkv_ref)

        dkv_ref[...] += dkv

    return _pcall(body, name="memattn_bwd", grid=(T // tr,),
                  in_specs=[_row_spec(tr, MW), _full_spec(mkv.shape), _row_spec(tr, MW)],
                  out_specs=[_row_spec(tr, MW), _full_spec(mkv.shape)],
                  out_shape=[jax.ShapeDtypeStruct((T, MW), F32), jax.ShapeDtypeStruct(mkv.shape, F32)],
                  compiler_params=_cp(("arbitrary",)))(p, mkv, do)


def _fox_cum_fwd(p, b_f_pad, off, blk=512):
    T = p.shape[0]
    blk = _tile(T, blk)

    def body(f_ref, b_ref, cum_ref, cumt_ref, carry):
        @pl.when(pl.program_id(0) == 0)
        def _():
            carry[...] = jnp.zeros_like(carry)

        z = f_ref[...] + b_ref[...]
        logf = -_softplus(-z)
        row = lax.broadcasted_iota(jnp.int32, (blk, blk), 0)
        col = lax.broadcasted_iota(jnp.int32, (blk, blk), 1)
        tri = (col <= row).astype(F32)
        c = _dg(tri, logf, False, False, "hi") + carry[...]
        cum_ref[...] = c
        cumt_ref[...] = c.T
        carry[...] += jnp.sum(logf, axis=0, keepdims=True)

    return _pcall(body, name="fox_cum_fwd", grid=(T // blk,),
                  in_specs=[_row_spec(blk, LANE, off // LANE), _full_spec((1, LANE))],
                  out_specs=[_row_spec(blk, LANE), pl.BlockSpec((LANE, blk), lambda i: (0, i))],
                  out_shape=[jax.ShapeDtypeStruct((T, LANE), F32), jax.ShapeDtypeStruct((LANE, T), F32)],
                  scratch_shapes=[pltpu.VMEM((1, LANE), F32)], compiler_params=_cp(("arbitrary",)))(p, b_f_pad)


def _fox_cum_bwd(p, b_f_pad, dcum, off, blk=512):
    T = p.shape[0]
    blk = _tile(T, blk)
    nb = T // blk

    def body(f_ref, b_ref, dc_ref, df_ref, db_ref, carry):
        @pl.when(pl.program_id(0) == 0)
        def _():
            carry[...] = jnp.zeros_like(carry)
            db_ref[...] = jnp.zeros_like(db_ref)

        row = lax.broadcasted_iota(jnp.int32, (blk, blk), 0)
        col = lax.broadcasted_iota(jnp.int32, (blk, blk), 1)
        tri = (col >= row).astype(F32)
        dlogf = _dg(tri, dc_ref[...], False, False, "hi") + carry[...]
        carry[...] += jnp.sum(dc_ref[...], axis=0, keepdims=True)
        z = f_ref[...] + b_ref[...]
        dz = dlogf * (1.0 - _sigmoid(z))
        df_ref[...] = dz
        db_ref[...] += jnp.sum(dz, axis=0, keepdims=True)

    rev = lambda i: (nb - 1 - i, 0)
    return _pcall(body, name="fox_cum_bwd", grid=(nb,),
                  in_specs=[pl.BlockSpec((blk, LANE), lambda i: (nb - 1 - i, off // LANE)), _full_spec((1, LANE)),
                            pl.BlockSpec((blk, LANE), rev)],
                  out_specs=[pl.BlockSpec((blk, LANE), rev), _full_spec((1, LANE))],
                  out_shape=[jax.ShapeDtypeStruct((T, LANE), F32), jax.ShapeDtypeStruct((1, LANE), F32)],
                  scratch_shapes=[pltpu.VMEM((1, LANE), F32)], compiler_params=_cp(("arbitrary",)))(p, b_f_pad, dcum)


FOX_SCALE = HEAD ** -0.5
FOX_TQ, FOX_TK = 512, 1024


def _fox_scores(q, k, ck, q0=None, k0=None):
    s = _dg(q, k, False, True, "bf") - ck
    if q0 is None:
        return s
    qpos = q0 + lax.broadcasted_iota(jnp.int32, s.shape, 0)
    kpos = k0 + lax.broadcasted_iota(jnp.int32, s.shape, 1)
    return jnp.where(kpos <= qpos, s, NEG)


def _fox_c0(ck_ref, hh, pos):
    return ck_ref[0, hh:hh + 1, pl.ds(pl.multiple_of(pos, LANE), LANE)][:, 0:1]


def _fox_tiles(T):
    assert T % LANE == 0
    tq, tk = _tile(T, FOX_TQ), _tile(T, FOX_TK)
    shift = (tk // tq).bit_length() - 1
    assert tk == tq << shift
    return tq, tk, shift


def _fox_fwd(p, ck, C, offs):
    T = p.shape[0]
    tq, tk, shift = _fox_tiles(T)
    npair = C // LANE
    cb = lambda name: offs[name] // LANE

    def body(q_ref, k_ref, v_ref, ck_ref, o_ref, lse_ref):
        i = pl.program_id(1)
        nfull = i >> shift
        lse_ref[...] = jnp.zeros_like(lse_ref)
        heads = [slice(hh * HEAD, (hh + 1) * HEAD) for hh in range(2)]
        qs = [(q_ref[:, sl] * FOX_SCALE).astype(BF16) for sl in heads]
        c0s = [_fox_c0(ck_ref, hh, i * tq) for hh in range(2)]

        def step(j, carry, masked):
            off = pl.multiple_of(j * tk, tk)
            out = []
            for hh, sl in enumerate(heads):
                m, l, acc = carry[hh]
                k = k_ref[pl.ds(off, tk), sl].astype(BF16)
                v = v_ref[pl.ds(off, tk), sl].astype(BF16)
                ckv = ck_ref[0, hh:hh + 1, pl.ds(off, tk)] - c0s[hh]
                s = _fox_scores(qs[hh], k, ckv, i * tq, off) if masked else _fox_scores(qs[hh], k, ckv)
                m_new = jnp.maximum(m, jnp.max(s, axis=-1, keepdims=True))
                pr = jnp.exp(s - m_new)
                al = jnp.exp(m - m_new)
                l = al * l + jnp.sum(pr, axis=-1, keepdims=True)
                acc = al * acc + _dg(pr, v, False, False, "bf")
                out.append((m_new, l, acc))
            return tuple(out)

        one = (jnp.full((tq, 1), NEG, F32), jnp.zeros((tq, 1), F32), jnp.zeros((tq, HEAD), F32))
        carry = lax.fori_loop(0, nfull, lambda j, c: step(j, c, False), (one, one))
        for hh, (m, l, acc) in enumerate(step(nfull, carry, True)):
            o_ref[:, heads[hh]] = acc / l
            lse_ref[0, :, hh:hh + 1] = m + jnp.log(l)

    return _pcall(body, name="fox_fwd", grid=(npair, T // tq),
                  in_specs=[pl.BlockSpec((tq, LANE), lambda h, i: (i, cb("fq") + h)),
                            pl.BlockSpec((T, LANE), lambda h, i: (0, cb("fk") + h)),
                            pl.BlockSpec((T, LANE), lambda h, i: (0, cb("fv") + h)),
                            pl.BlockSpec((1, 8, T), lambda h, i: (h, 0, 0))],
                  out_specs=[pl.BlockSpec((tq, LANE), lambda h, i: (i, h)),
                             pl.BlockSpec((1, tq, 8), lambda h, i: (h, i, 0))],
                  out_shape=[jax.ShapeDtypeStruct((T, C), F32), jax.ShapeDtypeStruct((npair, T, 8), F32)],
                  compiler_params=_cp(("arbitrary", "arbitrary")))(p, p, p, ck)


def _fox_delta(p, ck, do, lse, C, offs):
    T = p.shape[0]
    tq, tk, shift = _fox_tiles(T)
    npair = C // LANE
    cb = lambda name: offs[name] // LANE

    def body(q_ref, k_ref, v_ref, ck_ref, do_ref, lse_ref, d_ref):
        i = pl.program_id(1)
        nfull = i >> shift
        d_ref[...] = jnp.zeros_like(d_ref)
        heads = [slice(hh * HEAD, (hh + 1) * HEAD) for hh in range(2)]
        qs = [(q_ref[:, sl] * FOX_SCALE).astype(BF16) for sl in heads]
        dos = [do_ref[:, sl].astype(BF16) for sl in heads]
        lses = [lse_ref[0, :, hh:hh + 1] for hh in range(2)]
        c0s = [_fox_c0(ck_ref, hh, i * tq) for hh in range(2)]

        def step(j, accs, masked):
            off = pl.multiple_of(j * tk, tk)
            out = []
            for hh, sl in enumerate(heads):
                k = k_ref[pl.ds(off, tk), sl].astype(BF16)
                v = v_ref[pl.ds(off, tk), sl].astype(BF16)
                ckv = ck_ref[0, hh:hh + 1, pl.ds(off, tk)] - c0s[hh]
                s = _fox_scores(qs[hh], k, ckv, i * tq, off) if masked else _fox_scores(qs[hh], k, ckv)
                pr = jnp.exp(s - lses[hh])
                dp = _dg(dos[hh], v, False, True, "bf")
                out.append(accs[hh] + jnp.sum(pr * dp, axis=-1, keepdims=True))
            return tuple(out)

        z = jnp.zeros((tq, 1), F32)
        accs = lax.fori_loop(0, nfull, lambda j, c: step(j, c, False), (z, z))
        for hh, acc in enumerate(step(nfull, accs, True)):
            d_ref[0, :, hh:hh + 1] = acc

    return _pcall(body, name="fox_delta", grid=(npair, T // tq),
                  in_specs=[pl.BlockSpec((tq, LANE), lambda h, i: (i, cb("fq") + h)),
                            pl.BlockSpec((T, LANE), lambda h, i: (0, cb("fk") + h)),
                            pl.BlockSpec((T, LANE), lambda h, i: (0, cb("fv") + h)),
                            pl.BlockSpec((1, 8, T), lambda h, i: (h, 0, 0)),
                            pl.BlockSpec((tq, LANE), lambda h, i: (i, h)),
                            pl.BlockSpec((1, tq, 8), lambda h, i: (h, i, 0))],
                  out_specs=pl.BlockSpec((1, tq, 8), lambda h, i: (h, i, 0)),
                  out_shape=jax.ShapeDtypeStruct((npair, T, 8), F32),
                  compiler_params=_cp(("arbitrary", "arbitrary")))(p, p, p, ck, do, lse)


def _fox_bwd(p, ck, delta, do, lse, C, offs):
    T = p.shape[0]
    tq, tk, shift = _fox_tiles(T)
    ratio = tk // tq
    nq = T // tq
    npair = C // LANE
    cb = lambda name: offs[name] // LANE

    def body(q_ref, k_ref, v_ref, ck_ref, ckall_ref, dl_ref, do_ref, lse_ref, dq_ref, dk_ref, dv_ref, dck_ref):
        j = pl.program_id(1)

        @pl.when(j == 0)
        def _():
            dq_ref[...] = jnp.zeros_like(dq_ref)

        dck_ref[...] = jnp.zeros_like(dck_ref)
        heads = [slice(hh * HEAD, (hh + 1) * HEAD) for hh in range(2)]
        ks = [k_ref[:, sl].astype(BF16) for sl in heads]
        vs = [v_ref[:, sl].astype(BF16) for sl in heads]
        cks = [ck_ref[0, hh:hh + 1, :] for hh in range(2)]

        def step(i, carry, masked):
            off = pl.multiple_of(i * tq, tq)
            out = []
            for hh, sl in enumerate(heads):
                dk, dv, dck = carry[hh]
                q = (q_ref[pl.ds(off, tq), sl] * FOX_SCALE).astype(BF16)
                dov = do_ref[pl.ds(off, tq), sl]
                lsev = lse_ref[0, pl.ds(off, tq), hh:hh + 1]
                ckv = cks[hh] - _fox_c0(ckall_ref, hh, off)
                s = _fox_scores(q, ks[hh], ckv, off, j * tk) if masked else _fox_scores(q, ks[hh], ckv)
                pr = jnp.exp(s - lsev)
                dv = dv + _dg(pr, dov, True, False, "bf")
                dp = _dg(dov, vs[hh], False, True, "bf")
                ds = pr * (dp - dl_ref[0, pl.ds(off, tq), hh:hh + 1])
                dk = dk + _dg(ds, q, True, False, "bf")
                dq_ref[pl.ds(off, tq), sl] += _dg(ds, ks[hh], False, False, "bf") * FOX_SCALE
                out.append((dk, dv, dck - jnp.sum(ds, axis=0, keepdims=True)))
            return tuple(out)

        z = jnp.zeros((tk, HEAD), F32)
        carry = ((z, z, jnp.zeros((1, tk), F32)),) * 2
        for r in range(ratio):
            carry = step(j * ratio + r, carry, True)
        carry = lax.fori_loop((j + 1) * ratio, nq, lambda i, c: step(i, c, False), carry)
        for hh, (dk, dv, dck) in enumerate(carry):
            dk_ref[:, heads[hh]] = dk
            dv_ref[:, heads[hh]] = dv
            dck_ref[0, hh:hh + 1, :] = dck

    full = lambda h, j: (0, h)
    return _pcall(body, name="fox_bwd", grid=(npair, T // tk),
                  in_specs=[pl.BlockSpec((T, LANE), lambda h, j: (0, cb("fq") + h)),
                            pl.BlockSpec((tk, LANE), lambda h, j: (j, cb("fk") + h)),
                            pl.BlockSpec((tk, LANE), lambda h, j: (j, cb("fv") + h)),
                            pl.BlockSpec((1, 8, tk), lambda h, j: (h, 0, j)),
                            pl.BlockSpec((1, 8, T), lambda h, j: (h, 0, 0)),
                            pl.BlockSpec((1, T, 8), lambda h, j: (h, 0, 0)), pl.BlockSpec((T, LANE), full),
                            pl.BlockSpec((1, T, 8), lambda h, j: (h, 0, 0))],
                  out_specs=[pl.BlockSpec((T, LANE), full),
                             pl.BlockSpec((tk, LANE), lambda h, j: (j, h)),
                             pl.BlockSpec((tk, LANE), lambda h, j: (j, h)),
                             pl.BlockSpec((1, 8, tk), lambda h, j: (h, 0, j))],
                  out_shape=[jax.ShapeDtypeStruct((T, C), F32)] * 3 + [jax.ShapeDtypeStruct((npair, 8, T), F32)],
                  compiler_params=_cp(("arbitrary", "arbitrary")))(p, p, p, ck, ck, delta, do, lse)


@jax.custom_vjp
def _tri_sum(tri, x):
    return _dg(tri, x, False, False, "e3")


def _tri_sum_fwd(tri, x):
    return _dg(tri, x, False, False, "e3"), tri


def _tri_sum_bwd(tri, g):
    return jnp.zeros_like(tri), _dg(tri, g, True, False, "e3")


_tri_sum.defvjp(_tri_sum_fwd, _tri_sum_bwd)


@jax.custom_vjp
def _solve(a, rhs):
    return _solve_fwd(a, rhs)[0]


def _solve_fwd(a, rhs):
    powers = [a]
    for _ in range(CHUNK.bit_length() - 2):
        powers.append(_dg(powers[-1], powers[-1], False, False, "x3"))
    u = rhs
    for pw in powers:
        u = u + _dg(pw, u, False, False, "x3")
    return u, (powers, u)


def _solve_bwd(res, g):
    powers, u = res
    w = g
    for pw in powers:
        w = w + _dg(pw, w, True, False, "x3")
    return _dg(w, u, False, True, "x3"), w


_solve.defvjp(_solve_fwd, _solve_bwd)


def _chunk_fn(S0, r, lw, k, v, a, b):
    nh, n = r.shape[0], r.shape[1]
    row = lax.broadcasted_iota(jnp.int32, (nh, n, n), 1)
    col = lax.broadcasted_iota(jnp.int32, (nh, n, n), 2)
    incl, strict = col <= row, col < row
    mm = lambda x, y, ta=False, tb=False: _mm(x, y, ta, tb, "x3")
    g = _tri_sum(incl.astype(F32), lw)
    einv = jnp.exp(-g)
    rt, at, bt, kt = r * jnp.exp(g), a * jnp.exp(g - lw), b * einv, k * einv
    ar = jnp.concatenate([at, rt], axis=1)
    blk = mm(ar, jnp.concatenate([bt, kt], axis=1), tb=True)
    a_ab = jnp.where(strict, blk[:, :n, :n], 0.0)
    a_ak = jnp.where(strict, blk[:, :n, n:], 0.0)
    r_b = jnp.where(incl, blk[:, n:, :n], 0.0)
    r_k = jnp.where(incl, blk[:, n:, n:], 0.0)
    on_s0 = mm(ar, S0, tb=True)
    on_v = mm(jnp.concatenate([a_ak, r_k], axis=1), v)
    u = _solve(a_ab, on_s0[:, :n] + on_v[:, :n])
    y = on_s0[:, n:] + mm(r_b, u) + on_v[:, n:]
    g_end = jnp.sum(lw, axis=1, keepdims=True)
    s_end = (S0 + mm(jnp.concatenate([u, v], axis=1), jnp.concatenate([bt, kt], axis=1), ta=True)) * jnp.exp(g_end)
    return y, s_end


SCAN_HEADS = 24
SCAN_ROWS = 64


def _scan_group(C):
    nh = SCAN_HEADS
    while C % (nh * HEAD):
        nh -= 2
    return nh, nh * HEAD


def _scan_fwd(r, lw, k, v, a, b):
    T, C = r.shape
    tc = _tile(T, SCAN_ROWS)
    ncs = tc // CHUNK
    nh, GW = _scan_group(C)
    ngroup = C // GW

    def body(r_ref, lw_ref, k_ref, v_ref, a_ref, b_ref, y_ref, ck_ref, state):
        @pl.when(pl.program_id(1) == 0)
        def _():
            state[...] = jnp.zeros_like(state)

        heads = [slice(hh * HEAD, (hh + 1) * HEAD) for hh in range(nh)]
        split = lambda ref, rows: jnp.stack([ref[rows, sl] for sl in heads])
        st = split(state, slice(None))
        for c in range(ncs):
            rows = slice(c * CHUNK, (c + 1) * CHUNK)
            for hh, sl in enumerate(heads):
                ck_ref[0, c, :, sl] = st[hh]
            y, st = _chunk_fn(st, *[split(ref, rows) for ref in (r_ref, lw_ref, k_ref, v_ref, a_ref, b_ref)])
            for hh, sl in enumerate(heads):
                y_ref[rows, sl] = y[hh]
        for hh, sl in enumerate(heads):
            state[:, sl] = st[hh]

    spec = pl.BlockSpec((tc, GW), lambda h, t: (t, h))
    return _pcall(body, name="rwkv_scan_fwd", grid=(ngroup, T // tc),
                  in_specs=[spec] * 6,
                  out_specs=[spec, pl.BlockSpec((1, ncs, HEAD, GW), lambda h, t: (h, t, 0, 0))],
                  out_shape=[jax.ShapeDtypeStruct((T, C), F32),
                             jax.ShapeDtypeStruct((ngroup, T // CHUNK, HEAD, GW), F32)],
                  scratch_shapes=[pltpu.VMEM((HEAD, GW), F32)],
                  compiler_params=_cp(("arbitrary", "arbitrary")))(r, lw, k, v, a, b)


def _scan_bwd(r, lw, k, v, a, b, ckpt, dy):
    T, C = r.shape
    tc = _tile(T, SCAN_ROWS)
    ncs = tc // CHUNK
    nh, GW = _scan_group(C)
    ngroup = C // GW
    nt = T // tc

    def body(r_ref, lw_ref, k_ref, v_ref, a_ref, b_ref, ck_ref, dy_ref, dr, dlw, dk, dv, da, db, dstate):
        @pl.when(pl.program_id(1) == 0)
        def _():
            dstate[...] = jnp.zeros_like(dstate)

        outs = (dr, dlw, dk, dv, da, db)
        heads = [slice(hh * HEAD, (hh + 1) * HEAD) for hh in range(nh)]
        split = lambda ref, rows: jnp.stack([ref[rows, sl] for sl in heads])
        dst = split(dstate, slice(None))
        for c in reversed(range(ncs)):
            rows = slice(c * CHUNK, (c + 1) * CHUNK)
            s0 = jnp.stack([ck_ref[0, c, :, sl] for sl in heads])
            _, vjp = jax.vjp(_chunk_fn, s0, *[split(ref, rows) for ref in (r_ref, lw_ref, k_ref, v_ref, a_ref, b_ref)])
            grads = vjp((split(dy_ref, rows), dst))
            dst = grads[0]
            for o, gval in zip(outs, grads[1:]):
                for hh, sl in enumerate(heads):
                    o[rows, sl] = gval[hh]
        for hh, sl in enumerate(heads):
            dstate[:, sl] = dst[hh]

    spec = pl.BlockSpec((tc, GW), lambda h, t: (nt - 1 - t, h))
    return _pcall(body, name="rwkv_scan_bwd", grid=(ngroup, nt),
                  in_specs=[spec] * 6 + [pl.BlockSpec((1, ncs, HEAD, GW), lambda h, t: (h, nt - 1 - t, 0, 0)), spec],
                  out_specs=[spec] * 6, out_shape=[jax.ShapeDtypeStruct((T, C), F32)] * 6,
                  scratch_shapes=[pltpu.VMEM((HEAD, GW), F32)],
                  compiler_params=_cp(("arbitrary", "arbitrary")))(r, lw, k, v, a, b, ckpt, dy)


def _adam(w, g, m, v):
    m = ADAM_B1 * m + (1.0 - ADAM_B1) * g
    v = ADAM_B2 * v + (1.0 - ADAM_B2) * (g * g)
    m_hat = m / (1.0 - ADAM_B1 ** ADAM_STEP)
    v_hat = v / (1.0 - ADAM_B2 ** ADAM_STEP)
    return -ADAM_LR * (m_hat / (jnp.sqrt(v_hat) + ADAM_EPS) + ADAM_WD * w), m, v


def _sum_adam(parts, w, m, v, name):
    n, R, W = parts.shape
    tr = _tile(R, max(8, min(256, (1 << 20) // (n * W))))

    def body(p_ref, w_ref, m_ref, v_ref, g_ref, d_ref, nm_ref, nv_ref):
        g = p_ref[0].astype(F32)
        for s in range(1, n):
            g = g + p_ref[s].astype(F32)
        d, nm, nv = _adam(w_ref[...], g, m_ref[...], v_ref[...])
        g_ref[...] = g
        d_ref[...] = d
        nm_ref[...] = nm
        nv_ref[...] = nv

    return _pcall(body, name=name, grid=(R // tr,),
                  in_specs=[pl.BlockSpec((n, tr, W), lambda i: (0, i, 0))] + [_row_spec(tr, W)] * 3,
                  out_specs=[_row_spec(tr, W)] * 4, out_shape=[jax.ShapeDtypeStruct((R, W), F32)] * 4,
                  compiler_params=_cp(("arbitrary",)))(parts, w, m, v)


def _pad_lanes(vec, width):
    return jnp.pad(vec, ((0, 0), (0, width - vec.shape[1])))


def _logical_cols(blocks, lo, hi):
    B, out = blocks.shape[2], []
    while lo < hi:
        j, o = divmod(lo, B)
        n = min(hi - lo, B - o)
        out.append(blocks[j, :, o:o + n])
        lo += n
    return out


def kernel(x, mem, g_pre, w_in, mu_rwkv, w0, w_decay_up, a0, w_iclr_up, k_k, k_a, r_k, ln_x_w, ln_x_b, b_f, g_mem, w_mem_kv, w_out, g_post, loss_target, m_g_pre, m_w_in, m_mu_rwkv, m_w0, m_w_decay_up, m_a0, m_w_iclr_up, m_k_k, m_k_a, m_r_k, m_ln_x_w, m_ln_x_b, m_b_f, m_g_mem, m_w_mem_kv, m_w_out, m_g_post, v_g_pre, v_w_in, v_mu_rwkv, v_w0, v_w_decay_up, v_a0, v_w_iclr_up, v_k_k, v_k_a, v_r_k, v_ln_x_w, v_ln_x_b, v_b_f, v_g_mem, v_w_mem_kv, v_w_out, v_g_post):
    T, D = x.shape[1], x.shape[2]
    C = w0.shape[1]
    L = w_decay_up.shape[1]
    H = C // HEAD
    MW = w_mem_kv.shape[2] // 2
    SH = 3 * C + 2 * L
    IN = NDEV * w_in.shape[2]
    assert IN == SH + 5 * C + H + 2 * MW and D == 2 * C + MW and H % 2 == 0 and H <= LANE
    assert C % LANE == 0 and L % LANE == 0 and MW % (MEM_HEADS * HEAD) == 0 and T % CHUNK == 0
    offs = dict(grw=SH, fq=SH + C, fk=SH + 2 * C, fv=SH + 3 * C, gfx=SH + 4 * C, mq=SH + 5 * C, gmq=SH + 5 * C + MW,
                fl=SH + 5 * C + 2 * MW)
    NI = -(-(offs["fl"] + LANE) // 1024) * 1024
    l_fl = SH + 4 * C

    x2, mem2, tgt2 = x[0], mem[0], loss_target[0]

    wg = _exchange(w_in[0].astype(BF16), "gather_w_in", True)
    w_perm = jnp.concatenate(_logical_cols(wg, 0, l_fl) + _logical_cols(wg, l_fl + H, IN)
                             + _logical_cols(wg, l_fl, l_fl + H) + [jnp.zeros((D, NI - IN), BF16)], axis=1)

    h = _rmsnorm_fwd(x2, g_pre, "rmsnorm_pre")
    p, w_out_f, w_kv_f, lora = _matmul(
        h, w_perm, False, False, "in_proj",
        riders=[(w_out[0].astype(BF16), True), (w_mem_kv[0].astype(BF16), True),
                (jnp.concatenate([w_decay_up[0], w_iclr_up[0]], axis=0), True)])
    w_out_f, w_kv_f = w_out_f.reshape(D, D), w_kv_f.reshape(D, 2 * MW)
    lora = jnp.transpose(lora, (1, 0, 2)).reshape(2 * L, C)
    wdu_f, wiu_f = lora[:L], lora[L:]

    E, ET = _head_indicator(C, LANE)
    prep_params = [mu_rwkv, w0, a0, k_k, k_a, wdu_f, wiu_f, E, ET]
    mix_params = [ln_x_w, ln_x_b, r_k.reshape(1, C), E, ET]
    b_f_pad = _pad_lanes(b_f, LANE)
    col = lambda name, w: p[:, offs[name]:offs[name] + w]
    g_rwkv, g_fox, mq, g_mq = col("grw", C), col("gfx", C), col("mq", MW), col("gmq", MW)
    r, lw, kmod, v, a, b = _prep_fwd(p, prep_params, C, L)
    y_scan, ckpt = _scan_fwd(r, lw, kmod, v, a, b)

    cum, cum_t = _fox_cum_fwd(p, b_f_pad, offs["fl"])
    ck = jnp.pad(cum_t[:H].reshape(H // 2, 2, T), ((0, 0), (0, 6), (0, 0)))
    y_fox, lse = _fox_fwd(p, ck, C, offs)

    memn = _rmsnorm_fwd(mem2, g_mem, "rmsnorm_mem")
    mkv = _matmul(memn, w_kv_f, False, False, "mem_kv_proj")
    y_mem = _memattn_fwd(mq, mkv, MW)

    acts = [y_scan, r, kmod, v, g_rwkv, y_fox, g_fox, y_mem, g_mq]
    ycat = _mix_fwd(acts, mix_params, C, MW)
    yo = _matmul(ycat, w_out_f, False, False, "out_proj")
    d_yo, d_out, loss_part, dg_post = _post(yo, x2, tgt2, g_post)
    loss = lax.psum(loss_part[0, 0], AXES)

    g_w_out = _matmul(ycat, d_yo, True, False, "grad_w_out", BF16)
    d_ycat = _matmul(d_yo, w_out_f, False, True, "d_ycat")
    (d_y, d_r1, d_k1, d_v1, d_grw, d_yfox, d_gfx, d_ymem, d_gmq, dg_lnw, dg_lnb, dg_rk) = _mix_bwd(
        acts, mix_params, d_ycat, C, MW)

    d_mq, d_mkv = _memattn_bwd(mq, mkv, d_ymem, MW)
    g_w_kv = _matmul(memn, d_mkv, True, False, "grad_w_mem_kv", BF16)
    d_memn = _matmul(d_mkv, w_kv_f, False, True, "d_memn")
    _, dg_mem = _rmsnorm_bwd(mem2, g_mem, d_memn, None, "rmsnorm_mem_bwd")

    delta = _fox_delta(p, ck, d_yfox, lse, C, offs)
    d_fq, d_fk, d_fv, d_ck = _fox_bwd(p, ck, delta, d_yfox, lse, C, offs)
    d_cum = _pad_lanes(d_ck[:, :2, :].reshape(H, T).T, LANE)
    d_fl, dg_bf = _fox_cum_bwd(p, b_f_pad, d_cum, offs["fl"])

    d_r, d_lw, d_k, d_v, d_a, d_b = _scan_bwd(r, lw, kmod, v, a, b, ckpt, d_y)
    cts = [d_r, d_lw, d_k, d_v, d_a, d_b, d_r1, d_k1, d_v1]
    (d_ps, d_prev, dg_mu, dg_w0, dg_a0, dg_kk, dg_ka, dg_wdu, dg_wiu) = _prep_bwd(p, prep_params, cts, C, L)
    d_sh = _shift_combine(d_ps, d_prev)

    tobf = lambda z: z.astype(BF16)
    dp = jnp.concatenate([d_sh, tobf(d_grw), tobf(d_fq), tobf(d_fk), tobf(d_fv), tobf(d_gfx), tobf(d_mq), tobf(d_gmq),
                          tobf(d_fl), jnp.zeros((T, NI - offs["fl"] - LANE), BF16)], axis=1)
    g_lora = jnp.concatenate([dg_wdu, dg_wiu], axis=0)
    g_w_perm, parts_out, parts_kv, parts_lora = _matmul(
        h, dp, True, False, "grad_w_in", BF16,
        riders=[(g_w_out.reshape(NDEV, D // NDEV, D), False), (g_w_kv.reshape(NDEV, D // NDEV, 2 * MW), False),
                (jnp.transpose(g_lora.reshape(2 * L, NDEV, C // NDEV), (1, 0, 2)), False)])

    def internal_cols(lo, hi):
        out = []
        for first, last, shift in ((0, l_fl, 0), (l_fl, l_fl + H, offs["fl"] - l_fl), (l_fl + H, IN, -H)):
            s0, s1 = max(lo, first), min(hi, last)
            if s0 < s1:
                out.append(g_w_perm[:, s0 + shift:s1 + shift])
        return out

    blk = IN // NDEV
    g_blocks = jnp.stack([jnp.concatenate(internal_cols(j * blk, (j + 1) * blk), axis=1) for j in range(NDEV)])
    d_h, parts_in = _matmul(dp, w_perm, False, True, "d_h", tk=5120, riders=[(g_blocks, False)])
    grad_x, dg_pre = _rmsnorm_bwd(x2, g_pre, d_h, d_out, "rmsnorm_pre_bwd")

    gw_in, dw_in, nm_w_in, nv_w_in = _sum_adam(parts_in, w_in[0], m_w_in[0], v_w_in[0], "adam_w_in")
    gw_out, dw_out, nm_w_out, nv_w_out = _sum_adam(parts_out, w_out[0], m_w_out[0], v_w_out[0], "adam_w_out")
    gw_kv, dw_kv, nm_w_kv, nv_w_kv = _sum_adam(parts_kv, w_mem_kv[0], m_w_mem_kv[0], v_w_mem_kv[0], "adam_w_mem_kv")
    cat2 = lambda u, w_: jnp.concatenate([u[0], w_[0]], axis=0)
    lora_res = _sum_adam(parts_lora, cat2(w_decay_up, w_iclr_up), cat2(m_w_decay_up, m_w_iclr_up),
                         cat2(v_w_decay_up, v_w_iclr_up), "adam_lora")

    small = [("g_pre", g_pre, m_g_pre, v_g_pre, dg_pre), ("mu_rwkv", mu_rwkv, m_mu_rwkv, v_mu_rwkv, dg_mu),
             ("w0", w0, m_w0, v_w0, dg_w0), ("a0", a0, m_a0, v_a0, dg_a0), ("k_k", k_k, m_k_k, v_k_k, dg_kk),
             ("k_a", k_a, m_k_a, v_k_a, dg_ka), ("r_k", r_k.reshape(1, C), m_r_k.reshape(1, C), v_r_k.reshape(1, C), dg_rk),
             ("ln_x_w", ln_x_w, m_ln_x_w, v_ln_x_w, dg_lnw), ("ln_x_b", ln_x_b, m_ln_x_b, v_ln_x_b, dg_lnb),
             ("b_f", _pad_lanes(b_f, LANE), _pad_lanes(m_b_f, LANE), _pad_lanes(v_b_f, LANE), dg_bf),
             ("g_mem", g_mem, m_g_mem, v_g_mem, dg_mem), ("g_post", g_post, m_g_post, v_g_post, dg_post)]
    widths = [s[1].shape[1] for s in small]
    pack = lambda idx: jnp.concatenate([s[idx] for s in small], axis=1).reshape(-1, LANE)
    parts_small = _exchange(pack(4), "gather_small_grads", True)
    res_small = _sum_adam(parts_small, pack(1), pack(2), pack(3), "adam_small")

    def unpack(flat):
        flat = flat.reshape(1, -1)
        out, o = {}, 0
        for (name, *_), wd in zip(small, widths):
            out[name] = flat[:, o:o + wd]
            o += wd
        out["b_f"] = out["b_f"][:, :H]
        out["r_k"] = out["r_k"].reshape(1, H, HEAD)
        return out

    sg, sd, sm, sv = [unpack(z) for z in res_small]
    big = {"w_in": (gw_in, dw_in, nm_w_in, nv_w_in), "w_out": (gw_out, dw_out, nm_w_out, nv_w_out),
           "w_mem_kv": (gw_kv, dw_kv, nm_w_kv, nv_w_kv),
           "w_decay_up": tuple(z[:L] for z in lora_res), "w_iclr_up": tuple(z[L:] for z in lora_res)}
    order = ["g_pre", "w_in", "mu_rwkv", "w0", "w_decay_up", "a0", "w_iclr_up", "k_k", "k_a", "r_k", "ln_x_w", "ln_x_b",
             "b_f", "g_mem", "w_mem_kv", "w_out", "g_post"]

    def pick(name, idx):
        if name in big:
            return big[name][idx][None]
        return (sg, sd, sm, sv)[idx][name]

    outs = [loss, grad_x[None]]
    for idx in range(4):
        outs += [pick(n, idx) for n in order]
    return tuple(outs)
```
